```python
import math
import jax, jax.numpy as jnp
from jax import lax
import numpy as np

D_MODEL = 1024
BATCH = 2
SEQ = 8192
DEPTH = 4
DEC_BATCH = 128
DEC_SEQ = 1
PAST_LEN = 8192
PAGE_SIZE = 128

D_A = 512
HD_A = 64
H_A = D_A // HD_A
KV_A = 2
G_A = H_A // KV_A
WINDOW = 128
N_BUCKETS = 32
MAX_DIST = 128
D_B = 512
H_B = 8
DH_B = D_B // H_B
CHUNK_B = 128
D_C = 512
H_C = 4
DK_TOT = D_C // 2
DK_C = DK_TOT // H_C
DV_C = D_C // H_C
GLA_RANK = 16
GLA_TAU = 16.0
GLA_CHUNK = 64

D_MIX = D_A + D_B + D_C
EPS = 1e-6
NEG = -1e30
SPLITS = [D_A, KV_A * HD_A, KV_A * HD_A, D_A,
          D_B, D_B, D_B,
          DK_TOT, DK_TOT, D_C, D_C, GLA_RANK]
D_IN = sum(SPLITS)

kernel_name = "hymba_swa_gmlp_gla_step"


def rmsnorm(x, g):
    xf = x.astype(jnp.float32)
    y = xf * lax.rsqrt(jnp.mean(xf * xf, axis=-1, keepdims=True) + EPS)
    return (y * g.astype(jnp.float32)).astype(x.dtype)


def t5_bucket(dist):
    n = np.maximum(dist, 0)
    max_exact = N_BUCKETS // 2
    large = max_exact + (np.log(np.maximum(n, 1) / max_exact) / np.log(MAX_DIST / max_exact)
                         * (N_BUCKETS - max_exact)).astype(np.int32)
    large = np.minimum(large, N_BUCKETS - 1)
    return np.where(n < max_exact, n, large).astype(np.int32)


def rel_bias_lookup(rel_bias, dist):
    b = jnp.take(rel_bias.astype(jnp.float32), jnp.asarray(t5_bucket(dist)), axis=0)
    return jnp.moveaxis(b, -1, 0).reshape(KV_A, G_A, *dist.shape)


def attend_sinks(q, k, v, bias, mask, sink):
    logits = jnp.einsum('...qkgd,...skd->...kgqs', q, k).astype(jnp.float32) * (HD_A ** -0.5)
    logits = jnp.where(mask, logits + bias, NEG)
    s = sink.astype(jnp.float32).reshape(KV_A, G_A)[:, :, None]
    lse = jnp.logaddexp(jax.nn.logsumexp(logits, axis=-1), s)
    p = jnp.exp(logits - lse[..., None])
    return jnp.einsum('...kgqs,...skd->...qkgd', p.astype(v.dtype), v)


def swa_prompt(q, k, v, rel_bias, sink):
    B, S = q.shape[:2]
    nb = S // WINDOW
    qb = q.reshape(B, nb, WINDOW, KV_A, G_A, HD_A)
    pad = ((0, 0), (WINDOW, 0), (0, 0), (0, 0))
    kb = jnp.pad(k, pad).reshape(B, nb + 1, WINDOW, KV_A, HD_A)
    vb = jnp.pad(v, pad).reshape(B, nb + 1, WINDOW, KV_A, HD_A)
    keys = jnp.concatenate([kb[:, :-1], kb[:, 1:]], axis=2)
    vals = jnp.concatenate([vb[:, :-1], vb[:, 1:]], axis=2)
    i = np.arange(WINDOW)[:, None]
    j = np.arange(2 * WINDOW)[None, :]
    dist = i + WINDOW - j
    band = (dist >= 0) & (dist < WINDOW)
    kpos = np.arange(nb)[:, None, None] * WINDOW + j[None] - WINDOW
    mask = (band[None] & (kpos >= 0))[:, None, None]
    o = attend_sinks(qb, keys, vals, rel_bias_lookup(rel_bias, dist), mask, sink)
    return o.reshape(B, S, D_A)


def swa_sample(q, k, v, buf_k, buf_v, rel_bias, sink):
    Bd, T = q.shape[:2]
    Wb = buf_k.shape[1]
    keys = jnp.concatenate([buf_k.astype(k.dtype), k], axis=1)
    vals = jnp.concatenate([buf_v.astype(v.dtype), v], axis=1)
    dist = np.arange(T)[:, None] + Wb - np.arange(Wb + T)[None, :]
    mask = (dist >= 0) & (dist < WINDOW)
    o = attend_sinks(q.reshape(Bd, T, KV_A, G_A, HD_A), keys, vals,
                     rel_bias_lookup(rel_bias, dist), mask, sink)
    return o.reshape(Bd, T, D_A), keys[:, T:], vals[:, T:]


def chunk_mlp(u, v_raw, w_s, b_s, ln_g, ln_b):
    B, T = u.shape[:2]
    L = CHUNK_B if T % CHUNK_B == 0 else T
    nc = T // L
    vf = v_raw.astype(jnp.float32)
    mu = jnp.mean(vf, axis=-1, keepdims=True)
    var = jnp.mean(jnp.square(vf - mu), axis=-1, keepdims=True)
    vn = (vf - mu) * lax.rsqrt(var + EPS) * ln_g.astype(jnp.float32) + ln_b.astype(jnp.float32)
    wm = jnp.where(np.tril(np.ones((L, L), bool)), w_s[:, :L, :L].astype(jnp.float32), 0.0)
    mix = jnp.einsum('hij,bcjhd->bcihd', wm, vn.reshape(B, nc, L, H_B, DH_B))
    mix = mix + b_s[:, :L].astype(jnp.float32).T[:, :, None]
    out = u.astype(jnp.float32) * mix.reshape(B, T, D_B)
    return out.astype(u.dtype), vn.astype(u.dtype)


def gla(q, k, v, log_a, s0):
    B, T = q.shape[:2]
    L = math.gcd(T, GLA_CHUNK)
    nc = T // L
    f = lambda t, d: t.astype(jnp.float32).reshape(B, nc, L, H_C, d)
    q = f(q, DK_C) * (DK_C ** -0.5)
    k = f(k, DK_C)
    v = f(v, DV_C)
    b = jnp.cumsum(f(log_a, DK_C), axis=2)
    q_dec = q * jnp.exp(b)
    k_inv = k * jnp.exp(-b)
    causal = np.tril(np.ones((L, L), bool))
    att = jnp.where(causal, jnp.einsum('bnqhd,bnshd->bnhqs', q_dec, k_inv), 0.0)
    o = jnp.einsum('bnhqs,bnshv->bnqhv', att, v)
    b_last = b[:, :, -1]
    k_dec = k * jnp.exp(b_last[:, :, None] - b)
    dS = jnp.einsum('bnshd,bnshv->bnhdv', k_dec, v)

    def step(S, xs):
        a, d = xs
        return a[..., None] * S + d, S

    S_fin, S_prev = lax.scan(step, s0.astype(jnp.float32),
                             (jnp.moveaxis(jnp.exp(b_last), 1, 0), jnp.moveaxis(dS, 1, 0)))
    o = o + jnp.einsum('bnqhd,nbhdv->bnqhv', q_dec, S_prev)
    return o.reshape(B, T, H_C, DV_C), S_fin


def layer(x, g_norm, w_in_l, w_out_l, sink_l, sw_l, sb_l, ln_g_l, ln_b_l, wup_l, bup_l, gn_l,
          rel_bias, buf_k, buf_v, s0):
    B, T = x.shape[:2]
    h = rmsnorm(x, g_norm)
    proj = h @ w_in_l
    offs = [int(o) for o in np.cumsum(SPLITS)[:-1]]
    (qa, ka, va, ga, ub, vb, gb, qc, kc, vc, gc, lr) = jnp.split(proj, offs, axis=-1)
    ka = ka.reshape(B, T, KV_A, HD_A)
    va = va.reshape(B, T, KV_A, HD_A)
    if buf_k is None:
        ya = swa_prompt(qa, ka, va, rel_bias, sink_l)
        wb = min(WINDOW, T)
        new_k, new_v = ka[:, T - wb:], va[:, T - wb:]
    else:
        ya, new_k, new_v = swa_sample(qa, ka, va, buf_k, buf_v, rel_bias, sink_l)
    yb, v_rows = chunk_mlp(ub, vb, sw_l, sb_l, ln_g_l, ln_b_l)
    log_a = jax.nn.log_sigmoid((lr @ wup_l + bup_l).astype(jnp.float32)) / GLA_TAU
    oc, S_fin = gla(qc, kc, vc, log_a, s0)
    oc = oc * lax.rsqrt(jnp.mean(oc * oc, axis=-1, keepdims=True) + EPS)
    yc = (oc * gn_l.astype(jnp.float32).reshape(H_C, DV_C)).reshape(B, T, D_C).astype(x.dtype)
    y = jnp.concatenate([ya * jax.nn.silu(ga), yb * jax.nn.silu(gb), yc * jax.nn.silu(gc)], axis=-1)
    x = x + (y @ w_out_l).astype(x.dtype)
    return x, new_k, new_v, S_fin.astype(x.dtype), v_rows


def setup_inputs(seed: int = 0) -> dict:
    key = jax.random.key(seed)
    ks = jax.random.split(key, 20)
    nrm = lambda k, shape, s: jax.random.normal(k, shape, jnp.float32) * s
    wb = min(WINDOW, PAST_LEN)
    return {
        "x_prompt": nrm(ks[0], (BATCH, SEQ, D_MODEL), 1.0),
        "x_sample": nrm(ks[1], (DEC_BATCH, DEC_SEQ, D_MODEL), 1.0),
        "state_swa_k": nrm(ks[2], (DEPTH, DEC_BATCH, wb, KV_A, HD_A), 1.0),
        "state_swa_v": nrm(ks[3], (DEPTH, DEC_BATCH, wb, KV_A, HD_A), 1.0),
        "state_gla": nrm(ks[4], (DEPTH, DEC_BATCH, H_C, DK_C, DV_C), 0.5),
        "rel_bias": nrm(ks[5], (N_BUCKETS, H_A), 0.5),
        "norm_g": 1.0 + nrm(ks[6], (DEPTH, D_MODEL), 0.01),
        "w_in": nrm(ks[7], (DEPTH, D_MODEL, D_IN), D_MODEL ** -0.5),
        "sinks": nrm(ks[8], (DEPTH, H_A), 0.5),
        "spatial_w": nrm(ks[9], (DEPTH, H_B, CHUNK_B, CHUNK_B), 0.5 * CHUNK_B ** -0.5),
        "spatial_b": 1.0 + nrm(ks[10], (DEPTH, H_B, CHUNK_B), 0.01),
        "chunk_ln_g": 1.0 + nrm(ks[11], (DEPTH, D_B), 0.01),
        "chunk_ln_b": nrm(ks[12], (DEPTH, D_B), 0.01),
        "gla_w_up": nrm(ks[13], (DEPTH, GLA_RANK, DK_TOT), GLA_RANK ** -0.5),
        "gla_b_up": nrm(ks[14], (DEPTH, DK_TOT), 0.1),
        "gla_norm_g": 1.0 + nrm(ks[15], (DEPTH, D_C), 0.01),
        "w_out": nrm(ks[16], (DEPTH, D_MIX, D_MODEL), 0.5 * D_MIX ** -0.5),
        "final_norm_g": 1.0 + nrm(ks[17], (D_MODEL,), 0.01),
    }


def reference(x_prompt, x_sample, state_swa_k, state_swa_v, state_gla, rel_bias, norm_g, w_in, sinks,
              spatial_w, spatial_b, chunk_ln_g, chunk_ln_b, gla_w_up, gla_b_up, gla_norm_g, w_out,
              final_norm_g):
    xp, xs = x_prompt, x_sample
    s0_prompt = jnp.zeros((xp.shape[0], H_C, DK_C, DV_C), jnp.float32)
    kp_l, vp_l, sp_l, ks_l, vs_l, ss_l, cv_l = [], [], [], [], [], [], []
    for l in range(DEPTH):
        w = (norm_g[l], w_in[l], w_out[l], sinks[l], spatial_w[l], spatial_b[l], chunk_ln_g[l],
             chunk_ln_b[l], gla_w_up[l], gla_b_up[l], gla_norm_g[l], rel_bias)
        xp, kp, vp, sp, _ = layer(xp, *w, None, None, s0_prompt)
        xs, k_s, v_s, s_s, cv = layer(xs, *w, state_swa_k[l], state_swa_v[l], state_gla[l])
        kp_l.append(kp); vp_l.append(vp); sp_l.append(sp)
        ks_l.append(k_s); vs_l.append(v_s); ss_l.append(s_s); cv_l.append(cv)
    y_prompt = rmsnorm(xp, final_norm_g)
    y_sample = rmsnorm(xs, final_norm_g)
    return (y_prompt, y_sample, jnp.stack(kp_l), jnp.stack(vp_l), jnp.stack(sp_l),
            jnp.stack(ks_l), jnp.stack(vs_l), jnp.stack(ss_l), jnp.stack(cv_l))
```

```python
import functools

import numpy as np
import jax
import jax.numpy as jnp
from jax import lax
from jax.experimental import pallas as pl
from jax.experimental.pallas import tpu as pltpu

D_MODEL = 1024
D_A, HD_A, H_A, KV_A, G_A = 512, 64, 8, 2, 4
WINDOW, N_BUCKETS, MAX_DIST = 128, 32, 128
D_B, H_B, DH_B, CHUNK_B = 512, 8, 64, 128
D_C, H_C, DK_TOT, DK_C, DV_C = 512, 4, 256, 64, 128
GLA_RANK, GLA_TAU, GLA_CHUNK = 16, 16.0, 64
D_MIX = D_A + D_B + D_C
EPS = 1e-6
NEG = -1e30
SPLITS = [D_A, KV_A * HD_A, KV_A * HD_A, D_A, D_B, D_B, D_B, DK_TOT, DK_TOT, D_C, D_C, GLA_RANK]
D_IN = sum(SPLITS)

LANES = 128
HALF = LANES // 2
D_IN_PAD = ((D_IN + LANES - 1) // LANES) * LANES
VMEM_LIMIT_BYTES = 56 * 1024 * 1024

O_QA, O_KA, O_VA, O_GA = 0, 512, 640, 768
O_UB, O_VB, O_GB = 1280, 1792, 2304
O_QC, O_KC, O_VC, O_GC, O_LR = 2816, 3072, 3328, 3840, 4352
Y_A, Y_B, Y_C = 0, D_A, D_A + D_B

HEAD_PERM = [0, 4, 1, 5, 2, 6, 3, 7]

TOKEN_BLOCK = 256
SEQ_GROUP = 16

_NT = (((1,), (1,)), ((), ()))


def _t5_bucket(dist):
    n = np.maximum(dist, 0)
    max_exact = N_BUCKETS // 2
    large = max_exact + (np.log(np.maximum(n, 1) / max_exact) / np.log(MAX_DIST / max_exact)
                         * (N_BUCKETS - max_exact)).astype(np.int32)
    large = np.minimum(large, N_BUCKETS - 1)
    return np.where(n < max_exact, n, large).astype(np.int32)


def _silu(x):
    return x * (1.0 / (1.0 + jnp.exp(-x)))


def _log_sigmoid(x):
    return jnp.minimum(x, 0.0) - jnp.log1p(jnp.exp(-jnp.abs(x)))


def _bdot(a, b):
    return jnp.dot(a.astype(jnp.bfloat16), b.astype(jnp.bfloat16), preferred_element_type=jnp.float32)


def _bdot_nt(a, b):
    return lax.dot_general(a.astype(jnp.bfloat16), b.astype(jnp.bfloat16), _NT,
                           preferred_element_type=jnp.float32)


def _rmsnorm(x, g):
    return x * lax.rsqrt(jnp.mean(x * x, axis=-1, keepdims=True) + EPS) * g


def _layernorm(v, g, b):
    mu = jnp.mean(v, axis=-1, keepdims=True)
    xc = v - mu
    var = jnp.mean(xc * xc, axis=-1, keepdims=True)
    return xc * lax.rsqrt(var + EPS) * g + b


def _softmax_sink(s, sink):
    m = jnp.maximum(jnp.max(s, axis=-1, keepdims=True), sink)
    p = jnp.exp(s - m)
    den = jnp.sum(p, axis=-1, keepdims=True) + jnp.exp(sink - m)
    return p, 1.0 / den


def _bias_table_kernel(rb_ref, bucket_ref, band_ref, out_ref):
    bucket = bucket_ref[...]
    band = band_ref[...] > 0
    own = lax.broadcasted_iota(jnp.int32, bucket.shape, 1) >= WINDOW
    for h in range(H_A):
        acc = jnp.zeros(bucket.shape, jnp.float32)
        for b in range(N_BUCKETS):
            acc = jnp.where(bucket == b, rb_ref[b, h], acc)
        kv, g = divmod(h, G_A)
        rows = pl.ds(g * WINDOW, WINDOW)
        out_ref[0, kv, rows, :] = jnp.where(band, acc, NEG)
        out_ref[1, kv, rows, :] = jnp.where(band & own, acc, NEG)


def _bias_table(rel_bias):
    i = np.arange(WINDOW)[:, None]
    j = np.arange(2 * WINDOW)[None, :]
    dist = i + WINDOW - j
    band = ((dist >= 0) & (dist < WINDOW)).astype(np.int32)
    return pl.pallas_call(
        _bias_table_kernel,
        out_shape=jax.ShapeDtypeStruct((2, KV_A, G_A * WINDOW, 2 * WINDOW), jnp.float32),
        in_specs=[pl.BlockSpec(memory_space=pltpu.SMEM),
                  pl.BlockSpec(memory_space=pltpu.VMEM),
                  pl.BlockSpec(memory_space=pltpu.VMEM)],
        out_specs=pl.BlockSpec(memory_space=pltpu.VMEM),
        name="rel_bias_table",
    )(rel_bias.astype(jnp.float32), jnp.asarray(_t5_bucket(dist)), jnp.asarray(band))


def _prompt_layer_kernel(x_ref, ng_ref, win_ref, wout_ref, bias_ref, sink_ref, sw_ref, bsf_ref, lng_ref,
                         lnb_ref, wup_ref, bup_ref, gn_ref, fg_ref,
                         xo_ref, ko_ref, vo_ref, so_ref,
                         proj_ref, y_ref, kprev_ref, vprev_ref, st_ref, wm_ref, *, tb, final):
    b_id = pl.program_id(0)
    i_id = pl.program_id(1)
    n_i = pl.num_programs(1)

    @pl.when((b_id == 0) & (i_id == 0))
    def _():
        r = lax.broadcasted_iota(jnp.int32, (CHUNK_B, CHUNK_B), 0)
        c = lax.broadcasted_iota(jnp.int32, (CHUNK_B, CHUNK_B), 1)
        for h in range(H_B):
            wm_ref[h] = jnp.where(c <= r, sw_ref[h], 0.0).astype(jnp.bfloat16)

    @pl.when(i_id == 0)
    def _():
        kprev_ref[...] = jnp.zeros_like(kprev_ref)
        vprev_ref[...] = jnp.zeros_like(vprev_ref)
        st_ref[...] = jnp.zeros_like(st_ref)

    x = x_ref[...]
    h = _rmsnorm(x, ng_ref[...])
    proj_ref[...] = _bdot(h, win_ref[...])

    lane = lax.broadcasted_iota(jnp.int32, (WINDOW, LANES), 1)
    low = lane < HALF
    low64 = lax.broadcasted_iota(jnp.int32, (GLA_CHUNK, LANES), 1) < HALF
    ri = lax.broadcasted_iota(jnp.int32, (LANES, LANES), 0)
    ci = lax.broadcasted_iota(jnp.int32, (LANES, LANES), 1)
    bd_tril = ((ri >= GLA_CHUNK) == (ci >= GLA_CHUNK)) & (ci <= ri)
    cum_mat = jnp.where(bd_tril, 1.0, 0.0).astype(jnp.bfloat16)

    def sub_block(s, carry):
        r0 = pl.multiple_of(s * WINDOW, WINDOW)
        rows = pl.ds(r0, WINDOW)

        k_new = proj_ref[rows, O_KA:O_KA + LANES]
        v_new = proj_ref[rows, O_VA:O_VA + LANES]
        ko_ref[...] = k_new
        vo_ref[...] = v_new
        k_newb = k_new.astype(jnp.bfloat16)
        v_newb = v_new.astype(jnp.bfloat16)
        kcat = jnp.concatenate([kprev_ref[...], k_newb], axis=0)
        vcat = jnp.concatenate([vprev_ref[...], v_newb], axis=0)
        first = jnp.where((i_id == 0) & (s == 0), 1, 0)
        o_heads = []
        for kv in range(KV_A):
            qs = []
            for g in range(G_A):
                qp = proj_ref[rows, O_QA + g * LANES:O_QA + (g + 1) * LANES]
                keep = low if kv == 0 else jnp.logical_not(low)
                qs.append((jnp.where(keep, qp, 0.0) * (HD_A ** -0.5)).astype(jnp.bfloat16))
            q4 = jnp.concatenate(qs, axis=0)
            sc = _bdot_nt(q4, kcat) + bias_ref[first, kv]
            ps, invs = [], []
            for g in range(G_A):
                p, inv = _softmax_sink(sc[g * WINDOW:(g + 1) * WINDOW], sink_ref[kv * G_A + g])
                ps.append(p.astype(jnp.bfloat16))
                invs.append(inv)
            o4 = _bdot(jnp.concatenate(ps, axis=0), vcat)
            o_heads.append([o4[g * WINDOW:(g + 1) * WINDOW] * invs[g] for g in range(G_A)])
        for g in range(G_A):
            ya = jnp.where(low, o_heads[0][g], o_heads[1][g])
            gate = proj_ref[rows, O_GA + g * LANES:O_GA + (g + 1) * LANES]
            y_ref[rows, Y_A + g * LANES:Y_A + (g + 1) * LANES] = (ya * _silu(gate)).astype(jnp.bfloat16)
        kprev_ref[...] = k_newb
        vprev_ref[...] = v_newb

        vn = _layernorm(proj_ref[rows, O_VB:O_VB + D_B], lng_ref[...], lnb_ref[...]).astype(jnp.bfloat16)
        for j in range(H_B // 2):
            vp = vn[:, j * LANES:(j + 1) * LANES]
            m0 = jnp.dot(wm_ref[2 * j], vp, preferred_element_type=jnp.float32)
            m1 = jnp.dot(wm_ref[2 * j + 1], vp, preferred_element_type=jnp.float32)
            mix = jnp.where(low, m0, m1) + bsf_ref[:, j * LANES:(j + 1) * LANES]
            u = proj_ref[rows, O_UB + j * LANES:O_UB + (j + 1) * LANES]
            gate = proj_ref[rows, O_GB + j * LANES:O_GB + (j + 1) * LANES]
            y_ref[rows, Y_B + j * LANES:Y_B + (j + 1) * LANES] = (u * mix * _silu(gate)).astype(jnp.bfloat16)

        z = _bdot(proj_ref[rows, O_LR:O_LR + LANES], wup_ref[...]) + bup_ref[...]
        la = _log_sigmoid(z) * (1.0 / GLA_TAU)
        hi = la.astype(jnp.bfloat16)
        r1 = la - hi.astype(jnp.float32)
        mid = r1.astype(jnp.bfloat16)
        lo = (r1 - mid.astype(jnp.float32)).astype(jnp.bfloat16)
        c3 = jnp.dot(cum_mat, jnp.concatenate([hi, mid, lo], axis=1), preferred_element_type=jnp.float32)
        bc = (c3[:, 0:DK_TOT] + c3[:, DK_TOT:2 * DK_TOT]) + c3[:, 2 * DK_TOT:3 * DK_TOT]
        for c in range(WINDOW // GLA_CHUNK):
            rc = pl.ds(pl.multiple_of(r0 + c * GLA_CHUNK, GLA_CHUNK), GLA_CHUNK)
            for p in range(H_C // 2):
                bp = bc[c * GLA_CHUNK:(c + 1) * GLA_CHUNK, p * LANES:(p + 1) * LANES]
                qp = proj_ref[rc, O_QC + p * LANES:O_QC + (p + 1) * LANES]
                kp = proj_ref[rc, O_KC + p * LANES:O_KC + (p + 1) * LANES]
                b_last = bp[GLA_CHUNK - 1:GLA_CHUNK, :]
                qd = qp * (DK_C ** -0.5) * jnp.exp(bp)
                ki = (kp * jnp.exp(-bp)).astype(jnp.bfloat16)
                kd = kp * jnp.exp(b_last - bp)
                q_st = jnp.concatenate([jnp.where(low64, qd, 0.0), jnp.where(low64, 0.0, qd)],
                                       axis=0).astype(jnp.bfloat16)
                st = st_ref[p]
                rhs = jnp.concatenate([ki, ki, st.astype(jnp.bfloat16)], axis=0)
                res = _bdot_nt(q_st, rhs)
                att = jnp.where(bd_tril, res[:, 0:LANES], 0.0).astype(jnp.bfloat16)
                v0 = proj_ref[rc, O_VC + (2 * p) * DV_C:O_VC + (2 * p + 1) * DV_C]
                v1 = proj_ref[rc, O_VC + (2 * p + 1) * DV_C:O_VC + (2 * p + 2) * DV_C]
                v_st = jnp.concatenate([v0, v1], axis=0)
                o = jnp.dot(att, v_st.astype(jnp.bfloat16), preferred_element_type=jnp.float32) + res[:, LANES:]
                kd_st = jnp.concatenate([jnp.where(low64, kd, 0.0), jnp.where(low64, 0.0, kd)],
                                        axis=0).astype(jnp.bfloat16)
                d_st = jnp.dot(v_st.T.astype(jnp.bfloat16), kd_st, preferred_element_type=jnp.float32)
                st_ref[p] = st * jnp.exp(b_last) + d_st
                for hh in range(2):
                    hd = 2 * p + hh
                    oh = o[hh * GLA_CHUNK:(hh + 1) * GLA_CHUNK]
                    on = _rmsnorm(oh, gn_ref[:, hd * DV_C:(hd + 1) * DV_C])
                    gate = proj_ref[rc, O_GC + hd * DV_C:O_GC + (hd + 1) * DV_C]
                    y_ref[rc, Y_C + hd * DV_C:Y_C + (hd + 1) * DV_C] = (on * _silu(gate)).astype(jnp.bfloat16)
        return carry

    lax.fori_loop(0, tb // WINDOW, sub_block, 0)

    x_new = x + jnp.dot(y_ref[...], wout_ref[...], preferred_element_type=jnp.float32)
    if final:
        x_new = _rmsnorm(x_new, fg_ref[...])
    xo_ref[...] = x_new

    @pl.when(i_id == n_i - 1)
    def _():
        for p in range(H_C // 2):
            so_ref[p] = st_ref[p].T


def _const_spec(shape):
    nd = len(shape)
    return pl.BlockSpec(shape, lambda b, i, _nd=nd: (0,) * _nd, pipeline_mode=pl.Buffered(1))


def _prompt_layer(x, ng, win, wout, bias_tab, sink, sw, bsf, lng, lnb, wup, bup, gn, fg, *, final):
    bsz, seq, _ = x.shape
    tb = min(TOKEN_BLOCK, seq)
    assert seq % tb == 0 and tb % WINDOW == 0
    grid = (bsz, seq // tb)
    kern = functools.partial(_prompt_layer_kernel, tb=tb, final=final)
    return pl.pallas_call(
        kern,
        grid=grid,
        in_specs=[
            pl.BlockSpec((None, tb, D_MODEL), lambda b, i: (b, i, 0)),
            _const_spec((1, D_MODEL)),
            _const_spec((D_MODEL, D_IN_PAD)),
            _const_spec((D_MIX, D_MODEL)),
            _const_spec((2, KV_A, G_A * WINDOW, 2 * WINDOW)),
            pl.BlockSpec(memory_space=pltpu.SMEM),
            _const_spec((H_B, CHUNK_B, CHUNK_B)),
            _const_spec((CHUNK_B, D_B)),
            _const_spec((1, D_B)),
            _const_spec((1, D_B)),
            _const_spec((LANES, DK_TOT)),
            _const_spec((1, DK_TOT)),
            _const_spec((1, D_C)),
            _const_spec((1, D_MODEL)),
        ],
        out_specs=[
            pl.BlockSpec((None, tb, D_MODEL), lambda b, i: (b, i, 0)),
            pl.BlockSpec((None, WINDOW, LANES), lambda b, i: (b, 0, 0)),
            pl.BlockSpec((None, WINDOW, LANES), lambda b, i: (b, 0, 0)),
            pl.BlockSpec((None, H_C // 2, LANES, DV_C), lambda b, i: (b, 0, 0, 0)),
        ],
        out_shape=[
            jax.ShapeDtypeStruct((bsz, seq, D_MODEL), jnp.float32),
            jax.ShapeDtypeStruct((bsz, WINDOW, LANES), jnp.float32),
            jax.ShapeDtypeStruct((bsz, WINDOW, LANES), jnp.float32),
            jax.ShapeDtypeStruct((bsz, H_C // 2, LANES, DV_C), jnp.float32),
        ],
        scratch_shapes=[
            pltpu.VMEM((tb, D_IN_PAD), jnp.float32),
            pltpu.VMEM((tb, D_MIX), jnp.bfloat16),
            pltpu.VMEM((WINDOW, LANES), jnp.bfloat16),
            pltpu.VMEM((WINDOW, LANES), jnp.bfloat16),
            pltpu.VMEM((H_C // 2, DV_C, LANES), jnp.float32),
            pltpu.VMEM((H_B, CHUNK_B, CHUNK_B), jnp.bfloat16),
        ],
        compiler_params=pltpu.CompilerParams(
            dimension_semantics=("arbitrary", "arbitrary"),
            vmem_limit_bytes=VMEM_LIMIT_BYTES),
        name="prompt_layer",
    )(x, ng, win, wout, bias_tab, sink, sw, bsf, lng, lnb, wup, bup, gn, fg)


def _sample_layer_kernel(x_ref, ng_ref, win_ref, wout_ref, biass_ref, sinkc_ref, w00_ref, b0_ref, lng_ref,
                         lnb_ref, wup_ref, bup_ref, gn_ref, fg_ref, bk_ref, bv_ref, s0_ref,
                         xo_ref, nk_ref, nv_ref, so_ref, cv_ref,
                         proj_ref, y_ref, qs_ref, os_ref, qd_ref, ea_ref, oc_ref, *, nseq, gs, final):
    i_id = pl.program_id(0)
    n_i = pl.num_programs(0)
    lane = lax.broadcasted_iota(jnp.int32, (nseq, LANES), 1)
    low = lane < HALF

    @pl.when(i_id == 0)
    def _():
        h = _rmsnorm(x_ref[...], ng_ref[...])
        proj_ref[...] = _bdot(h, win_ref[...])
        for g in range(G_A):
            qp = proj_ref[:, O_QA + g * LANES:O_QA + (g + 1) * LANES] * (HD_A ** -0.5)
            for kv in range(KV_A):
                keep = low if kv == 0 else jnp.logical_not(low)
                qs_ref[pl.ds(kv * G_A + g, nseq, stride=H_A), :] = jnp.where(keep, qp, 0.0)
        vn = _layernorm(proj_ref[:, O_VB:O_VB + D_B], lng_ref[...], lnb_ref[...])
        cv_ref[...] = vn
        mix = w00_ref[...] * vn + b0_ref[...]
        y_ref[:, Y_B:Y_B + D_B] = (proj_ref[:, O_UB:O_UB + D_B] * mix) * _silu(proj_ref[:, O_GB:O_GB + D_B])
        z = _bdot(proj_ref[:, O_LR:O_LR + LANES], wup_ref[...]) + bup_ref[...]
        la = _log_sigmoid(z) * (1.0 / GLA_TAU)
        ea = jnp.exp(la)
        qd = proj_ref[:, O_QC:O_QC + DK_TOT] * (DK_C ** -0.5) * ea
        ki = proj_ref[:, O_KC:O_KC + DK_TOT] * jnp.exp(-la)
        qd_ref[...] = qd
        ea_ref[...] = ea
        prod = qd * ki
        lane_c = lax.broadcasted_iota(jnp.int32, prod.shape, 1)
        for hd in range(H_C):
            in_head = (lane_c >= hd * DK_C) & (lane_c < (hd + 1) * DK_C)
            att = jnp.sum(jnp.where(in_head, prod, 0.0), axis=-1, keepdims=True)
            oc_ref[:, hd * DV_C:(hd + 1) * DV_C] = att * proj_ref[:, O_VC + hd * DV_C:O_VC + (hd + 1) * DV_C]

    g0 = pl.multiple_of(i_id * gs, gs)
    grows = pl.ds(g0, gs)

    k_rows = proj_ref[grows, O_KA:O_KA + LANES]
    v_rows = proj_ref[grows, O_VA:O_VA + LANES]
    for j in range(gs):
        nk_ref[j, 0:WINDOW - 1, :] = bk_ref[j, 1:WINDOW, :]
        nk_ref[j, WINDOW - 1:WINDOW, :] = k_rows[j:j + 1]
        nv_ref[j, 0:WINDOW - 1, :] = bv_ref[j, 1:WINDOW, :]
        nv_ref[j, WINDOW - 1:WINDOW, :] = v_rows[j:j + 1]
        r8 = pl.ds(pl.multiple_of((g0 + j) * H_A, H_A), H_A)
        sc = _bdot_nt(qs_ref[r8, :], nk_ref[j]) + biass_ref[...]
        p, inv = _softmax_sink(sc, sinkc_ref[...])
        os_ref[r8, :] = _bdot(p, nv_ref[j]) * inv

    for hd in range(H_C):
        v_h = proj_ref[grows, O_VC + hd * DV_C:O_VC + (hd + 1) * DV_C]
        acc = jnp.zeros((gs, DV_C), jnp.float32)
        for d in range(DK_C):
            c = hd * DK_C + d
            s_rows = s0_ref[pl.ds(c, gs, stride=DK_TOT), :]
            acc = acc + qd_ref[grows, c:c + 1] * s_rows
            so_ref[pl.ds(c, gs, stride=DK_TOT), :] = (ea_ref[grows, c:c + 1] * s_rows
                                                     + proj_ref[grows, O_KC + c:O_KC + c + 1] * v_h)
        oc_ref[grows, hd * DV_C:(hd + 1) * DV_C] += acc

    @pl.when(i_id == n_i - 1)
    def _():
        for g in range(G_A):
            o0 = os_ref[pl.ds(g, nseq, stride=H_A), :]
            o1 = os_ref[pl.ds(G_A + g, nseq, stride=H_A), :]
            gate = proj_ref[:, O_GA + g * LANES:O_GA + (g + 1) * LANES]
            y_ref[:, Y_A + g * LANES:Y_A + (g + 1) * LANES] = jnp.where(low, o0, o1) * _silu(gate)
        for hd in range(H_C):
            on = _rmsnorm(oc_ref[:, hd * DV_C:(hd + 1) * DV_C], gn_ref[:, hd * DV_C:(hd + 1) * DV_C])
            gate = proj_ref[:, O_GC + hd * DV_C:O_GC + (hd + 1) * DV_C]
            y_ref[:, Y_C + hd * DV_C:Y_C + (hd + 1) * DV_C] = on * _silu(gate)
        x_new = x_ref[...] + _bdot(y_ref[...], wout_ref[...])
        if final:
            x_new = _rmsnorm(x_new, fg_ref[...])
        xo_ref[...] = x_new


def _sample_layer(x, ng, win, wout, bias_s, sink_c, w00, b0, lng, lnb, wup, bup, gn, fg, buf_k, buf_v, s0,
                  *, final):
    nseq = x.shape[0]
    gs = min(SEQ_GROUP, nseq)
    assert nseq % gs == 0
    wb = buf_k.shape[1]
    assert wb == WINDOW
    grid = (nseq // gs,)

    def cspec(shape):
        nd = len(shape)
        return pl.BlockSpec(shape, lambda i, _nd=nd: (0,) * _nd, pipeline_mode=pl.Buffered(1))

    kern = functools.partial(_sample_layer_kernel, nseq=nseq, gs=gs, final=final)
    return pl.pallas_call(
        kern,
        grid=grid,
        in_specs=[
            cspec((nseq, D_MODEL)),
            cspec((1, D_MODEL)),
            cspec((D_MODEL, D_IN_PAD)),
            cspec((D_MIX, D_MODEL)),
            cspec((H_A, WINDOW)),
            cspec((H_A, 1)),
            cspec((1, D_B)),
            cspec((1, D_B)),
            cspec((1, D_B)),
            cspec((1, D_B)),
            cspec((LANES, DK_TOT)),
            cspec((1, DK_TOT)),
            cspec((1, D_C)),
            cspec((1, D_MODEL)),
            pl.BlockSpec((gs, WINDOW, LANES), lambda i: (i, 0, 0)),
            pl.BlockSpec((gs, WINDOW, LANES), lambda i: (i, 0, 0)),
            pl.BlockSpec((gs * DK_TOT, DV_C), lambda i: (i, 0)),
        ],
        out_specs=[
            pl.BlockSpec((nseq, D_MODEL), lambda i: (0, 0)),
            pl.BlockSpec((gs, WINDOW, LANES), lambda i: (i, 0, 0)),
            pl.BlockSpec((gs, WINDOW, LANES), lambda i: (i, 0, 0)),
            pl.BlockSpec((gs * DK_TOT, DV_C), lambda i: (i, 0)),
            pl.BlockSpec((nseq, D_B), lambda i: (0, 0)),
        ],
        out_shape=[
            jax.ShapeDtypeStruct((nseq, D_MODEL), jnp.float32),
            jax.ShapeDtypeStruct((nseq, WINDOW, LANES), jnp.float32),
            jax.ShapeDtypeStruct((nseq, WINDOW, LANES), jnp.float32),
            jax.ShapeDtypeStruct((nseq * DK_TOT, DV_C), jnp.float32),
            jax.ShapeDtypeStruct((nseq, D_B), jnp.float32),
        ],
        scratch_shapes=[
            pltpu.VMEM((nseq, D_IN_PAD), jnp.float32),
            pltpu.VMEM((nseq, D_MIX), jnp.float32),
            pltpu.VMEM((nseq * H_A, LANES), jnp.float32),
            pltpu.VMEM((nseq * H_A, LANES), jnp.float32),
            pltpu.VMEM((nseq, DK_TOT), jnp.float32),
            pltpu.VMEM((nseq, DK_TOT), jnp.float32),
            pltpu.VMEM((nseq, D_C), jnp.float32),
        ],
        compiler_params=pltpu.CompilerParams(
            dimension_semantics=("arbitrary",),
            vmem_limit_bytes=VMEM_LIMIT_BYTES),
        name="sample_layer",
    )(x, ng, win, wout, bias_s, sink_c, w00, b0, lng, lnb, wup, bup, gn, fg, buf_k, buf_v, s0)


def _in_column_order():
    cols = np.arange(D_IN)
    head_cols = np.concatenate([np.arange(h * HD_A, (h + 1) * HD_A) for h in HEAD_PERM])
    cols[O_QA:O_QA + D_A] = O_QA + head_cols
    cols[O_GA:O_GA + D_A] = O_GA + head_cols
    return cols, head_cols


def kernel(x_prompt, x_sample, state_swa_k, state_swa_v, state_gla, rel_bias, norm_g, w_in, sinks, spatial_w,
           spatial_b, chunk_ln_g, chunk_ln_b, gla_w_up, gla_b_up, gla_norm_g, w_out, final_norm_g):
    depth = w_in.shape[0]
    nseq = x_sample.shape[0]
    f32 = jnp.float32

    in_cols, head_cols = _in_column_order()
    out_rows = np.arange(D_MIX)
    out_rows[:D_A] = head_cols
    win = jnp.pad(jnp.take(w_in, jnp.asarray(in_cols), axis=2),
                  ((0, 0), (0, 0), (0, D_IN_PAD - D_IN))).astype(jnp.bfloat16)
    wout = jnp.take(w_out, jnp.asarray(out_rows), axis=1).astype(jnp.bfloat16)
    wup = jnp.pad(gla_w_up, ((0, 0), (0, LANES - GLA_RANK), (0, 0))).astype(jnp.bfloat16)
    bsf = jnp.repeat(jnp.swapaxes(spatial_b, 1, 2), DH_B, axis=2).astype(f32)
    w00 = jnp.repeat(spatial_w[:, :, 0, 0], DH_B, axis=1).astype(f32)[:, None, :]
    b0 = jnp.repeat(spatial_b[:, :, 0], DH_B, axis=1).astype(f32)[:, None, :]
    sink_c = sinks.astype(f32)[:, :, None]
    row = lambda a: a.astype(f32)[:, None, :]
    ng, lng, lnb, bup, gn = row(norm_g), row(chunk_ln_g), row(chunk_ln_b), row(gla_b_up), row(gla_norm_g)
    fg = final_norm_g.astype(f32)[None, :]

    bias_tab = _bias_table(rel_bias)
    bias_s = bias_tab[0].reshape(H_A, WINDOW, 2 * WINDOW)[:, WINDOW - 1, WINDOW:]

    xp = x_prompt
    xs = x_sample.reshape(nseq, D_MODEL)
    buf_k = state_swa_k.reshape(depth, nseq, WINDOW, LANES)
    buf_v = state_swa_v.reshape(depth, nseq, WINDOW, LANES)
    s0 = state_gla.reshape(depth, nseq * DK_TOT, DV_C)

    kp_l, vp_l, sp_l, ks_l, vs_l, ss_l, cv_l = [], [], [], [], [], [], []
    for l in range(depth):
        final = l == depth - 1
        xp, kp, vp, sp = _prompt_layer(xp, ng[l], win[l], wout[l], bias_tab, sinks[l].astype(f32),
                                       spatial_w[l], bsf[l], lng[l], lnb[l], wup[l], bup[l], gn[l], fg,
                                       final=final)
        xs, k_s, v_s, s_s, cv = _sample_layer(xs, ng[l], win[l], wout[l], bias_s, sink_c[l], w00[l], b0[l],
                                              lng[l], lnb[l], wup[l], bup[l], gn[l], fg,
                                              buf_k[l], buf_v[l], s0[l], final=final)
        kp_l.append(kp); vp_l.append(vp); sp_l.append(sp)
        ks_l.append(k_s); vs_l.append(v_s); ss_l.append(s_s); cv_l.append(cv)

    bsz = x_prompt.shape[0]
    return (xp,
            xs.reshape(nseq, 1, D_MODEL),
            jnp.stack(kp_l).reshape(depth, bsz, WINDOW, KV_A, HD_A),
            jnp.stack(vp_l).reshape(depth, bsz, WINDOW, KV_A, HD_A),
            jnp.stack(sp_l).reshape(depth, bsz, H_C, DK_C, DV_C),
            jnp.stack(ks_l).reshape(depth, nseq, WINDOW, KV_A, HD_A),
            jnp.stack(vs_l).reshape(depth, nseq, WINDOW, KV_A, HD_A),
            jnp.stack(ss_l).reshape(depth, nseq, H_C, DK_C, DV_C),
            jnp.stack(cv_l).reshape(depth, nseq, 1, D_B))
```

```python
import functools

import numpy as np
import jax
import jax.numpy as jnp
from jax import lax
from jax.experimental import pallas as pl
from jax.experimental.pallas import tpu as pltpu

D_MODEL = 1024
D_A, HD_A, H_A, KV_A, G_A = 512, 64, 8, 2, 4
WINDOW, N_BUCKETS, MAX_DIST = 128, 32, 128
D_B, H_B, DH_B, CHUNK_B = 512, 8, 64, 128
D_C, H_C, DK_TOT, DK_C, DV_C = 512, 4, 256, 64, 128
GLA_RANK, GLA_TAU, GLA_CHUNK = 16, 16.0, 64
D_MIX = D_A + D_B + D_C
EPS = 1e-6
NEG = -1e30
SPLITS = [D_A, KV_A * HD_A, KV_A * HD_A, D_A, D_B, D_B, D_B, DK_TOT, DK_TOT, D_C, D_C, GLA_RANK]
D_IN = sum(SPLITS)

LANES = 128
HALF = LANES // 2
D_IN_PAD = ((D_IN + LANES - 1) // LANES) * LANES
VMEM_LIMIT_BYTES = 56 * 1024 * 1024

O_QA, O_KA, O_VA, O_GA = 0, 512, 640, 768
O_UB, O_VB, O_GB = 1280, 1792, 2304
O_QC, O_KC, O_VC, O_GC, O_LR = 2816, 3072, 3328, 3840, 4352
Y_A, Y_B, Y_C = 0, D_A, D_A + D_B

HEAD_PERM = [0, 4, 1, 5, 2, 6, 3, 7]

TOKEN_BLOCK = 256
SEQ_GROUP = 16

_NT = (((1,), (1,)), ((), ()))


def _t5_bucket(dist):
    n = np.maximum(dist, 0)
    max_exact = N_BUCKETS // 2
    large = max_exact + (np.log(np.maximum(n, 1) / max_exact) / np.log(MAX_DIST / max_exact)
                         * (N_BUCKETS - max_exact)).astype(np.int32)
    large = np.minimum(large, N_BUCKETS - 1)
    return np.where(n < max_exact, n, large).astype(np.int32)


def _silu(x):
    return x * (1.0 / (1.0 + jnp.exp(-x)))


def _log_sigmoid(x):
    return jnp.minimum(x, 0.0) - jnp.log1p(jnp.exp(-jnp.abs(x)))


def _bdot(a, b):
    return jnp.dot(a.astype(jnp.bfloat16), b.astype(jnp.bfloat16), preferred_element_type=jnp.float32)


def _bdot_nt(a, b):
    return lax.dot_general(a.astype(jnp.bfloat16), b.astype(jnp.bfloat16), _NT,
                           preferred_element_type=jnp.float32)


def _rmsnorm(x, g):
    return x * lax.rsqrt(jnp.mean(x * x, axis=-1, keepdims=True) + EPS) * g


def _layernorm(v, g, b):
    mu = jnp.mean(v, axis=-1, keepdims=True)
    xc = v - mu
    var = jnp.mean(xc * xc, axis=-1, keepdims=True)
    return xc * lax.rsqrt(var + EPS) * g + b


def _softmax_sink(s, sink):
    m = jnp.maximum(jnp.max(s, axis=-1, keepdims=True), sink)
    p = jnp.exp(s - m)
    den = jnp.sum(p, axis=-1, keepdims=True) + jnp.exp(sink - m)
    return p, 1.0 / den


def _bias_table_kernel(rb_ref, bucket_ref, band_ref, out_ref):
    bucket = bucket_ref[...]
    band = band_ref[...] > 0
    own = lax.broadcasted_iota(jnp.int32, bucket.shape, 1) >= WINDOW
    for h in range(H_A):
        acc = jnp.zeros(bucket.shape, jnp.float32)
        for b in range(N_BUCKETS):
            acc = jnp.where(bucket == b, rb_ref[b, h], acc)
        kv, g = divmod(h, G_A)
        rows = pl.ds(g * WINDOW, WINDOW)
        out_ref[0, kv, rows, :] = jnp.where(band, acc, NEG)
        out_ref[1, kv, rows, :] = jnp.where(band & own, acc, NEG)


def _bias_table(rel_bias):
    i = np.arange(WINDOW)[:, None]
    j = np.arange(2 * WINDOW)[None, :]
    dist = i + WINDOW - j
    band = ((dist >= 0) & (dist < WINDOW)).astype(np.int32)
    return pl.pallas_call(
        _bias_table_kernel,
        out_shape=jax.ShapeDtypeStruct((2, KV_A, G_A * WINDOW, 2 * WINDOW), jnp.float32),
        in_specs=[pl.BlockSpec(memory_space=pltpu.SMEM),
                  pl.BlockSpec(memory_space=pltpu.VMEM),
                  pl.BlockSpec(memory_space=pltpu.VMEM)],
        out_specs=pl.BlockSpec(memory_space=pltpu.VMEM),
        name="rel_bias_table",
    )(rel_bias.astype(jnp.float32), jnp.asarray(_t5_bucket(dist)), jnp.asarray(band))


def _prompt_layer_kernel(x_ref, ng_ref, win_ref, wout_ref, bias_ref, sink_ref, sw_ref, bsf_ref, lng_ref,
                         lnb_ref, wup_ref, bup_ref, gn_ref, fg_ref,
                         xo_ref, ko_ref, vo_ref, so_ref,
                         proj_ref, y_ref, kprev_ref, vprev_ref, st_ref, wm_ref, *, tb, layer, final):
    b_id = pl.program_id(0)
    i_id = pl.program_id(1)
    n_i = pl.num_programs(1)

    @pl.when((b_id == 0) & (i_id == 0))
    def _():
        r = lax.broadcasted_iota(jnp.int32, (CHUNK_B, CHUNK_B), 0)
        c = lax.broadcasted_iota(jnp.int32, (CHUNK_B, CHUNK_B), 1)
        for h in range(H_B):
            wm_ref[h] = jnp.where(c <= r, sw_ref[h], 0.0).astype(jnp.bfloat16)

    @pl.when(i_id == 0)
    def _():
        kprev_ref[...] = jnp.zeros_like(kprev_ref)
        vprev_ref[...] = jnp.zeros_like(vprev_ref)
        st_ref[...] = jnp.zeros_like(st_ref)

    x = x_ref[...]
    h = _rmsnorm(x, ng_ref[...])
    proj_ref[...] = _bdot(h, win_ref[...])

    lane = lax.broadcasted_iota(jnp.int32, (WINDOW, LANES), 1)
    low = lane < HALF
    low64 = lax.broadcasted_iota(jnp.int32, (GLA_CHUNK, LANES), 1) < HALF
    ri = lax.broadcasted_iota(jnp.int32, (LANES, LANES), 0)
    ci = lax.broadcasted_iota(jnp.int32, (LANES, LANES), 1)
    bd_tril = ((ri >= GLA_CHUNK) == (ci >= GLA_CHUNK)) & (ci <= ri)
    cum_mat = jnp.where(bd_tril, 1.0, 0.0).astype(jnp.bfloat16)

    def sub_block(s, carry):
        r0 = pl.multiple_of(s * WINDOW, WINDOW)
        rows = pl.ds(r0, WINDOW)

        ks = proj_ref[rows, O_KA:O_KA + LANES] * (HD_A ** -0.5)
        k_new = [jnp.where(low, ks, 0.0).astype(jnp.bfloat16), jnp.where(low, 0.0, ks).astype(jnp.bfloat16)]
        v_new = proj_ref[rows, O_VA:O_VA + LANES].astype(jnp.bfloat16)
        vcat = jnp.concatenate([vprev_ref[...], v_new], axis=0)
        first = jnp.where((i_id == 0) & (s == 0), 1, 0)
        q4 = jnp.concatenate([proj_ref[rows, O_QA + g * LANES:O_QA + (g + 1) * LANES].astype(jnp.bfloat16)
                              for g in range(G_A)], axis=0)
        o_heads = []
        for kv in range(KV_A):
            kcat = jnp.concatenate([kprev_ref[kv], k_new[kv]], axis=0)
            sc = lax.dot_general(q4, kcat, _NT, preferred_element_type=jnp.float32) + bias_ref[first, kv]
            ps, invs = [], []
            for g in range(G_A):
                p, inv = _softmax_sink(sc[g * WINDOW:(g + 1) * WINDOW], sink_ref[layer, kv * G_A + g])
                ps.append(p.astype(jnp.bfloat16))
                invs.append(inv)
            o4 = jnp.dot(jnp.concatenate(ps, axis=0), vcat, preferred_element_type=jnp.float32)
            o_heads.append([o4[g * WINDOW:(g + 1) * WINDOW] * invs[g] for g in range(G_A)])
            kprev_ref[kv] = k_new[kv]
        vprev_ref[...] = v_new
        for g in range(G_A):
            ya = jnp.where(low, o_heads[0][g], o_heads[1][g])
            gate = proj_ref[rows, O_GA + g * LANES:O_GA + (g + 1) * LANES]
            y_ref[rows, Y_A + g * LANES:Y_A + (g + 1) * LANES] = (ya * _silu(gate)).astype(jnp.bfloat16)

        vn = _layernorm(proj_ref[rows, O_VB:O_VB + D_B], lng_ref[...], lnb_ref[...]).astype(jnp.bfloat16)
        for j in range(H_B // 2):
            vp = vn[:, j * LANES:(j + 1) * LANES]
            m0 = jnp.dot(wm_ref[2 * j], vp, preferred_element_type=jnp.float32)
            m1 = jnp.dot(wm_ref[2 * j + 1], vp, preferred_element_type=jnp.float32)
            mix = jnp.where(low, m0, m1) + bsf_ref[:, j * LANES:(j + 1) * LANES]
            u = proj_ref[rows, O_UB + j * LANES:O_UB + (j + 1) * LANES]
            gate = proj_ref[rows, O_GB + j * LANES:O_GB + (j + 1) * LANES]
            y_ref[rows, Y_B + j * LANES:Y_B + (j + 1) * LANES] = (u * mix * _silu(gate)).astype(jnp.bfloat16)

        z = _bdot(proj_ref[rows, O_LR:O_LR + LANES], wup_ref[...]) + bup_ref[...]
        la = _log_sigmoid(z) * (1.0 / GLA_TAU)
        hi = la.astype(jnp.bfloat16)
        r1 = la - hi.astype(jnp.float32)
        mid = r1.astype(jnp.bfloat16)
        lo = (r1 - mid.astype(jnp.float32)).astype(jnp.bfloat16)
        c3 = jnp.dot(cum_mat, jnp.concatenate([hi, mid, lo], axis=1), preferred_element_type=jnp.float32)
        bc = (c3[:, 0:DK_TOT] + c3[:, DK_TOT:2 * DK_TOT]) + c3[:, 2 * DK_TOT:3 * DK_TOT]
        for c in range(WINDOW // GLA_CHUNK):
            rc = pl.ds(pl.multiple_of(r0 + c * GLA_CHUNK, GLA_CHUNK), GLA_CHUNK)
            for p in range(H_C // 2):
                bp = bc[c * GLA_CHUNK:(c + 1) * GLA_CHUNK, p * LANES:(p + 1) * LANES]
                qp = proj_ref[rc, O_QC + p * LANES:O_QC + (p + 1) * LANES]
                kp = proj_ref[rc, O_KC + p * LANES:O_KC + (p + 1) * LANES]
                b_last = bp[GLA_CHUNK - 1:GLA_CHUNK, :]
                qd = qp * (DK_C ** -0.5) * jnp.exp(bp)
                ki = (kp * jnp.exp(-bp)).astype(jnp.bfloat16)
                kd = kp * jnp.exp(b_last - bp)
                q_st = jnp.concatenate([jnp.where(low64, qd, 0.0), jnp.where(low64, 0.0, qd)],
                                       axis=0).astype(jnp.bfloat16)
                st = st_ref[p]
                rhs = jnp.concatenate([ki, ki, st.astype(jnp.bfloat16)], axis=0)
                res = _bdot_nt(q_st, rhs)
                att = jnp.where(bd_tril, res[:, 0:LANES], 0.0).astype(jnp.bfloat16)
                v0 = proj_ref[rc, O_VC + (2 * p) * DV_C:O_VC + (2 * p + 1) * DV_C]
                v1 = proj_ref[rc, O_VC + (2 * p + 1) * DV_C:O_VC + (2 * p + 2) * DV_C]
                v_st = jnp.concatenate([v0, v1], axis=0)
                o = jnp.dot(att, v_st.astype(jnp.bfloat16), preferred_element_type=jnp.float32) + res[:, LANES:]
                kd_st = jnp.concatenate([jnp.where(low64, kd, 0.0), jnp.where(low64, 0.0, kd)],
                                        axis=0).astype(jnp.bfloat16)
                d_st = jnp.dot(v_st.T.astype(jnp.bfloat16), kd_st, preferred_element_type=jnp.float32)
                st_ref[p] = st * jnp.exp(b_last) + d_st
                for hh in range(2):
                    hd = 2 * p + hh
                    oh = o[hh * GLA_CHUNK:(hh + 1) * GLA_CHUNK]
                    on = _rmsnorm(oh, gn_ref[:, hd * DV_C:(hd + 1) * DV_C])
                    gate = proj_ref[rc, O_GC + hd * DV_C:O_GC + (hd + 1) * DV_C]
                    y_ref[rc, Y_C + hd * DV_C:Y_C + (hd + 1) * DV_C] = (on * _silu(gate)).astype(jnp.bfloat16)
        return carry

    lax.fori_loop(0, tb // WINDOW, sub_block, 0)

    x_new = x + jnp.dot(y_ref[...], wout_ref[...], preferred_element_type=jnp.float32)
    if final:
        x_new = _rmsnorm(x_new, fg_ref[...])
    xo_ref[...] = x_new

    @pl.when(i_id == n_i - 1)
    def _():
        ko_ref[...] = proj_ref[tb - WINDOW:tb, O_KA:O_KA + LANES].T
        vo_ref[...] = proj_ref[tb - WINDOW:tb, O_VA:O_VA + LANES].T
        for p in range(H_C // 2):
            so_ref[p] = st_ref[p].T


def _prompt_layer(x, ng, win, wout, bias_tab, sinks, sw, bsf, lng, lnb, wup, bup, gn, fg, *, layer, final):
    bsz, seq, _ = x.shape
    tb = min(TOKEN_BLOCK, seq)
    assert seq % tb == 0 and tb % WINDOW == 0
    grid = (bsz, seq // tb)

    def const_spec(shape):
        nd = len(shape)
        return pl.BlockSpec(shape, lambda b, i: (0,) * nd, pipeline_mode=pl.Buffered(1))

    def layer_spec(shape):
        nd = len(shape)
        return pl.BlockSpec((None,) + shape, lambda b, i: (layer,) + (0,) * nd, pipeline_mode=pl.Buffered(1))

    kern = functools.partial(_prompt_layer_kernel, tb=tb, layer=layer, final=final)
    return pl.pallas_call(
        kern,
        grid=grid,
        in_specs=[
            pl.BlockSpec((None, tb, D_MODEL), lambda b, i: (b, i, 0)),
            layer_spec((1, D_MODEL)),
            layer_spec((D_MODEL, D_IN_PAD)),
            layer_spec((D_MIX, D_MODEL)),
            const_spec((2, KV_A, G_A * WINDOW, 2 * WINDOW)),
            pl.BlockSpec(memory_space=pltpu.SMEM),
            layer_spec((H_B, CHUNK_B, CHUNK_B)),
            layer_spec((CHUNK_B, D_B)),
            layer_spec((1, D_B)),
            layer_spec((1, D_B)),
            layer_spec((LANES, DK_TOT)),
            layer_spec((1, DK_TOT)),
            layer_spec((1, D_C)),
            const_spec((1, D_MODEL)),
        ],
        out_specs=[
            pl.BlockSpec((None, tb, D_MODEL), lambda b, i: (b, i, 0)),
            pl.BlockSpec((None, LANES, WINDOW), lambda b, i: (b, 0, 0)),
            pl.BlockSpec((None, LANES, WINDOW), lambda b, i: (b, 0, 0)),
            pl.BlockSpec((None, H_C // 2, LANES, DV_C), lambda b, i: (b, 0, 0, 0)),
        ],
        out_shape=[
            jax.ShapeDtypeStruct((bsz, seq, D_MODEL), jnp.float32),
            jax.ShapeDtypeStruct((bsz, LANES, WINDOW), jnp.float32),
            jax.ShapeDtypeStruct((bsz, LANES, WINDOW), jnp.float32),
            jax.ShapeDtypeStruct((bsz, H_C // 2, LANES, DV_C), jnp.float32),
        ],
        scratch_shapes=[
            pltpu.VMEM((tb, D_IN_PAD), jnp.float32),
            pltpu.VMEM((tb, D_MIX), jnp.bfloat16),
            pltpu.VMEM((KV_A, WINDOW, LANES), jnp.bfloat16),
            pltpu.VMEM((WINDOW, LANES), jnp.bfloat16),
            pltpu.VMEM((H_C // 2, DV_C, LANES), jnp.float32),
            pltpu.VMEM((H_B, CHUNK_B, CHUNK_B), jnp.bfloat16),
        ],
        compiler_params=pltpu.CompilerParams(
            dimension_semantics=("arbitrary", "arbitrary"),
            vmem_limit_bytes=VMEM_LIMIT_BYTES),
        name="prompt_layer",
    )(x, ng, win, wout, bias_tab, sinks, sw, bsf, lng, lnb, wup, bup, gn, fg)


def _sample_kernel(x_ref, ng_ref, win_ref, wout_ref, biass_ref, sinkc_ref, w00_ref, b0_ref, lng_ref,
                   lnb_ref, wup_ref, bup_ref, gn_ref, fg_ref, bk_ref, bv_ref, s0_ref,
                   xo_ref, nk_ref, nv_ref, so_ref, cv_ref,
                   xcur_ref, proj_ref, y_ref, qs_ref, os_ref, qd_ref, ea_ref, oc_ref, *, nseq, gs):
    l_id = pl.program_id(0)
    i_id = pl.program_id(1)
    n_l = pl.num_programs(0)
    n_i = pl.num_programs(1)
    lane = lax.broadcasted_iota(jnp.int32, (nseq, LANES), 1)
    low = lane < HALF

    @pl.when((l_id == 0) & (i_id == 0))
    def _():
        xcur_ref[...] = x_ref[...]

    @pl.when(i_id == 0)
    def _():
        h = _rmsnorm(xcur_ref[...], ng_ref[...])
        proj_ref[...] = _bdot(h, win_ref[...])
        for g in range(G_A):
            qp = proj_ref[:, O_QA + g * LANES:O_QA + (g + 1) * LANES] * (HD_A ** -0.5)
            for kv in range(KV_A):
                keep = low if kv == 0 else jnp.logical_not(low)
                qs_ref[pl.ds(kv * G_A + g, nseq, stride=H_A), :] = jnp.where(keep, qp, 0.0)
        vn = _layernorm(proj_ref[:, O_VB:O_VB + D_B], lng_ref[...], lnb_ref[...])
        cv_ref[...] = vn
        mix = w00_ref[...] * vn + b0_ref[...]
        y_ref[:, Y_B:Y_B + D_B] = (proj_ref[:, O_UB:O_UB + D_B] * mix) * _silu(proj_ref[:, O_GB:O_GB + D_B])
        z = _bdot(proj_ref[:, O_LR:O_LR + LANES], wup_ref[...]) + bup_ref[...]
        la = _log_sigmoid(z) * (1.0 / GLA_TAU)
        ea = jnp.exp(la)
        qd = proj_ref[:, O_QC:O_QC + DK_TOT] * (DK_C ** -0.5) * ea
        ki = proj_ref[:, O_KC:O_KC + DK_TOT] * jnp.exp(-la)
        qd_ref[...] = qd
        ea_ref[...] = ea
        prod = qd * ki
        lane_c = lax.broadcasted_iota(jnp.int32, prod.shape, 1)
        for hd in range(H_C):
            in_head = (lane_c >= hd * DK_C) & (lane_c < (hd + 1) * DK_C)
            att = jnp.sum(jnp.where(in_head, prod, 0.0), axis=-1, keepdims=True)
            oc_ref[:, hd * DV_C:(hd + 1) * DV_C] = att * proj_ref[:, O_VC + hd * DV_C:O_VC + (hd + 1) * DV_C]

    g0 = pl.multiple_of(i_id * gs, gs)
    grows = pl.ds(g0, gs)

    pad_rows = jnp.zeros((LANES - gs, LANES), jnp.float32)
    k_cols = jnp.concatenate([proj_ref[grows, O_KA:O_KA + LANES], pad_rows], axis=0).T
    v_cols = jnp.concatenate([proj_ref[grows, O_VA:O_VA + LANES], pad_rows], axis=0).T
    newest = lax.broadcasted_iota(jnp.int32, (LANES, WINDOW), 1) == WINDOW - 1
    for j in range(gs):
        kt = jnp.where(newest, k_cols[:, j:j + 1], pltpu.roll(bk_ref[j], WINDOW - 1, 1))
        vt = jnp.where(newest, v_cols[:, j:j + 1], pltpu.roll(bv_ref[j], WINDOW - 1, 1))
        nk_ref[j] = kt
        nv_ref[j] = vt
        r8 = pl.ds(pl.multiple_of((g0 + j) * H_A, H_A), H_A)
        sc = _bdot(qs_ref[r8, :], kt) + biass_ref[...]
        p, inv = _softmax_sink(sc, sinkc_ref[...])
        os_ref[r8, :] = _bdot_nt(p, vt) * inv

    for hd in range(H_C):
        v_h = proj_ref[grows, O_VC + hd * DV_C:O_VC + (hd + 1) * DV_C]
        acc = jnp.zeros((gs, DV_C), jnp.float32)
        for d in range(DK_C):
            c = hd * DK_C + d
            s_rows = s0_ref[pl.ds(c, gs, stride=DK_TOT), :]
            acc = acc + qd_ref[grows, c:c + 1] * s_rows
            so_ref[pl.ds(c, gs, stride=DK_TOT), :] = (ea_ref[grows, c:c + 1] * s_rows
                                                     + proj_ref[grows, O_KC + c:O_KC + c + 1] * v_h)
        oc_ref[grows, hd * DV_C:(hd + 1) * DV_C] += acc

    @pl.when(i_id == n_i - 1)
    def _():
        for g in range(G_A):
            o0 = os_ref[pl.ds(g, nseq, stride=H_A), :]
            o1 = os_ref[pl.ds(G_A + g, nseq, stride=H_A), :]
            gate = proj_ref[:, O_GA + g * LANES:O_GA + (g + 1) * LANES]
            y_ref[:, Y_A + g * LANES:Y_A + (g + 1) * LANES] = jnp.where(low, o0, o1) * _silu(gate)
        for hd in range(H_C):
            on = _rmsnorm(oc_ref[:, hd * DV_C:(hd + 1) * DV_C], gn_ref[:, hd * DV_C:(hd + 1) * DV_C])
            gate = proj_ref[:, O_GC + hd * DV_C:O_GC + (hd + 1) * DV_C]
            y_ref[:, Y_C + hd * DV_C:Y_C + (hd + 1) * DV_C] = on * _silu(gate)
        x_new = xcur_ref[...] + _bdot(y_ref[...], wout_ref[...])
        xcur_ref[...] = x_new

        @pl.when(l_id == n_l - 1)
        def _():
            xo_ref[...] = _rmsnorm(x_new, fg_ref[...])


def _sample_path(x, ng, win, wout, bias_s, sink_c, w00, b0, lng, lnb, wup, bup, gn, fg, buf_kt, buf_vt, s0):
    depth = win.shape[0]
    nseq = x.shape[0]
    gs = min(SEQ_GROUP, nseq)
    assert nseq % gs == 0 and gs % 8 == 0
    grid = (depth, nseq // gs)

    def const_spec(shape):
        nd = len(shape)
        return pl.BlockSpec(shape, lambda l, i: (0,) * nd)

    def layer_spec(shape):
        nd = len(shape)
        return pl.BlockSpec((None,) + shape, lambda l, i: (l,) + (0,) * nd)

    state_spec = pl.BlockSpec((None, gs, LANES, WINDOW), lambda l, i: (l, i, 0, 0))
    gla_spec = pl.BlockSpec((None, gs * DK_TOT, DV_C), lambda l, i: (l, i, 0))
    kern = functools.partial(_sample_kernel, nseq=nseq, gs=gs)
    return pl.pallas_call(
        kern,
        grid=grid,
        in_specs=[
            const_spec((nseq, D_MODEL)),
            layer_spec((1, D_MODEL)),
            layer_spec((D_MODEL, D_IN_PAD)),
            layer_spec((D_MIX, D_MODEL)),
            const_spec((H_A, WINDOW)),
            layer_spec((H_A, 1)),
            layer_spec((1, D_B)),
            layer_spec((1, D_B)),
            layer_spec((1, D_B)),
            layer_spec((1, D_B)),
            layer_spec((LANES, DK_TOT)),
            layer_spec((1, DK_TOT)),
            layer_spec((1, D_C)),
            const_spec((1, D_MODEL)),
            state_spec, state_spec, gla_spec,
        ],
        out_specs=[
            const_spec((nseq, D_MODEL)),
            state_spec, state_spec, gla_spec,
            layer_spec((nseq, D_B)),
        ],
        out_shape=[
            jax.ShapeDtypeStruct((nseq, D_MODEL), jnp.float32),
            jax.ShapeDtypeStruct((depth, nseq, LANES, WINDOW), jnp.float32),
            jax.ShapeDtypeStruct((depth, nseq, LANES, WINDOW), jnp.float32),
            jax.ShapeDtypeStruct((depth, nseq * DK_TOT, DV_C), jnp.float32),
            jax.ShapeDtypeStruct((depth, nseq, D_B), jnp.float32),
        ],
        scratch_shapes=[
            pltpu.VMEM((nseq, D_MODEL), jnp.float32),
            pltpu.VMEM((nseq, D_IN_PAD), jnp.float32),
            pltpu.VMEM((nseq, D_MIX), jnp.float32),
            pltpu.VMEM((nseq * H_A, LANES), jnp.float32),
            pltpu.VMEM((nseq * H_A, LANES), jnp.float32),
            pltpu.VMEM((nseq, DK_TOT), jnp.float32),
            pltpu.VMEM((nseq, DK_TOT), jnp.float32),
            pltpu.VMEM((nseq, D_C), jnp.float32),
        ],
        compiler_params=pltpu.CompilerParams(
            dimension_semantics=("arbitrary", "arbitrary"),
            vmem_limit_bytes=VMEM_LIMIT_BYTES),
        name="sample_path",
    )(x, ng, win, wout, bias_s, sink_c, w00, b0, lng, lnb, wup, bup, gn, fg, buf_kt, buf_vt, s0)


def _window_minor(state):
    depth, n = state.shape[:2]
    return jnp.transpose(state, (0, 1, 3, 4, 2)).reshape(depth, n, LANES, WINDOW)


def _window_major(state_t):
    depth, n = state_t.shape[:2]
    return jnp.transpose(state_t.reshape(depth, n, KV_A, HD_A, WINDOW), (0, 1, 4, 2, 3))


def kernel(x_prompt, x_sample, state_swa_k, state_swa_v, state_gla, rel_bias, norm_g, w_in, sinks, spatial_w,
           spatial_b, chunk_ln_g, chunk_ln_b, gla_w_up, gla_b_up, gla_norm_g, w_out, final_norm_g):
    depth = w_in.shape[0]
    nseq = x_sample.shape[0]
    bsz = x_prompt.shape[0]
    f32 = jnp.float32

    heads = lambda w, off, axis: [lax.slice_in_dim(w, off + h * HD_A, off + (h + 1) * HD_A, axis=axis)
                                  for h in HEAD_PERM]
    win = jnp.concatenate(
        heads(w_in, O_QA, 2) + [w_in[:, :, O_KA:O_GA]] + heads(w_in, O_GA, 2) + [w_in[:, :, O_UB:]]
        + [jnp.zeros((depth, D_MODEL, D_IN_PAD - D_IN), w_in.dtype)], axis=2).astype(jnp.bfloat16)
    wout = jnp.concatenate(heads(w_out, Y_A, 1) + [w_out[:, Y_B:, :]], axis=1).astype(jnp.bfloat16)
    wup = jnp.pad(gla_w_up, ((0, 0), (0, LANES - GLA_RANK), (0, 0))).astype(jnp.bfloat16)
    bsf = jnp.repeat(jnp.swapaxes(spatial_b, 1, 2), DH_B, axis=2).astype(f32)
    w00 = jnp.repeat(spatial_w[:, :, 0, 0], DH_B, axis=1).astype(f32)[:, None, :]
    b0 = jnp.repeat(spatial_b[:, :, 0], DH_B, axis=1).astype(f32)[:, None, :]
    sink_c = sinks.astype(f32)[:, :, None]
    row = lambda a: a.astype(f32)[:, None, :]
    ng, lng, lnb, bup, gn = row(norm_g), row(chunk_ln_g), row(chunk_ln_b), row(gla_b_up), row(gla_norm_g)
    fg = final_norm_g.astype(f32)[None, :]
    sw = spatial_w.astype(f32)
    sinks = sinks.astype(f32)

    bias_tab = _bias_table(rel_bias)
    bias_s = bias_tab[0].reshape(H_A, WINDOW, 2 * WINDOW)[:, WINDOW - 1, WINDOW:]

    xp = x_prompt
    kp_l, vp_l, sp_l = [], [], []
    for l in range(depth):
        xp, kp, vp, sp = _prompt_layer(xp, ng, win, wout, bias_tab, sinks, sw, bsf, lng, lnb, wup, bup, gn, fg,
                                       layer=l, final=l == depth - 1)
        kp_l.append(kp); vp_l.append(vp); sp_l.append(sp)

    xs, ks_t, vs_t, ss, cv = _sample_path(
        x_sample.reshape(nseq, D_MODEL), ng, win, wout, bias_s, sink_c, w00, b0, lng, lnb, wup, bup, gn, fg,
        _window_minor(state_swa_k), _window_minor(state_swa_v), state_gla.reshape(depth, nseq * DK_TOT, DV_C))

    return (xp,
            xs.reshape(nseq, 1, D_MODEL),
            _window_major(jnp.stack(kp_l)),
            _window_major(jnp.stack(vp_l)),
            jnp.stack(sp_l).reshape(depth, bsz, H_C, DK_C, DV_C),
            _window_major(ks_t),
            _window_major(vs_t),
            ss.reshape(depth, nseq, H_C, DK_C, DV_C),
            cv.reshape(depth, nseq, 1, D_B))
```

```python
import functools

import numpy as np
import jax
import jax.numpy as jnp
from jax import lax
from jax.experimental import pallas as pl
from jax.experimental.pallas import tpu as pltpu

D_MODEL = 1024
D_A, HD_A, H_A, KV_A, G_A = 512, 64, 8, 2, 4
WINDOW, N_BUCKETS, MAX_DIST = 128, 32, 128
D_B, H_B, DH_B, CHUNK_B = 512, 8, 64, 128
D_C, H_C, DK_TOT, DK_C, DV_C = 512, 4, 256, 64, 128
GLA_RANK, GLA_TAU, GLA_CHUNK = 16, 16.0, 64
D_MIX = D_A + D_B + D_C
EPS = 1e-6
NEG = -1e30
SPLITS = [D_A, KV_A * HD_A, KV_A * HD_A, D_A, D_B, D_B, D_B, DK_TOT, DK_TOT, D_C, D_C, GLA_RANK]
D_IN = sum(SPLITS)

LANES = 128
HALF = LANES // 2
D_IN_PAD = ((D_IN + LANES - 1) // LANES) * LANES
VMEM_LIMIT_BYTES = 56 * 1024 * 1024

O_QA, O_KA, O_VA, O_GA = 0, 512, 640, 768
O_UB, O_VB, O_GB = 1280, 1792, 2304
O_QC, O_KC, O_VC, O_GC, O_LR = 2816, 3072, 3328, 3840, 4352
Y_A, Y_B, Y_C = 0, D_A, D_A + D_B

HEAD_PERM = [0, 4, 1, 5, 2, 6, 3, 7]

TOKEN_BLOCK = 512
SEQ_GROUP = 16
IN_PROJ_CHUNK = 256
OUT_PROJ_CHUNK = 256
MIXER_SECTIONS = 22

_NT = (((1,), (1,)), ((), ()))


def _t5_bucket(dist):
    n = np.maximum(dist, 0)
    max_exact = N_BUCKETS // 2
    large = max_exact + (np.log(np.maximum(n, 1) / max_exact) / np.log(MAX_DIST / max_exact)
                         * (N_BUCKETS - max_exact)).astype(np.int32)
    large = np.minimum(large, N_BUCKETS - 1)
    return np.where(n < max_exact, n, large).astype(np.int32)


def _silu(x):
    return x * (1.0 / (1.0 + jnp.exp(-x)))


def _log_sigmoid(x):
    return jnp.minimum(x, 0.0) - jnp.log1p(jnp.exp(-jnp.abs(x)))


def _bdot(a, b):
    return jnp.dot(a.astype(jnp.bfloat16), b.astype(jnp.bfloat16), preferred_element_type=jnp.float32)


def _bdot_nt(a, b):
    return lax.dot_general(a.astype(jnp.bfloat16), b.astype(jnp.bfloat16), _NT,
                           preferred_element_type=jnp.float32)


def _rmsnorm(x, g):
    return x * lax.rsqrt(jnp.mean(x * x, axis=-1, keepdims=True) + EPS) * g


def _layernorm(v, g, b):
    mu = jnp.mean(v, axis=-1, keepdims=True)
    xc = v - mu
    var = jnp.mean(xc * xc, axis=-1, keepdims=True)
    return xc * lax.rsqrt(var + EPS) * g + b


def _softmax_sink(s, sink):
    m = jnp.maximum(jnp.max(s, axis=-1, keepdims=True), sink)
    p = jnp.exp(s - m)
    den = jnp.sum(p, axis=-1, keepdims=True) + jnp.exp(sink - m)
    return p, 1.0 / den


def _bias_table_kernel(rb_ref, bucket_ref, band_ref, out_ref):
    bucket = bucket_ref[...]
    band = band_ref[...] > 0
    own = lax.broadcasted_iota(jnp.int32, bucket.shape, 1) >= WINDOW
    for h in range(H_A):
        acc = jnp.zeros(bucket.shape, jnp.float32)
        for b in range(N_BUCKETS):
            acc = jnp.where(bucket == b, rb_ref[b, h], acc)
        kv, g = divmod(h, G_A)
        rows = pl.ds(g * WINDOW, WINDOW)
        out_ref[0, kv, rows, :] = jnp.where(band, acc, NEG)
        out_ref[1, kv, rows, :] = jnp.where(band & own, acc, NEG)


def _bias_table(rel_bias):
    i = np.arange(WINDOW)[:, None]
    j = np.arange(2 * WINDOW)[None, :]
    dist = i + WINDOW - j
    band = ((dist >= 0) & (dist < WINDOW)).astype(np.int32)
    return pl.pallas_call(
        _bias_table_kernel,
        out_shape=jax.ShapeDtypeStruct((2, KV_A, G_A * WINDOW, 2 * WINDOW), jnp.float32),
        in_specs=[pl.BlockSpec(memory_space=pltpu.SMEM),
                  pl.BlockSpec(memory_space=pltpu.VMEM),
                  pl.BlockSpec(memory_space=pltpu.VMEM)],
        out_specs=pl.BlockSpec(memory_space=pltpu.VMEM),
        name="rel_bias_table",
    )(rel_bias.astype(jnp.float32), jnp.asarray(_t5_bucket(dist)), jnp.asarray(band))


def _prompt_layer_kernel(x_ref, xn_ref, ng_ref, win_ref, wout_ref, bias_ref, sink_ref, sw_ref, bsf_ref, lng_ref,
                         lnb_ref, wup_ref, bup_ref, gn_ref, fg_ref,
                         xo_ref, ko_ref, vo_ref, so_ref,
                         pa_ref, pb_ref, hb_ref, hn_ref, ya_ref, yb_ref, kprev_ref, vprev_ref, st_ref, wm_ref,
                         *, tb, layer, final):
    b_id = pl.program_id(0)
    i_id = pl.program_id(1)
    n_i = pl.num_programs(1)
    half = tb // 2
    n_sub = half // WINDOW

    @pl.when((b_id == 0) & (i_id == 0))
    def _():
        r = lax.broadcasted_iota(jnp.int32, (CHUNK_B, CHUNK_B), 0)
        c = lax.broadcasted_iota(jnp.int32, (CHUNK_B, CHUNK_B), 1)
        for h in range(H_B):
            wm_ref[h] = jnp.where(c <= r, sw_ref[h], 0.0).astype(jnp.bfloat16)
        pa_ref[...] = _bdot(_rmsnorm(x_ref[0:half, :], ng_ref[...]), win_ref[...])

    @pl.when(i_id == 0)
    def _():
        kprev_ref[...] = jnp.zeros_like(kprev_ref)
        vprev_ref[...] = jnp.zeros_like(vprev_ref)
        st_ref[...] = jnp.zeros_like(st_ref)

    hb_ref[...] = _rmsnorm(x_ref[half:tb, :], ng_ref[...]).astype(jnp.bfloat16)
    hn_ref[...] = _rmsnorm(xn_ref[...], ng_ref[...]).astype(jnp.bfloat16)

    def in_proj_chunks(h_ref, p_ref):
        def chunk(c0, c1):
            def run():
                p_ref[:, c0:c1] = jnp.dot(h_ref[...], win_ref[:, c0:c1], preferred_element_type=jnp.float32)
            return run
        edges = list(range(0, D_IN_PAD, IN_PROJ_CHUNK)) + [D_IN_PAD]
        return [chunk(c0, c1) for c0, c1 in zip(edges[:-1], edges[1:])]

    def out_proj_chunks(y_ref, r0):
        def chunk(c0, c1):
            def run():
                xo_ref[r0:r0 + half, c0:c1] = x_ref[r0:r0 + half, c0:c1] + jnp.dot(
                    y_ref[...], wout_ref[:, c0:c1], preferred_element_type=jnp.float32)
            return run
        return [chunk(c0, c0 + OUT_PROJ_CHUNK) for c0 in range(0, D_MODEL, OUT_PROJ_CHUNK)]

    def final_norm(r0):
        if final:
            xo_ref[r0:r0 + half, :] = _rmsnorm(xo_ref[r0:r0 + half, :], fg_ref[...])

    lane = lax.broadcasted_iota(jnp.int32, (WINDOW, LANES), 1)
    low = lane < HALF
    low64 = lax.broadcasted_iota(jnp.int32, (GLA_CHUNK, LANES), 1) < HALF
    ri = lax.broadcasted_iota(jnp.int32, (LANES, LANES), 0)
    ci = lax.broadcasted_iota(jnp.int32, (LANES, LANES), 1)
    bd_tril = ((ri >= GLA_CHUNK) == (ci >= GLA_CHUNK)) & (ci <= ri)
    cum_mat = jnp.where(bd_tril, 1.0, 0.0).astype(jnp.bfloat16)

    def mixers(p_ref, y_ref, s, first):
        r0 = s * WINDOW
        rows = pl.ds(r0, WINDOW)

        ks = p_ref[rows, O_KA:O_KA + LANES] * (HD_A ** -0.5)
        k_new = [jnp.where(low, ks, 0.0).astype(jnp.bfloat16), jnp.where(low, 0.0, ks).astype(jnp.bfloat16)]
        v_new = p_ref[rows, O_VA:O_VA + LANES].astype(jnp.bfloat16)
        vcat = jnp.concatenate([vprev_ref[...], v_new], axis=0)
        q4 = jnp.concatenate([p_ref[rows, O_QA + g * LANES:O_QA + (g + 1) * LANES].astype(jnp.bfloat16)
                              for g in range(G_A)], axis=0)
        o_heads = []
        for kv in range(KV_A):
            kcat = jnp.concatenate([kprev_ref[kv], k_new[kv]], axis=0)
            sc = lax.dot_general(q4, kcat, _NT, preferred_element_type=jnp.float32) + bias_ref[first, kv]
            kprev_ref[kv] = k_new[kv]
            yield
            ps, invs = [], []
            for g in range(G_A):
                p, inv = _softmax_sink(sc[g * WINDOW:(g + 1) * WINDOW], sink_ref[layer, kv * G_A + g])
                ps.append(p.astype(jnp.bfloat16))
                invs.append(inv)
                if g % 2 == 1:
                    yield
            o4 = jnp.dot(jnp.concatenate(ps, axis=0), vcat, preferred_element_type=jnp.float32)
            o_heads.append([o4[g * WINDOW:(g + 1) * WINDOW] * invs[g] for g in range(G_A)])
            yield
        vprev_ref[...] = v_new
        for g in range(G_A):
            ya = jnp.where(low, o_heads[0][g], o_heads[1][g])
            gate = p_ref[rows, O_GA + g * LANES:O_GA + (g + 1) * LANES]
            y_ref[rows, Y_A + g * LANES:Y_A + (g + 1) * LANES] = (ya * _silu(gate)).astype(jnp.bfloat16)
        yield

        vn = _layernorm(p_ref[rows, O_VB:O_VB + D_B], lng_ref[...], lnb_ref[...]).astype(jnp.bfloat16)
        yield
        for j in range(H_B // 2):
            vp = vn[:, j * LANES:(j + 1) * LANES]
            m0 = jnp.dot(wm_ref[2 * j], vp, preferred_element_type=jnp.float32)
            m1 = jnp.dot(wm_ref[2 * j + 1], vp, preferred_element_type=jnp.float32)
            mix = jnp.where(low, m0, m1) + bsf_ref[:, j * LANES:(j + 1) * LANES]
            u = p_ref[rows, O_UB + j * LANES:O_UB + (j + 1) * LANES]
            gate = p_ref[rows, O_GB + j * LANES:O_GB + (j + 1) * LANES]
            y_ref[rows, Y_B + j * LANES:Y_B + (j + 1) * LANES] = (u * mix * _silu(gate)).astype(jnp.bfloat16)
            if j % 2 == 1:
                yield

        z = _bdot(p_ref[rows, O_LR:O_LR + LANES], wup_ref[...]) + bup_ref[...]
        la = _log_sigmoid(z) * (1.0 / GLA_TAU)
        yield
        hi = la.astype(jnp.bfloat16)
        r1 = la - hi.astype(jnp.float32)
        mid = r1.astype(jnp.bfloat16)
        lo = (r1 - mid.astype(jnp.float32)).astype(jnp.bfloat16)
        c3 = jnp.dot(cum_mat, jnp.concatenate([hi, mid, lo], axis=1), preferred_element_type=jnp.float32)
        bc = (c3[:, 0:DK_TOT] + c3[:, DK_TOT:2 * DK_TOT]) + c3[:, 2 * DK_TOT:3 * DK_TOT]
        yield
        for c in range(WINDOW // GLA_CHUNK):
            rc = pl.ds(r0 + c * GLA_CHUNK, GLA_CHUNK)
            for p in range(H_C // 2):
                bp = bc[c * GLA_CHUNK:(c + 1) * GLA_CHUNK, p * LANES:(p + 1) * LANES]
                qp = p_ref[rc, O_QC + p * LANES:O_QC + (p + 1) * LANES]
                kp = p_ref[rc, O_KC + p * LANES:O_KC + (p + 1) * LANES]
                b_last = bp[GLA_CHUNK - 1:GLA_CHUNK, :]
                qd = qp * (DK_C ** -0.5) * jnp.exp(bp)
                ki = (kp * jnp.exp(-bp)).astype(jnp.bfloat16)
                kd = kp * jnp.exp(b_last - bp)
                q_st = jnp.concatenate([jnp.where(low64, qd, 0.0), jnp.where(low64, 0.0, qd)],
                                       axis=0).astype(jnp.bfloat16)
                st = st_ref[p]
                rhs = jnp.concatenate([ki, ki, st.astype(jnp.bfloat16)], axis=0)
                res = _bdot_nt(q_st, rhs)
                att = jnp.where(bd_tril, res[:, 0:LANES], 0.0).astype(jnp.bfloat16)
                yield
                v0 =p_ref[rc, O_VC + (2 * p) * DV_C:O_VC + (2 * p + 1) * DV_C]
                v1 = p_ref[rc, O_VC + (2 * p + 1) * DV_C:O_VC + (2 * p + 2) * DV_C]
                v_st = jnp.concatenate([v0, v1], axis=0)
                o = jnp.dot(att, v_st.astype(jnp.bfloat16), preferred_element_type=jnp.float32) + res[:, LANES:]
                kd_st = jnp.concatenate([jnp.where(low64, kd, 0.0), jnp.where(low64, 0.0, kd)],
                                        axis=0).astype(jnp.bfloat16)
                d_st = jnp.dot(v_st.T.astype(jnp.bfloat16), kd_st, preferred_element_type=jnp.float32)
                st_ref[p] = st * jnp.exp(b_last) + d_st
                for hh in range(2):
                    hd = 2 * p + hh
                    oh = o[hh * GLA_CHUNK:(hh + 1) * GLA_CHUNK]
                    on = _rmsnorm(oh, gn_ref[:, hd * DV_C:(hd + 1) * DV_C])
                    gate = p_ref[rc, O_GC + hd * DV_C:O_GC + (hd + 1) * DV_C]
                    y_ref[rc, Y_C + hd * DV_C:Y_C + (hd + 1) * DV_C] = (on * _silu(gate)).astype(jnp.bfloat16)
                yield

    def half_mixers(p_ref, y_ref, first):
        for s in range(n_sub):
            yield from mixers(p_ref, y_ref, s, first if s == 0 else 0)

    def run_phase(sections, n_sections, chunks):
        done = 0
        for k in range(n_sections):
            next(sections)
            upto = (k + 1) * len(chunks) // n_sections
            for run in chunks[done:upto]:
                run()
            done = upto
        assert next(sections, None) is None

    n_sections = n_sub * MIXER_SECTIONS
    run_phase(half_mixers(pa_ref, ya_ref, jnp.where(i_id == 0, 1, 0)), n_sections,
              in_proj_chunks(hb_ref, pb_ref))
    run_phase(half_mixers(pb_ref, yb_ref, 0), n_sections,
              out_proj_chunks(ya_ref, 0) + in_proj_chunks(hn_ref, pa_ref))
    final_norm(0)
    for run in out_proj_chunks(yb_ref, half):
        run()
    final_norm(half)

    @pl.when(i_id == n_i - 1)
    def _():
        ko_ref[...] = pb_ref[half - WINDOW:half, O_KA:O_KA + LANES].T
        vo_ref[...] = pb_ref[half - WINDOW:half, O_VA:O_VA + LANES].T
        for p in range(H_C // 2):
            so_ref[p] = st_ref[p].T


def _prompt_layer(x, ng, win, wout, bias_tab, sinks, sw, bsf, lng, lnb, wup, bup, gn, fg, *, layer, final):
    bsz, seq, _ = x.shape
    tb = min(TOKEN_BLOCK, seq)
    half = tb // 2
    assert seq % tb == 0 and half % WINDOW == 0
    n_i = seq // tb
    grid = (bsz, n_i)

    def next_half(b, i):
        lin = jnp.minimum(b * n_i + i + 1, bsz * n_i - 1)
        return lin // n_i, (lin % n_i) * 2, 0

    def const_spec(shape):
        nd = len(shape)
        return pl.BlockSpec(shape, lambda b, i: (0,) * nd, pipeline_mode=pl.Buffered(1))

    def layer_spec(shape):
        nd = len(shape)
        return pl.BlockSpec((None,) + shape, lambda b, i: (layer,) + (0,) * nd, pipeline_mode=pl.Buffered(1))

    kern = functools.partial(_prompt_layer_kernel, tb=tb, layer=layer, final=final)
    return pl.pallas_call(
        kern,
        grid=grid,
        in_specs=[
            pl.BlockSpec((None, tb, D_MODEL), lambda b, i: (b, i, 0)),
            pl.BlockSpec((None, half, D_MODEL), next_half),
            layer_spec((1, D_MODEL)),
            layer_spec((D_MODEL, D_IN_PAD)),
            layer_spec((D_MIX, D_MODEL)),
            const_spec((2, KV_A, G_A * WINDOW, 2 * WINDOW)),
            pl.BlockSpec(memory_space=pltpu.SMEM),
            layer_spec((H_B, CHUNK_B, CHUNK_B)),
            layer_spec((CHUNK_B, D_B)),
            layer_spec((1, D_B)),
            layer_spec((1, D_B)),
            layer_spec((LANES, DK_TOT)),
            layer_spec((1, DK_TOT)),
            layer_spec((1, D_C)),
            const_spec((1, D_MODEL)),
        ],
        out_specs=[
            pl.BlockSpec((None, tb, D_MODEL), lambda b, i: (b, i, 0)),
            pl.BlockSpec((None, LANES, WINDOW), lambda b, i: (b, 0, 0)),
            pl.BlockSpec((None, LANES, WINDOW), lambda b, i: (b, 0, 0)),
            pl.BlockSpec((None, H_C // 2, LANES, DV_C), lambda b, i: (b, 0, 0, 0)),
        ],
        out_shape=[
            jax.ShapeDtypeStruct((bsz, seq, D_MODEL), jnp.float32),
            jax.ShapeDtypeStruct((bsz, LANES, WINDOW), jnp.float32),
            jax.ShapeDtypeStruct((bsz, LANES, WINDOW), jnp.float32),
            jax.ShapeDtypeStruct((bsz, H_C // 2, LANES, DV_C), jnp.float32),
        ],
        scratch_shapes=[
            pltpu.VMEM((half, D_IN_PAD), jnp.float32),
            pltpu.VMEM((half, D_IN_PAD), jnp.float32),
            pltpu.VMEM((half, D_MODEL), jnp.bfloat16),
            pltpu.VMEM((half, D_MODEL), jnp.bfloat16),
            pltpu.VMEM((half, D_MIX), jnp.bfloat16),
            pltpu.VMEM((half, D_MIX), jnp.bfloat16),
            pltpu.VMEM((KV_A, WINDOW, LANES), jnp.bfloat16),
            pltpu.VMEM((WINDOW, LANES), jnp.bfloat16),
            pltpu.VMEM((H_C // 2, DV_C, LANES), jnp.float32),
            pltpu.VMEM((H_B, CHUNK_B, CHUNK_B), jnp.bfloat16),
        ],
        compiler_params=pltpu.CompilerParams(
            dimension_semantics=("arbitrary", "arbitrary"),
            vmem_limit_bytes=VMEM_LIMIT_BYTES),
        name="prompt_layer",
    )(x, x, ng, win, wout, bias_tab, sinks, sw, bsf, lng, lnb, wup, bup, gn, fg)


def _sample_kernel(x_ref, ng_ref, win_ref, wout_ref, biass_ref, sinkc_ref, w00_ref, b0_ref, lng_ref,
                   lnb_ref, wup_ref, bup_ref, gn_ref, fg_ref, bk_ref, bv_ref, s0_ref,
                   xo_ref, nk_ref, nv_ref, so_ref, cv_ref,
                   xcur_ref, proj_ref, y_ref, qs_ref, os_ref, qd_ref, ea_ref, oc_ref, *, nseq, gs):
    l_id = pl.program_id(0)
    i_id = pl.program_id(1)
    n_l = pl.num_programs(0)
    n_i = pl.num_programs(1)
    lane = lax.broadcasted_iota(jnp.int32, (nseq, LANES), 1)
    low = lane < HALF

    @pl.when((l_id == 0) & (i_id == 0))
    def _():
        xcur_ref[...] = x_ref[...]

    @pl.when(i_id == 0)
    def _():
        h = _rmsnorm(xcur_ref[...], ng_ref[...])
        proj_ref[...] = _bdot(h, win_ref[...])
        for g in range(G_A):
            qp = proj_ref[:, O_QA + g * LANES:O_QA + (g + 1) * LANES] * (HD_A ** -0.5)
            for kv in range(KV_A):
                keep = low if kv == 0 else jnp.logical_not(low)
                qs_ref[pl.ds(kv * G_A + g, nseq, stride=H_A), :] = jnp.where(keep, qp, 0.0)
        vn = _layernorm(proj_ref[:, O_VB:O_VB + D_B], lng_ref[...], lnb_ref[...])
        cv_ref[...] = vn
        mix = w00_ref[...] * vn + b0_ref[...]
        y_ref[:, Y_B:Y_B + D_B] = (proj_ref[:, O_UB:O_UB + D_B] * mix) * _silu(proj_ref[:, O_GB:O_GB + D_B])
        z = _bdot(proj_ref[:, O_LR:O_LR + LANES], wup_ref[...]) + bup_ref[...]
        la = _log_sigmoid(z) * (1.0 / GLA_TAU)
        ea = jnp.exp(la)
        qd = proj_ref[:, O_QC:O_QC + DK_TOT] * (DK_C ** -0.5) * ea
        ki = proj_ref[:, O_KC:O_KC + DK_TOT] * jnp.exp(-la)
        qd_ref[...] = qd
        ea_ref[...] = ea
        prod = qd * ki
        lane_c = lax.broadcasted_iota(jnp.int32, prod.shape, 1)
        for hd in range(H_C):
            in_head = (lane_c >= hd * DK_C) & (lane_c < (hd + 1) * DK_C)
            att = jnp.sum(jnp.where(in_head, prod, 0.0), axis=-1, keepdims=True)
            oc_ref[:, hd * DV_C:(hd + 1) * DV_C] = att * proj_ref[:, O_VC + hd * DV_C:O_VC + (hd + 1) * DV_C]

    g0 = pl.multiple_of(i_id * gs, gs)
    grows = pl.ds(g0, gs)

    pad_rows = jnp.zeros((LANES - gs, LANES), jnp.float32)
    k_cols = jnp.concatenate([proj_ref[grows, O_KA:O_KA + LANES], pad_rows], axis=0).T
    v_cols = jnp.concatenate([proj_ref[grows, O_VA:O_VA + LANES], pad_rows], axis=0).T
    newest = lax.broadcasted_iota(jnp.int32, (LANES, WINDOW), 1) == WINDOW - 1
    for j in range(gs):
        kt = jnp.where(newest, k_cols[:, j:j + 1], pltpu.roll(bk_ref[j], WINDOW - 1, 1))
        vt = jnp.where(newest, v_cols[:, j:j + 1], pltpu.roll(bv_ref[j], WINDOW - 1, 1))
        nk_ref[j] = kt
        nv_ref[j] = vt
        r8 = pl.ds(pl.multiple_of((g0 + j) * H_A, H_A), H_A)
        sc = _bdot(qs_ref[r8, :], kt) + biass_ref[...]
        p, inv = _softmax_sink(sc, sinkc_ref[...])
        os_ref[r8, :] = _bdot_nt(p, vt) * inv

    for hd in range(H_C):
        v_h = proj_ref[grows, O_VC + hd * DV_C:O_VC + (hd + 1) * DV_C]
        acc = jnp.zeros((gs, DV_C), jnp.float32)
        for d in range(DK_C):
            c = hd * DK_C + d
            s_rows = s0_ref[pl.ds(c, gs, stride=DK_TOT), :]
            acc = acc + qd_ref[grows, c:c + 1] * s_rows
            so_ref[pl.ds(c, gs, stride=DK_TOT), :] = (ea_ref[grows, c:c + 1] * s_rows
                                                     + proj_ref[grows, O_KC + c:O_KC + c + 1] * v_h)
        oc_ref[grows, hd * DV_C:(hd + 1) * DV_C] += acc

    @pl.when(i_id == n_i - 1)
    def _():
        for g in range(G_A):
            o0 = os_ref[pl.ds(g, nseq, stride=H_A), :]
            o1 = os_ref[pl.ds(G_A + g, nseq, stride=H_A), :]
            gate = proj_ref[:, O_GA + g * LANES:O_GA + (g + 1) * LANES]
            y_ref[:, Y_A + g * LANES:Y_A + (g + 1) * LANES] = jnp.where(low, o0, o1) * _silu(gate)
        for hd in range(H_C):
            on = _rmsnorm(oc_ref[:, hd * DV_C:(hd + 1) * DV_C], gn_ref[:, hd * DV_C:(hd + 1) * DV_C])
            gate = proj_ref[:, O_GC + hd * DV_C:O_GC + (hd + 1) * DV_C]
            y_ref[:, Y_C + hd * DV_C:Y_C + (hd + 1) * DV_C] = on * _silu(gate)
        x_new = xcur_ref[...] + _bdot(y_ref[...], wout_ref[...])
        xcur_ref[...] = x_new

        @pl.when(l_id == n_l - 1)
        def _():
            xo_ref[...] = _rmsnorm(x_new, fg_ref[...])


def _sample_path(x, ng, win, wout, bias_s, sink_c, w00, b0, lng, lnb, wup, bup, gn, fg, buf_kt, buf_vt, s0):
    depth = win.shape[0]
    nseq = x.shape[0]
    gs = min(SEQ_GROUP, nseq)
    assert nseq % gs == 0 and gs % 8 == 0
    grid = (depth, nseq // gs)

    def const_spec(shape):
        nd = len(shape)
        return pl.BlockSpec(shape, lambda l, i: (0,) * nd)

    def layer_spec(shape):
        nd = len(shape)
        return pl.BlockSpec((None,) + shape, lambda l, i: (l,) + (0,) * nd)

    state_spec = pl.BlockSpec((None, gs, LANES, WINDOW), lambda l, i: (l, i, 0, 0))
    gla_spec = pl.BlockSpec((None, gs * DK_TOT, DV_C), lambda l, i: (l, i, 0))
    kern = functools.partial(_sample_kernel, nseq=nseq, gs=gs)
    return pl.pallas_call(
        kern,
        grid=grid,
        in_specs=[
            const_spec((nseq, D_MODEL)),
            layer_spec((1, D_MODEL)),
            layer_spec((D_MODEL, D_IN_PAD)),
            layer_spec((D_MIX, D_MODEL)),
            const_spec((H_A, WINDOW)),
            layer_spec((H_A, 1)),
            layer_spec((1, D_B)),
            layer_spec((1, D_B)),
            layer_spec((1, D_B)),
            layer_spec((1, D_B)),
            layer_spec((LANES, DK_TOT)),
            layer_spec((1, DK_TOT)),
            layer_spec((1, D_C)),
            const_spec((1, D_MODEL)),
            state_spec, state_spec, gla_spec,
        ],
        out_specs=[
            const_spec((nseq, D_MODEL)),
            state_spec, state_spec, gla_spec,
            layer_spec((nseq, D_B)),
        ],
        out_shape=[
            jax.ShapeDtypeStruct((nseq, D_MODEL), jnp.float32),
            jax.ShapeDtypeStruct((depth, nseq, LANES, WINDOW), jnp.float32),
            jax.ShapeDtypeStruct((depth, nseq, LANES, WINDOW), jnp.float32),
            jax.ShapeDtypeStruct((depth, nseq * DK_TOT, DV_C), jnp.float32),
            jax.ShapeDtypeStruct((depth, nseq, D_B), jnp.float32),
        ],
        scratch_shapes=[
            pltpu.VMEM((nseq, D_MODEL), jnp.float32),
            pltpu.VMEM((nseq, D_IN_PAD), jnp.float32),
            pltpu.VMEM((nseq, D_MIX), jnp.float32),
            pltpu.VMEM((nseq * H_A, LANES), jnp.float32),
            pltpu.VMEM((nseq * H_A, LANES), jnp.float32),
            pltpu.VMEM((nseq, DK_TOT), jnp.float32),
            pltpu.VMEM((nseq, DK_TOT), jnp.float32),
            pltpu.VMEM((nseq, D_C), jnp.float32),
        ],
        compiler_params=pltpu.CompilerParams(
            dimension_semantics=("arbitrary", "arbitrary"),
            vmem_limit_bytes=VMEM_LIMIT_BYTES),
        name="sample_path",
    )(x, ng, win, wout, bias_s, sink_c, w00, b0, lng, lnb, wup, bup, gn, fg, buf_kt, buf_vt, s0)


def _window_minor(state):
    depth, n = state.shape[:2]
    return jnp.transpose(state, (0, 1, 3, 4, 2)).reshape(depth, n, LANES, WINDOW)


def _window_major(state_t):
    depth, n = state_t.shape[:2]
    return jnp.transpose(state_t.reshape(depth, n, KV_A, HD_A, WINDOW), (0, 1, 4, 2, 3))


def kernel(x_prompt, x_sample, state_swa_k, state_swa_v, state_gla, rel_bias, norm_g, w_in, sinks, spatial_w,
           spatial_b, chunk_ln_g, chunk_ln_b, gla_w_up, gla_b_up, gla_norm_g, w_out, final_norm_g):
    depth = w_in.shape[0]
    nseq = x_sample.shape[0]
    bsz = x_prompt.shape[0]
    f32 = jnp.float32

    heads = lambda w, off, axis: [lax.slice_in_dim(w, off + h * HD_A, off + (h + 1) * HD_A, axis=axis)
                                  for h in HEAD_PERM]
    win = jnp.concatenate(
        heads(w_in, O_QA, 2) + [w_in[:, :, O_KA:O_GA]] + heads(w_in, O_GA, 2) + [w_in[:, :, O_UB:]]
        + [jnp.zeros((depth, D_MODEL, D_IN_PAD - D_IN), w_in.dtype)], axis=2).astype(jnp.bfloat16)
    wout = jnp.concatenate(heads(w_out, Y_A, 1) + [w_out[:, Y_B:, :]], axis=1).astype(jnp.bfloat16)
    wup = jnp.pad(gla_w_up, ((0, 0), (0, LANES - GLA_RANK), (0, 0))).astype(jnp.bfloat16)
    bsf = jnp.repeat(jnp.swapaxes(spatial_b, 1, 2), DH_B, axis=2).astype(f32)
    w00 = jnp.repeat(spatial_w[:, :, 0, 0], DH_B, axis=1).astype(f32)[:, None, :]
    b0 = jnp.repeat(spatial_b[:, :, 0], DH_B, axis=1).astype(f32)[:, None, :]
    sink_c = sinks.astype(f32)[:, :, None]
    row = lambda a: a.astype(f32)[:, None, :]
    ng, lng, lnb, bup, gn = row(norm_g), row(chunk_ln_g), row(chunk_ln_b), row(gla_b_up), row(gla_norm_g)
    fg = final_norm_g.astype(f32)[None, :]
    sw = spatial_w.astype(f32)
    sinks = sinks.astype(f32)

    bias_tab = _bias_table(rel_bias)
    bias_s = bias_tab[0].reshape(H_A, WINDOW, 2 * WINDOW)[:, WINDOW - 1, WINDOW:]

    xp = x_prompt
    kp_l, vp_l, sp_l = [], [], []
    for l in range(depth):
        xp, kp, vp, sp = _prompt_layer(xp, ng, win, wout, bias_tab, sinks, sw, bsf, lng, lnb, wup, bup, gn, fg,
                                       layer=l, final=l == depth - 1)
        kp_l.append(kp); vp_l.append(vp); sp_l.append(sp)

    xs, ks_t, vs_t, ss, cv = _sample_path(
        x_sample.reshape(nseq, D_MODEL), ng, win, wout, bias_s, sink_c, w00, b0, lng, lnb, wup, bup, gn, fg,
        _window_minor(state_swa_k), _window_minor(state_swa_v), state_gla.reshape(depth, nseq * DK_TOT, DV_C))

    return (xp,
            xs.reshape(nseq, 1, D_MODEL),
            _window_major(jnp.stack(kp_l)),
            _window_major(jnp.stack(vp_l)),
            jnp.stack(sp_l).reshape(depth, bsz, H_C, DK_C, DV_C),
            _window_major(ks_t),
            _window_major(vs_t),
            ss.reshape(depth, nseq, H_C, DK_C, DV_C),
            cv.reshape(depth, nseq, 1, D_B))
```

```python
import functools

import numpy as np
import jax
import jax.numpy as jnp
from jax import lax
from jax.experimental import pallas as pl
from jax.experimental.pallas import tpu as pltpu

D_MODEL = 1024
D_A, HD_A, H_A, KV_A, G_A = 512, 64, 8, 2, 4
WINDOW, N_BUCKETS, MAX_DIST = 128, 32, 128
D_B, H_B, DH_B, CHUNK_B = 512, 8, 64, 128
D_C, H_C, DK_TOT, DK_C, DV_C = 512, 4, 256, 64, 128
GLA_RANK, GLA_TAU, GLA_CHUNK = 16, 16.0, 64
D_MIX = D_A + D_B + D_C
EPS = 1e-6
NEG = -1e30
SPLITS = [D_A, KV_A * HD_A, KV_A * HD_A, D_A, D_B, D_B, D_B, DK_TOT, DK_TOT, D_C, D_C, GLA_RANK]
D_IN = sum(SPLITS)

LANES = 128
HALF = LANES // 2
D_IN_PAD = ((D_IN + LANES - 1) // LANES) * LANES
VMEM_LIMIT_BYTES = 56 * 1024 * 1024

O_QA, O_KA, O_VA, O_GA = 0, 512, 640, 768
O_UB, O_VB, O_GB = 1280, 1792, 2304
O_QC, O_KC, O_VC, O_GC, O_LR = 2816, 3072, 3328, 3840, 4352
Y_A, Y_B, Y_C = 0, D_A, D_A + D_B

HEAD_PERM = [0, 4, 1, 5, 2, 6, 3, 7]

TOKEN_BLOCK = 512
SEQ_GROUP = 16
IN_PROJ_CHUNK = 256
OUT_PROJ_CHUNK = 256
MIXER_SECTIONS = 22
PREP_SPLIT = 4

_NT = (((1,), (1,)), ((), ()))


def _t5_bucket(dist):
    n = np.maximum(dist, 0)
    max_exact = N_BUCKETS // 2
    large = max_exact + (np.log(np.maximum(n, 1) / max_exact) / np.log(MAX_DIST / max_exact)
                         * (N_BUCKETS - max_exact)).astype(np.int32)
    large = np.minimum(large, N_BUCKETS - 1)
    return np.where(n < max_exact, n, large).astype(np.int32)


def _silu(x):
    return x * (1.0 / (1.0 + jnp.exp(-x)))


def _log_sigmoid(x):
    return jnp.minimum(x, 0.0) - jnp.log1p(jnp.exp(-jnp.abs(x)))


def _bdot(a, b):
    return jnp.dot(a.astype(jnp.bfloat16), b.astype(jnp.bfloat16), preferred_element_type=jnp.float32)


def _bdot_nt(a, b):
    return lax.dot_general(a.astype(jnp.bfloat16), b.astype(jnp.bfloat16), _NT,
                           preferred_element_type=jnp.float32)


def _rmsnorm(x, g):
    return x * lax.rsqrt(jnp.mean(x * x, axis=-1, keepdims=True) + EPS) * g


def _layernorm(v, g, b):
    mu = jnp.mean(v, axis=-1, keepdims=True)
    xc = v - mu
    var = jnp.mean(xc * xc, axis=-1, keepdims=True)
    return xc * lax.rsqrt(var + EPS) * g + b


def _softmax_sink(s, sink):
    m = jnp.maximum(jnp.max(s, axis=-1, keepdims=True), sink)
    p = jnp.exp(s - m)
    den = jnp.sum(p, axis=-1, keepdims=True) + jnp.exp(sink - m)
    return p, 1.0 / den


def _bias_table_kernel(rb_ref, bucket_ref, band_ref, out_ref):
    bucket = bucket_ref[...]
    band = band_ref[...] > 0
    own = lax.broadcasted_iota(jnp.int32, bucket.shape, 1) >= WINDOW
    for h in range(H_A):
        acc = jnp.zeros(bucket.shape, jnp.float32)
        for b in range(N_BUCKETS):
            acc = jnp.where(bucket == b, rb_ref[b, h], acc)
        kv, g = divmod(h, G_A)
        rows = pl.ds(g * WINDOW, WINDOW)
        out_ref[0, kv, rows, :] = jnp.where(band, acc, NEG)
        out_ref[1, kv, rows, :] = jnp.where(band & own, acc, NEG)


def _bias_table(rel_bias):
    i = np.arange(WINDOW)[:, None]
    j = np.arange(2 * WINDOW)[None, :]
    dist = i + WINDOW - j
    band = ((dist >= 0) & (dist < WINDOW)).astype(np.int32)
    return pl.pallas_call(
        _bias_table_kernel,
        out_shape=jax.ShapeDtypeStruct((2, KV_A, G_A * WINDOW, 2 * WINDOW), jnp.float32),
        in_specs=[pl.BlockSpec(memory_space=pltpu.SMEM),
                  pl.BlockSpec(memory_space=pltpu.VMEM),
                  pl.BlockSpec(memory_space=pltpu.VMEM)],
        out_specs=pl.BlockSpec(memory_space=pltpu.VMEM),
        name="rel_bias_table",
    )(rel_bias.astype(jnp.float32), jnp.asarray(_t5_bucket(dist)), jnp.asarray(band))


def _win_prep_kernel(w_ref, lr_ref, o_ref):
    low = lax.broadcasted_iota(jnp.int32, (w_ref.shape[0], LANES), 1) < HALF
    for off in (O_QA, O_GA):
        for g in range(G_A):
            a = w_ref[:, off + (g // 2) * LANES:off + (g // 2 + 1) * LANES]
            b = w_ref[:, off + (G_A // 2 + g // 2) * LANES:off + (G_A // 2 + g // 2 + 1) * LANES]
            if g % 2 == 0:
                t = jnp.where(low, a, pltpu.roll(b, HALF, 1))
            else:
                t = jnp.where(low, pltpu.roll(a, HALF, 1), b)
            o_ref[:, off + g * LANES:off + (g + 1) * LANES] = t.astype(jnp.bfloat16)
    o_ref[:, O_KA:O_GA] = w_ref[:, O_KA:O_GA].astype(jnp.bfloat16)
    o_ref[:, O_UB:O_LR] = w_ref[:, O_UB:O_LR].astype(jnp.bfloat16)
    o_ref[:, O_LR:D_IN_PAD] = lr_ref[...].astype(jnp.bfloat16)


def _wout_prep_kernel(w_ref, o_ref):
    for j, h in enumerate(HEAD_PERM):
        o_ref[j * HD_A:(j + 1) * HD_A, :] = w_ref[h * HD_A:(h + 1) * HD_A, :].astype(jnp.bfloat16)
    o_ref[Y_B:, :] = w_ref[Y_B:, :].astype(jnp.bfloat16)


def _prepare_weights(w_in, w_out):
    depth = w_in.shape[0]
    rows = D_MODEL // PREP_SPLIT
    cols = D_MODEL // PREP_SPLIT
    w_lr = jnp.pad(w_in[:, :, O_LR:], ((0, 0), (0, 0), (0, D_IN_PAD - D_IN)))
    win = pl.pallas_call(
        _win_prep_kernel,
        grid=(depth, PREP_SPLIT),
        in_specs=[pl.BlockSpec((None, rows, O_LR), lambda l, r: (l, r, 0)),
                  pl.BlockSpec((None, rows, LANES), lambda l, r: (l, r, 0))],
        out_specs=pl.BlockSpec((None, rows, D_IN_PAD), lambda l, r: (l, r, 0)),
        out_shape=jax.ShapeDtypeStruct((depth, D_MODEL, D_IN_PAD), jnp.bfloat16),
        compiler_params=pltpu.CompilerParams(dimension_semantics=("arbitrary", "arbitrary"),
                                             vmem_limit_bytes=VMEM_LIMIT_BYTES),
        name="w_in_prep",
    )(w_in, w_lr)
    wout = pl.pallas_call(
        _wout_prep_kernel,
        grid=(depth, PREP_SPLIT),
        in_specs=[pl.BlockSpec((None, D_MIX, cols), lambda l, c: (l, 0, c))],
        out_specs=pl.BlockSpec((None, D_MIX, cols), lambda l, c: (l, 0, c)),
        out_shape=jax.ShapeDtypeStruct((depth, D_MIX, D_MODEL), jnp.bfloat16),
        compiler_params=pltpu.CompilerParams(dimension_semantics=("arbitrary", "arbitrary"),
                                             vmem_limit_bytes=VMEM_LIMIT_BYTES),
        name="w_out_prep",
    )(w_out)
    return win, wout


def _prompt_layer_kernel(x_ref, xn_ref, ng_ref, win_ref, wout_ref, bias_ref, sink_ref, sw_ref, bsf_ref, lng_ref,
                         lnb_ref, wup_ref, bup_ref, gn_ref, fg_ref,
                         xo_ref, ko_ref, vo_ref, so_ref,
                         pa_ref, pb_ref, hb_ref, hn_ref, ya_ref, yb_ref, kprev_ref, vprev_ref, st_ref, wm_ref,
                         *, tb, layer, final):
    b_id = pl.program_id(0)
    i_id = pl.program_id(1)
    n_i = pl.num_programs(1)
    half = tb // 2
    n_sub = half // WINDOW

    @pl.when((b_id == 0) & (i_id == 0))
    def _():
        r = lax.broadcasted_iota(jnp.int32, (CHUNK_B, CHUNK_B), 0)
        c = lax.broadcasted_iota(jnp.int32, (CHUNK_B, CHUNK_B), 1)
        for h in range(H_B):
            wm_ref[h // 2, :, (h % 2) * CHUNK_B:(h % 2 + 1) * CHUNK_B] = jnp.where(
                c <= r, sw_ref[h], 0.0).astype(jnp.bfloat16)
        pa_ref[...] = _bdot(_rmsnorm(x_ref[0:half, :], ng_ref[...]), win_ref[...])

    @pl.when(i_id == 0)
    def _():
        kprev_ref[...] = jnp.zeros_like(kprev_ref)
        vprev_ref[...] = jnp.zeros_like(vprev_ref)
        st_ref[...] = jnp.zeros_like(st_ref)

    hb_ref[...] = _rmsnorm(x_ref[half:tb, :], ng_ref[...]).astype(jnp.bfloat16)
    hn_ref[...] = _rmsnorm(xn_ref[...], ng_ref[...]).astype(jnp.bfloat16)

    def in_proj_chunks(h_ref, p_ref):
        def chunk(c0, c1):
            def run():
                p_ref[:, c0:c1] = jnp.dot(h_ref[...], win_ref[:, c0:c1], preferred_element_type=jnp.float32)
            return run
        edges = list(range(0, D_IN_PAD, IN_PROJ_CHUNK)) + [D_IN_PAD]
        return [chunk(c0, c1) for c0, c1 in zip(edges[:-1], edges[1:])]

    def out_proj_chunks(y_ref, r0):
        def chunk(c0, c1):
            def run():
                xo_ref[r0:r0 + half, c0:c1] = x_ref[r0:r0 + half, c0:c1] + jnp.dot(
                    y_ref[...], wout_ref[:, c0:c1], preferred_element_type=jnp.float32)
            return run
        return [chunk(c0, c0 + OUT_PROJ_CHUNK) for c0 in range(0, D_MODEL, OUT_PROJ_CHUNK)]

    def final_norm(r0):
        if final:
            xo_ref[r0:r0 + half, :] = _rmsnorm(xo_ref[r0:r0 + half, :], fg_ref[...])

    lane = lax.broadcasted_iota(jnp.int32, (WINDOW, LANES), 1)
    low = lane < HALF
    low64 = lax.broadcasted_iota(jnp.int32, (GLA_CHUNK, LANES), 1) < HALF
    ri = lax.broadcasted_iota(jnp.int32, (LANES, LANES), 0)
    ci = lax.broadcasted_iota(jnp.int32, (LANES, LANES), 1)
    bd_tril = ((ri >= GLA_CHUNK) == (ci >= GLA_CHUNK)) & (ci <= ri)
    cum_mat = jnp.where(bd_tril, 1.0, 0.0).astype(jnp.bfloat16)

    def mixers(p_ref, y_ref, s, first):
        r0 = s * WINDOW
        rows = pl.ds(r0, WINDOW)

        ks = p_ref[rows, O_KA:O_KA + LANES] * (HD_A ** -0.5)
        k_new = [jnp.where(low, ks, 0.0).astype(jnp.bfloat16), jnp.where(low, 0.0, ks).astype(jnp.bfloat16)]
        v_new = p_ref[rows, O_VA:O_VA + LANES].astype(jnp.bfloat16)
        vcat = jnp.concatenate([vprev_ref[...], v_new], axis=0)
        q4 = jnp.concatenate([p_ref[rows, O_QA + g * LANES:O_QA + (g + 1) * LANES].astype(jnp.bfloat16)
                              for g in range(G_A)], axis=0)
        o_heads = []
        for kv in range(KV_A):
            kcat = jnp.concatenate([kprev_ref[kv], k_new[kv]], axis=0)
            sc = lax.dot_general(q4, kcat, _NT, preferred_element_type=jnp.float32) + bias_ref[first, kv]
            kprev_ref[kv] = k_new[kv]
            yield
            ps, invs = [], []
            for g in range(G_A):
                p, inv = _softmax_sink(sc[g * WINDOW:(g + 1) * WINDOW], sink_ref[layer, kv * G_A + g])
                ps.append(p.astype(jnp.bfloat16))
                invs.append(inv)
                if g % 2 == 1:
                    yield
            o4 = jnp.dot(jnp.concatenate(ps, axis=0), vcat, preferred_element_type=jnp.float32)
            o_heads.append([o4[g * WINDOW:(g + 1) * WINDOW] * invs[g] for g in range(G_A)])
            yield
        vprev_ref[...] = v_new
        for g in range(G_A):
            ya = jnp.where(low, o_heads[0][g], o_heads[1][g])
            gate = p_ref[rows, O_GA + g * LANES:O_GA + (g + 1) * LANES]
            y_ref[rows, Y_A + g * LANES:Y_A + (g + 1) * LANES] = (ya * _silu(gate)).astype(jnp.bfloat16)
        yield

        vn = _layernorm(p_ref[rows, O_VB:O_VB + D_B], lng_ref[...], lnb_ref[...]).astype(jnp.bfloat16)
        yield
        for j in range(H_B // 2):
            vp = vn[:, j * LANES:(j + 1) * LANES]
            zero = jnp.zeros_like(vp)
            v_st = jnp.concatenate([jnp.where(low, vp, zero), jnp.where(low, zero, vp)], axis=0)
            mix = (jnp.dot(wm_ref[j], v_st, preferred_element_type=jnp.float32)
                   + bsf_ref[:, j * LANES:(j + 1) * LANES])
            u = p_ref[rows, O_UB + j * LANES:O_UB + (j + 1) * LANES]
            gate = p_ref[rows, O_GB + j * LANES:O_GB + (j + 1) * LANES]
            y_ref[rows, Y_B + j * LANES:Y_B + (j + 1) * LANES] = (u * mix * _silu(gate)).astype(jnp.bfloat16)
            if j % 2 == 1:
                yield

        z = _bdot(p_ref[rows, O_LR:O_LR + LANES], wup_ref[...]) + bup_ref[...]
        la = _log_sigmoid(z) * (1.0 / GLA_TAU)
        yield
        hi = la.astype(jnp.bfloat16)
        r1 = la - hi.astype(jnp.float32)
        mid = r1.astype(jnp.bfloat16)
        lo = (r1 - mid.astype(jnp.float32)).astype(jnp.bfloat16)
        c3 = jnp.dot(cum_mat, jnp.concatenate([hi, mid, lo], axis=1), preferred_element_type=jnp.float32)
        bc = (c3[:, 0:DK_TOT] + c3[:, DK_TOT:2 * DK_TOT]) + c3[:, 2 * DK_TOT:3 * DK_TOT]
        yield
        for c in range(WINDOW // GLA_CHUNK):
            rc = pl.ds(r0 + c * GLA_CHUNK, GLA_CHUNK)
            for p in range(H_C // 2):
                bp = bc[c * GLA_CHUNK:(c + 1) * GLA_CHUNK, p * LANES:(p + 1) * LANES]
                qp = p_ref[rc, O_QC + p * LANES:O_QC + (p + 1) * LANES]
                kp = p_ref[rc, O_KC + p * LANES:O_KC + (p + 1) * LANES]
                b_last = bp[GLA_CHUNK - 1:GLA_CHUNK, :]
                qd = qp * (DK_C ** -0.5) * jnp.exp(bp)
                ki = (kp * jnp.exp(-bp)).astype(jnp.bfloat16)
                kd = kp * jnp.exp(b_last - bp)
                q_st = jnp.concatenate([jnp.where(low64, qd, 0.0), jnp.where(low64, 0.0, qd)],
                                       axis=0).astype(jnp.bfloat16)
                st = st_ref[p]
                rhs = jnp.concatenate([ki, ki, st.astype(jnp.bfloat16)], axis=0)
                res = _bdot_nt(q_st, rhs)
                att = jnp.where(bd_tril, res[:, 0:LANES], 0.0).astype(jnp.bfloat16)
                yield
                v0 =p_ref[rc, O_VC + (2 * p) * DV_C:O_VC + (2 * p + 1) * DV_C]
                v1 = p_ref[rc, O_VC + (2 * p + 1) * DV_C:O_VC + (2 * p + 2) * DV_C]
                v_st = jnp.concatenate([v0, v1], axis=0)
                o = jnp.dot(att, v_st.astype(jnp.bfloat16), preferred_element_type=jnp.float32) + res[:, LANES:]
                kd_st = jnp.concatenate([jnp.where(low64, kd, 0.0), jnp.where(low64, 0.0, kd)],
                                        axis=0).astype(jnp.bfloat16)
                d_st = jnp.dot(v_st.T.astype(jnp.bfloat16), kd_st, preferred_element_type=jnp.float32)
                st_ref[p] = st * jnp.exp(b_last) + d_st
                for hh in range(2):
                    hd = 2 * p + hh
                    oh = o[hh * GLA_CHUNK:(hh + 1) * GLA_CHUNK]
                    on = _rmsnorm(oh, gn_ref[:, hd * DV_C:(hd + 1) * DV_C])
                    gate = p_ref[rc, O_GC + hd * DV_C:O_GC + (hd + 1) * DV_C]
                    y_ref[rc, Y_C + hd * DV_C:Y_C + (hd + 1) * DV_C] = (on * _silu(gate)).astype(jnp.bfloat16)
                yield

    def half_mixers(p_ref, y_ref, first):
        for s in range(n_sub):
            yield from mixers(p_ref, y_ref, s, first if s == 0 else 0)

    def run_phase(sections, n_sections, chunks):
        done = 0
        for k in range(n_sections):
            next(sections)
            upto = (k + 1) * len(chunks) // n_sections
            for run in chunks[done:upto]:
                run()
            done = upto
        assert next(sections, None) is None

    n_sections = n_sub * MIXER_SECTIONS
    run_phase(half_mixers(pa_ref, ya_ref, jnp.where(i_id == 0, 1, 0)), n_sections,
              in_proj_chunks(hb_ref, pb_ref))
    run_phase(half_mixers(pb_ref, yb_ref, 0), n_sections,
              out_proj_chunks(ya_ref, 0) + in_proj_chunks(hn_ref, pa_ref))
    final_norm(0)
    for run in out_proj_chunks(yb_ref, half):
        run()
    final_norm(half)

    @pl.when(i_id == n_i - 1)
    def _():
        ko_ref[...] = pb_ref[half - WINDOW:half, O_KA:O_KA + LANES].T
        vo_ref[...] = pb_ref[half - WINDOW:half, O_VA:O_VA + LANES].T
        for p in range(H_C // 2):
            so_ref[p] = st_ref[p].T


def _prompt_layer(x, ng, win, wout, bias_tab, sinks, sw, bsf, lng, lnb, wup, bup, gn, fg, *, layer, final):
    bsz, seq, _ = x.shape
    tb = min(TOKEN_BLOCK, seq)
    half = tb // 2
    assert seq % tb == 0 and half % WINDOW == 0
    n_i = seq // tb
    grid = (bsz, n_i)

    def next_half(b, i):
        lin = jnp.minimum(b * n_i + i + 1, bsz * n_i - 1)
        return lin // n_i, (lin % n_i) * 2, 0

    def const_spec(shape):
        nd = len(shape)
        return pl.BlockSpec(shape, lambda b, i: (0,) * nd, pipeline_mode=pl.Buffered(1))

    def layer_spec(shape):
        nd = len(shape)
        return pl.BlockSpec((None,) + shape, lambda b, i: (layer,) + (0,) * nd, pipeline_mode=pl.Buffered(1))

    kern = functools.partial(_prompt_layer_kernel, tb=tb, layer=layer, final=final)
    return pl.pallas_call(
        kern,
        grid=grid,
        in_specs=[
            pl.BlockSpec((None, tb, D_MODEL), lambda b, i: (b, i, 0)),
            pl.BlockSpec((None, half, D_MODEL), next_half),
            layer_spec((1, D_MODEL)),
            layer_spec((D_MODEL, D_IN_PAD)),
            layer_spec((D_MIX, D_MODEL)),
            const_spec((2, KV_A, G_A * WINDOW, 2 * WINDOW)),
            pl.BlockSpec(memory_space=pltpu.SMEM),
            layer_spec((H_B, CHUNK_B, CHUNK_B)),
            layer_spec((CHUNK_B, D_B)),
            layer_spec((1, D_B)),
            layer_spec((1, D_B)),
            layer_spec((LANES, DK_TOT)),
            layer_spec((1, DK_TOT)),
            layer_spec((1, D_C)),
            const_spec((1, D_MODEL)),
        ],
        out_specs=[
            pl.BlockSpec((None, tb, D_MODEL), lambda b, i: (b, i, 0)),
            pl.BlockSpec((None, LANES, WINDOW), lambda b, i: (b, 0, 0)),
            pl.BlockSpec((None, LANES, WINDOW), lambda b, i: (b, 0, 0)),
            pl.BlockSpec((None, H_C // 2, LANES, DV_C), lambda b, i: (b, 0, 0, 0)),
        ],
        out_shape=[
            jax.ShapeDtypeStruct((bsz, seq, D_MODEL), jnp.float32),
            jax.ShapeDtypeStruct((bsz, LANES, WINDOW), jnp.float32),
            jax.ShapeDtypeStruct((bsz, LANES, WINDOW), jnp.float32),
            jax.ShapeDtypeStruct((bsz, H_C // 2, LANES, DV_C), jnp.float32),
        ],
        scratch_shapes=[
            pltpu.VMEM((half, D_IN_PAD), jnp.float32),
            pltpu.VMEM((half, D_IN_PAD), jnp.float32),
            pltpu.VMEM((half, D_MODEL), jnp.bfloat16),
            pltpu.VMEM((half, D_MODEL), jnp.bfloat16),
            pltpu.VMEM((half, D_MIX), jnp.bfloat16),
            pltpu.VMEM((half, D_MIX), jnp.bfloat16),
            pltpu.VMEM((KV_A, WINDOW, LANES), jnp.bfloat16),
            pltpu.VMEM((WINDOW, LANES), jnp.bfloat16),
            pltpu.VMEM((H_C // 2, DV_C, LANES), jnp.float32),
            pltpu.VMEM((H_B // 2, CHUNK_B, 2 * CHUNK_B), jnp.bfloat16),
        ],
        compiler_params=pltpu.CompilerParams(
            dimension_semantics=("arbitrary", "arbitrary"),
            vmem_limit_bytes=VMEM_LIMIT_BYTES),
        name="prompt_layer",
    )(x, x, ng, win, wout, bias_tab, sinks, sw, bsf, lng, lnb, wup, bup, gn, fg)


def _sample_kernel(x_ref, ng_ref, win_ref, wout_ref, biass_ref, sinkc_ref, w00_ref, b0_ref, lng_ref,
                   lnb_ref, wup_ref, bup_ref, gn_ref, fg_ref, bk_ref, bv_ref, s0_ref,
                   xo_ref, nk_ref, nv_ref, so_ref, cv_ref,
                   xcur_ref, proj_ref, y_ref, qs_ref, os_ref, qd_ref, ea_ref, oc_ref, *, nseq, gs):
    l_id = pl.program_id(0)
    i_id = pl.program_id(1)
    n_l = pl.num_programs(0)
    n_i = pl.num_programs(1)
    lane = lax.broadcasted_iota(jnp.int32, (nseq, LANES), 1)
    low = lane < HALF

    @pl.when((l_id == 0) & (i_id == 0))
    def _():
        xcur_ref[...] = x_ref[...]

    @pl.when(i_id == 0)
    def _():
        h = _rmsnorm(xcur_ref[...], ng_ref[...])
        proj_ref[...] = _bdot(h, win_ref[...])
        for g in range(G_A):
            qp = proj_ref[:, O_QA + g * LANES:O_QA + (g + 1) * LANES] * (HD_A ** -0.5)
            for kv in range(KV_A):
                keep = low if kv == 0 else jnp.logical_not(low)
                qs_ref[pl.ds(kv * G_A + g, nseq, stride=H_A), :] = jnp.where(keep, qp, 0.0)
        vn = _layernorm(proj_ref[:, O_VB:O_VB + D_B], lng_ref[...], lnb_ref[...])
        cv_ref[...] = vn
        mix = w00_ref[...] * vn + b0_ref[...]
        y_ref[:, Y_B:Y_B + D_B] = (proj_ref[:, O_UB:O_UB + D_B] * mix) * _silu(proj_ref[:, O_GB:O_GB + D_B])
        z = _bdot(proj_ref[:, O_LR:O_LR + LANES], wup_ref[...]) + bup_ref[...]
        la = _log_sigmoid(z) * (1.0 / GLA_TAU)
        ea = jnp.exp(la)
        qd = proj_ref[:, O_QC:O_QC + DK_TOT] * (DK_C ** -0.5) * ea
        ki = proj_ref[:, O_KC:O_KC + DK_TOT] * jnp.exp(-la)
        qd_ref[...] = qd
        ea_ref[...] = ea
        prod = qd * ki
        lane_c = lax.broadcasted_iota(jnp.int32, prod.shape, 1)
        for hd in range(H_C):
            in_head = (lane_c >= hd * DK_C) & (lane_c < (hd + 1) * DK_C)
            att = jnp.sum(jnp.where(in_head, prod, 0.0), axis=-1, keepdims=True)
            oc_ref[:, hd * DV_C:(hd + 1) * DV_C] = att * proj_ref[:, O_VC + hd * DV_C:O_VC + (hd + 1) * DV_C]

    g0 = pl.multiple_of(i_id * gs, gs)
    grows = pl.ds(g0, gs)

    pad_rows = jnp.zeros((LANES - gs, LANES), jnp.float32)
    k_cols = jnp.concatenate([proj_ref[grows, O_KA:O_KA + LANES], pad_rows], axis=0).T
    v_cols = jnp.concatenate([proj_ref[grows, O_VA:O_VA + LANES], pad_rows], axis=0).T
    newest = lax.broadcasted_iota(jnp.int32, (LANES, WINDOW), 1) == WINDOW - 1
    for j in range(gs):
        kt = jnp.where(newest, k_cols[:, j:j + 1], pltpu.roll(bk_ref[j], WINDOW - 1, 1))
        vt = jnp.where(newest, v_cols[:, j:j + 1], pltpu.roll(bv_ref[j], WINDOW - 1, 1))
        nk_ref[j] = kt
        nv_ref[j] = vt
        r8 = pl.ds(pl.multiple_of((g0 + j) * H_A, H_A), H_A)
        sc = _bdot(qs_ref[r8, :], kt) + biass_ref[...]
        p, inv = _softmax_sink(sc, sinkc_ref[...])
        os_ref[r8, :] = _bdot_nt(p, vt) * inv

    for hd in range(H_C):
        v_h = proj_ref[grows, O_VC + hd * DV_C:O_VC + (hd + 1) * DV_C]
        acc = jnp.zeros((gs, DV_C), jnp.float32)
        for d in range(DK_C):
            c = hd * DK_C + d
            s_rows = s0_ref[pl.ds(c, gs, stride=DK_TOT), :]
            acc = acc + qd_ref[grows, c:c + 1] * s_rows
            so_ref[pl.ds(c, gs, stride=DK_TOT), :] = (ea_ref[grows, c:c + 1] * s_rows
                                                     + proj_ref[grows, O_KC + c:O_KC + c + 1] * v_h)
        oc_ref[grows, hd * DV_C:(hd + 1) * DV_C] += acc

    @pl.when(i_id == n_i - 1)
    def _():
        for g in range(G_A):
            o0 = os_ref[pl.ds(g, nseq, stride=H_A), :]
            o1 = os_ref[pl.ds(G_A + g, nseq, stride=H_A), :]
            gate = proj_ref[:, O_GA + g * LANES:O_GA + (g + 1) * LANES]
            y_ref[:, Y_A + g * LANES:Y_A + (g + 1) * LANES] = jnp.where(low, o0, o1) * _silu(gate)
        for hd in range(H_C):
            on = _rmsnorm(oc_ref[:, hd * DV_C:(hd + 1) * DV_C], gn_ref[:, hd * DV_C:(hd + 1) * DV_C])
            gate = proj_ref[:, O_GC + hd * DV_C:O_GC + (hd + 1) * DV_C]
            y_ref[:, Y_C + hd * DV_C:Y_C + (hd + 1) * DV_C] = on * _silu(gate)
        x_new = xcur_ref[...] + _bdot(y_ref[...], wout_ref[...])
        xcur_ref[...] = x_new

        @pl.when(l_id == n_l - 1)
        def _():
            xo_ref[...] = _rmsnorm(x_new, fg_ref[...])


def _sample_path(x, ng, win, wout, bias_s, sink_c, w00, b0, lng, lnb, wup, bup, gn, fg, buf_kt, buf_vt, s0):
    depth = win.shape[0]
    nseq = x.shape[0]
    gs = min(SEQ_GROUP, nseq)
    assert nseq % gs == 0 and gs % 8 == 0
    grid = (depth, nseq // gs)

    def const_spec(shape):
        nd = len(shape)
        return pl.BlockSpec(shape, lambda l, i: (0,) * nd)

    def layer_spec(shape):
        nd = len(shape)
        return pl.BlockSpec((None,) + shape, lambda l, i: (l,) + (0,) * nd)

    state_spec = pl.BlockSpec((None, gs, LANES, WINDOW), lambda l, i: (l, i, 0, 0))
    gla_spec = pl.BlockSpec((None, gs * DK_TOT, DV_C), lambda l, i: (l, i, 0))
    kern = functools.partial(_sample_kernel, nseq=nseq, gs=gs)
    return pl.pallas_call(
        kern,
        grid=grid,
        in_specs=[
            const_spec((nseq, D_MODEL)),
            layer_spec((1, D_MODEL)),
            layer_spec((D_MODEL, D_IN_PAD)),
            layer_spec((D_MIX, D_MODEL)),
            const_spec((H_A, WINDOW)),
            layer_spec((H_A, 1)),
            layer_spec((1, D_B)),
            layer_spec((1, D_B)),
            layer_spec((1, D_B)),
            layer_spec((1, D_B)),
            layer_spec((LANES, DK_TOT)),
            layer_spec((1, DK_TOT)),
            layer_spec((1, D_C)),
            const_spec((1, D_MODEL)),
            state_spec, state_spec, gla_spec,
        ],
        out_specs=[
            const_spec((nseq, D_MODEL)),
            state_spec, state_spec, gla_spec,
            layer_spec((nseq, D_B)),
        ],
        out_shape=[
            jax.ShapeDtypeStruct((nseq, D_MODEL), jnp.float32),
            jax.ShapeDtypeStruct((depth, nseq, LANES, WINDOW), jnp.float32),
            jax.ShapeDtypeStruct((depth, nseq, LANES, WINDOW), jnp.float32),
            jax.ShapeDtypeStruct((depth, nseq * DK_TOT, DV_C), jnp.float32),
            jax.ShapeDtypeStruct((depth, nseq, D_B), jnp.float32),
        ],
        scratch_shapes=[
            pltpu.VMEM((nseq, D_MODEL), jnp.float32),
            pltpu.VMEM((nseq, D_IN_PAD), jnp.float32),
            pltpu.VMEM((nseq, D_MIX), jnp.float32),
            pltpu.VMEM((nseq * H_A, LANES), jnp.float32),
            pltpu.VMEM((nseq * H_A, LANES), jnp.float32),
            pltpu.VMEM((nseq, DK_TOT), jnp.float32),
            pltpu.VMEM((nseq, DK_TOT), jnp.float32),
            pltpu.VMEM((nseq, D_C), jnp.float32),
        ],
        compiler_params=pltpu.CompilerParams(
            dimension_semantics=("arbitrary", "arbitrary"),
            vmem_limit_bytes=VMEM_LIMIT_BYTES),
        name="sample_path",
    )(x, ng, win, wout, bias_s, sink_c, w00, b0, lng, lnb, wup, bup, gn, fg, buf_kt, buf_vt, s0)


def _window_minor(state):
    depth, n = state.shape[:2]
    return jnp.transpose(state, (0, 1, 3, 4, 2)).reshape(depth, n, LANES, WINDOW)


def _window_major(state_t):
    depth, n = state_t.shape[:2]
    return jnp.transpose(state_t.reshape(depth, n, KV_A, HD_A, WINDOW), (0, 1, 4, 2, 3))


def kernel(x_prompt, x_sample, state_swa_k, state_swa_v, state_gla, rel_bias, norm_g, w_in, sinks, spatial_w,
           spatial_b, chunk_ln_g, chunk_ln_b, gla_w_up, gla_b_up, gla_norm_g, w_out, final_norm_g):
    depth = w_in.shape[0]
    nseq = x_sample.shape[0]
    bsz = x_prompt.shape[0]
    f32 = jnp.float32

    win, wout = _prepare_weights(w_in, w_out)
    wup = jnp.pad(gla_w_up, ((0, 0), (0, LANES - GLA_RANK), (0, 0))).astype(jnp.bfloat16)
    bsf = jnp.repeat(jnp.swapaxes(spatial_b, 1, 2), DH_B, axis=2).astype(f32)
    w00 = jnp.repeat(spatial_w[:, :, 0, 0], DH_B, axis=1).astype(f32)[:, None, :]
    b0 = jnp.repeat(spatial_b[:, :, 0], DH_B, axis=1).astype(f32)[:, None, :]
    sink_c = sinks.astype(f32)[:, :, None]
    row = lambda a: a.astype(f32)[:, None, :]
    ng, lng, lnb, bup, gn = row(norm_g), row(chunk_ln_g), row(chunk_ln_b), row(gla_b_up), row(gla_norm_g)
    fg = final_norm_g.astype(f32)[None, :]
    sw = spatial_w.astype(f32)
    sinks = sinks.astype(f32)

    bias_tab = _bias_table(rel_bias)
    bias_s = bias_tab[0].reshape(H_A, WINDOW, 2 * WINDOW)[:, WINDOW - 1, WINDOW:]

    xp = x_prompt
    kp_l, vp_l, sp_l = [], [], []
    for l in range(depth):
        xp, kp, vp, sp = _prompt_layer(xp, ng, win, wout, bias_tab, sinks, sw, bsf, lng, lnb, wup, bup, gn, fg,
                                       layer=l, final=l == depth - 1)
        kp_l.append(kp); vp_l.append(vp); sp_l.append(sp)

    xs, ks_t, vs_t, ss, cv = _sample_path(
        x_sample.reshape(nseq, D_MODEL), ng, win, wout, bias_s, sink_c, w00, b0, lng, lnb, wup, bup, gn, fg,
        _window_minor(state_swa_k), _window_minor(state_swa_v), state_gla.reshape(depth, nseq * DK_TOT, DV_C))

    return (xp,
            xs.reshape(nseq, 1, D_MODEL),
            _window_major(jnp.stack(kp_l)),
            _window_major(jnp.stack(vp_l)),
            jnp.stack(sp_l).reshape(depth, bsz, H_C, DK_C, DV_C),
            _window_major(ks_t),
            _window_major(vs_t),
            ss.reshape(depth, nseq, H_C, DK_C, DV_C),
            cv.reshape(depth, nseq, 1, D_B))
```

```python
import functools

import numpy as np
import jax
import jax.numpy as jnp
from jax import lax
from jax.experimental import pallas as pl
from jax.experimental.pallas import tpu as pltpu

D_MODEL = 1024
D_A, HD_A, H_A, KV_A, G_A = 512, 64, 8, 2, 4
WINDOW, N_BUCKETS, MAX_DIST = 128, 32, 128
D_B, H_B, DH_B, CHUNK_B = 512, 8, 64, 128
D_C, H_C, DK_TOT, DK_C, DV_C = 512, 4, 256, 64, 128
GLA_RANK, GLA_TAU, GLA_CHUNK = 16, 16.0, 64
D_MIX = D_A + D_B + D_C
EPS = 1e-6
NEG = -1e30
SPLITS = [D_A, KV_A * HD_A, KV_A * HD_A, D_A, D_B, D_B, D_B, DK_TOT, DK_TOT, D_C, D_C, GLA_RANK]
D_IN = sum(SPLITS)

LANES = 128
HALF = LANES // 2
D_IN_PAD = ((D_IN + LANES - 1) // LANES) * LANES
VMEM_LIMIT_BYTES = 56 * 1024 * 1024

O_QA, O_KA, O_VA, O_GA = 0, 512, 640, 768
O_UB, O_VB, O_GB = 1280, 1792, 2304
O_QC, O_KC, O_VC, O_GC, O_LR = 2816, 3072, 3328, 3840, 4352
Y_A, Y_B, Y_C = 0, D_A, D_A + D_B

HEAD_PERM = [0, 4, 1, 5, 2, 6, 3, 7]

TOKEN_BLOCK = 512
SEQ_GROUP = 16
IN_PROJ_CHUNK = 256
OUT_PROJ_CHUNK = 256
MIXER_SECTIONS = 22
PREP_SPLIT = 2

_NT = (((1,), (1,)), ((), ()))


def _t5_bucket(dist):
    n = np.maximum(dist, 0)
    max_exact = N_BUCKETS // 2
    large = max_exact + (np.log(np.maximum(n, 1) / max_exact) / np.log(MAX_DIST / max_exact)
                         * (N_BUCKETS - max_exact)).astype(np.int32)
    large = np.minimum(large, N_BUCKETS - 1)
    return np.where(n < max_exact, n, large).astype(np.int32)


def _silu(x):
    return x * (1.0 / (1.0 + jnp.exp(-x)))


def _log_sigmoid(x):
    return jnp.minimum(x, 0.0) - jnp.log1p(jnp.exp(-jnp.abs(x)))


def _bdot(a, b):
    return jnp.dot(a.astype(jnp.bfloat16), b.astype(jnp.bfloat16), preferred_element_type=jnp.float32)


def _bdot_nt(a, b):
    return lax.dot_general(a.astype(jnp.bfloat16), b.astype(jnp.bfloat16), _NT,
                           preferred_element_type=jnp.float32)


def _rmsnorm(x, g):
    return x * lax.rsqrt(jnp.mean(x * x, axis=-1, keepdims=True) + EPS) * g


def _layernorm(v, g, b):
    mu = jnp.mean(v, axis=-1, keepdims=True)
    xc = v - mu
    var = jnp.mean(xc * xc, axis=-1, keepdims=True)
    return xc * lax.rsqrt(var + EPS) * g + b


def _softmax_sink(s, sink):
    m = jnp.maximum(jnp.max(s, axis=-1, keepdims=True), sink)
    p = jnp.exp(s - m)
    den = jnp.sum(p, axis=-1, keepdims=True) + jnp.exp(sink - m)
    return p, 1.0 / den


def _bias_table_kernel(rb_ref, bucket_ref, band_ref, out_ref):
    bucket = bucket_ref[...]
    band = band_ref[...] > 0
    own = lax.broadcasted_iota(jnp.int32, bucket.shape, 1) >= WINDOW
    for h in range(H_A):
        acc = jnp.zeros(bucket.shape, jnp.float32)
        for b in range(N_BUCKETS):
            acc = jnp.where(bucket == b, rb_ref[b, h], acc)
        kv, g = divmod(h, G_A)
        rows = pl.ds(g * WINDOW, WINDOW)
        out_ref[0, kv, rows, :] = jnp.where(band, acc, NEG)
        out_ref[1, kv, rows, :] = jnp.where(band & own, acc, NEG)


def _bias_table(rel_bias):
    i = np.arange(WINDOW)[:, None]
    j = np.arange(2 * WINDOW)[None, :]
    dist = i + WINDOW - j
    band = ((dist >= 0) & (dist < WINDOW)).astype(np.int32)
    return pl.pallas_call(
        _bias_table_kernel,
        out_shape=jax.ShapeDtypeStruct((2, KV_A, G_A * WINDOW, 2 * WINDOW), jnp.float32),
        in_specs=[pl.BlockSpec(memory_space=pltpu.SMEM),
                  pl.BlockSpec(memory_space=pltpu.VMEM),
                  pl.BlockSpec(memory_space=pltpu.VMEM)],
        out_specs=pl.BlockSpec(memory_space=pltpu.VMEM),
        name="rel_bias_table",
    )(rel_bias.astype(jnp.float32), jnp.asarray(_t5_bucket(dist)), jnp.asarray(band))


def _win_prep_kernel(wt_ref, lr_ref, o_ref):
    c = pl.program_id(1)
    tiles = wt_ref.shape[0] // LANES
    for cc in range(PREP_SPLIT):
        @pl.when(c == cc)
        def _(cc=cc):
            for t in range(tiles):
                col = (cc * tiles + t) * LANES
                base = next((b for b in (O_QA, O_GA) if b <= col < b + D_A), None)
                if base is None:
                    src = wt_ref[t * LANES:(t + 1) * LANES, :]
                else:
                    g = (col - base) // LANES
                    r0 = base - cc * tiles * LANES
                    assert r0 >= 0 and r0 + D_A <= wt_ref.shape[0]
                    src = jnp.concatenate([wt_ref[r0 + g * HD_A:r0 + (g + 1) * HD_A, :],
                                           wt_ref[r0 + (G_A + g) * HD_A:r0 + (G_A + g + 1) * HD_A, :]], axis=0)
                o_ref[:, col:col + LANES] = src.T.astype(jnp.bfloat16)
            if cc == PREP_SPLIT - 1:
                o_ref[:, O_LR:D_IN_PAD] = lr_ref[...].T.astype(jnp.bfloat16)


def _wout_prep_kernel(w_ref, o_ref):
    for j, h in enumerate(HEAD_PERM):
        o_ref[j * HD_A:(j + 1) * HD_A, :] = w_ref[h * HD_A:(h + 1) * HD_A, :].astype(jnp.bfloat16)
    o_ref[Y_B:, :] = w_ref[Y_B:, :].astype(jnp.bfloat16)


def _prepare_weights(w_in, w_out):
    depth = w_in.shape[0]
    cols = D_MODEL // PREP_SPLIT
    wt = jnp.swapaxes(w_in, 1, 2)
    assert O_LR % (PREP_SPLIT * LANES) == 0
    rows = O_LR // PREP_SPLIT
    wt_lr = jnp.pad(wt[:, O_LR:, :], ((0, 0), (0, D_IN_PAD - D_IN), (0, 0)))
    win = pl.pallas_call(
        _win_prep_kernel,
        grid=(depth, PREP_SPLIT),
        in_specs=[pl.BlockSpec((None, rows, D_MODEL), lambda l, r: (l, r, 0)),
                  pl.BlockSpec((None, LANES, D_MODEL), lambda l, r: (l, 0, 0))],
        out_specs=pl.BlockSpec((None, D_MODEL, D_IN_PAD), lambda l, r: (l, 0, 0)),
        out_shape=jax.ShapeDtypeStruct((depth, D_MODEL, D_IN_PAD), jnp.bfloat16),
        compiler_params=pltpu.CompilerParams(dimension_semantics=("arbitrary", "arbitrary"),
                                             vmem_limit_bytes=VMEM_LIMIT_BYTES),
        name="w_in_prep",
    )(wt, wt_lr)
    wout = pl.pallas_call(
        _wout_prep_kernel,
        grid=(depth, PREP_SPLIT),
        in_specs=[pl.BlockSpec((None, D_MIX, cols), lambda l, c: (l, 0, c))],
        out_specs=pl.BlockSpec((None, D_MIX, cols), lambda l, c: (l, 0, c)),
        out_shape=jax.ShapeDtypeStruct((depth, D_MIX, D_MODEL), jnp.bfloat16),
        compiler_params=pltpu.CompilerParams(dimension_semantics=("arbitrary", "arbitrary"),
                                             vmem_limit_bytes=VMEM_LIMIT_BYTES),
        name="w_out_prep",
    )(w_out)
    return win, wout


def _prompt_layer_kernel(x_ref, xn_ref, ng_ref, win_ref, wout_ref, bias_ref, sink_ref, sw_ref, bsf_ref, lng_ref,
                         lnb_ref, wup_ref, bup_ref, gn_ref, fg_ref,
                         xo_ref, ko_ref, vo_ref, so_ref,
                         pa_ref, pb_ref, hb_ref, hn_ref, ya_ref, yb_ref, kprev_ref, vprev_ref, st_ref, wm_ref,
                         *, tb, layer, final):
    b_id = pl.program_id(0)
    i_id = pl.program_id(1)
    n_i = pl.num_programs(1)
    half = tb // 2
    n_sub = half // WINDOW

    @pl.when((b_id == 0) & (i_id == 0))
    def _():
        r = lax.broadcasted_iota(jnp.int32, (CHUNK_B, CHUNK_B), 0)
        c = lax.broadcasted_iota(jnp.int32, (CHUNK_B, CHUNK_B), 1)
        for h in range(H_B):
            wm_ref[h // 2, :, (h % 2) * CHUNK_B:(h % 2 + 1) * CHUNK_B] = jnp.where(
                c <= r, sw_ref[h], 0.0).astype(jnp.bfloat16)
        pa_ref[...] = _bdot(_rmsnorm(x_ref[0:half, :], ng_ref[...]), win_ref[...])

    @pl.when(i_id == 0)
    def _():
        kprev_ref[...] = jnp.zeros_like(kprev_ref)
        vprev_ref[...] = jnp.zeros_like(vprev_ref)
        st_ref[...] = jnp.zeros_like(st_ref)

    hb_ref[...] = _rmsnorm(x_ref[half:tb, :], ng_ref[...]).astype(jnp.bfloat16)
    hn_ref[...] = _rmsnorm(xn_ref[...], ng_ref[...]).astype(jnp.bfloat16)

    def in_proj_chunks(h_ref, p_ref):
        def chunk(c0, c1):
            def run():
                p_ref[:, c0:c1] = jnp.dot(h_ref[...], win_ref[:, c0:c1], preferred_element_type=jnp.float32)
            return run
        edges = list(range(0, D_IN_PAD, IN_PROJ_CHUNK)) + [D_IN_PAD]
        return [chunk(c0, c1) for c0, c1 in zip(edges[:-1], edges[1:])]

    def out_proj_chunks(y_ref, r0):
        def chunk(c0, c1):
            def run():
                xo_ref[r0:r0 + half, c0:c1] = x_ref[r0:r0 + half, c0:c1] + jnp.dot(
                    y_ref[...], wout_ref[:, c0:c1], preferred_element_type=jnp.float32)
            return run
        return [chunk(c0, c0 + OUT_PROJ_CHUNK) for c0 in range(0, D_MODEL, OUT_PROJ_CHUNK)]

    def final_norm(r0):
        if final:
            xo_ref[r0:r0 + half, :] = _rmsnorm(xo_ref[r0:r0 + half, :], fg_ref[...])

    lane = lax.broadcasted_iota(jnp.int32, (WINDOW, LANES), 1)
    low = lane < HALF
    low64 = lax.broadcasted_iota(jnp.int32, (GLA_CHUNK, LANES), 1) < HALF
    ri = lax.broadcasted_iota(jnp.int32, (LANES, LANES), 0)
    ci = lax.broadcasted_iota(jnp.int32, (LANES, LANES), 1)
    bd_tril = ((ri >= GLA_CHUNK) == (ci >= GLA_CHUNK)) & (ci <= ri)
    cum_mat = jnp.where(bd_tril, 1.0, 0.0).astype(jnp.bfloat16)

    def mixers(p_ref, y_ref, s, first):
        r0 = s * WINDOW
        rows = pl.ds(r0, WINDOW)

        ks = p_ref[rows, O_KA:O_KA + LANES] * (HD_A ** -0.5)
        k_new = [jnp.where(low, ks, 0.0).astype(jnp.bfloat16), jnp.where(low, 0.0, ks).astype(jnp.bfloat16)]
        v_new = p_ref[rows, O_VA:O_VA + LANES].astype(jnp.bfloat16)
        vcat = jnp.concatenate([vprev_ref[...], v_new], axis=0)
        q4 = jnp.concatenate([p_ref[rows, O_QA + g * LANES:O_QA + (g + 1) * LANES].astype(jnp.bfloat16)
                              for g in range(G_A)], axis=0)
        o_heads = []
        for kv in range(KV_A):
            kcat = jnp.concatenate([kprev_ref[kv], k_new[kv]], axis=0)
            sc = lax.dot_general(q4, kcat, _NT, preferred_element_type=jnp.float32) + bias_ref[first, kv]
            kprev_ref[kv] = k_new[kv]
            yield
            ps, invs = [], []
            for g in range(G_A):
                p, inv = _softmax_sink(sc[g * WINDOW:(g + 1) * WINDOW], sink_ref[layer, kv * G_A + g])
                ps.append(p.astype(jnp.bfloat16))
                invs.append(inv)
                if g % 2 == 1:
                    yield
            o4 = jnp.dot(jnp.concatenate(ps, axis=0), vcat, preferred_element_type=jnp.float32)
            o_heads.append([o4[g * WINDOW:(g + 1) * WINDOW] * invs[g] for g in range(G_A)])
            yield
        vprev_ref[...] = v_new
        for g in range(G_A):
            ya = jnp.where(low, o_heads[0][g], o_heads[1][g])
            gate = p_ref[rows, O_GA + g * LANES:O_GA + (g + 1) * LANES]
            y_ref[rows, Y_A + g * LANES:Y_A + (g + 1) * LANES] = (ya * _silu(gate)).astype(jnp.bfloat16)
        yield

        vn = _layernorm(p_ref[rows, O_VB:O_VB + D_B], lng_ref[...], lnb_ref[...]).astype(jnp.bfloat16)
        yield
        for j in range(H_B // 2):
            vp = vn[:, j * LANES:(j + 1) * LANES]
            zero = jnp.zeros_like(vp)
            v_st = jnp.concatenate([jnp.where(low, vp, zero), jnp.where(low, zero, vp)], axis=0)
            mix = (jnp.dot(wm_ref[j], v_st, preferred_element_type=jnp.float32)
                   + bsf_ref[:, j * LANES:(j + 1) * LANES])
            u = p_ref[rows, O_UB + j * LANES:O_UB + (j + 1) * LANES]
            gate = p_ref[rows, O_GB + j * LANES:O_GB + (j + 1) * LANES]
            y_ref[rows, Y_B + j * LANES:Y_B + (j + 1) * LANES] = (u * mix * _silu(gate)).astype(jnp.bfloat16)
            if j % 2 == 1:
                yield

        z = _bdot(p_ref[rows, O_LR:O_LR + LANES], wup_ref[...]) + bup_ref[...]
        la = _log_sigmoid(z) * (1.0 / GLA_TAU)
        yield
        hi = la.astype(jnp.bfloat16)
        r1 = la - hi.astype(jnp.float32)
        mid = r1.astype(jnp.bfloat16)
        lo = (r1 - mid.astype(jnp.float32)).astype(jnp.bfloat16)
        c3 = jnp.dot(cum_mat, jnp.concatenate([hi, mid, lo], axis=1), preferred_element_type=jnp.float32)
        bc = (c3[:, 0:DK_TOT] + c3[:, DK_TOT:2 * DK_TOT]) + c3[:, 2 * DK_TOT:3 * DK_TOT]
        yield
        for c in range(WINDOW // GLA_CHUNK):
            rc = pl.ds(r0 + c * GLA_CHUNK, GLA_CHUNK)
            for p in range(H_C // 2):
                bp = bc[c * GLA_CHUNK:(c + 1) * GLA_CHUNK, p * LANES:(p + 1) * LANES]
                qp = p_ref[rc, O_QC + p * LANES:O_QC + (p + 1) * LANES]
                kp = p_ref[rc, O_KC + p * LANES:O_KC + (p + 1) * LANES]
                b_last = bp[GLA_CHUNK - 1:GLA_CHUNK, :]
                qd = qp * (DK_C ** -0.5) * jnp.exp(bp)
                ki = (kp * jnp.exp(-bp)).astype(jnp.bfloat16)
                kd = kp * jnp.exp(b_last - bp)
                q_st = jnp.concatenate([jnp.where(low64, qd, 0.0), jnp.where(low64, 0.0, qd)],
                                       axis=0).astype(jnp.bfloat16)
                st = st_ref[p]
                rhs = jnp.concatenate([ki, ki, st.astype(jnp.bfloat16)], axis=0)
                res = _bdot_nt(q_st, rhs)
                att = jnp.where(bd_tril, res[:, 0:LANES], 0.0).astype(jnp.bfloat16)
                yield
                v0 =p_ref[rc, O_VC + (2 * p) * DV_C:O_VC + (2 * p + 1) * DV_C]
                v1 = p_ref[rc, O_VC + (2 * p + 1) * DV_C:O_VC + (2 * p + 2) * DV_C]
                v_st = jnp.concatenate([v0, v1], axis=0)
                o = jnp.dot(att, v_st.astype(jnp.bfloat16), preferred_element_type=jnp.float32) + res[:, LANES:]
                kd_st = jnp.concatenate([jnp.where(low64, kd, 0.0), jnp.where(low64, 0.0, kd)],
                                        axis=0).astype(jnp.bfloat16)
                d_st = jnp.dot(v_st.T.astype(jnp.bfloat16), kd_st, preferred_element_type=jnp.float32)
                st_ref[p] = st * jnp.exp(b_last) + d_st
                for hh in range(2):
                    hd = 2 * p + hh
                    oh = o[hh * GLA_CHUNK:(hh + 1) * GLA_CHUNK]
                    on = _rmsnorm(oh, gn_ref[:, hd * DV_C:(hd + 1) * DV_C])
                    gate = p_ref[rc, O_GC + hd * DV_C:O_GC + (hd + 1) * DV_C]
                    y_ref[rc, Y_C + hd * DV_C:Y_C + (hd + 1) * DV_C] = (on * _silu(gate)).astype(jnp.bfloat16)
                yield

    def half_mixers(p_ref, y_ref, first):
        for s in range(n_sub):
            yield from mixers(p_ref, y_ref, s, first if s == 0 else 0)

    def run_phase(sections, n_sections, chunks):
        done = 0
        for k in range(n_sections):
            next(sections)
            upto = (k + 1) * len(chunks) // n_sections
            for run in chunks[done:upto]:
                run()
            done = upto
        assert next(sections, None) is None

    n_sections = n_sub * MIXER_SECTIONS
    run_phase(half_mixers(pa_ref, ya_ref, jnp.where(i_id == 0, 1, 0)), n_sections,
              in_proj_chunks(hb_ref, pb_ref))
    run_phase(half_mixers(pb_ref, yb_ref, 0), n_sections,
              out_proj_chunks(ya_ref, 0) + in_proj_chunks(hn_ref, pa_ref))
    final_norm(0)
    for run in out_proj_chunks(yb_ref, half):
        run()
    final_norm(half)

    @pl.when(i_id == n_i - 1)
    def _():
        ko_ref[...] = pb_ref[half - WINDOW:half, O_KA:O_KA + LANES].T
        vo_ref[...] = pb_ref[half - WINDOW:half, O_VA:O_VA + LANES].T
        for p in range(H_C // 2):
            so_ref[p] = st_ref[p].T


def _prompt_layer(x, ng, win, wout, bias_tab, sinks, sw, bsf, lng, lnb, wup, bup, gn, fg, *, layer, final):
    bsz, seq, _ = x.shape
    tb = min(TOKEN_BLOCK, seq)
    half = tb // 2
    assert seq % tb == 0 and half % WINDOW == 0
    n_i = seq // tb
    grid = (bsz, n_i)

    def next_half(b, i):
        lin = jnp.minimum(b * n_i + i + 1, bsz * n_i - 1)
        return lin // n_i, (lin % n_i) * 2, 0

    def const_spec(shape):
        nd = len(shape)
        return pl.BlockSpec(shape, lambda b, i: (0,) * nd, pipeline_mode=pl.Buffered(1))

    def layer_spec(shape):
        nd = len(shape)
        return pl.BlockSpec((None,) + shape, lambda b, i: (layer,) + (0,) * nd, pipeline_mode=pl.Buffered(1))

    kern = functools.partial(_prompt_layer_kernel, tb=tb, layer=layer, final=final)
    return pl.pallas_call(
        kern,
        grid=grid,
        in_specs=[
            pl.BlockSpec((None, tb, D_MODEL), lambda b, i: (b, i, 0)),
            pl.BlockSpec((None, half, D_MODEL), next_half),
            layer_spec((1, D_MODEL)),
            layer_spec((D_MODEL, D_IN_PAD)),
            layer_spec((D_MIX, D_MODEL)),
            const_spec((2, KV_A, G_A * WINDOW, 2 * WINDOW)),
            pl.BlockSpec(memory_space=pltpu.SMEM),
            layer_spec((H_B, CHUNK_B, CHUNK_B)),
            layer_spec((CHUNK_B, D_B)),
            layer_spec((1, D_B)),
            layer_spec((1, D_B)),
            layer_spec((LANES, DK_TOT)),
            layer_spec((1, DK_TOT)),
            layer_spec((1, D_C)),
            const_spec((1, D_MODEL)),
        ],
        out_specs=[
            pl.BlockSpec((None, tb, D_MODEL), lambda b, i: (b, i, 0)),
            pl.BlockSpec((None, LANES, WINDOW), lambda b, i: (b, 0, 0)),
            pl.BlockSpec((None, LANES, WINDOW), lambda b, i: (b, 0, 0)),
            pl.BlockSpec((None, H_C // 2, LANES, DV_C), lambda b, i: (b, 0, 0, 0)),
        ],
        out_shape=[
            jax.ShapeDtypeStruct((bsz, seq, D_MODEL), jnp.float32),
            jax.ShapeDtypeStruct((bsz, LANES, WINDOW), jnp.float32),
            jax.ShapeDtypeStruct((bsz, LANES, WINDOW), jnp.float32),
            jax.ShapeDtypeStruct((bsz, H_C // 2, LANES, DV_C), jnp.float32),
        ],
        scratch_shapes=[
            pltpu.VMEM((half, D_IN_PAD), jnp.float32),
            pltpu.VMEM((half, D_IN_PAD), jnp.float32),
            pltpu.VMEM((half, D_MODEL), jnp.bfloat16),
            pltpu.VMEM((half, D_MODEL), jnp.bfloat16),
            pltpu.VMEM((half, D_MIX), jnp.bfloat16),
            pltpu.VMEM((half, D_MIX), jnp.bfloat16),
            pltpu.VMEM((KV_A, WINDOW, LANES), jnp.bfloat16),
            pltpu.VMEM((WINDOW, LANES), jnp.bfloat16),
            pltpu.VMEM((H_C // 2, DV_C, LANES), jnp.float32),
            pltpu.VMEM((H_B // 2, CHUNK_B, 2 * CHUNK_B), jnp.bfloat16),
        ],
        compiler_params=pltpu.CompilerParams(
            dimension_semantics=("arbitrary", "arbitrary"),
            vmem_limit_bytes=VMEM_LIMIT_BYTES),
        name="prompt_layer",
    )(x, x, ng, win, wout, bias_tab, sinks, sw, bsf, lng, lnb, wup, bup, gn, fg)


def _sample_kernel(x_ref, ng_ref, win_ref, wout_ref, biass_ref, sinkc_ref, w00_ref, b0_ref, lng_ref,
                   lnb_ref, wup_ref, bup_ref, gn_ref, fg_ref, bk_ref, bv_ref, s0_ref,
                   xo_ref, nk_ref, nv_ref, so_ref, cv_ref,
                   xcur_ref, proj_ref, y_ref, qs_ref, os_ref, qd_ref, ea_ref, oc_ref, *, nseq, gs):
    l_id = pl.program_id(0)
    i_id = pl.program_id(1)
    n_l = pl.num_programs(0)
    n_i = pl.num_programs(1)
    lane = lax.broadcasted_iota(jnp.int32, (nseq, LANES), 1)
    low = lane < HALF

    @pl.when((l_id == 0) & (i_id == 0))
    def _():
        xcur_ref[...] = x_ref[...]

    @pl.when(i_id == 0)
    def _():
        h = _rmsnorm(xcur_ref[...], ng_ref[...])
        proj_ref[...] = _bdot(h, win_ref[...])
        for g in range(G_A):
            qp = proj_ref[:, O_QA + g * LANES:O_QA + (g + 1) * LANES] * (HD_A ** -0.5)
            for kv in range(KV_A):
                keep = low if kv == 0 else jnp.logical_not(low)
                qs_ref[pl.ds(kv * G_A + g, nseq, stride=H_A), :] = jnp.where(keep, qp, 0.0)
        vn = _layernorm(proj_ref[:, O_VB:O_VB + D_B], lng_ref[...], lnb_ref[...])
        cv_ref[...] = vn
        mix = w00_ref[...] * vn + b0_ref[...]
        y_ref[:, Y_B:Y_B + D_B] = (proj_ref[:, O_UB:O_UB + D_B] * mix) * _silu(proj_ref[:, O_GB:O_GB + D_B])
        z = _bdot(proj_ref[:, O_LR:O_LR + LANES], wup_ref[...]) + bup_ref[...]
        la = _log_sigmoid(z) * (1.0 / GLA_TAU)
        ea = jnp.exp(la)
        qd = proj_ref[:, O_QC:O_QC + DK_TOT] * (DK_C ** -0.5) * ea
        ki = proj_ref[:, O_KC:O_KC + DK_TOT] * jnp.exp(-la)
        qd_ref[...] = qd
        ea_ref[...] = ea
        prod = qd * ki
        lane_c = lax.broadcasted_iota(jnp.int32, prod.shape, 1)
        for hd in range(H_C):
            in_head = (lane_c >= hd * DK_C) & (lane_c < (hd + 1) * DK_C)
            att = jnp.sum(jnp.where(in_head, prod, 0.0), axis=-1, keepdims=True)
            oc_ref[:, hd * DV_C:(hd + 1) * DV_C] = att * proj_ref[:, O_VC + hd * DV_C:O_VC + (hd + 1) * DV_C]

    g0 = pl.multiple_of(i_id * gs, gs)
    grows = pl.ds(g0, gs)

    pad_rows = jnp.zeros((LANES - gs, LANES), jnp.float32)
    k_cols = jnp.concatenate([proj_ref[grows, O_KA:O_KA + LANES], pad_rows], axis=0).T
    v_cols = jnp.concatenate([proj_ref[grows, O_VA:O_VA + LANES], pad_rows], axis=0).T
    newest = lax.broadcasted_iota(jnp.int32, (LANES, WINDOW), 1) == WINDOW - 1
    for j in range(gs):
        kt = jnp.where(newest, k_cols[:, j:j + 1], pltpu.roll(bk_ref[j], WINDOW - 1, 1))
        vt = jnp.where(newest, v_cols[:, j:j + 1], pltpu.roll(bv_ref[j], WINDOW - 1, 1))
        nk_ref[j] = kt
        nv_ref[j] = vt
        r8 = pl.ds(pl.multiple_of((g0 + j) * H_A, H_A), H_A)
        sc = _bdot(qs_ref[r8, :], kt) + biass_ref[...]
        p, inv = _softmax_sink(sc, sinkc_ref[...])
        os_ref[r8, :] = _bdot_nt(p, vt) * inv

    pad_c = jnp.zeros((LANES - gs, DK_TOT), jnp.float32)
    ea_cols = jnp.concatenate([ea_ref[grows, :], pad_c], axis=0).T
    k_cols_c = jnp.concatenate([proj_ref[grows, O_KC:O_KC + DK_TOT], pad_c], axis=0).T
    qd_g = qd_ref[grows, :]
    row_g = lax.broadcasted_iota(jnp.int32, (gs, LANES), 0)
    low_g = lax.broadcasted_iota(jnp.int32, (gs, LANES), 1) < HALF
    seqs_per_dot = 2 * LANES // DK_C
    for hd in range(H_C):
        v_h = proj_ref[grows, O_VC + hd * DV_C:O_VC + (hd + 1) * DV_C]
        q_pair = qd_g[:, (hd // 2) * LANES:(hd // 2 + 1) * LANES]
        q_other = pltpu.roll(q_pair, HALF, 1)
        q_lo, q_hi = (q_pair, q_other) if hd % 2 == 0 else (q_other, q_pair)
        acc = jnp.zeros((gs, DV_C), jnp.float32)
        for j0 in range(0, gs, seqs_per_dot):
            tiles, lhs = [], []
            for j in range(j0, j0 + seqs_per_dot):
                srows = pl.ds(j * DK_TOT + hd * DK_C, DK_C)
                s_old = s0_ref[srows, :]
                tiles.append(s_old.astype(jnp.bfloat16))
                crows = slice(hd * DK_C, (hd + 1) * DK_C)
                so_ref[srows, :] = ea_cols[crows, j:j + 1] * s_old + k_cols_c[crows, j:j + 1] * v_h[j:j + 1, :]
            for j in range(j0, j0 + seqs_per_dot, 2):
                lhs.append(jnp.where((row_g == j) & low_g, q_lo,
                                     jnp.where((row_g == j + 1) & jnp.logical_not(low_g), q_hi, 0.0)))
            acc = acc + _bdot(jnp.concatenate(lhs, axis=1), jnp.concatenate(tiles, axis=0))
        oc_ref[grows, hd * DV_C:(hd + 1) * DV_C] += acc

    @pl.when(i_id == n_i - 1)
    def _():
        for g in range(G_A):
            o0 = os_ref[pl.ds(g, nseq, stride=H_A), :]
            o1 = os_ref[pl.ds(G_A + g, nseq, stride=H_A), :]
            gate = proj_ref[:, O_GA + g * LANES:O_GA + (g + 1) * LANES]
            y_ref[:, Y_A + g * LANES:Y_A + (g + 1) * LANES] = jnp.where(low, o0, o1) * _silu(gate)
        for hd in range(H_C):
            on = _rmsnorm(oc_ref[:, hd * DV_C:(hd + 1) * DV_C], gn_ref[:, hd * DV_C:(hd + 1) * DV_C])
            gate = proj_ref[:, O_GC + hd * DV_C:O_GC + (hd + 1) * DV_C]
            y_ref[:, Y_C + hd * DV_C:Y_C + (hd + 1) * DV_C] = on * _silu(gate)
        x_new = xcur_ref[...] + _bdot(y_ref[...], wout_ref[...])
        xcur_ref[...] = x_new

        @pl.when(l_id == n_l - 1)
        def _():
            xo_ref[...] = _rmsnorm(x_new, fg_ref[...])


def _sample_path(x, ng, win, wout, bias_s, sink_c, w00, b0, lng, lnb, wup, bup, gn, fg, buf_kt, buf_vt, s0):
    depth = win.shape[0]
    nseq = x.shape[0]
    gs = min(SEQ_GROUP, nseq)
    assert nseq % gs == 0 and gs % 8 == 0
    grid = (depth, nseq // gs)

    def const_spec(shape):
        nd = len(shape)
        return pl.BlockSpec(shape, lambda l, i: (0,) * nd)

    def layer_spec(shape):
        nd = len(shape)
        return pl.BlockSpec((None,) + shape, lambda l, i: (l,) + (0,) * nd)

    state_spec = pl.BlockSpec((None, gs, LANES, WINDOW), lambda l, i: (l, i, 0, 0))
    gla_spec = pl.BlockSpec((None, gs * DK_TOT, DV_C), lambda l, i: (l, i, 0))
    kern = functools.partial(_sample_kernel, nseq=nseq, gs=gs)
    return pl.pallas_call(
        kern,
        grid=grid,
        in_specs=[
            const_spec((nseq, D_MODEL)),
            layer_spec((1, D_MODEL)),
            layer_spec((D_MODEL, D_IN_PAD)),
            layer_spec((D_MIX, D_MODEL)),
            const_spec((H_A, WINDOW)),
            layer_spec((H_A, 1)),
            layer_spec((1, D_B)),
            layer_spec((1, D_B)),
            layer_spec((1, D_B)),
            layer_spec((1, D_B)),
            layer_spec((LANES, DK_TOT)),
            layer_spec((1, DK_TOT)),
            layer_spec((1, D_C)),
            const_spec((1, D_MODEL)),
            state_spec, state_spec, gla_spec,
        ],
        out_specs=[
            const_spec((nseq, D_MODEL)),
            state_spec, state_spec, gla_spec,
            layer_spec((nseq, D_B)),
        ],
        out_shape=[
            jax.ShapeDtypeStruct((nseq, D_MODEL), jnp.float32),
            jax.ShapeDtypeStruct((depth, nseq, LANES, WINDOW), jnp.float32),
            jax.ShapeDtypeStruct((depth, nseq, LANES, WINDOW), jnp.float32),
            jax.ShapeDtypeStruct((depth, nseq * DK_TOT, DV_C), jnp.float32),
            jax.ShapeDtypeStruct((depth, nseq, D_B), jnp.float32),
        ],
        scratch_shapes=[
            pltpu.VMEM((nseq, D_MODEL), jnp.float32),
            pltpu.VMEM((nseq, D_IN_PAD), jnp.float32),
            pltpu.VMEM((nseq, D_MIX), jnp.float32),
            pltpu.VMEM((nseq * H_A, LANES), jnp.float32),
            pltpu.VMEM((nseq * H_A, LANES), jnp.float32),
            pltpu.VMEM((nseq, DK_TOT), jnp.float32),
            pltpu.VMEM((nseq, DK_TOT), jnp.float32),
            pltpu.VMEM((nseq, D_C), jnp.float32),
        ],
        compiler_params=pltpu.CompilerParams(
            dimension_semantics=("arbitrary", "arbitrary"),
            vmem_limit_bytes=VMEM_LIMIT_BYTES),
        name="sample_path",
    )(x, ng, win, wout, bias_s, sink_c, w00, b0, lng, lnb, wup, bup, gn, fg, buf_kt, buf_vt, s0)


def _window_minor(state):
    depth, n = state.shape[:2]
    return jnp.transpose(state, (0, 1, 3, 4, 2)).reshape(depth, n, LANES, WINDOW)


def _window_major(state_t):
    depth, n = state_t.shape[:2]
    return jnp.transpose(state_t.reshape(depth, n, KV_A, HD_A, WINDOW), (0, 1, 4, 2, 3))


def kernel(x_prompt, x_sample, state_swa_k, state_swa_v, state_gla, rel_bias, norm_g, w_in, sinks, spatial_w,
           spatial_b, chunk_ln_g, chunk_ln_b, gla_w_up, gla_b_up, gla_norm_g, w_out, final_norm_g):
    depth = w_in.shape[0]
    nseq = x_sample.shape[0]
    bsz = x_prompt.shape[0]
    f32 = jnp.float32

    win, wout = _prepare_weights(w_in, w_out)
    wup = jnp.pad(gla_w_up, ((0, 0), (0, LANES - GLA_RANK), (0, 0))).astype(jnp.bfloat16)
    bsf = jnp.repeat(jnp.swapaxes(spatial_b, 1, 2), DH_B, axis=2).astype(f32)
    w00 = jnp.repeat(spatial_w[:, :, 0, 0], DH_B, axis=1).astype(f32)[:, None, :]
    b0 = jnp.repeat(spatial_b[:, :, 0], DH_B, axis=1).astype(f32)[:, None, :]
    sink_c = sinks.astype(f32)[:, :, None]
    row = lambda a: a.astype(f32)[:, None, :]
    ng, lng, lnb, bup, gn = row(norm_g), row(chunk_ln_g), row(chunk_ln_b), row(gla_b_up), row(gla_norm_g)
    fg = final_norm_g.astype(f32)[None, :]
    sw = spatial_w.astype(f32)
    sinks = sinks.astype(f32)

    bias_tab = _bias_table(rel_bias)
    bias_s = bias_tab[0].reshape(H_A, WINDOW, 2 * WINDOW)[:, WINDOW - 1, WINDOW:]

    xp = x_prompt
    kp_l, vp_l, sp_l = [], [], []
    for l in range(depth):
        xp, kp, vp, sp = _prompt_layer(xp, ng, win, wout, bias_tab, sinks, sw, bsf, lng, lnb, wup, bup, gn, fg,
                                       layer=l, final=l == depth - 1)
        kp_l.append(kp); vp_l.append(vp); sp_l.append(sp)

    xs, ks_t, vs_t, ss, cv = _sample_path(
        x_sample.reshape(nseq, D_MODEL), ng, win, wout, bias_s, sink_c, w00, b0, lng, lnb, wup, bup, gn, fg,
        _window_minor(state_swa_k), _window_minor(state_swa_v), state_gla.reshape(depth, nseq * DK_TOT, DV_C))

    return (xp,
            xs.reshape(nseq, 1, D_MODEL),
            _window_major(jnp.stack(kp_l)),
            _window_major(jnp.stack(vp_l)),
            jnp.stack(sp_l).reshape(depth, bsz, H_C, DK_C, DV_C),
            _window_major(ks_t),
            _window_major(vs_t),
            ss.reshape(depth, nseq, H_C, DK_C, DV_C),
            cv.reshape(depth, nseq, 1, D_B))
```

```python
import functools

import numpy as np
import jax
import jax.numpy as jnp
from jax import lax
from jax.experimental import pallas as pl
from jax.experimental.pallas import tpu as pltpu

D_MODEL = 1024
D_A, HD_A, H_A, KV_A, G_A = 512, 64, 8, 2, 4
WINDOW, N_BUCKETS, MAX_DIST = 128, 32, 128
D_B, H_B, DH_B, CHUNK_B = 512, 8, 64, 128
D_C, H_C, DK_TOT, DK_C, DV_C = 512, 4, 256, 64, 128
GLA_RANK, GLA_TAU, GLA_CHUNK = 16, 16.0, 64
D_MIX = D_A + D_B + D_C
EPS = 1e-6
NEG = -1e30
SPLITS = [D_A, KV_A * HD_A, KV_A * HD_A, D_A, D_B, D_B, D_B, DK_TOT, DK_TOT, D_C, D_C, GLA_RANK]
D_IN = sum(SPLITS)

LANES = 128
HALF = LANES // 2
D_IN_PAD = ((D_IN + LANES - 1) // LANES) * LANES
VMEM_LIMIT_BYTES = 56 * 1024 * 1024

O_QA, O_KA, O_VA, O_GA = 0, 512, 640, 768
O_UB, O_VB, O_GB = 1280, 1792, 2304
O_QC, O_KC, O_VC, O_GC, O_LR = 2816, 3072, 3328, 3840, 4352
Y_A, Y_B, Y_C = 0, D_A, D_A + D_B

HEAD_PERM = [0, 4, 1, 5, 2, 6, 3, 7]

TOKEN_BLOCK = 512
SEQ_GROUP = 16
IN_PROJ_CHUNK = 256
OUT_PROJ_CHUNK = 256
MIXER_SECTIONS = 22
PREP_SPLIT = 2

_NT = (((1,), (1,)), ((), ()))


def _t5_bucket(dist):
    n = np.maximum(dist, 0)
    max_exact = N_BUCKETS // 2
    large = max_exact + (np.log(np.maximum(n, 1) / max_exact) / np.log(MAX_DIST / max_exact)
                         * (N_BUCKETS - max_exact)).astype(np.int32)
    large = np.minimum(large, N_BUCKETS - 1)
    return np.where(n < max_exact, n, large).astype(np.int32)


def _silu(x):
    return x * (1.0 / (1.0 + jnp.exp(-x)))


def _log_sigmoid(x):
    return jnp.minimum(x, 0.0) - jnp.log1p(jnp.exp(-jnp.abs(x)))


def _bdot(a, b):
    return jnp.dot(a.astype(jnp.bfloat16), b.astype(jnp.bfloat16), preferred_element_type=jnp.float32)


def _bdot_nt(a, b):
    return lax.dot_general(a.astype(jnp.bfloat16), b.astype(jnp.bfloat16), _NT,
                           preferred_element_type=jnp.float32)


def _rmsnorm(x, g):
    return x * lax.rsqrt(jnp.mean(x * x, axis=-1, keepdims=True) + EPS) * g


def _layernorm(v, g, b):
    mu = jnp.mean(v, axis=-1, keepdims=True)
    xc = v - mu
    var = jnp.mean(xc * xc, axis=-1, keepdims=True)
    return xc * lax.rsqrt(var + EPS) * g + b


def _softmax_sink(s, sink):
    m = jnp.maximum(jnp.max(s, axis=-1, keepdims=True), sink)
    p = jnp.exp(s - m)
    den = jnp.sum(p, axis=-1, keepdims=True) + jnp.exp(sink - m)
    return p, 1.0 / den


def _bias_table_kernel(rb_ref, bucket_ref, band_ref, out_ref):
    bucket = bucket_ref[...]
    band = band_ref[...] > 0
    own = lax.broadcasted_iota(jnp.int32, bucket.shape, 1) >= WINDOW
    for h in range(H_A):
        acc = jnp.zeros(bucket.shape, jnp.float32)
        for b in range(N_BUCKETS):
            acc = jnp.where(bucket == b, rb_ref[b, h], acc)
        kv, g = divmod(h, G_A)
        rows = pl.ds(g * WINDOW, WINDOW)
        out_ref[0, kv, rows, :] = jnp.where(band, acc, NEG)
        out_ref[1, kv, rows, :] = jnp.where(band & own, acc, NEG)


def _bias_table(rel_bias):
    i = np.arange(WINDOW)[:, None]
    j = np.arange(2 * WINDOW)[None, :]
    dist = i + WINDOW - j
    band = ((dist >= 0) & (dist < WINDOW)).astype(np.int32)
    return pl.pallas_call(
        _bias_table_kernel,
        out_shape=jax.ShapeDtypeStruct((2, KV_A, G_A * WINDOW, 2 * WINDOW), jnp.float32),
        in_specs=[pl.BlockSpec(memory_space=pltpu.SMEM),
                  pl.BlockSpec(memory_space=pltpu.VMEM),
                  pl.BlockSpec(memory_space=pltpu.VMEM)],
        out_specs=pl.BlockSpec(memory_space=pltpu.VMEM),
        name="rel_bias_table",
    )(rel_bias.astype(jnp.float32), jnp.asarray(_t5_bucket(dist)), jnp.asarray(band))


def _win_prep_kernel(wt_ref, lr_ref, o_ref):
    c = pl.program_id(1)
    tiles = wt_ref.shape[0] // LANES
    for cc in range(PREP_SPLIT):
        @pl.when(c == cc)
        def _(cc=cc):
            for t in range(tiles):
                col = (cc * tiles + t) * LANES
                base = next((b for b in (O_QA, O_GA) if b <= col < b + D_A), None)
                if base is None:
                    src = wt_ref[t * LANES:(t + 1) * LANES, :]
                else:
                    g = (col - base) // LANES
                    r0 = base - cc * tiles * LANES
                    assert r0 >= 0 and r0 + D_A <= wt_ref.shape[0]
                    src = jnp.concatenate([wt_ref[r0 + g * HD_A:r0 + (g + 1) * HD_A, :],
                                           wt_ref[r0 + (G_A + g) * HD_A:r0 + (G_A + g + 1) * HD_A, :]], axis=0)
                o_ref[:, col:col + LANES] = src.T.astype(jnp.bfloat16)
            if cc == PREP_SPLIT - 1:
                o_ref[:, O_LR:D_IN_PAD] = lr_ref[...].T.astype(jnp.bfloat16)


def _wout_prep_kernel(w_ref, o_ref):
    for j, h in enumerate(HEAD_PERM):
        o_ref[j * HD_A:(j + 1) * HD_A, :] = w_ref[h * HD_A:(h + 1) * HD_A, :].astype(jnp.bfloat16)
    o_ref[Y_B:, :] = w_ref[Y_B:, :].astype(jnp.bfloat16)


def _prepare_weights(w_in, w_out):
    depth = w_in.shape[0]
    cols = D_MODEL // PREP_SPLIT
    wt = jnp.swapaxes(w_in, 1, 2)
    assert O_LR % (PREP_SPLIT * LANES) == 0
    rows = O_LR // PREP_SPLIT
    wt_lr = jnp.pad(wt[:, O_LR:, :], ((0, 0), (0, D_IN_PAD - D_IN), (0, 0)))
    win = pl.pallas_call(
        _win_prep_kernel,
        grid=(depth, PREP_SPLIT),
        in_specs=[pl.BlockSpec((None, rows, D_MODEL), lambda l, r: (l, r, 0)),
                  pl.BlockSpec((None, LANES, D_MODEL), lambda l, r: (l, 0, 0))],
        out_specs=pl.BlockSpec((None, D_MODEL, D_IN_PAD), lambda l, r: (l, 0, 0)),
        out_shape=jax.ShapeDtypeStruct((depth, D_MODEL, D_IN_PAD), jnp.bfloat16),
        compiler_params=pltpu.CompilerParams(dimension_semantics=("arbitrary", "arbitrary"),
                                             vmem_limit_bytes=VMEM_LIMIT_BYTES),
        name="w_in_prep",
    )(wt, wt_lr)
    wout = pl.pallas_call(
        _wout_prep_kernel,
        grid=(depth, PREP_SPLIT),
        in_specs=[pl.BlockSpec((None, D_MIX, cols), lambda l, c: (l, 0, c))],
        out_specs=pl.BlockSpec((None, D_MIX, cols), lambda l, c: (l, 0, c)),
        out_shape=jax.ShapeDtypeStruct((depth, D_MIX, D_MODEL), jnp.bfloat16),
        compiler_params=pltpu.CompilerParams(dimension_semantics=("arbitrary", "arbitrary"),
                                             vmem_limit_bytes=VMEM_LIMIT_BYTES),
        name="w_out_prep",
    )(w_out)
    return win, wout


def _prompt_layer_kernel(x_ref, xn_ref, ng_ref, win_ref, wout_ref, bias_ref, sink_ref, sw_ref, bsf_ref, lng_ref,
                         lnb_ref, wup_ref, bup_ref, gn_ref, fg_ref,
                         xo_ref, ko_ref, vo_ref, so_ref,
                         pa_ref, pb_ref, hb_ref, hn_ref, ya_ref, yb_ref, kprev_ref, vprev_ref, st_ref, wm_ref,
                         *, tb, layer, final):
    b_id = pl.program_id(0)
    i_id = pl.program_id(1)
    n_i = pl.num_programs(1)
    half = tb // 2
    n_sub = half // WINDOW

    @pl.when((b_id == 0) & (i_id == 0))
    def _():
        r = lax.broadcasted_iota(jnp.int32, (CHUNK_B, CHUNK_B), 0)
        c = lax.broadcasted_iota(jnp.int32, (CHUNK_B, CHUNK_B), 1)
        for h in range(H_B):
            wm_ref[h // 2, :, (h % 2) * CHUNK_B:(h % 2 + 1) * CHUNK_B] = jnp.where(
                c <= r, sw_ref[h], 0.0).astype(jnp.bfloat16)
        pa_ref[...] = _bdot(_rmsnorm(x_ref[0:half, :], ng_ref[...]), win_ref[...])

    @pl.when(i_id == 0)
    def _():
        kprev_ref[...] = jnp.zeros_like(kprev_ref)
        vprev_ref[...] = jnp.zeros_like(vprev_ref)
        st_ref[...] = jnp.zeros_like(st_ref)

    hb_ref[...] = _rmsnorm(x_ref[half:tb, :], ng_ref[...]).astype(jnp.bfloat16)
    hn_ref[...] = _rmsnorm(xn_ref[...], ng_ref[...]).astype(jnp.bfloat16)

    def in_proj_chunks(h_ref, p_ref):
        def chunk(c0, c1):
            def run():
                p_ref[:, c0:c1] = jnp.dot(h_ref[...], win_ref[:, c0:c1], preferred_element_type=jnp.float32)
            return run
        edges = list(range(0, D_IN_PAD, IN_PROJ_CHUNK)) + [D_IN_PAD]
        return [chunk(c0, c1) for c0, c1 in zip(edges[:-1], edges[1:])]

    def out_proj_chunks(y_ref, r0, s):
        def chunk(c0, c1):
            def run():
                rows = slice(r0 + s * WINDOW, r0 + (s + 1) * WINDOW)
                xo_ref[rows, c0:c1] = x_ref[rows, c0:c1] + jnp.dot(
                    y_ref[s * WINDOW:(s + 1) * WINDOW, :], wout_ref[:, c0:c1],
                    preferred_element_type=jnp.float32)
            return run
        return [chunk(c0, c0 + OUT_PROJ_CHUNK) for c0 in range(0, D_MODEL, OUT_PROJ_CHUNK)]

    def final_norm(r0):
        if final:
            xo_ref[r0:r0 + half, :] = _rmsnorm(xo_ref[r0:r0 + half, :], fg_ref[...])

    lane = lax.broadcasted_iota(jnp.int32, (WINDOW, LANES), 1)
    low = lane < HALF
    low64 = lax.broadcasted_iota(jnp.int32, (GLA_CHUNK, LANES), 1) < HALF
    ri = lax.broadcasted_iota(jnp.int32, (LANES, LANES), 0)
    ci = lax.broadcasted_iota(jnp.int32, (LANES, LANES), 1)
    bd_tril = ((ri >= GLA_CHUNK) == (ci >= GLA_CHUNK)) & (ci <= ri)
    cum_mat = jnp.where(bd_tril, 1.0, 0.0).astype(jnp.bfloat16)

    def mixers(p_ref, y_ref, s, first):
        r0 = s * WINDOW
        rows = pl.ds(r0, WINDOW)

        ks = p_ref[rows, O_KA:O_KA + LANES] * (HD_A ** -0.5)
        k_new = [jnp.where(low, ks, 0.0).astype(jnp.bfloat16), jnp.where(low, 0.0, ks).astype(jnp.bfloat16)]
        v_new = p_ref[rows, O_VA:O_VA + LANES].astype(jnp.bfloat16)
        vcat = jnp.concatenate([vprev_ref[...], v_new], axis=0)
        q4 = jnp.concatenate([p_ref[rows, O_QA + g * LANES:O_QA + (g + 1) * LANES].astype(jnp.bfloat16)
                              for g in range(G_A)], axis=0)
        o_heads = []
        for kv in range(KV_A):
            kcat = jnp.concatenate([kprev_ref[kv], k_new[kv]], axis=0)
            sc = lax.dot_general(q4, kcat, _NT, preferred_element_type=jnp.float32) + bias_ref[first, kv]
            kprev_ref[kv] = k_new[kv]
            yield
            ps, invs = [], []
            for g in range(G_A):
                p, inv = _softmax_sink(sc[g * WINDOW:(g + 1) * WINDOW], sink_ref[layer, kv * G_A + g])
                ps.append(p.astype(jnp.bfloat16))
                invs.append(inv)
                if g % 2 == 1:
                    yield
            o4 = jnp.dot(jnp.concatenate(ps, axis=0), vcat, preferred_element_type=jnp.float32)
            o_heads.append([o4[g * WINDOW:(g + 1) * WINDOW] * invs[g] for g in range(G_A)])
            yield
        vprev_ref[...] = v_new
        for g in range(G_A):
            ya = jnp.where(low, o_heads[0][g], o_heads[1][g])
            gate = p_ref[rows, O_GA + g * LANES:O_GA + (g + 1) * LANES]
            y_ref[rows, Y_A + g * LANES:Y_A + (g + 1) * LANES] = (ya * _silu(gate)).astype(jnp.bfloat16)
        yield

        vn = _layernorm(p_ref[rows, O_VB:O_VB + D_B], lng_ref[...], lnb_ref[...]).astype(jnp.bfloat16)
        yield
        for j in range(H_B // 2):
            vp = vn[:, j * LANES:(j + 1) * LANES]
            zero = jnp.zeros_like(vp)
            v_st = jnp.concatenate([jnp.where(low, vp, zero), jnp.where(low, zero, vp)], axis=0)
            mix = (jnp.dot(wm_ref[j], v_st, preferred_element_type=jnp.float32)
                   + bsf_ref[:, j * LANES:(j + 1) * LANES])
            u = p_ref[rows, O_UB + j * LANES:O_UB + (j + 1) * LANES]
            gate = p_ref[rows, O_GB + j * LANES:O_GB + (j + 1) * LANES]
            y_ref[rows, Y_B + j * LANES:Y_B + (j + 1) * LANES] = (u * mix * _silu(gate)).astype(jnp.bfloat16)
            if j % 2 == 1:
                yield

        z = _bdot(p_ref[rows, O_LR:O_LR + LANES], wup_ref[...]) + bup_ref[...]
        la = _log_sigmoid(z) * (1.0 / GLA_TAU)
        yield
        hi = la.astype(jnp.bfloat16)
        r1 = la - hi.astype(jnp.float32)
        mid = r1.astype(jnp.bfloat16)
        lo = (r1 - mid.astype(jnp.float32)).astype(jnp.bfloat16)
        c3 = jnp.dot(cum_mat, jnp.concatenate([hi, mid, lo], axis=1), preferred_element_type=jnp.float32)
        bc = (c3[:, 0:DK_TOT] + c3[:, DK_TOT:2 * DK_TOT]) + c3[:, 2 * DK_TOT:3 * DK_TOT]
        yield
        for c in range(WINDOW // GLA_CHUNK):
            rc = pl.ds(r0 + c * GLA_CHUNK, GLA_CHUNK)
            atts, o_inter, v_sts, kd_sts, decays, sts = [], [], [], [], [], []
            for p in range(H_C // 2):
                bp = bc[c * GLA_CHUNK:(c + 1) * GLA_CHUNK, p * LANES:(p + 1) * LANES]
                qp = p_ref[rc, O_QC + p * LANES:O_QC + (p + 1) * LANES]
                kp = p_ref[rc, O_KC + p * LANES:O_KC + (p + 1) * LANES]
                b_last = bp[GLA_CHUNK - 1:GLA_CHUNK, :]
                qd = qp * (DK_C ** -0.5) * jnp.exp(bp)
                ki = (kp * jnp.exp(-bp)).astype(jnp.bfloat16)
                kd = kp * jnp.exp(b_last - bp)
                q_st = jnp.concatenate([jnp.where(low64, qd, 0.0), jnp.where(low64, 0.0, qd)],
                                       axis=0).astype(jnp.bfloat16)
                st = st_ref[p]
                rhs = jnp.concatenate([ki, ki, st.astype(jnp.bfloat16)], axis=0)
                res = _bdot_nt(q_st, rhs)
                atts.append(jnp.where(bd_tril, res[:, 0:LANES], 0.0).astype(jnp.bfloat16))
                o_inter.append(res[:, LANES:])
                v0 = p_ref[rc, O_VC + (2 * p) * DV_C:O_VC + (2 * p + 1) * DV_C]
                v1 = p_ref[rc, O_VC + (2 * p + 1) * DV_C:O_VC + (2 * p + 2) * DV_C]
                v_sts.append(jnp.concatenate([v0, v1], axis=0))
                kd_sts.append(jnp.concatenate([jnp.where(low64, kd, 0.0), jnp.where(low64, 0.0, kd)],
                                              axis=0).astype(jnp.bfloat16))
                decays.append(jnp.exp(b_last))
                sts.append(st)
                yield
            zero = jnp.zeros((LANES, LANES), jnp.bfloat16)
            blockdiag = lambda a, b: jnp.concatenate([jnp.concatenate([a, zero], axis=1),
                                                      jnp.concatenate([zero, b], axis=1)], axis=0)
            v_b = [v.astype(jnp.bfloat16) for v in v_sts]
            o_all = jnp.dot(jnp.concatenate(atts, axis=1), blockdiag(*v_b),
                            preferred_element_type=jnp.float32)
            d_all = jnp.dot(jnp.concatenate([v.T.astype(jnp.bfloat16) for v in v_sts], axis=1),
                            blockdiag(*kd_sts), preferred_element_type=jnp.float32)
            for p in range(H_C // 2):
                st_ref[p] = sts[p] * decays[p] + d_all[:, p * LANES:(p + 1) * LANES]
                o = o_all[:, p * DV_C:(p + 1) * DV_C] + o_inter[p]
                for hh in range(2):
                    hd = 2 * p + hh
                    oh = o[hh * GLA_CHUNK:(hh + 1) * GLA_CHUNK]
                    on = _rmsnorm(oh, gn_ref[:, hd * DV_C:(hd + 1) * DV_C])
                    gate = p_ref[rc, O_GC + hd * DV_C:O_GC + (hd + 1) * DV_C]
                    y_ref[rc, Y_C + hd * DV_C:Y_C + (hd + 1) * DV_C] = (on * _silu(gate)).astype(jnp.bfloat16)
                yield

    def half_mixers(p_ref, y_ref, first):
        for s in range(n_sub):
            yield from mixers(p_ref, y_ref, s, first if s == 0 else 0)

    def run_phase(sections, chunks, late_chunks):
        n_sections = n_sub * MIXER_SECTIONS
        plan = [[] for _ in range(n_sections)]
        for idx, run in enumerate(chunks):
            plan[idx * n_sections // len(chunks)].append(run)
        for s, runs in late_chunks.items():
            k0 = (s + 1) * MIXER_SECTIONS
            for idx, run in enumerate(runs):
                plan[k0 + idx * (n_sections - k0) // len(runs)].append(run)
        for k in range(n_sections):
            next(sections)
            for run in plan[k]:
                run()
        assert next(sections, None) is None

    run_phase(half_mixers(pa_ref, ya_ref, jnp.where(i_id == 0, 1, 0)),
              in_proj_chunks(hb_ref, pb_ref),
              {s: out_proj_chunks(ya_ref, 0, s) for s in range(n_sub - 1)})
    run_phase(half_mixers(pb_ref, yb_ref, 0),
              out_proj_chunks(ya_ref, 0, n_sub - 1) + in_proj_chunks(hn_ref, pa_ref),
              {s: out_proj_chunks(yb_ref, half, s) for s in range(n_sub - 1)})
    final_norm(0)
    for run in out_proj_chunks(yb_ref, half, n_sub - 1):
        run()
    final_norm(half)

    @pl.when(i_id == n_i - 1)
    def _():
        ko_ref[...] = pb_ref[half - WINDOW:half, O_KA:O_KA + LANES].T
        vo_ref[...] = pb_ref[half - WINDOW:half, O_VA:O_VA + LANES].T
        for p in range(H_C // 2):
            so_ref[p] = st_ref[p].T


def _prompt_layer(x, ng, win, wout, bias_tab, sinks, sw, bsf, lng, lnb, wup, bup, gn, fg, *, layer, final):
    bsz, seq, _ = x.shape
    tb = min(TOKEN_BLOCK, seq)
    half = tb // 2
    assert seq % tb == 0 and half % WINDOW == 0
    n_i = seq // tb
    grid = (bsz, n_i)

    def next_half(b, i):
        lin = jnp.minimum(b * n_i + i + 1, bsz * n_i - 1)
        return lin // n_i, (lin % n_i) * 2, 0

    def const_spec(shape):
        nd = len(shape)
        return pl.BlockSpec(shape, lambda b, i: (0,) * nd, pipeline_mode=pl.Buffered(1))

    def layer_spec(shape):
        nd = len(shape)
        return pl.BlockSpec((None,) + shape, lambda b, i: (layer,) + (0,) * nd, pipeline_mode=pl.Buffered(1))

    kern = functools.partial(_prompt_layer_kernel, tb=tb, layer=layer, final=final)
    return pl.pallas_call(
        kern,
        grid=grid,
        in_specs=[
            pl.BlockSpec((None, tb, D_MODEL), lambda b, i: (b, i, 0)),
            pl.BlockSpec((None, half, D_MODEL), next_half),
            layer_spec((1, D_MODEL)),
            layer_spec((D_MODEL, D_IN_PAD)),
            layer_spec((D_MIX, D_MODEL)),
            const_spec((2, KV_A, G_A * WINDOW, 2 * WINDOW)),
            pl.BlockSpec(memory_space=pltpu.SMEM),
            layer_spec((H_B, CHUNK_B, CHUNK_B)),
            layer_spec((CHUNK_B, D_B)),
            layer_spec((1, D_B)),
            layer_spec((1, D_B)),
            layer_spec((LANES, DK_TOT)),
            layer_spec((1, DK_TOT)),
            layer_spec((1, D_C)),
            const_spec((1, D_MODEL)),
        ],
        out_specs=[
            pl.BlockSpec((None, tb, D_MODEL), lambda b, i: (b, i, 0)),
            pl.BlockSpec((None, LANES, WINDOW), lambda b, i: (b, 0, 0)),
            pl.BlockSpec((None, LANES, WINDOW), lambda b, i: (b, 0, 0)),
            pl.BlockSpec((None, H_C // 2, LANES, DV_C), lambda b, i: (b, 0, 0, 0)),
        ],
        out_shape=[
            jax.ShapeDtypeStruct((bsz, seq, D_MODEL), jnp.float32),
            jax.ShapeDtypeStruct((bsz, LANES, WINDOW), jnp.float32),
            jax.ShapeDtypeStruct((bsz, LANES, WINDOW), jnp.float32),
            jax.ShapeDtypeStruct((bsz, H_C // 2, LANES, DV_C), jnp.float32),
        ],
        scratch_shapes=[
            pltpu.VMEM((half, D_IN_PAD), jnp.float32),
            pltpu.VMEM((half, D_IN_PAD), jnp.float32),
            pltpu.VMEM((half, D_MODEL), jnp.bfloat16),
            pltpu.VMEM((half, D_MODEL), jnp.bfloat16),
            pltpu.VMEM((half, D_MIX), jnp.bfloat16),
            pltpu.VMEM((half, D_MIX), jnp.bfloat16),
            pltpu.VMEM((KV_A, WINDOW, LANES), jnp.bfloat16),
            pltpu.VMEM((WINDOW, LANES), jnp.bfloat16),
            pltpu.VMEM((H_C // 2, DV_C, LANES), jnp.float32),
            pltpu.VMEM((H_B // 2, CHUNK_B, 2 * CHUNK_B), jnp.bfloat16),
        ],
        compiler_params=pltpu.CompilerParams(
            dimension_semantics=("arbitrary", "arbitrary"),
            vmem_limit_bytes=VMEM_LIMIT_BYTES),
        name="prompt_layer",
    )(x, x, ng, win, wout, bias_tab, sinks, sw, bsf, lng, lnb, wup, bup, gn, fg)


def _sample_kernel(x_ref, ng_ref, win_ref, wout_ref, biass_ref, sinkc_ref, w00_ref, b0_ref, lng_ref,
                   lnb_ref, wup_ref, bup_ref, gn_ref, fg_ref, bk_ref, bv_ref, s0_ref,
                   xo_ref, nk_ref, nv_ref, so_ref, cv_ref,
                   xcur_ref, proj_ref, y_ref, qs_ref, os_ref, qd_ref, ea_ref, oc_ref, *, nseq, gs):
    l_id = pl.program_id(0)
    i_id = pl.program_id(1)
    n_l = pl.num_programs(0)
    n_i = pl.num_programs(1)
    lane = lax.broadcasted_iota(jnp.int32, (nseq, LANES), 1)
    low = lane < HALF

    @pl.when((l_id == 0) & (i_id == 0))
    def _():
        xcur_ref[...] = x_ref[...]

    @pl.when(i_id == 0)
    def _():
        h = _rmsnorm(xcur_ref[...], ng_ref[...])
        proj_ref[...] = _bdot(h, win_ref[...])
        for g in range(G_A):
            qp = proj_ref[:, O_QA + g * LANES:O_QA + (g + 1) * LANES] * (HD_A ** -0.5)
            for kv in range(KV_A):
                keep = low if kv == 0 else jnp.logical_not(low)
                qs_ref[pl.ds(kv * G_A + g, nseq, stride=H_A), :] = jnp.where(keep, qp, 0.0)
        vn = _layernorm(proj_ref[:, O_VB:O_VB + D_B], lng_ref[...], lnb_ref[...])
        cv_ref[...] = vn
        mix = w00_ref[...] * vn + b0_ref[...]
        y_ref[:, Y_B:Y_B + D_B] = (proj_ref[:, O_UB:O_UB + D_B] * mix) * _silu(proj_ref[:, O_GB:O_GB + D_B])
        z = _bdot(proj_ref[:, O_LR:O_LR + LANES], wup_ref[...]) + bup_ref[...]
        la = _log_sigmoid(z) * (1.0 / GLA_TAU)
        ea = jnp.exp(la)
        qd = proj_ref[:, O_QC:O_QC + DK_TOT] * (DK_C ** -0.5) * ea
        ki = proj_ref[:, O_KC:O_KC + DK_TOT] * jnp.exp(-la)
        qd_ref[...] = qd
        ea_ref[...] = ea
        prod = qd * ki
        lane_c = lax.broadcasted_iota(jnp.int32, prod.shape, 1)
        for hd in range(H_C):
            in_head = (lane_c >= hd * DK_C) & (lane_c < (hd + 1) * DK_C)
            att = jnp.sum(jnp.where(in_head, prod, 0.0), axis=-1, keepdims=True)
            oc_ref[:, hd * DV_C:(hd + 1) * DV_C] = att * proj_ref[:, O_VC + hd * DV_C:O_VC + (hd + 1) * DV_C]

    g0 = pl.multiple_of(i_id * gs, gs)
    grows = pl.ds(g0, gs)

    pad_rows = jnp.zeros((LANES - gs, LANES), jnp.float32)
    k_cols = jnp.concatenate([proj_ref[grows, O_KA:O_KA + LANES], pad_rows], axis=0).T
    v_cols = jnp.concatenate([proj_ref[grows, O_VA:O_VA + LANES], pad_rows], axis=0).T
    newest = lax.broadcasted_iota(jnp.int32, (LANES, WINDOW), 1) == WINDOW - 1
    for j in range(gs):
        kt = jnp.where(newest, k_cols[:, j:j + 1], pltpu.roll(bk_ref[j], WINDOW - 1, 1))
        vt = jnp.where(newest, v_cols[:, j:j + 1], pltpu.roll(bv_ref[j], WINDOW - 1, 1))
        nk_ref[j] = kt
        nv_ref[j] = vt
        r8 = pl.ds(pl.multiple_of((g0 + j) * H_A, H_A), H_A)
        sc = _bdot(qs_ref[r8, :], kt) + biass_ref[...]
        p, inv = _softmax_sink(sc, sinkc_ref[...])
        os_ref[r8, :] = _bdot_nt(p, vt) * inv

    pad_c = jnp.zeros((LANES - gs, DK_TOT), jnp.float32)
    ea_cols = jnp.concatenate([ea_ref[grows, :], pad_c], axis=0).T
    k_cols_c = jnp.concatenate([proj_ref[grows, O_KC:O_KC + DK_TOT], pad_c], axis=0).T
    qd_g = qd_ref[grows, :]
    row_g = lax.broadcasted_iota(jnp.int32, (gs, LANES), 0)
    low_g = lax.broadcasted_iota(jnp.int32, (gs, LANES), 1) < HALF
    seqs_per_dot = 2 * LANES // DK_C
    for hd in range(H_C):
        v_h = proj_ref[grows, O_VC + hd * DV_C:O_VC + (hd + 1) * DV_C]
        q_pair = qd_g[:, (hd // 2) * LANES:(hd // 2 + 1) * LANES]
        q_other = pltpu.roll(q_pair, HALF, 1)
        q_lo, q_hi = (q_pair, q_other) if hd % 2 == 0 else (q_other, q_pair)
        acc = jnp.zeros((gs, DV_C), jnp.float32)
        for j0 in range(0, gs, seqs_per_dot):
            tiles, lhs = [], []
            for j in range(j0, j0 + seqs_per_dot):
                srows = pl.ds(j * DK_TOT + hd * DK_C, DK_C)
                s_old = s0_ref[srows, :]
                tiles.append(s_old.astype(jnp.bfloat16))
                crows = slice(hd * DK_C, (hd + 1) * DK_C)
                so_ref[srows, :] = ea_cols[crows, j:j + 1] * s_old + k_cols_c[crows, j:j + 1] * v_h[j:j + 1, :]
            for j in range(j0, j0 + seqs_per_dot, 2):
                lhs.append(jnp.where((row_g == j) & low_g, q_lo,
                                     jnp.where((row_g == j + 1) & jnp.logical_not(low_g), q_hi, 0.0)))
            acc = acc + _bdot(jnp.concatenate(lhs, axis=1), jnp.concatenate(tiles, axis=0))
        oc_ref[grows, hd * DV_C:(hd + 1) * DV_C] += acc

    @pl.when(i_id == n_i - 1)
    def _():
        for g in range(G_A):
            o0 = os_ref[pl.ds(g, nseq, stride=H_A), :]
            o1 = os_ref[pl.ds(G_A + g, nseq, stride=H_A), :]
            gate = proj_ref[:, O_GA + g * LANES:O_GA + (g + 1) * LANES]
            y_ref[:, Y_A + g * LANES:Y_A + (g + 1) * LANES] = jnp.where(low, o0, o1) * _silu(gate)
        for hd in range(H_C):
            on = _rmsnorm(oc_ref[:, hd * DV_C:(hd + 1) * DV_C], gn_ref[:, hd * DV_C:(hd + 1) * DV_C])
            gate = proj_ref[:, O_GC + hd * DV_C:O_GC + (hd + 1) * DV_C]
            y_ref[:, Y_C + hd * DV_C:Y_C + (hd + 1) * DV_C] = on * _silu(gate)
        x_new = xcur_ref[...] + _bdot(y_ref[...], wout_ref[...])
        xcur_ref[...] = x_new

        @pl.when(l_id == n_l - 1)
        def _():
            xo_ref[...] = _rmsnorm(x_new, fg_ref[...])


def _sample_path(x, ng, win, wout, bias_s, sink_c, w00, b0, lng, lnb, wup, bup, gn, fg, buf_kt, buf_vt, s0):
    depth = win.shape[0]
    nseq = x.shape[0]
    gs = min(SEQ_GROUP, nseq)
    assert nseq % gs == 0 and gs % 8 == 0
    grid = (depth, nseq // gs)

    def const_spec(shape):
        nd = len(shape)
        return pl.BlockSpec(shape, lambda l, i: (0,) * nd)

    def layer_spec(shape):
        nd = len(shape)
        return pl.BlockSpec((None,) + shape, lambda l, i: (l,) + (0,) * nd)

    state_spec = pl.BlockSpec((None, gs, LANES, WINDOW), lambda l, i: (l, i, 0, 0))
    gla_spec = pl.BlockSpec((None, gs * DK_TOT, DV_C), lambda l, i: (l, i, 0))
    kern = functools.partial(_sample_kernel, nseq=nseq, gs=gs)
    return pl.pallas_call(
        kern,
        grid=grid,
        in_specs=[
            const_spec((nseq, D_MODEL)),
            layer_spec((1, D_MODEL)),
            layer_spec((D_MODEL, D_IN_PAD)),
            layer_spec((D_MIX, D_MODEL)),
            const_spec((H_A, WINDOW)),
            layer_spec((H_A, 1)),
            layer_spec((1, D_B)),
            layer_spec((1, D_B)),
            layer_spec((1, D_B)),
            layer_spec((1, D_B)),
            layer_spec((LANES, DK_TOT)),
            layer_spec((1, DK_TOT)),
            layer_spec((1, D_C)),
            const_spec((1, D_MODEL)),
            state_spec, state_spec, gla_spec,
        ],
        out_specs=[
            const_spec((nseq, D_MODEL)),
            state_spec, state_spec, gla_spec,
            layer_spec((nseq, D_B)),
        ],
        out_shape=[
            jax.ShapeDtypeStruct((nseq, D_MODEL), jnp.float32),
            jax.ShapeDtypeStruct((depth, nseq, LANES, WINDOW), jnp.float32),
            jax.ShapeDtypeStruct((depth, nseq, LANES, WINDOW), jnp.float32),
            jax.ShapeDtypeStruct((depth, nseq * DK_TOT, DV_C), jnp.float32),
            jax.ShapeDtypeStruct((depth, nseq, D_B), jnp.float32),
        ],
        scratch_shapes=[
            pltpu.VMEM((nseq, D_MODEL), jnp.float32),
            pltpu.VMEM((nseq, D_IN_PAD), jnp.float32),
            pltpu.VMEM((nseq, D_MIX), jnp.float32),
            pltpu.VMEM((nseq * H_A, LANES), jnp.float32),
            pltpu.VMEM((nseq * H_A, LANES), jnp.float32),
            pltpu.VMEM((nseq, DK_TOT), jnp.float32),
            pltpu.VMEM((nseq, DK_TOT), jnp.float32),
            pltpu.VMEM((nseq, D_C), jnp.float32),
        ],
        compiler_params=pltpu.CompilerParams(
            dimension_semantics=("arbitrary", "arbitrary"),
            vmem_limit_bytes=VMEM_LIMIT_BYTES),
        name="sample_path",
    )(x, ng, win, wout, bias_s, sink_c, w00, b0, lng, lnb, wup, bup, gn, fg, buf_kt, buf_vt, s0)


def _window_minor(state):
    depth, n = state.shape[:2]
    return jnp.transpose(state, (0, 1, 3, 4, 2)).reshape(depth, n, LANES, WINDOW)


def _window_major(state_t):
    depth, n = state_t.shape[:2]
    return jnp.transpose(state_t.reshape(depth, n, KV_A, HD_A, WINDOW), (0, 1, 4, 2, 3))


def kernel(x_prompt, x_sample, state_swa_k, state_swa_v, state_gla, rel_bias, norm_g, w_in, sinks, spatial_w,
           spatial_b, chunk_ln_g, chunk_ln_b, gla_w_up, gla_b_up, gla_norm_g, w_out, final_norm_g):
    depth = w_in.shape[0]
    nseq = x_sample.shape[0]
    bsz = x_prompt.shape[0]
    f32 = jnp.float32

    win, wout = _prepare_weights(w_in, w_out)
    wup = jnp.pad(gla_w_up, ((0, 0), (0, LANES - GLA_RANK), (0, 0))).astype(jnp.bfloat16)
    bsf = jnp.repeat(jnp.swapaxes(spatial_b, 1, 2), DH_B, axis=2).astype(f32)
    w00 = jnp.repeat(spatial_w[:, :, 0, 0], DH_B, axis=1).astype(f32)[:, None, :]
    b0 = jnp.repeat(spatial_b[:, :, 0], DH_B, axis=1).astype(f32)[:, None, :]
    sink_c = sinks.astype(f32)[:, :, None]
    row = lambda a: a.astype(f32)[:, None, :]
    ng, lng, lnb, bup, gn = row(norm_g), row(chunk_ln_g), row(chunk_ln_b), row(gla_b_up), row(gla_norm_g)
    fg = final_norm_g.astype(f32)[None, :]
    sw = spatial_w.astype(f32)
    sinks = sinks.astype(f32)

    bias_tab = _bias_table(rel_bias)
    bias_s = bias_tab[0].reshape(H_A, WINDOW, 2 * WINDOW)[:, WINDOW - 1, WINDOW:]

    xp = x_prompt
    kp_l, vp_l, sp_l = [], [], []
    for l in range(depth):
        xp, kp, vp, sp = _prompt_layer(xp, ng, win, wout, bias_tab, sinks, sw, bsf, lng, lnb, wup, bup, gn, fg,
                                       layer=l, final=l == depth - 1)
        kp_l.append(kp); vp_l.append(vp); sp_l.append(sp)

    xs, ks_t, vs_t, ss, cv = _sample_path(
        x_sample.reshape(nseq, D_MODEL), ng, win, wout, bias_s, sink_c, w00, b0, lng, lnb, wup, bup, gn, fg,
        _window_minor(state_swa_k), _window_minor(state_swa_v), state_gla.reshape(depth, nseq * DK_TOT, DV_C))

    return (xp,
            xs.reshape(nseq, 1, D_MODEL),
            _window_major(jnp.stack(kp_l)),
            _window_major(jnp.stack(vp_l)),
            jnp.stack(sp_l).reshape(depth, bsz, H_C, DK_C, DV_C),
            _window_major(ks_t),
            _window_major(vs_t),
            ss.reshape(depth, nseq, H_C, DK_C, DV_C),
            cv.reshape(depth, nseq, 1, D_B))
```

```python
import functools

import numpy as np
import jax
import jax.numpy as jnp
from jax import lax
from jax.experimental import pallas as pl
from jax.experimental.pallas import tpu as pltpu

D_MODEL = 1024
D_A, HD_A, H_A, KV_A, G_A = 512, 64, 8, 2, 4
WINDOW, N_BUCKETS, MAX_DIST = 128, 32, 128
D_B, H_B, DH_B, CHUNK_B = 512, 8, 64, 128
D_C, H_C, DK_TOT, DK_C, DV_C = 512, 4, 256, 64, 128
GLA_RANK, GLA_TAU, GLA_CHUNK = 16, 16.0, 64
D_MIX = D_A + D_B + D_C
EPS = 1e-6
NEG = -1e30
SPLITS = [D_A, KV_A * HD_A, KV_A * HD_A, D_A, D_B, D_B, D_B, DK_TOT, DK_TOT, D_C, D_C, GLA_RANK]
D_IN = sum(SPLITS)

LANES = 128
HALF = LANES // 2
D_IN_PAD = ((D_IN + LANES - 1) // LANES) * LANES
VMEM_LIMIT_BYTES = 56 * 1024 * 1024

O_QA, O_KA, O_VA, O_GA = 0, 512, 640, 768
O_UB, O_VB, O_GB = 1280, 1792, 2304
O_QC, O_KC, O_VC, O_GC, O_LR = 2816, 3072, 3328, 3840, 4352
Y_A, Y_B, Y_C = 0, D_A, D_A + D_B

HEAD_PERM = [0, 4, 1, 5, 2, 6, 3, 7]

TOKEN_BLOCK = 512
SEQ_GROUP = 16
IN_PROJ_CHUNK = 256
IN_PROJ_ROWS = 256
OUT_PROJ_CHUNK = 256
MIXER_SECTIONS = 22
PREP_SPLIT = 2

_NT = (((1,), (1,)), ((), ()))


def _t5_bucket(dist):
    n = np.maximum(dist, 0)
    max_exact = N_BUCKETS // 2
    large = max_exact + (np.log(np.maximum(n, 1) / max_exact) / np.log(MAX_DIST / max_exact)
                         * (N_BUCKETS - max_exact)).astype(np.int32)
    large = np.minimum(large, N_BUCKETS - 1)
    return np.where(n < max_exact, n, large).astype(np.int32)


def _silu(x):
    return x * (1.0 / (1.0 + jnp.exp(-x)))


def _log_sigmoid(x):
    return jnp.minimum(x, 0.0) - jnp.log1p(jnp.exp(-jnp.abs(x)))


def _bdot(a, b):
    return jnp.dot(a.astype(jnp.bfloat16), b.astype(jnp.bfloat16), preferred_element_type=jnp.float32)


def _bdot_nt(a, b):
    return lax.dot_general(a.astype(jnp.bfloat16), b.astype(jnp.bfloat16), _NT,
                           preferred_element_type=jnp.float32)


def _rmsnorm(x, g):
    return x * lax.rsqrt(jnp.mean(x * x, axis=-1, keepdims=True) + EPS) * g


def _layernorm(v, g, b):
    mu = jnp.mean(v, axis=-1, keepdims=True)
    xc = v - mu
    var = jnp.mean(xc * xc, axis=-1, keepdims=True)
    return xc * lax.rsqrt(var + EPS) * g + b


def _softmax_sink(s, sink):
    m = jnp.maximum(jnp.max(s, axis=-1, keepdims=True), sink)
    p = jnp.exp(s - m)
    den = jnp.sum(p, axis=-1, keepdims=True) + jnp.exp(sink - m)
    return p, 1.0 / den


def _bias_table_kernel(rb_ref, bucket_ref, band_ref, out_ref):
    bucket = bucket_ref[...]
    band = band_ref[...] > 0
    own = lax.broadcasted_iota(jnp.int32, bucket.shape, 1) >= WINDOW
    for h in range(H_A):
        acc = jnp.zeros(bucket.shape, jnp.float32)
        for b in range(N_BUCKETS):
            acc = jnp.where(bucket == b, rb_ref[b, h], acc)
        kv, g = divmod(h, G_A)
        rows = pl.ds(g * WINDOW, WINDOW)
        out_ref[0, kv, rows, :] = jnp.where(band, acc, NEG)
        out_ref[1, kv, rows, :] = jnp.where(band & own, acc, NEG)


def _bias_table(rel_bias):
    i = np.arange(WINDOW)[:, None]
    j = np.arange(2 * WINDOW)[None, :]
    dist = i + WINDOW - j
    band = ((dist >= 0) & (dist < WINDOW)).astype(np.int32)
    return pl.pallas_call(
        _bias_table_kernel,
        out_shape=jax.ShapeDtypeStruct((2, KV_A, G_A * WINDOW, 2 * WINDOW), jnp.float32),
        in_specs=[pl.BlockSpec(memory_space=pltpu.SMEM),
                  pl.BlockSpec(memory_space=pltpu.VMEM),
                  pl.BlockSpec(memory_space=pltpu.VMEM)],
        out_specs=pl.BlockSpec(memory_space=pltpu.VMEM),
        name="rel_bias_table",
    )(rel_bias.astype(jnp.float32), jnp.asarray(_t5_bucket(dist)), jnp.asarray(band))


def _win_prep_kernel(wt_ref, lr_ref, o_ref):
    c = pl.program_id(1)
    tiles = wt_ref.shape[0] // LANES
    for cc in range(PREP_SPLIT):
        @pl.when(c == cc)
        def _(cc=cc):
            for t in range(tiles):
                col = (cc * tiles + t) * LANES
                base = next((b for b in (O_QA, O_GA) if b <= col < b + D_A), None)
                if base is None:
                    src = wt_ref[t * LANES:(t + 1) * LANES, :]
                else:
                    g = (col - base) // LANES
                    r0 = base - cc * tiles * LANES
                    assert r0 >= 0 and r0 + D_A <= wt_ref.shape[0]
                    src = jnp.concatenate([wt_ref[r0 + g * HD_A:r0 + (g + 1) * HD_A, :],
                                           wt_ref[r0 + (G_A + g) * HD_A:r0 + (G_A + g + 1) * HD_A, :]], axis=0)
                o_ref[:, col:col + LANES] = src.T.astype(jnp.bfloat16)
            if cc == PREP_SPLIT - 1:
                o_ref[:, O_LR:D_IN_PAD] = lr_ref[...].T.astype(jnp.bfloat16)


def _wout_prep_kernel(w_ref, o_ref):
    for j, h in enumerate(HEAD_PERM):
        o_ref[j * HD_A:(j + 1) * HD_A, :] = w_ref[h * HD_A:(h + 1) * HD_A, :].astype(jnp.bfloat16)
    o_ref[Y_B:, :] = w_ref[Y_B:, :].astype(jnp.bfloat16)


def _prepare_weights(w_in, w_out):
    depth = w_in.shape[0]
    cols = D_MODEL // PREP_SPLIT
    wt = jnp.swapaxes(w_in, 1, 2)
    assert O_LR % (PREP_SPLIT * LANES) == 0
    rows = O_LR // PREP_SPLIT
    wt_lr = jnp.pad(wt[:, O_LR:, :], ((0, 0), (0, D_IN_PAD - D_IN), (0, 0)))
    win = pl.pallas_call(
        _win_prep_kernel,
        grid=(depth, PREP_SPLIT),
        in_specs=[pl.BlockSpec((None, rows, D_MODEL), lambda l, r: (l, r, 0)),
                  pl.BlockSpec((None, LANES, D_MODEL), lambda l, r: (l, 0, 0))],
        out_specs=pl.BlockSpec((None, D_MODEL, D_IN_PAD), lambda l, r: (l, 0, 0)),
        out_shape=jax.ShapeDtypeStruct((depth, D_MODEL, D_IN_PAD), jnp.bfloat16),
        compiler_params=pltpu.CompilerParams(dimension_semantics=("arbitrary", "arbitrary"),
                                             vmem_limit_bytes=VMEM_LIMIT_BYTES),
        name="w_in_prep",
    )(wt, wt_lr)
    wout = pl.pallas_call(
        _wout_prep_kernel,
        grid=(depth, PREP_SPLIT),
        in_specs=[pl.BlockSpec((None, D_MIX, cols), lambda l, c: (l, 0, c))],
        out_specs=pl.BlockSpec((None, D_MIX, cols), lambda l, c: (l, 0, c)),
        out_shape=jax.ShapeDtypeStruct((depth, D_MIX, D_MODEL), jnp.bfloat16),
        compiler_params=pltpu.CompilerParams(dimension_semantics=("arbitrary", "arbitrary"),
                                             vmem_limit_bytes=VMEM_LIMIT_BYTES),
        name="w_out_prep",
    )(w_out)
    return win, wout


def _prompt_layer_kernel(x_ref, xn_ref, ng_ref, win_ref, wout_ref, bias_ref, sink_ref, sw_ref, bsf_ref, lng_ref,
                         lnb_ref, wup_ref, bup_ref, gn_ref, fg_ref,
                         xo_ref, ko_ref, vo_ref, so_ref,
                         pa_ref, pb_ref, hb_ref, hn_ref, ya_ref, yb_ref, kprev_ref, vprev_ref, st_ref, wm_ref,
                         *, tb, layer, final):
    b_id = pl.program_id(0)
    i_id = pl.program_id(1)
    n_i = pl.num_programs(1)
    half = tb // 2
    n_sub = half // WINDOW

    @pl.when((b_id == 0) & (i_id == 0))
    def _():
        r = lax.broadcasted_iota(jnp.int32, (CHUNK_B, CHUNK_B), 0)
        c = lax.broadcasted_iota(jnp.int32, (CHUNK_B, CHUNK_B), 1)
        for h in range(H_B):
            wm_ref[h // 2, :, (h % 2) * CHUNK_B:(h % 2 + 1) * CHUNK_B] = jnp.where(
                c <= r, sw_ref[h], 0.0).astype(jnp.bfloat16)
        pa_ref[...] = _bdot(_rmsnorm(x_ref[0:half, :], ng_ref[...]), win_ref[...])

    @pl.when(i_id == 0)
    def _():
        kprev_ref[...] = jnp.zeros_like(kprev_ref)
        vprev_ref[...] = jnp.zeros_like(vprev_ref)
        st_ref[...] = jnp.zeros_like(st_ref)

    hb_ref[...] = _rmsnorm(x_ref[half:tb, :], ng_ref[...]).astype(jnp.bfloat16)
    hn_ref[...] = _rmsnorm(xn_ref[...], ng_ref[...]).astype(jnp.bfloat16)

    def in_proj_chunks(h_ref, p_ref):
        def chunk(r0, c0, c1):
            def run():
                p_ref[r0:r0 + IN_PROJ_ROWS, c0:c1] = jnp.dot(h_ref[r0:r0 + IN_PROJ_ROWS, :], win_ref[:, c0:c1],
                                                             preferred_element_type=jnp.float32)
            return run
        edges = list(range(0, D_IN_PAD, IN_PROJ_CHUNK)) + [D_IN_PAD]
        return [chunk(r0, c0, c1) for c0, c1 in zip(edges[:-1], edges[1:])
                for r0 in range(0, half, IN_PROJ_ROWS)]

    def out_proj_chunks(y_ref, r0, s):
        def chunk(c0, c1):
            def run():
                rows = slice(r0 + s * WINDOW, r0 + (s + 1) * WINDOW)
                xo_ref[rows, c0:c1] = x_ref[rows, c0:c1] + jnp.dot(
                    y_ref[s * WINDOW:(s + 1) * WINDOW, :], wout_ref[:, c0:c1],
                    preferred_element_type=jnp.float32)
            return run
        return [chunk(c0, c0 + OUT_PROJ_CHUNK) for c0 in range(0, D_MODEL, OUT_PROJ_CHUNK)]

    def final_norm(r0):
        if final:
            xo_ref[r0:r0 + half, :] = _rmsnorm(xo_ref[r0:r0 + half, :], fg_ref[...])

    lane = lax.broadcasted_iota(jnp.int32, (WINDOW, LANES), 1)
    low = lane < HALF
    low64 = lax.broadcasted_iota(jnp.int32, (GLA_CHUNK, LANES), 1) < HALF
    ri = lax.broadcasted_iota(jnp.int32, (LANES, LANES), 0)
    ci = lax.broadcasted_iota(jnp.int32, (LANES, LANES), 1)
    bd_tril = ((ri >= GLA_CHUNK) == (ci >= GLA_CHUNK)) & (ci <= ri)
    cum_mat = jnp.where(bd_tril, 1.0, 0.0).astype(jnp.bfloat16)

    def mixers(p_ref, y_ref, s, first):
        r0 = s * WINDOW
        rows = pl.ds(r0, WINDOW)

        ks = p_ref[rows, O_KA:O_KA + LANES] * (HD_A ** -0.5)
        k_new = [jnp.where(low, ks, 0.0).astype(jnp.bfloat16), jnp.where(low, 0.0, ks).astype(jnp.bfloat16)]
        v_new = p_ref[rows, O_VA:O_VA + LANES].T.astype(jnp.bfloat16)
        vcat = jnp.concatenate([vprev_ref[...], v_new], axis=1)
        q4 = jnp.concatenate([p_ref[rows, O_QA + g * LANES:O_QA + (g + 1) * LANES].astype(jnp.bfloat16)
                              for g in range(G_A)], axis=0)
        o_heads = []
        for kv in range(KV_A):
            kcat = jnp.concatenate([kprev_ref[kv], k_new[kv]], axis=0)
            sc = lax.dot_general(q4, kcat, _NT, preferred_element_type=jnp.float32) + bias_ref[first, kv]
            kprev_ref[kv] = k_new[kv]
            yield
            ps, invs = [], []
            for g in range(G_A):
                p, inv = _softmax_sink(sc[g * WINDOW:(g + 1) * WINDOW], sink_ref[layer, kv * G_A + g])
                ps.append(p.astype(jnp.bfloat16))
                invs.append(inv)
                if g % 2 == 1:
                    yield
            o_t = lax.dot_general(vcat[kv * HD_A:(kv + 1) * HD_A, :], jnp.concatenate(ps, axis=0), _NT,
                                  preferred_element_type=jnp.float32)
            o_heads.append((o_t, invs))
            yield
        vprev_ref[...] = v_new
        for g in range(G_A):
            o_pair = jnp.concatenate([o_heads[kv][0][:, g * WINDOW:(g + 1) * WINDOW] for kv in range(KV_A)],
                                     axis=0).T
            ya = o_pair * jnp.where(low, o_heads[0][1][g], o_heads[1][1][g])
            gate = p_ref[rows, O_GA + g * LANES:O_GA + (g + 1) * LANES]
            y_ref[rows, Y_A + g * LANES:Y_A + (g + 1) * LANES] = (ya * _silu(gate)).astype(jnp.bfloat16)
        yield

        vn = _layernorm(p_ref[rows, O_VB:O_VB + D_B], lng_ref[...], lnb_ref[...]).astype(jnp.bfloat16)
        yield
        for j in range(H_B // 2):
            vp = vn[:, j * LANES:(j + 1) * LANES]
            zero = jnp.zeros_like(vp)
            v_st = jnp.concatenate([jnp.where(low, vp, zero), jnp.where(low, zero, vp)], axis=0)
            mix = (jnp.dot(wm_ref[j], v_st, preferred_element_type=jnp.float32)
                   + bsf_ref[:, j * LANES:(j + 1) * LANES])
            u = p_ref[rows, O_UB + j * LANES:O_UB + (j + 1) * LANES]
            gate = p_ref[rows, O_GB + j * LANES:O_GB + (j + 1) * LANES]
            y_ref[rows, Y_B + j * LANES:Y_B + (j + 1) * LANES] = (u * mix * _silu(gate)).astype(jnp.bfloat16)
            if j % 2 == 1:
                yield

        z = _bdot(p_ref[rows, O_LR:O_LR + LANES], wup_ref[...]) + bup_ref[...]
        la = _log_sigmoid(z) * (1.0 / GLA_TAU)
        yield
        hi = la.astype(jnp.bfloat16)
        r1 = la - hi.astype(jnp.float32)
        mid = r1.astype(jnp.bfloat16)
        lo = (r1 - mid.astype(jnp.float32)).astype(jnp.bfloat16)
        c3 = jnp.dot(cum_mat, jnp.concatenate([hi, mid, lo], axis=1), preferred_element_type=jnp.float32)
        bc = (c3[:, 0:DK_TOT] + c3[:, DK_TOT:2 * DK_TOT]) + c3[:, 2 * DK_TOT:3 * DK_TOT]
        yield
        for c in range(WINDOW // GLA_CHUNK):
            rc = pl.ds(r0 + c * GLA_CHUNK, GLA_CHUNK)
            atts, o_inter, v_sts, kd_sts, decays, sts = [], [], [], [], [], []
            for p in range(H_C // 2):
                bp = bc[c * GLA_CHUNK:(c + 1) * GLA_CHUNK, p * LANES:(p + 1) * LANES]
                qp = p_ref[rc, O_QC + p * LANES:O_QC + (p + 1) * LANES]
                kp = p_ref[rc, O_KC + p * LANES:O_KC + (p + 1) * LANES]
                b_last = bp[GLA_CHUNK - 1:GLA_CHUNK, :]
                qd = qp * (DK_C ** -0.5) * jnp.exp(bp)
                ki = (kp * jnp.exp(-bp)).astype(jnp.bfloat16)
                kd = kp * jnp.exp(b_last - bp)
                q_st = jnp.concatenate([jnp.where(low64, qd, 0.0), jnp.where(low64, 0.0, qd)],
                                       axis=0).astype(jnp.bfloat16)
                st = st_ref[p]
                rhs = jnp.concatenate([ki, ki, st.astype(jnp.bfloat16)], axis=0)
                res = _bdot_nt(q_st, rhs)
                atts.append(jnp.where(bd_tril, res[:, 0:LANES], 0.0).astype(jnp.bfloat16))
                o_inter.append(res[:, LANES:])
                v0 = p_ref[rc, O_VC + (2 * p) * DV_C:O_VC + (2 * p + 1) * DV_C]
                v1 = p_ref[rc, O_VC + (2 * p + 1) * DV_C:O_VC + (2 * p + 2) * DV_C]
                v_sts.append(jnp.concatenate([v0, v1], axis=0))
                kd_sts.append(jnp.concatenate([jnp.where(low64, kd, 0.0), jnp.where(low64, 0.0, kd)],
                                              axis=0).astype(jnp.bfloat16))
                decays.append(jnp.exp(b_last))
                sts.append(st)
                yield
            zero = jnp.zeros((LANES, LANES), jnp.bfloat16)
            blockdiag = lambda a, b: jnp.concatenate([jnp.concatenate([a, zero], axis=1),
                                                      jnp.concatenate([zero, b], axis=1)], axis=0)
            v_b = [v.astype(jnp.bfloat16) for v in v_sts]
            o_all = jnp.dot(jnp.concatenate(atts, axis=1), blockdiag(*v_b),
                            preferred_element_type=jnp.float32)
            d_all = jnp.dot(jnp.concatenate([v.T.astype(jnp.bfloat16) for v in v_sts], axis=1),
                            blockdiag(*kd_sts), preferred_element_type=jnp.float32)
            for p in range(H_C // 2):
                st_ref[p] = sts[p] * decays[p] + d_all[:, p * LANES:(p + 1) * LANES]
                o = o_all[:, p * DV_C:(p + 1) * DV_C] + o_inter[p]
                for hh in range(2):
                    hd = 2 * p + hh
                    oh = o[hh * GLA_CHUNK:(hh + 1) * GLA_CHUNK]
                    on = _rmsnorm(oh, gn_ref[:, hd * DV_C:(hd + 1) * DV_C])
                    gate = p_ref[rc, O_GC + hd * DV_C:O_GC + (hd + 1) * DV_C]
                    y_ref[rc, Y_C + hd * DV_C:Y_C + (hd + 1) * DV_C] = (on * _silu(gate)).astype(jnp.bfloat16)
                yield

    def half_mixers(p_ref, y_ref, first):
        for s in range(n_sub):
            yield from mixers(p_ref, y_ref, s, first if s == 0 else 0)

    def run_phase(sections, chunks, late_chunks):
        n_sections = n_sub * MIXER_SECTIONS
        plan = [[] for _ in range(n_sections)]
        for idx, run in enumerate(chunks):
            plan[idx * n_sections // len(chunks)].append(run)
        for s, runs in late_chunks.items():
            k0 = (s + 1) * MIXER_SECTIONS
            for idx, run in enumerate(runs):
                plan[k0 + idx * (n_sections - k0) // len(runs)].append(run)
        for k in range(n_sections):
            next(sections)
            for run in plan[k]:
                run()
        assert next(sections, None) is None

    run_phase(half_mixers(pa_ref, ya_ref, jnp.where(i_id == 0, 1, 0)),
              in_proj_chunks(hb_ref, pb_ref),
              {s: out_proj_chunks(ya_ref, 0, s) for s in range(n_sub - 1)})
    run_phase(half_mixers(pb_ref, yb_ref, 0),
              out_proj_chunks(ya_ref, 0, n_sub - 1) + in_proj_chunks(hn_ref, pa_ref),
              {s: out_proj_chunks(yb_ref, half, s) for s in range(n_sub - 1)})
    final_norm(0)
    for run in out_proj_chunks(yb_ref, half, n_sub - 1):
        run()
    final_norm(half)

    @pl.when(i_id == n_i - 1)
    def _():
        ko_ref[...] = pb_ref[half - WINDOW:half, O_KA:O_KA + LANES].T
        vo_ref[...] = pb_ref[half - WINDOW:half, O_VA:O_VA + LANES].T
        for p in range(H_C // 2):
            so_ref[p] = st_ref[p].T


def _prompt_layer(x, ng, win, wout, bias_tab, sinks, sw, bsf, lng, lnb, wup, bup, gn, fg, *, layer, final):
    bsz, seq, _ = x.shape
    tb = min(TOKEN_BLOCK, seq)
    half = tb // 2
    assert seq % tb == 0 and half % WINDOW == 0
    n_i = seq // tb
    grid = (bsz, n_i)

    def next_half(b, i):
        lin = jnp.minimum(b * n_i + i + 1, bsz * n_i - 1)
        return lin // n_i, (lin % n_i) * 2, 0

    def const_spec(shape):
        nd = len(shape)
        return pl.BlockSpec(shape, lambda b, i: (0,) * nd, pipeline_mode=pl.Buffered(1))

    def layer_spec(shape):
        nd = len(shape)
        return pl.BlockSpec((None,) + shape, lambda b, i: (layer,) + (0,) * nd, pipeline_mode=pl.Buffered(1))

    kern = functools.partial(_prompt_layer_kernel, tb=tb, layer=layer, final=final)
    return pl.pallas_call(
        kern,
        grid=grid,
        in_specs=[
            pl.BlockSpec((None, tb, D_MODEL), lambda b, i: (b, i, 0)),
            pl.BlockSpec((None, half, D_MODEL), next_half),
            layer_spec((1, D_MODEL)),
            layer_spec((D_MODEL, D_IN_PAD)),
            layer_spec((D_MIX, D_MODEL)),
            const_spec((2, KV_A, G_A * WINDOW, 2 * WINDOW)),
            pl.BlockSpec(memory_space=pltpu.SMEM),
            layer_spec((H_B, CHUNK_B, CHUNK_B)),
            layer_spec((CHUNK_B, D_B)),
            layer_spec((1, D_B)),
            layer_spec((1, D_B)),
            layer_spec((LANES, DK_TOT)),
            layer_spec((1, DK_TOT)),
            layer_spec((1, D_C)),
            const_spec((1, D_MODEL)),
        ],
        out_specs=[
            pl.BlockSpec((None, tb, D_MODEL), lambda b, i: (b, i, 0)),
            pl.BlockSpec((None, LANES, WINDOW), lambda b, i: (b, 0, 0)),
            pl.BlockSpec((None, LANES, WINDOW), lambda b, i: (b, 0, 0)),
            pl.BlockSpec((None, H_C // 2, LANES, DV_C), lambda b, i: (b, 0, 0, 0)),
        ],
        out_shape=[
            jax.ShapeDtypeStruct((bsz, seq, D_MODEL), jnp.float32),
            jax.ShapeDtypeStruct((bsz, LANES, WINDOW), jnp.float32),
            jax.ShapeDtypeStruct((bsz, LANES, WINDOW), jnp.float32),
            jax.ShapeDtypeStruct((bsz, H_C // 2, LANES, DV_C), jnp.float32),
        ],
        scratch_shapes=[
            pltpu.VMEM((half, D_IN_PAD), jnp.float32),
            pltpu.VMEM((half, D_IN_PAD), jnp.float32),
            pltpu.VMEM((half, D_MODEL), jnp.bfloat16),
            pltpu.VMEM((half, D_MODEL), jnp.bfloat16),
            pltpu.VMEM((half, D_MIX), jnp.bfloat16),
            pltpu.VMEM((half, D_MIX), jnp.bfloat16),
            pltpu.VMEM((KV_A, WINDOW, LANES), jnp.bfloat16),
            pltpu.VMEM((WINDOW, LANES), jnp.bfloat16),
            pltpu.VMEM((H_C // 2, DV_C, LANES), jnp.float32),
            pltpu.VMEM((H_B // 2, CHUNK_B, 2 * CHUNK_B), jnp.bfloat16),
        ],
        compiler_params=pltpu.CompilerParams(
            dimension_semantics=("arbitrary", "arbitrary"),
            vmem_limit_bytes=VMEM_LIMIT_BYTES),
        name="prompt_layer",
    )(x, x, ng, win, wout, bias_tab, sinks, sw, bsf, lng, lnb, wup, bup, gn, fg)


def _sample_kernel(x_ref, ng_ref, win_ref, wout_ref, biass_ref, sinkc_ref, w00_ref, b0_ref, lng_ref,
                   lnb_ref, wup_ref, bup_ref, gn_ref, fg_ref, bk_ref, bv_ref, s0_ref,
                   xo_ref, nk_ref, nv_ref, so_ref, cv_ref,
                   xcur_ref, proj_ref, y_ref, qs_ref, os_ref, qd_ref, ea_ref, oc_ref, *, nseq, gs):
    l_id = pl.program_id(0)
    i_id = pl.program_id(1)
    n_l = pl.num_programs(0)
    n_i = pl.num_programs(1)
    lane = lax.broadcasted_iota(jnp.int32, (nseq, LANES), 1)
    low = lane < HALF

    @pl.when((l_id == 0) & (i_id == 0))
    def _():
        xcur_ref[...] = x_ref[...]

    @pl.when(i_id == 0)
    def _():
        h = _rmsnorm(xcur_ref[...], ng_ref[...])
        proj_ref[...] = _bdot(h, win_ref[...])
        for g in range(G_A):
            qp = proj_ref[:, O_QA + g * LANES:O_QA + (g + 1) * LANES] * (HD_A ** -0.5)
            for kv in range(KV_A):
                keep = low if kv == 0 else jnp.logical_not(low)
                qs_ref[pl.ds(kv * G_A + g, nseq, stride=H_A), :] = jnp.where(keep, qp, 0.0)
        vn = _layernorm(proj_ref[:, O_VB:O_VB + D_B], lng_ref[...], lnb_ref[...])
        cv_ref[...] = vn
        mix = w00_ref[...] * vn + b0_ref[...]
        y_ref[:, Y_B:Y_B + D_B] = (proj_ref[:, O_UB:O_UB + D_B] * mix) * _silu(proj_ref[:, O_GB:O_GB + D_B])
        z = _bdot(proj_ref[:, O_LR:O_LR + LANES], wup_ref[...]) + bup_ref[...]
        la = _log_sigmoid(z) * (1.0 / GLA_TAU)
        ea = jnp.exp(la)
        qd = proj_ref[:, O_QC:O_QC + DK_TOT] * (DK_C ** -0.5) * ea
        ki = proj_ref[:, O_KC:O_KC + DK_TOT] * jnp.exp(-la)
        qd_ref[...] = qd
        ea_ref[...] = ea
        prod = qd * ki
        lane_c = lax.broadcasted_iota(jnp.int32, prod.shape, 1)
        for hd in range(H_C):
            in_head = (lane_c >= hd * DK_C) & (lane_c < (hd + 1) * DK_C)
            att = jnp.sum(jnp.where(in_head, prod, 0.0), axis=-1, keepdims=True)
            oc_ref[:, hd * DV_C:(hd + 1) * DV_C] = att * proj_ref[:, O_VC + hd * DV_C:O_VC + (hd + 1) * DV_C]

    g0 = pl.multiple_of(i_id * gs, gs)
    grows = pl.ds(g0, gs)

    pad_rows = jnp.zeros((LANES - gs, LANES), jnp.float32)
    k_cols = jnp.concatenate([proj_ref[grows, O_KA:O_KA + LANES], pad_rows], axis=0).T
    v_cols = jnp.concatenate([proj_ref[grows, O_VA:O_VA + LANES], pad_rows], axis=0).T
    newest = lax.broadcasted_iota(jnp.int32, (LANES, WINDOW), 1) == WINDOW - 1
    for j in range(gs):
        kt = jnp.where(newest, k_cols[:, j:j + 1], pltpu.roll(bk_ref[j], WINDOW - 1, 1))
        vt = jnp.where(newest, v_cols[:, j:j + 1], pltpu.roll(bv_ref[j], WINDOW - 1, 1))
        nk_ref[j] = kt
        nv_ref[j] = vt
        r8 = pl.ds(pl.multiple_of((g0 + j) * H_A, H_A), H_A)
        sc = _bdot(qs_ref[r8, :], kt) + biass_ref[...]
        p, inv = _softmax_sink(sc, sinkc_ref[...])
        os_ref[r8, :] = _bdot_nt(p, vt) * inv

    pad_c = jnp.zeros((LANES - gs, DK_TOT), jnp.float32)
    ea_cols = jnp.concatenate([ea_ref[grows, :], pad_c], axis=0).T
    k_cols_c = jnp.concatenate([proj_ref[grows, O_KC:O_KC + DK_TOT], pad_c], axis=0).T
    qd_g = qd_ref[grows, :]
    row_g = lax.broadcasted_iota(jnp.int32, (gs, LANES), 0)
    low_g = lax.broadcasted_iota(jnp.int32, (gs, LANES), 1) < HALF
    seqs_per_dot = 2 * LANES // DK_C
    for hd in range(H_C):
        v_h = proj_ref[grows, O_VC + hd * DV_C:O_VC + (hd + 1) * DV_C]
        q_pair = qd_g[:, (hd // 2) * LANES:(hd // 2 + 1) * LANES]
        q_other = pltpu.roll(q_pair, HALF, 1)
        q_lo, q_hi = (q_pair, q_other) if hd % 2 == 0 else (q_other, q_pair)
        acc = jnp.zeros((gs, DV_C), jnp.float32)
        for j0 in range(0, gs, seqs_per_dot):
            tiles, lhs = [], []
            for j in range(j0, j0 + seqs_per_dot):
                srows = pl.ds(j * DK_TOT + hd * DK_C, DK_C)
                s_old = s0_ref[srows, :]
                tiles.append(s_old.astype(jnp.bfloat16))
                crows = slice(hd * DK_C, (hd + 1) * DK_C)
                so_ref[srows, :] = ea_cols[crows, j:j + 1] * s_old + k_cols_c[crows, j:j + 1] * v_h[j:j + 1, :]
            for j in range(j0, j0 + seqs_per_dot, 2):
                lhs.append(jnp.where((row_g == j) & low_g, q_lo,
                                     jnp.where((row_g == j + 1) & jnp.logical_not(low_g), q_hi, 0.0)))
            acc = acc + _bdot(jnp.concatenate(lhs, axis=1), jnp.concatenate(tiles, axis=0))
        oc_ref[grows, hd * DV_C:(hd + 1) * DV_C] += acc

    @pl.when(i_id == n_i - 1)
    def _():
        for g in range(G_A):
            o0 = os_ref[pl.ds(g, nseq, stride=H_A), :]
            o1 = os_ref[pl.ds(G_A + g, nseq, stride=H_A), :]
            gate = proj_ref[:, O_GA + g * LANES:O_GA + (g + 1) * LANES]
            y_ref[:, Y_A + g * LANES:Y_A + (g + 1) * LANES] = jnp.where(low, o0, o1) * _silu(gate)
        for hd in range(H_C):
            on = _rmsnorm(oc_ref[:, hd * DV_C:(hd + 1) * DV_C], gn_ref[:, hd * DV_C:(hd + 1) * DV_C])
            gate = proj_ref[:, O_GC + hd * DV_C:O_GC + (hd + 1) * DV_C]
            y_ref[:, Y_C + hd * DV_C:Y_C + (hd + 1) * DV_C] = on * _silu(gate)
        x_new = xcur_ref[...] + _bdot(y_ref[...], wout_ref[...])
        xcur_ref[...] = x_new

        @pl.when(l_id == n_l - 1)
        def _():
            xo_ref[...] = _rmsnorm(x_new, fg_ref[...])


def _sample_path(x, ng, win, wout, bias_s, sink_c, w00, b0, lng, lnb, wup, bup, gn, fg, buf_kt, buf_vt, s0):
    depth = win.shape[0]
    nseq = x.shape[0]
    gs = min(SEQ_GROUP, nseq)
    assert nseq % gs == 0 and gs % 8 == 0
    grid = (depth, nseq // gs)

    def const_spec(shape):
        nd = len(shape)
        return pl.BlockSpec(shape, lambda l, i: (0,) * nd)

    def layer_spec(shape):
        nd = len(shape)
        return pl.BlockSpec((None,) + shape, lambda l, i: (l,) + (0,) * nd)

    state_spec = pl.BlockSpec((None, gs, LANES, WINDOW), lambda l, i: (l, i, 0, 0))
    gla_spec = pl.BlockSpec((None, gs * DK_TOT, DV_C), lambda l, i: (l, i, 0))
    kern = functools.partial(_sample_kernel, nseq=nseq, gs=gs)
    return pl.pallas_call(
        kern,
        grid=grid,
        in_specs=[
            const_spec((nseq, D_MODEL)),
            layer_spec((1, D_MODEL)),
            layer_spec((D_MODEL, D_IN_PAD)),
            layer_spec((D_MIX, D_MODEL)),
            const_spec((H_A, WINDOW)),
            layer_spec((H_A, 1)),
            layer_spec((1, D_B)),
            layer_spec((1, D_B)),
            layer_spec((1, D_B)),
            layer_spec((1, D_B)),
            layer_spec((LANES, DK_TOT)),
            layer_spec((1, DK_TOT)),
            layer_spec((1, D_C)),
            const_spec((1, D_MODEL)),
            state_spec, state_spec, gla_spec,
        ],
        out_specs=[
            const_spec((nseq, D_MODEL)),
            state_spec, state_spec, gla_spec,
            layer_spec((nseq, D_B)),
        ],
        out_shape=[
            jax.ShapeDtypeStruct((nseq, D_MODEL), jnp.float32),
            jax.ShapeDtypeStruct((depth, nseq, LANES, WINDOW), jnp.float32),
            jax.ShapeDtypeStruct((depth, nseq, LANES, WINDOW), jnp.float32),
            jax.ShapeDtypeStruct((depth, nseq * DK_TOT, DV_C), jnp.float32),
            jax.ShapeDtypeStruct((depth, nseq, D_B), jnp.float32),
        ],
        scratch_shapes=[
            pltpu.VMEM((nseq, D_MODEL), jnp.float32),
            pltpu.VMEM((nseq, D_IN_PAD), jnp.float32),
            pltpu.VMEM((nseq, D_MIX), jnp.float32),
            pltpu.VMEM((nseq * H_A, LANES), jnp.float32),
            pltpu.VMEM((nseq * H_A, LANES), jnp.float32),
            pltpu.VMEM((nseq, DK_TOT), jnp.float32),
            pltpu.VMEM((nseq, DK_TOT), jnp.float32),
            pltpu.VMEM((nseq, D_C), jnp.float32),
        ],
        compiler_params=pltpu.CompilerParams(
            dimension_semantics=("arbitrary", "arbitrary"),
            vmem_limit_bytes=VMEM_LIMIT_BYTES),
        name="sample_path",
    )(x, ng, win, wout, bias_s, sink_c, w00, b0, lng, lnb, wup, bup, gn, fg, buf_kt, buf_vt, s0)


def _window_minor(state):
    depth, n = state.shape[:2]
    return jnp.transpose(state, (0, 1, 3, 4, 2)).reshape(depth, n, LANES, WINDOW)


def _window_major(state_t):
    depth, n = state_t.shape[:2]
    return jnp.transpose(state_t.reshape(depth, n, KV_A, HD_A, WINDOW), (0, 1, 4, 2, 3))


def kernel(x_prompt, x_sample, state_swa_k, state_swa_v, state_gla, rel_bias, norm_g, w_in, sinks, spatial_w,
           spatial_b, chunk_ln_g, chunk_ln_b, gla_w_up, gla_b_up, gla_norm_g, w_out, final_norm_g):
    depth = w_in.shape[0]
    nseq = x_sample.shape[0]
    bsz = x_prompt.shape[0]
    f32 = jnp.float32

    win, wout = _prepare_weights(w_in, w_out)
    wup = jnp.pad(gla_w_up, ((0, 0), (0, LANES - GLA_RANK), (0, 0))).astype(jnp.bfloat16)
    bsf = jnp.repeat(jnp.swapaxes(spatial_b, 1, 2), DH_B, axis=2).astype(f32)
    w00 = jnp.repeat(spatial_w[:, :, 0, 0], DH_B, axis=1).astype(f32)[:, None, :]
    b0 = jnp.repeat(spatial_b[:, :, 0], DH_B, axis=1).astype(f32)[:, None, :]
    sink_c = sinks.astype(f32)[:, :, None]
    row = lambda a: a.astype(f32)[:, None, :]
    ng, lng, lnb, bup, gn = row(norm_g), row(chunk_ln_g), row(chunk_ln_b), row(gla_b_up), row(gla_norm_g)
    fg = final_norm_g.astype(f32)[None, :]
    sw = spatial_w.astype(f32)
    sinks = sinks.astype(f32)

    bias_tab = _bias_table(rel_bias)
    bias_s = bias_tab[0].reshape(H_A, WINDOW, 2 * WINDOW)[:, WINDOW - 1, WINDOW:]

    xp = x_prompt
    kp_l, vp_l, sp_l = [], [], []
    for l in range(depth):
        xp, kp, vp, sp = _prompt_layer(xp, ng, win, wout, bias_tab, sinks, sw, bsf, lng, lnb, wup, bup, gn, fg,
                                       layer=l, final=l == depth - 1)
        kp_l.append(kp); vp_l.append(vp); sp_l.append(sp)

    xs, ks_t, vs_t, ss, cv = _sample_path(
        x_sample.reshape(nseq, D_MODEL), ng, win, wout, bias_s, sink_c, w00, b0, lng, lnb, wup, bup, gn, fg,
        _window_minor(state_swa_k), _window_minor(state_swa_v), state_gla.reshape(depth, nseq * DK_TOT, DV_C))

    return (xp,
            xs.reshape(nseq, 1, D_MODEL),
            _window_major(jnp.stack(kp_l)),
            _window_major(jnp.stack(vp_l)),
            jnp.stack(sp_l).reshape(depth, bsz, H_C, DK_C, DV_C),
            _window_major(ks_t),
            _window_major(vs_t),
            ss.reshape(depth, nseq, H_C, DK_C, DV_C),
            cv.reshape(depth, nseq, 1, D_B))
```

```python
import functools

import numpy as np
import jax
import jax.numpy as jnp
from jax import lax
from jax.experimental import pallas as pl
from jax.experimental.pallas import tpu as pltpu

D_MODEL = 1024
D_A, HD_A, H_A, KV_A, G_A = 512, 64, 8, 2, 4
WINDOW, N_BUCKETS, MAX_DIST = 128, 32, 128
D_B, H_B, DH_B, CHUNK_B = 512, 8, 64, 128
D_C, H_C, DK_TOT, DK_C, DV_C = 512, 4, 256, 64, 128
GLA_RANK, GLA_TAU, GLA_CHUNK = 16, 16.0, 64
D_MIX = D_A + D_B + D_C
EPS = 1e-6
NEG = -1e30
SPLITS = [D_A, KV_A * HD_A, KV_A * HD_A, D_A, D_B, D_B, D_B, DK_TOT, DK_TOT, D_C, D_C, GLA_RANK]
D_IN = sum(SPLITS)

LANES = 128
HALF = LANES // 2
D_IN_PAD = ((D_IN + LANES - 1) // LANES) * LANES
VMEM_LIMIT_BYTES = 61 * 1024 * 1024

O_QA, O_KA, O_VA, O_GA = 0, 512, 640, 768
O_UB, O_VB, O_GB = 1280, 1792, 2304
O_QC, O_KC, O_VC, O_GC, O_LR = 2816, 3072, 3328, 3840, 4352
Y_A, Y_B, Y_C = 0, D_A, D_A + D_B

HEAD_PERM = [0, 4, 1, 5, 2, 6, 3, 7]

TOKEN_BLOCK = 1024
SEQ_GROUP = 16
IN_PROJ_CHUNK = 256
IN_PROJ_ROWS = 256
OUT_PROJ_CHUNK = 256
MIXER_SECTIONS = 22
PREP_SPLIT = 2

_NT = (((1,), (1,)), ((), ()))


def _t5_bucket(dist):
    n = np.maximum(dist, 0)
    max_exact = N_BUCKETS // 2
    large = max_exact + (np.log(np.maximum(n, 1) / max_exact) / np.log(MAX_DIST / max_exact)
                         * (N_BUCKETS - max_exact)).astype(np.int32)
    large = np.minimum(large, N_BUCKETS - 1)
    return np.where(n < max_exact, n, large).astype(np.int32)


def _silu(x):
    return x * (1.0 / (1.0 + jnp.exp(-x)))


def _log_sigmoid(x):
    return jnp.minimum(x, 0.0) - jnp.log1p(jnp.exp(-jnp.abs(x)))


def _bdot(a, b):
    return jnp.dot(a.astype(jnp.bfloat16), b.astype(jnp.bfloat16), preferred_element_type=jnp.float32)


def _bdot_nt(a, b):
    return lax.dot_general(a.astype(jnp.bfloat16), b.astype(jnp.bfloat16), _NT,
                           preferred_element_type=jnp.float32)


def _rmsnorm(x, g):
    return x * lax.rsqrt(jnp.mean(x * x, axis=-1, keepdims=True) + EPS) * g


def _layernorm(v, g, b):
    mu = jnp.mean(v, axis=-1, keepdims=True)
    xc = v - mu
    var = jnp.mean(xc * xc, axis=-1, keepdims=True)
    return xc * lax.rsqrt(var + EPS) * g + b


def _softmax_sink(s, sink):
    m = jnp.maximum(jnp.max(s, axis=-1, keepdims=True), sink)
    p = jnp.exp(s - m)
    den = jnp.sum(p, axis=-1, keepdims=True) + jnp.exp(sink - m)
    return p, 1.0 / den


def _bias_table_kernel(rb_ref, bucket_ref, band_ref, out_ref):
    bucket = bucket_ref[...]
    band = band_ref[...] > 0
    own = lax.broadcasted_iota(jnp.int32, bucket.shape, 1) >= WINDOW
    for h in range(H_A):
        acc = jnp.zeros(bucket.shape, jnp.float32)
        for b in range(N_BUCKETS):
            acc = jnp.where(bucket == b, rb_ref[b, h], acc)
        kv, g = divmod(h, G_A)
        rows = pl.ds(g * WINDOW, WINDOW)
        out_ref[0, kv, rows, :] = jnp.where(band, acc, NEG)
        out_ref[1, kv, rows, :] = jnp.where(band & own, acc, NEG)


def _bias_table(rel_bias):
    i = np.arange(WINDOW)[:, None]
    j = np.arange(2 * WINDOW)[None, :]
    dist = i + WINDOW - j
    band = ((dist >= 0) & (dist < WINDOW)).astype(np.int32)
    return pl.pallas_call(
        _bias_table_kernel,
        out_shape=jax.ShapeDtypeStruct((2, KV_A, G_A * WINDOW, 2 * WINDOW), jnp.float32),
        in_specs=[pl.BlockSpec(memory_space=pltpu.SMEM),
                  pl.BlockSpec(memory_space=pltpu.VMEM),
                  pl.BlockSpec(memory_space=pltpu.VMEM)],
        out_specs=pl.BlockSpec(memory_space=pltpu.VMEM),
        name="rel_bias_table",
    )(rel_bias.astype(jnp.float32), jnp.asarray(_t5_bucket(dist)), jnp.asarray(band))


def _win_prep_kernel(wt_ref, lr_ref, o_ref):
    c = pl.program_id(1)
    tiles = wt_ref.shape[0] // LANES
    for cc in range(PREP_SPLIT):
        @pl.when(c == cc)
        def _(cc=cc):
            for t in range(tiles):
                col = (cc * tiles + t) * LANES
                base = next((b for b in (O_QA, O_GA) if b <= col < b + D_A), None)
                if base is None:
                    src = wt_ref[t * LANES:(t + 1) * LANES, :]
                else:
                    g = (col - base) // LANES
                    r0 = base - cc * tiles * LANES
                    assert r0 >= 0 and r0 + D_A <= wt_ref.shape[0]
                    src = jnp.concatenate([wt_ref[r0 + g * HD_A:r0 + (g + 1) * HD_A, :],
                                           wt_ref[r0 + (G_A + g) * HD_A:r0 + (G_A + g + 1) * HD_A, :]], axis=0)
                o_ref[:, col:col + LANES] = src.T.astype(jnp.bfloat16)
            if cc == PREP_SPLIT - 1:
                o_ref[:, O_LR:D_IN_PAD] = lr_ref[...].T.astype(jnp.bfloat16)


def _wout_prep_kernel(w_ref, o_ref):
    for j, h in enumerate(HEAD_PERM):
        o_ref[j * HD_A:(j + 1) * HD_A, :] = w_ref[h * HD_A:(h + 1) * HD_A, :].astype(jnp.bfloat16)
    o_ref[Y_B:, :] = w_ref[Y_B:, :].astype(jnp.bfloat16)


def _prepare_weights(w_in, w_out):
    depth = w_in.shape[0]
    cols = D_MODEL // PREP_SPLIT
    wt = jnp.swapaxes(w_in, 1, 2)
    assert O_LR % (PREP_SPLIT * LANES) == 0
    rows = O_LR // PREP_SPLIT
    wt_lr = jnp.pad(wt[:, O_LR:, :], ((0, 0), (0, D_IN_PAD - D_IN), (0, 0)))
    win = pl.pallas_call(
        _win_prep_kernel,
        grid=(depth, PREP_SPLIT),
        in_specs=[pl.BlockSpec((None, rows, D_MODEL), lambda l, r: (l, r, 0)),
                  pl.BlockSpec((None, LANES, D_MODEL), lambda l, r: (l, 0, 0))],
        out_specs=pl.BlockSpec((None, D_MODEL, D_IN_PAD), lambda l, r: (l, 0, 0)),
        out_shape=jax.ShapeDtypeStruct((depth, D_MODEL, D_IN_PAD), jnp.bfloat16),
        compiler_params=pltpu.CompilerParams(dimension_semantics=("arbitrary", "arbitrary"),
                                             vmem_limit_bytes=VMEM_LIMIT_BYTES),
        name="w_in_prep",
    )(wt, wt_lr)
    wout = pl.pallas_call(
        _wout_prep_kernel,
        grid=(depth, PREP_SPLIT),
        in_specs=[pl.BlockSpec((None, D_MIX, cols), lambda l, c: (l, 0, c))],
        out_specs=pl.BlockSpec((None, D_MIX, cols), lambda l, c: (l, 0, c)),
        out_shape=jax.ShapeDtypeStruct((depth, D_MIX, D_MODEL), jnp.bfloat16),
        compiler_params=pltpu.CompilerParams(dimension_semantics=("arbitrary", "arbitrary"),
                                             vmem_limit_bytes=VMEM_LIMIT_BYTES),
        name="w_out_prep",
    )(w_out)
    return win, wout


def _prompt_layer_kernel(x_ref, xn_ref, ng_ref, win_ref, wout_ref, bias_ref, sink_ref, sw_ref, bsf_ref, lng_ref,
                         lnb_ref, wup_ref, bup_ref, gn_ref, fg_ref,
                         xo_ref, ko_ref, vo_ref, so_ref,
                         pa_ref, pb_ref, hb_ref, hn_ref, ya_ref, yb_ref, kprev_ref, vprev_ref, st_ref, wm_ref,
                         *, tb, layer, final):
    b_id = pl.program_id(0)
    i_id = pl.program_id(1)
    n_i = pl.num_programs(1)
    half = tb // 2
    n_sub = half // WINDOW

    @pl.when((b_id == 0) & (i_id == 0))
    def _():
        r = lax.broadcasted_iota(jnp.int32, (CHUNK_B, CHUNK_B), 0)
        c = lax.broadcasted_iota(jnp.int32, (CHUNK_B, CHUNK_B), 1)
        for h in range(H_B):
            wm_ref[h // 2, :, (h % 2) * CHUNK_B:(h % 2 + 1) * CHUNK_B] = jnp.where(
                c <= r, sw_ref[h], 0.0).astype(jnp.bfloat16)
        pa_ref[...] = _bdot(_rmsnorm(x_ref[0:half, :], ng_ref[...]), win_ref[...])

    @pl.when(i_id == 0)
    def _():
        kprev_ref[...] = jnp.zeros_like(kprev_ref)
        vprev_ref[...] = jnp.zeros_like(vprev_ref)
        st_ref[...] = jnp.zeros_like(st_ref)

    hb_ref[...] = _rmsnorm(x_ref[half:tb, :], ng_ref[...]).astype(jnp.bfloat16)
    hn_ref[...] = _rmsnorm(xn_ref[...], ng_ref[...]).astype(jnp.bfloat16)

    def in_proj_chunks(h_ref, p_ref):
        def chunk(r0, c0, c1):
            def run():
                p_ref[r0:r0 + IN_PROJ_ROWS, c0:c1] = jnp.dot(h_ref[r0:r0 + IN_PROJ_ROWS, :], win_ref[:, c0:c1],
                                                             preferred_element_type=jnp.float32)
            return run
        edges = list(range(0, D_IN_PAD, IN_PROJ_CHUNK)) + [D_IN_PAD]
        return [chunk(r0, c0, c1) for c0, c1 in zip(edges[:-1], edges[1:])
                for r0 in range(0, half, IN_PROJ_ROWS)]

    def out_proj_chunks(y_ref, r0, s):
        def chunk(c0, c1):
            def run():
                rows = slice(r0 + s * WINDOW, r0 + (s + 1) * WINDOW)
                xo_ref[rows, c0:c1] = x_ref[rows, c0:c1] + jnp.dot(
                    y_ref[s * WINDOW:(s + 1) * WINDOW, :], wout_ref[:, c0:c1],
                    preferred_element_type=jnp.float32)
            return run
        return [chunk(c0, c0 + OUT_PROJ_CHUNK) for c0 in range(0, D_MODEL, OUT_PROJ_CHUNK)]

    def final_norm(r0):
        if final:
            xo_ref[r0:r0 + half, :] = _rmsnorm(xo_ref[r0:r0 + half, :], fg_ref[...])

    lane = lax.broadcasted_iota(jnp.int32, (WINDOW, LANES), 1)
    low = lane < HALF
    low64 = lax.broadcasted_iota(jnp.int32, (GLA_CHUNK, LANES), 1) < HALF
    ri = lax.broadcasted_iota(jnp.int32, (LANES, LANES), 0)
    ci = lax.broadcasted_iota(jnp.int32, (LANES, LANES), 1)
    bd_tril = ((ri >= GLA_CHUNK) == (ci >= GLA_CHUNK)) & (ci <= ri)
    cum_mat = jnp.where(bd_tril, 1.0, 0.0).astype(jnp.bfloat16)

    def mixers(p_ref, y_ref, s, first):
        r0 = s * WINDOW
        rows = pl.ds(r0, WINDOW)

        ks = p_ref[rows, O_KA:O_KA + LANES] * (HD_A ** -0.5)
        k_new = [jnp.where(low, ks, 0.0).astype(jnp.bfloat16), jnp.where(low, 0.0, ks).astype(jnp.bfloat16)]
        v_new = p_ref[rows, O_VA:O_VA + LANES].T.astype(jnp.bfloat16)
        vcat = jnp.concatenate([vprev_ref[...], v_new], axis=1)
        q4 = jnp.concatenate([p_ref[rows, O_QA + g * LANES:O_QA + (g + 1) * LANES].astype(jnp.bfloat16)
                              for g in range(G_A)], axis=0)
        o_heads = []
        for kv in range(KV_A):
            kcat = jnp.concatenate([kprev_ref[kv], k_new[kv]], axis=0)
            sc = lax.dot_general(q4, kcat, _NT, preferred_element_type=jnp.float32) + bias_ref[first, kv]
            kprev_ref[kv] = k_new[kv]
            yield
            ps, invs = [], []
            for g in range(G_A):
                p, inv = _softmax_sink(sc[g * WINDOW:(g + 1) * WINDOW], sink_ref[layer, kv * G_A + g])
                ps.append(p.astype(jnp.bfloat16))
                invs.append(inv)
                if g % 2 == 1:
                    yield
            o_t = lax.dot_general(vcat[kv * HD_A:(kv + 1) * HD_A, :], jnp.concatenate(ps, axis=0), _NT,
                                  preferred_element_type=jnp.float32)
            o_heads.append((o_t, invs))
            yield
        vprev_ref[...] = v_new
        for g in range(G_A):
            o_pair = jnp.concatenate([o_heads[kv][0][:, g * WINDOW:(g + 1) * WINDOW] for kv in range(KV_A)],
                                     axis=0).T
            ya = o_pair * jnp.where(low, o_heads[0][1][g], o_heads[1][1][g])
            gate = p_ref[rows, O_GA + g * LANES:O_GA + (g + 1) * LANES]
            y_ref[rows, Y_A + g * LANES:Y_A + (g + 1) * LANES] = (ya * _silu(gate)).astype(jnp.bfloat16)
        yield

        vn = _layernorm(p_ref[rows, O_VB:O_VB + D_B], lng_ref[...], lnb_ref[...]).astype(jnp.bfloat16)
        yield
        for j in range(H_B // 2):
            vp = vn[:, j * LANES:(j + 1) * LANES]
            zero = jnp.zeros_like(vp)
            v_st = jnp.concatenate([jnp.where(low, vp, zero), jnp.where(low, zero, vp)], axis=0)
            mix = (jnp.dot(wm_ref[j], v_st, preferred_element_type=jnp.float32)
                   + bsf_ref[:, j * LANES:(j + 1) * LANES])
            u = p_ref[rows, O_UB + j * LANES:O_UB + (j + 1) * LANES]
            gate = p_ref[rows, O_GB + j * LANES:O_GB + (j + 1) * LANES]
            y_ref[rows, Y_B + j * LANES:Y_B + (j + 1) * LANES] = (u * mix * _silu(gate)).astype(jnp.bfloat16)
            if j % 2 == 1:
                yield

        z = _bdot(p_ref[rows, O_LR:O_LR + LANES], wup_ref[...]) + bup_ref[...]
        la = _log_sigmoid(z) * (1.0 / GLA_TAU)
        yield
        hi = la.astype(jnp.bfloat16)
        r1 = la - hi.astype(jnp.float32)
        mid = r1.astype(jnp.bfloat16)
        lo = (r1 - mid.astype(jnp.float32)).astype(jnp.bfloat16)
        c3 = jnp.dot(cum_mat, jnp.concatenate([hi, mid, lo], axis=1), preferred_element_type=jnp.float32)
        bc = (c3[:, 0:DK_TOT] + c3[:, DK_TOT:2 * DK_TOT]) + c3[:, 2 * DK_TOT:3 * DK_TOT]
        yield
        for c in range(WINDOW // GLA_CHUNK):
            rc = pl.ds(r0 + c * GLA_CHUNK, GLA_CHUNK)
            atts, o_inter, v_sts, kd_sts, decays, sts = [], [], [], [], [], []
            for p in range(H_C // 2):
                bp = bc[c * GLA_CHUNK:(c + 1) * GLA_CHUNK, p * LANES:(p + 1) * LANES]
                qp = p_ref[rc, O_QC + p * LANES:O_QC + (p + 1) * LANES]
                kp = p_ref[rc, O_KC + p * LANES:O_KC + (p + 1) * LANES]
                b_last = bp[GLA_CHUNK - 1:GLA_CHUNK, :]
                qd = qp * (DK_C ** -0.5) * jnp.exp(bp)
                ki = (kp * jnp.exp(-bp)).astype(jnp.bfloat16)
                kd = kp * jnp.exp(b_last - bp)
                q_st = jnp.concatenate([jnp.where(low64, qd, 0.0), jnp.where(low64, 0.0, qd)],
                                       axis=0).astype(jnp.bfloat16)
                st = st_ref[p]
                rhs = jnp.concatenate([ki, ki, st.astype(jnp.bfloat16)], axis=0)
                res = _bdot_nt(q_st, rhs)
                atts.append(jnp.where(bd_tril, res[:, 0:LANES], 0.0).astype(jnp.bfloat16))
                o_inter.append(res[:, LANES:])
                v0 = p_ref[rc, O_VC + (2 * p) * DV_C:O_VC + (2 * p + 1) * DV_C]
                v1 = p_ref[rc, O_VC + (2 * p + 1) * DV_C:O_VC + (2 * p + 2) * DV_C]
                v_sts.append(jnp.concatenate([v0, v1], axis=0))
                kd_sts.append(jnp.concatenate([jnp.where(low64, kd, 0.0), jnp.where(low64, 0.0, kd)],
                                              axis=0).astype(jnp.bfloat16))
                decays.append(jnp.exp(b_last))
                sts.append(st)
                yield
            zero = jnp.zeros((LANES, LANES), jnp.bfloat16)
            blockdiag = lambda a, b: jnp.concatenate([jnp.concatenate([a, zero], axis=1),
                                                      jnp.concatenate([zero, b], axis=1)], axis=0)
            v_b = [v.astype(jnp.bfloat16) for v in v_sts]
            o_all = jnp.dot(jnp.concatenate(atts, axis=1), blockdiag(*v_b),
                            preferred_element_type=jnp.float32)
            d_all = jnp.dot(jnp.concatenate([v.T.astype(jnp.bfloat16) for v in v_sts], axis=1),
                            blockdiag(*kd_sts), preferred_element_type=jnp.float32)
            for p in range(H_C // 2):
                st_ref[p] = sts[p] * decays[p] + d_all[:, p * LANES:(p + 1) * LANES]
                o = o_all[:, p * DV_C:(p + 1) * DV_C] + o_inter[p]
                for hh in range(2):
                    hd = 2 * p + hh
                    oh = o[hh * GLA_CHUNK:(hh + 1) * GLA_CHUNK]
                    on = _rmsnorm(oh, gn_ref[:, hd * DV_C:(hd + 1) * DV_C])
                    gate = p_ref[rc, O_GC + hd * DV_C:O_GC + (hd + 1) * DV_C]
                    y_ref[rc, Y_C + hd * DV_C:Y_C + (hd + 1) * DV_C] = (on * _silu(gate)).astype(jnp.bfloat16)
                yield

    def half_mixers(p_ref, y_ref, first):
        for s in range(n_sub):
            yield from mixers(p_ref, y_ref, s, first if s == 0 else 0)

    def run_phase(sections, chunks, late_chunks):
        n_sections = n_sub * MIXER_SECTIONS
        plan = [[] for _ in range(n_sections)]
        for idx, run in enumerate(chunks):
            plan[idx * n_sections // len(chunks)].append(run)
        for s, runs in late_chunks.items():
            k0 = (s + 1) * MIXER_SECTIONS
            for idx, run in enumerate(runs):
                plan[k0 + idx * (n_sections - k0) // len(runs)].append(run)
        for k in range(n_sections):
            next(sections)
            for run in plan[k]:
                run()
        assert next(sections, None) is None

    run_phase(half_mixers(pa_ref, ya_ref, jnp.where(i_id == 0, 1, 0)),
              in_proj_chunks(hb_ref, pb_ref),
              {s: out_proj_chunks(ya_ref, 0, s) for s in range(n_sub - 1)})
    run_phase(half_mixers(pb_ref, yb_ref, 0),
              out_proj_chunks(ya_ref, 0, n_sub - 1) + in_proj_chunks(hn_ref, pa_ref),
              {s: out_proj_chunks(yb_ref, half, s) for s in range(n_sub - 1)})
    final_norm(0)
    for run in out_proj_chunks(yb_ref, half, n_sub - 1):
        run()
    final_norm(half)

    @pl.when(i_id == n_i - 1)
    def _():
        ko_ref[...] = pb_ref[half - WINDOW:half, O_KA:O_KA + LANES].T
        vo_ref[...] = pb_ref[half - WINDOW:half, O_VA:O_VA + LANES].T
        for p in range(H_C // 2):
            so_ref[p] = st_ref[p].T


def _prompt_layer(x, ng, win, wout, bias_tab, sinks, sw, bsf, lng, lnb, wup, bup, gn, fg, *, layer, final):
    bsz, seq, _ = x.shape
    tb = min(TOKEN_BLOCK, seq)
    half = tb // 2
    assert seq % tb == 0 and half % WINDOW == 0
    n_i = seq // tb
    grid = (bsz, n_i)

    def next_half(b, i):
        lin = jnp.minimum(b * n_i + i + 1, bsz * n_i - 1)
        return lin // n_i, (lin % n_i) * 2, 0

    def const_spec(shape):
        nd = len(shape)
        return pl.BlockSpec(shape, lambda b, i: (0,) * nd, pipeline_mode=pl.Buffered(1))

    def layer_spec(shape):
        nd = len(shape)
        return pl.BlockSpec((None,) + shape, lambda b, i: (layer,) + (0,) * nd, pipeline_mode=pl.Buffered(1))

    kern = functools.partial(_prompt_layer_kernel, tb=tb, layer=layer, final=final)
    return pl.pallas_call(
        kern,
        grid=grid,
        in_specs=[
            pl.BlockSpec((None, tb, D_MODEL), lambda b, i: (b, i, 0)),
            pl.BlockSpec((None, half, D_MODEL), next_half),
            layer_spec((1, D_MODEL)),
            layer_spec((D_MODEL, D_IN_PAD)),
            layer_spec((D_MIX, D_MODEL)),
            const_spec((2, KV_A, G_A * WINDOW, 2 * WINDOW)),
            pl.BlockSpec(memory_space=pltpu.SMEM),
            layer_spec((H_B, CHUNK_B, CHUNK_B)),
            layer_spec((CHUNK_B, D_B)),
            layer_spec((1, D_B)),
            layer_spec((1, D_B)),
            layer_spec((LANES, DK_TOT)),
            layer_spec((1, DK_TOT)),
            layer_spec((1, D_C)),
            const_spec((1, D_MODEL)),
        ],
        out_specs=[
            pl.BlockSpec((None, tb, D_MODEL), lambda b, i: (b, i, 0)),
            pl.BlockSpec((None, LANES, WINDOW), lambda b, i: (b, 0, 0)),
            pl.BlockSpec((None, LANES, WINDOW), lambda b, i: (b, 0, 0)),
            pl.BlockSpec((None, H_C // 2, LANES, DV_C), lambda b, i: (b, 0, 0, 0)),
        ],
        out_shape=[
            jax.ShapeDtypeStruct((bsz, seq, D_MODEL), jnp.float32),
            jax.ShapeDtypeStruct((bsz, LANES, WINDOW), jnp.float32),
            jax.ShapeDtypeStruct((bsz, LANES, WINDOW), jnp.float32),
            jax.ShapeDtypeStruct((bsz, H_C // 2, LANES, DV_C), jnp.float32),
        ],
        scratch_shapes=[
            pltpu.VMEM((half, D_IN_PAD), jnp.float32),
            pltpu.VMEM((half, D_IN_PAD), jnp.float32),
            pltpu.VMEM((half, D_MODEL), jnp.bfloat16),
            pltpu.VMEM((half, D_MODEL), jnp.bfloat16),
            pltpu.VMEM((half, D_MIX), jnp.bfloat16),
            pltpu.VMEM((half, D_MIX), jnp.bfloat16),
            pltpu.VMEM((KV_A, WINDOW, LANES), jnp.bfloat16),
            pltpu.VMEM((WINDOW, LANES), jnp.bfloat16),
            pltpu.VMEM((H_C // 2, DV_C, LANES), jnp.float32),
            pltpu.VMEM((H_B // 2, CHUNK_B, 2 * CHUNK_B), jnp.bfloat16),
        ],
        compiler_params=pltpu.CompilerParams(
            dimension_semantics=("arbitrary", "arbitrary"),
            vmem_limit_bytes=VMEM_LIMIT_BYTES),
        name="prompt_layer",
    )(x, x, ng, win, wout, bias_tab, sinks, sw, bsf, lng, lnb, wup, bup, gn, fg)


def _sample_kernel(x_ref, ng_ref, win_ref, wout_ref, biass_ref, sinkc_ref, w00_ref, b0_ref, lng_ref,
                   lnb_ref, wup_ref, bup_ref, gn_ref, fg_ref, bk_ref, bv_ref, s0_ref,
                   xo_ref, nk_ref, nv_ref, so_ref, cv_ref,
                   xcur_ref, proj_ref, y_ref, qs_ref, os_ref, qd_ref, ea_ref, oc_ref, *, nseq, gs):
    l_id = pl.program_id(0)
    i_id = pl.program_id(1)
    n_l = pl.num_programs(0)
    n_i = pl.num_programs(1)
    lane = lax.broadcasted_iota(jnp.int32, (nseq, LANES), 1)
    low = lane < HALF

    @pl.when((l_id == 0) & (i_id == 0))
    def _():
        xcur_ref[...] = x_ref[...]

    @pl.when(i_id == 0)
    def _():
        h = _rmsnorm(xcur_ref[...], ng_ref[...])
        proj_ref[...] = _bdot(h, win_ref[...])
        for g in range(G_A):
            qp = proj_ref[:, O_QA + g * LANES:O_QA + (g + 1) * LANES] * (HD_A ** -0.5)
            for kv in range(KV_A):
                keep = low if kv == 0 else jnp.logical_not(low)
                qs_ref[pl.ds(kv * G_A + g, nseq, stride=H_A), :] = jnp.where(keep, qp, 0.0)
        vn = _layernorm(proj_ref[:, O_VB:O_VB + D_B], lng_ref[...], lnb_ref[...])
        cv_ref[...] = vn
        mix = w00_ref[...] * vn + b0_ref[...]
        y_ref[:, Y_B:Y_B + D_B] = (proj_ref[:, O_UB:O_UB + D_B] * mix) * _silu(proj_ref[:, O_GB:O_GB + D_B])
        z = _bdot(proj_ref[:, O_LR:O_LR + LANES], wup_ref[...]) + bup_ref[...]
        la = _log_sigmoid(z) * (1.0 / GLA_TAU)
        ea = jnp.exp(la)
        qd = proj_ref[:, O_QC:O_QC + DK_TOT] * (DK_C ** -0.5) * ea
        ki = proj_ref[:, O_KC:O_KC + DK_TOT] * jnp.exp(-la)
        qd_ref[...] = qd
        ea_ref[...] = ea
        prod = qd * ki
        lane_c = lax.broadcasted_iota(jnp.int32, prod.shape, 1)
        for hd in range(H_C):
            in_head = (lane_c >= hd * DK_C) & (lane_c < (hd + 1) * DK_C)
            att = jnp.sum(jnp.where(in_head, prod, 0.0), axis=-1, keepdims=True)
            oc_ref[:, hd * DV_C:(hd + 1) * DV_C] = att * proj_ref[:, O_VC + hd * DV_C:O_VC + (hd + 1) * DV_C]

    g0 = pl.multiple_of(i_id * gs, gs)
    grows = pl.ds(g0, gs)

    pad_rows = jnp.zeros((LANES - gs, LANES), jnp.float32)
    k_cols = jnp.concatenate([proj_ref[grows, O_KA:O_KA + LANES], pad_rows], axis=0).T
    v_cols = jnp.concatenate([proj_ref[grows, O_VA:O_VA + LANES], pad_rows], axis=0).T
    newest = lax.broadcasted_iota(jnp.int32, (LANES, WINDOW), 1) == WINDOW - 1
    for j in range(gs):
        kt = jnp.where(newest, k_cols[:, j:j + 1], pltpu.roll(bk_ref[j], WINDOW - 1, 1))
        vt = jnp.where(newest, v_cols[:, j:j + 1], pltpu.roll(bv_ref[j], WINDOW - 1, 1))
        nk_ref[j] = kt
        nv_ref[j] = vt
        r8 = pl.ds(pl.multiple_of((g0 + j) * H_A, H_A), H_A)
        sc = _bdot(qs_ref[r8, :], kt) + biass_ref[...]
        p, inv = _softmax_sink(sc, sinkc_ref[...])
        os_ref[r8, :] = _bdot_nt(p, vt) * inv

    pad_c = jnp.zeros((LANES - gs, DK_TOT), jnp.float32)
    ea_cols = jnp.concatenate([ea_ref[grows, :], pad_c], axis=0).T
    k_cols_c = jnp.concatenate([proj_ref[grows, O_KC:O_KC + DK_TOT], pad_c], axis=0).T
    qd_g = qd_ref[grows, :]
    row_g = lax.broadcasted_iota(jnp.int32, (gs, LANES), 0)
    low_g = lax.broadcasted_iota(jnp.int32, (gs, LANES), 1) < HALF
    seqs_per_dot = 2 * LANES // DK_C
    for hd in range(H_C):
        v_h = proj_ref[grows, O_VC + hd * DV_C:O_VC + (hd + 1) * DV_C]
        q_pair = qd_g[:, (hd // 2) * LANES:(hd // 2 + 1) * LANES]
        q_other = pltpu.roll(q_pair, HALF, 1)
        q_lo, q_hi = (q_pair, q_other) if hd % 2 == 0 else (q_other, q_pair)
        acc = jnp.zeros((gs, DV_C), jnp.float32)
        for j0 in range(0, gs, seqs_per_dot):
            tiles, lhs = [], []
            for j in range(j0, j0 + seqs_per_dot):
                srows = pl.ds(j * DK_TOT + hd * DK_C, DK_C)
                s_old = s0_ref[srows, :]
                tiles.append(s_old.astype(jnp.bfloat16))
                crows = slice(hd * DK_C, (hd + 1) * DK_C)
                so_ref[srows, :] = ea_cols[crows, j:j + 1] * s_old + k_cols_c[crows, j:j + 1] * v_h[j:j + 1, :]
            for j in range(j0, j0 + seqs_per_dot, 2):
                lhs.append(jnp.where((row_g == j) & low_g, q_lo,
                                     jnp.where((row_g == j + 1) & jnp.logical_not(low_g), q_hi, 0.0)))
            acc = acc + _bdot(jnp.concatenate(lhs, axis=1), jnp.concatenate(tiles, axis=0))
        oc_ref[grows, hd * DV_C:(hd + 1) * DV_C] += acc

    @pl.when(i_id == n_i - 1)
    def _():
        for g in range(G_A):
            o0 = os_ref[pl.ds(g, nseq, stride=H_A), :]
            o1 = os_ref[pl.ds(G_A + g, nseq, stride=H_A), :]
            gate = proj_ref[:, O_GA + g * LANES:O_GA + (g + 1) * LANES]
            y_ref[:, Y_A + g * LANES:Y_A + (g + 1) * LANES] = jnp.where(low, o0, o1) * _silu(gate)
        for hd in range(H_C):
            on = _rmsnorm(oc_ref[:, hd * DV_C:(hd + 1) * DV_C], gn_ref[:, hd * DV_C:(hd + 1) * DV_C])
            gate = proj_ref[:, O_GC + hd * DV_C:O_GC + (hd + 1) * DV_C]
            y_ref[:, Y_C + hd * DV_C:Y_C + (hd + 1) * DV_C] = on * _silu(gate)
        x_new = xcur_ref[...] + _bdot(y_ref[...], wout_ref[...])
        xcur_ref[...] = x_new

        @pl.when(l_id == n_l - 1)
        def _():
            xo_ref[...] = _rmsnorm(x_new, fg_ref[...])


def _sample_path(x, ng, win, wout, bias_s, sink_c, w00, b0, lng, lnb, wup, bup, gn, fg, buf_kt, buf_vt, s0):
    depth = win.shape[0]
    nseq = x.shape[0]
    gs = min(SEQ_GROUP, nseq)
    assert nseq % gs == 0 and gs % 8 == 0
    grid = (depth, nseq // gs)

    def const_spec(shape):
        nd = len(shape)
        return pl.BlockSpec(shape, lambda l, i: (0,) * nd)

    def layer_spec(shape):
        nd = len(shape)
        return pl.BlockSpec((None,) + shape, lambda l, i: (l,) + (0,) * nd)

    state_spec = pl.BlockSpec((None, gs, LANES, WINDOW), lambda l, i: (l, i, 0, 0))
    gla_spec = pl.BlockSpec((None, gs * DK_TOT, DV_C), lambda l, i: (l, i, 0))
    kern = functools.partial(_sample_kernel, nseq=nseq, gs=gs)
    return pl.pallas_call(
        kern,
        grid=grid,
        in_specs=[
            const_spec((nseq, D_MODEL)),
            layer_spec((1, D_MODEL)),
            layer_spec((D_MODEL, D_IN_PAD)),
            layer_spec((D_MIX, D_MODEL)),
            const_spec((H_A, WINDOW)),
            layer_spec((H_A, 1)),
            layer_spec((1, D_B)),
            layer_spec((1, D_B)),
            layer_spec((1, D_B)),
            layer_spec((1, D_B)),
            layer_spec((LANES, DK_TOT)),
            layer_spec((1, DK_TOT)),
            layer_spec((1, D_C)),
            const_spec((1, D_MODEL)),
            state_spec, state_spec, gla_spec,
        ],
        out_specs=[
            const_spec((nseq, D_MODEL)),
            state_spec, state_spec, gla_spec,
            layer_spec((nseq, D_B)),
        ],
        out_shape=[
            jax.ShapeDtypeStruct((nseq, D_MODEL), jnp.float32),
            jax.ShapeDtypeStruct((depth, nseq, LANES, WINDOW), jnp.float32),
            jax.ShapeDtypeStruct((depth, nseq, LANES, WINDOW), jnp.float32),
            jax.ShapeDtypeStruct((depth, nseq * DK_TOT, DV_C), jnp.float32),
            jax.ShapeDtypeStruct((depth, nseq, D_B), jnp.float32),
        ],
        scratch_shapes=[
            pltpu.VMEM((nseq, D_MODEL), jnp.float32),
            pltpu.VMEM((nseq, D_IN_PAD), jnp.float32),
            pltpu.VMEM((nseq, D_MIX), jnp.float32),
            pltpu.VMEM((nseq * H_A, LANES), jnp.float32),
            pltpu.VMEM((nseq * H_A, LANES), jnp.float32),
            pltpu.VMEM((nseq, DK_TOT), jnp.float32),
            pltpu.VMEM((nseq, DK_TOT), jnp.float32),
            pltpu.VMEM((nseq, D_C), jnp.float32),
        ],
        compiler_params=pltpu.CompilerParams(
            dimension_semantics=("arbitrary", "arbitrary"),
            vmem_limit_bytes=VMEM_LIMIT_BYTES),
        name="sample_path",
    )(x, ng, win, wout, bias_s, sink_c, w00, b0, lng, lnb, wup, bup, gn, fg, buf_kt, buf_vt, s0)


def _window_minor(state):
    depth, n = state.shape[:2]
    return jnp.transpose(state, (0, 1, 3, 4, 2)).reshape(depth, n, LANES, WINDOW)


def _window_major(state_t):
    depth, n = state_t.shape[:2]
    return jnp.transpose(state_t.reshape(depth, n, KV_A, HD_A, WINDOW), (0, 1, 4, 2, 3))


def kernel(x_prompt, x_sample, state_swa_k, state_swa_v, state_gla, rel_bias, norm_g, w_in, sinks, spatial_w,
           spatial_b, chunk_ln_g, chunk_ln_b, gla_w_up, gla_b_up, gla_norm_g, w_out, final_norm_g):
    depth = w_in.shape[0]
    nseq = x_sample.shape[0]
    bsz = x_prompt.shape[0]
    f32 = jnp.float32

    win, wout = _prepare_weights(w_in, w_out)
    wup = jnp.pad(gla_w_up, ((0, 0), (0, LANES - GLA_RANK), (0, 0))).astype(jnp.bfloat16)
    bsf = jnp.repeat(jnp.swapaxes(spatial_b, 1, 2), DH_B, axis=2).astype(f32)
    w00 = jnp.repeat(spatial_w[:, :, 0, 0], DH_B, axis=1).astype(f32)[:, None, :]
    b0 = jnp.repeat(spatial_b[:, :, 0], DH_B, axis=1).astype(f32)[:, None, :]
    sink_c = sinks.astype(f32)[:, :, None]
    row = lambda a: a.astype(f32)[:, None, :]
    ng, lng, lnb, bup, gn = row(norm_g), row(chunk_ln_g), row(chunk_ln_b), row(gla_b_up), row(gla_norm_g)
    fg = final_norm_g.astype(f32)[None, :]
    sw = spatial_w.astype(f32)
    sinks = sinks.astype(f32)

    bias_tab = _bias_table(rel_bias)
    bias_s = bias_tab[0].reshape(H_A, WINDOW, 2 * WINDOW)[:, WINDOW - 1, WINDOW:]

    xp = x_prompt
    kp_l, vp_l, sp_l = [], [], []
    for l in range(depth):
        xp, kp, vp, sp = _prompt_layer(xp, ng, win, wout, bias_tab, sinks, sw, bsf, lng, lnb, wup, bup, gn, fg,
                                       layer=l, final=l == depth - 1)
        kp_l.append(kp); vp_l.append(vp); sp_l.append(sp)

    xs, ks_t, vs_t, ss, cv = _sample_path(
        x_sample.reshape(nseq, D_MODEL), ng, win, wout, bias_s, sink_c, w00, b0, lng, lnb, wup, bup, gn, fg,
        _window_minor(state_swa_k), _window_minor(state_swa_v), state_gla.reshape(depth, nseq * DK_TOT, DV_C))

    return (xp,
            xs.reshape(nseq, 1, D_MODEL),
            _window_major(jnp.stack(kp_l)),
            _window_major(jnp.stack(vp_l)),
            jnp.stack(sp_l).reshape(depth, bsz, H_C, DK_C, DV_C),
            _window_major(ks_t),
            _window_major(vs_t),
            ss.reshape(depth, nseq, H_C, DK_C, DV_C),
            cv.reshape(depth, nseq, 1, D_B))
```

```python
import functools

import numpy as np
import jax
import jax.numpy as jnp
from jax import lax
from jax.experimental import pallas as pl
from jax.experimental.pallas import tpu as pltpu

D_MODEL = 1024
D_A, HD_A, H_A, KV_A, G_A = 512, 64, 8, 2, 4
WINDOW, N_BUCKETS, MAX_DIST = 128, 32, 128
D_B, H_B, DH_B, CHUNK_B = 512, 8, 64, 128
D_C, H_C, DK_TOT, DK_C, DV_C = 512, 4, 256, 64, 128
GLA_RANK, GLA_TAU, GLA_CHUNK = 16, 16.0, 64
D_MIX = D_A + D_B + D_C
EPS = 1e-6
NEG = -1e30
SPLITS = [D_A, KV_A * HD_A, KV_A * HD_A, D_A, D_B, D_B, D_B, DK_TOT, DK_TOT, D_C, D_C, GLA_RANK]
D_IN = sum(SPLITS)

LANES = 128
HALF = LANES // 2
D_IN_PAD = ((D_IN + LANES - 1) // LANES) * LANES
VMEM_LIMIT_BYTES = 56 * 1024 * 1024

O_QA, O_KA, O_VA, O_GA = 0, 512, 640, 768
O_UB, O_VB, O_GB = 1280, 1792, 2304
O_QC, O_KC, O_VC, O_GC, O_LR = 2816, 3072, 3328, 3840, 4352
Y_A, Y_B, Y_C = 0, D_A, D_A + D_B

HEAD_PERM = [0, 4, 1, 5, 2, 6, 3, 7]

TOKEN_BLOCK = 512
SEQ_GROUP = 16
IN_PROJ_CHUNK = 256
IN_PROJ_ROWS = 256
OUT_PROJ_CHUNK = 256
MIXER_SECTIONS = 22
PREP_SPLIT = 2

_NT = (((1,), (1,)), ((), ()))


def _t5_bucket(dist):
    n = np.maximum(dist, 0)
    max_exact = N_BUCKETS // 2
    large = max_exact + (np.log(np.maximum(n, 1) / max_exact) / np.log(MAX_DIST / max_exact)
                         * (N_BUCKETS - max_exact)).astype(np.int32)
    large = np.minimum(large, N_BUCKETS - 1)
    return np.where(n < max_exact, n, large).astype(np.int32)


def _silu(x):
    return x * (1.0 / (1.0 + jnp.exp(-x)))


def _log_sigmoid(x):
    return jnp.minimum(x, 0.0) - jnp.log1p(jnp.exp(-jnp.abs(x)))


def _bdot(a, b):
    return jnp.dot(a.astype(jnp.bfloat16), b.astype(jnp.bfloat16), preferred_element_type=jnp.float32)


def _bdot_nt(a, b):
    return lax.dot_general(a.astype(jnp.bfloat16), b.astype(jnp.bfloat16), _NT,
                           preferred_element_type=jnp.float32)


def _rmsnorm(x, g):
    return x * lax.rsqrt(jnp.mean(x * x, axis=-1, keepdims=True) + EPS) * g


def _layernorm(v, g, b):
    mu = jnp.mean(v, axis=-1, keepdims=True)
    xc = v - mu
    var = jnp.mean(xc * xc, axis=-1, keepdims=True)
    return xc * lax.rsqrt(var + EPS) * g + b


def _softmax_sink(s, sink):
    m = jnp.maximum(jnp.max(s, axis=-1, keepdims=True), sink)
    p = jnp.exp(s - m)
    den = jnp.sum(p, axis=-1, keepdims=True) + jnp.exp(sink - m)
    return p, 1.0 / den


def _bias_table_kernel(rb_ref, bucket_ref, band_ref, out_ref):
    bucket = bucket_ref[...]
    band = band_ref[...] > 0
    own = lax.broadcasted_iota(jnp.int32, bucket.shape, 1) >= WINDOW
    for h in range(H_A):
        acc = jnp.zeros(bucket.shape, jnp.float32)
        for b in range(N_BUCKETS):
            acc = jnp.where(bucket == b, rb_ref[b, h], acc)
        kv, g = divmod(h, G_A)
        rows = pl.ds(g * WINDOW, WINDOW)
        out_ref[0, kv, rows, :] = jnp.where(band, acc, NEG)
        out_ref[1, kv, rows, :] = jnp.where(band & own, acc, NEG)


def _bias_table(rel_bias):
    i = np.arange(WINDOW)[:, None]
    j = np.arange(2 * WINDOW)[None, :]
    dist = i + WINDOW - j
    band = ((dist >= 0) & (dist < WINDOW)).astype(np.int32)
    return pl.pallas_call(
        _bias_table_kernel,
        out_shape=jax.ShapeDtypeStruct((2, KV_A, G_A * WINDOW, 2 * WINDOW), jnp.float32),
        in_specs=[pl.BlockSpec(memory_space=pltpu.SMEM),
                  pl.BlockSpec(memory_space=pltpu.VMEM),
                  pl.BlockSpec(memory_space=pltpu.VMEM)],
        out_specs=pl.BlockSpec(memory_space=pltpu.VMEM),
        name="rel_bias_table",
    )(rel_bias.astype(jnp.float32), jnp.asarray(_t5_bucket(dist)), jnp.asarray(band))


def _win_prep_kernel(wt_ref, lr_ref, o_ref):
    c = pl.program_id(1)
    tiles = wt_ref.shape[0] // LANES
    for cc in range(PREP_SPLIT):
        @pl.when(c == cc)
        def _(cc=cc):
            for t in range(tiles):
                col = (cc * tiles + t) * LANES
                base = next((b for b in (O_QA, O_GA) if b <= col < b + D_A), None)
                if base is None:
                    src = wt_ref[t * LANES:(t + 1) * LANES, :]
                else:
                    g = (col - base) // LANES
                    r0 = base - cc * tiles * LANES
                    assert r0 >= 0 and r0 + D_A <= wt_ref.shape[0]
                    src = jnp.concatenate([wt_ref[r0 + g * HD_A:r0 + (g + 1) * HD_A, :],
                                           wt_ref[r0 + (G_A + g) * HD_A:r0 + (G_A + g + 1) * HD_A, :]], axis=0)
                o_ref[:, col:col + LANES] = src.T.astype(jnp.bfloat16)
            if cc == PREP_SPLIT - 1:
                o_ref[:, O_LR:D_IN_PAD] = lr_ref[...].T.astype(jnp.bfloat16)


def _wout_prep_kernel(w_ref, o_ref):
    for j, h in enumerate(HEAD_PERM):
        o_ref[j * HD_A:(j + 1) * HD_A, :] = w_ref[h * HD_A:(h + 1) * HD_A, :].astype(jnp.bfloat16)
    o_ref[Y_B:, :] = w_ref[Y_B:, :].astype(jnp.bfloat16)


def _prepare_weights(w_in, w_out):
    depth = w_in.shape[0]
    cols = D_MODEL // PREP_SPLIT
    wt = jnp.swapaxes(w_in, 1, 2)
    assert O_LR % (PREP_SPLIT * LANES) == 0
    rows = O_LR // PREP_SPLIT
    wt_lr = jnp.pad(wt[:, O_LR:, :], ((0, 0), (0, D_IN_PAD - D_IN), (0, 0)))
    win = pl.pallas_call(
        _win_prep_kernel,
        grid=(depth, PREP_SPLIT),
        in_specs=[pl.BlockSpec((None, rows, D_MODEL), lambda l, r: (l, r, 0)),
                  pl.BlockSpec((None, LANES, D_MODEL), lambda l, r: (l, 0, 0))],
        out_specs=pl.BlockSpec((None, D_MODEL, D_IN_PAD), lambda l, r: (l, 0, 0)),
        out_shape=jax.ShapeDtypeStruct((depth, D_MODEL, D_IN_PAD), jnp.bfloat16),
        compiler_params=pltpu.CompilerParams(dimension_semantics=("arbitrary", "arbitrary"),
                                             vmem_limit_bytes=VMEM_LIMIT_BYTES),
        name="w_in_prep",
    )(wt, wt_lr)
    wout = pl.pallas_call(
        _wout_prep_kernel,
        grid=(depth, PREP_SPLIT),
        in_specs=[pl.BlockSpec((None, D_MIX, cols), lambda l, c: (l, 0, c))],
        out_specs=pl.BlockSpec((None, D_MIX, cols), lambda l, c: (l, 0, c)),
        out_shape=jax.ShapeDtypeStruct((depth, D_MIX, D_MODEL), jnp.bfloat16),
        compiler_params=pltpu.CompilerParams(dimension_semantics=("arbitrary", "arbitrary"),
                                             vmem_limit_bytes=VMEM_LIMIT_BYTES),
        name="w_out_prep",
    )(w_out)
    return win, wout


def _prompt_layer_kernel(x_ref, xn_ref, ng_ref, win_ref, wout_ref, bias_ref, sink_ref, sw_ref, bsf_ref, lng_ref,
                         lnb_ref, wup_ref, bup_ref, gn_ref, fg_ref,
                         xo_ref, ko_ref, vo_ref, so_ref,
                         pa_ref, pb_ref, hb_ref, hn_ref, ya_ref, yb_ref, kprev_ref, vprev_ref, st_ref, wm_ref,
                         *, tb, layer, final):
    b_id = pl.program_id(0)
    i_id = pl.program_id(1)
    n_i = pl.num_programs(1)
    half = tb // 2
    n_sub = half // WINDOW

    @pl.when((b_id == 0) & (i_id == 0))
    def _():
        r = lax.broadcasted_iota(jnp.int32, (CHUNK_B, CHUNK_B), 0)
        c = lax.broadcasted_iota(jnp.int32, (CHUNK_B, CHUNK_B), 1)
        for h in range(H_B):
            wm_ref[h // 2, :, (h % 2) * CHUNK_B:(h % 2 + 1) * CHUNK_B] = jnp.where(
                c <= r, sw_ref[h], 0.0).astype(jnp.bfloat16)
        pa_ref[...] = _bdot(_rmsnorm(x_ref[0:half, :], ng_ref[...]), win_ref[...])

    @pl.when(i_id == 0)
    def _():
        kprev_ref[...] = jnp.zeros_like(kprev_ref)
        vprev_ref[...] = jnp.zeros_like(vprev_ref)
        st_ref[...] = jnp.zeros_like(st_ref)

    hb_ref[...] = _rmsnorm(x_ref[half:tb, :], ng_ref[...]).astype(jnp.bfloat16)
    hn_ref[...] = _rmsnorm(xn_ref[...], ng_ref[...]).astype(jnp.bfloat16)

    def in_proj_chunks(h_ref, p_ref):
        def chunk(r0, c0, c1):
            def run():
                p_ref[r0:r0 + IN_PROJ_ROWS, c0:c1] = jnp.dot(h_ref[r0:r0 + IN_PROJ_ROWS, :], win_ref[:, c0:c1],
                                                             preferred_element_type=jnp.float32)
            return run
        edges = list(range(0, D_IN_PAD, IN_PROJ_CHUNK)) + [D_IN_PAD]
        return [chunk(r0, c0, c1) for c0, c1 in zip(edges[:-1], edges[1:])
                for r0 in range(0, half, IN_PROJ_ROWS)]

    def out_proj_chunks(y_ref, r0, s):
        def chunk(c0, c1):
            def run():
                rows = slice(r0 + s * WINDOW, r0 + (s + 1) * WINDOW)
                xo_ref[rows, c0:c1] = x_ref[rows, c0:c1] + jnp.dot(
                    y_ref[s * WINDOW:(s + 1) * WINDOW, :], wout_ref[:, c0:c1],
                    preferred_element_type=jnp.float32)
            return run
        return [chunk(c0, c0 + OUT_PROJ_CHUNK) for c0 in range(0, D_MODEL, OUT_PROJ_CHUNK)]

    def final_norm(r0):
        if final:
            xo_ref[r0:r0 + half, :] = _rmsnorm(xo_ref[r0:r0 + half, :], fg_ref[...])

    lane = lax.broadcasted_iota(jnp.int32, (WINDOW, LANES), 1)
    low = lane < HALF
    low64 = lax.broadcasted_iota(jnp.int32, (GLA_CHUNK, LANES), 1) < HALF
    ri = lax.broadcasted_iota(jnp.int32, (LANES, LANES), 0)
    ci = lax.broadcasted_iota(jnp.int32, (LANES, LANES), 1)
    bd_tril = ((ri >= GLA_CHUNK) == (ci >= GLA_CHUNK)) & (ci <= ri)
    cum_mat = jnp.where(bd_tril, 1.0, 0.0).astype(jnp.bfloat16)
    ri2 = lax.broadcasted_iota(jnp.int32, (LANES, 2 * LANES), 0)
    ci2 = lax.broadcasted_iota(jnp.int32, (LANES, 2 * LANES), 1) & (LANES - 1)
    bd_tril2 = ((ri2 >= GLA_CHUNK) == (ci2 >= GLA_CHUNK)) & (ci2 <= ri2)

    def mixers(p_ref, y_ref, s, first):
        r0 = s * WINDOW
        rows = pl.ds(r0, WINDOW)

        ks = p_ref[rows, O_KA:O_KA + LANES] * (HD_A ** -0.5)
        k_new = [jnp.where(low, ks, 0.0).astype(jnp.bfloat16), jnp.where(low, 0.0, ks).astype(jnp.bfloat16)]
        v_new = p_ref[rows, O_VA:O_VA + LANES].T.astype(jnp.bfloat16)
        vcat = jnp.concatenate([carry["v"], v_new], axis=1)
        q4 = jnp.concatenate([p_ref[rows, O_QA + g * LANES:O_QA + (g + 1) * LANES].astype(jnp.bfloat16)
                              for g in range(G_A)], axis=0)
        o_heads = []
        for kv in range(KV_A):
            kcat = jnp.concatenate([carry["k"][kv], k_new[kv]], axis=0)
            sc = lax.dot_general(q4, kcat, _NT, preferred_element_type=jnp.float32) + bias_ref[first, kv]
            carry["k"][kv] = k_new[kv]
            yield
            ps, invs = [], []
            for g in range(G_A):
                p, inv = _softmax_sink(sc[g * WINDOW:(g + 1) * WINDOW], sink_ref[layer, kv * G_A + g])
                ps.append(p.astype(jnp.bfloat16))
                invs.append(inv)
                if g % 2 == 1:
                    yield
            o_t = lax.dot_general(vcat[kv * HD_A:(kv + 1) * HD_A, :], jnp.concatenate(ps, axis=0), _NT,
                                  preferred_element_type=jnp.float32)
            o_heads.append((o_t, invs))
            yield
        carry["v"] = v_new
        for g in range(G_A):
            o_pair = jnp.concatenate([o_heads[kv][0][:, g * WINDOW:(g + 1) * WINDOW] for kv in range(KV_A)],
                                     axis=0).T
            ya = o_pair * jnp.where(low, o_heads[0][1][g], o_heads[1][1][g])
            gate = p_ref[rows, O_GA + g * LANES:O_GA + (g + 1) * LANES]
            y_ref[rows, Y_A + g * LANES:Y_A + (g + 1) * LANES] = (ya * _silu(gate)).astype(jnp.bfloat16)
        yield

        vn = _layernorm(p_ref[rows, O_VB:O_VB + D_B], lng_ref[...], lnb_ref[...]).astype(jnp.bfloat16)
        yield
        for j in range(H_B // 2):
            vp = vn[:, j * LANES:(j + 1) * LANES]
            zero = jnp.zeros_like(vp)
            v_st = jnp.concatenate([jnp.where(low, vp, zero), jnp.where(low, zero, vp)], axis=0)
            mix = (jnp.dot(wm_ref[j], v_st, preferred_element_type=jnp.float32)
                   + bsf_ref[:, j * LANES:(j + 1) * LANES])
            u = p_ref[rows, O_UB + j * LANES:O_UB + (j + 1) * LANES]
            gate = p_ref[rows, O_GB + j * LANES:O_GB + (j + 1) * LANES]
            y_ref[rows, Y_B + j * LANES:Y_B + (j + 1) * LANES] = (u * mix * _silu(gate)).astype(jnp.bfloat16)
            if j % 2 == 1:
                yield

        z = _bdot(p_ref[rows, O_LR:O_LR + LANES], wup_ref[...]) + bup_ref[...]
        la = _log_sigmoid(z) * (1.0 / GLA_TAU)
        yield
        hi = la.astype(jnp.bfloat16)
        r1 = la - hi.astype(jnp.float32)
        mid = r1.astype(jnp.bfloat16)
        lo = (r1 - mid.astype(jnp.float32)).astype(jnp.bfloat16)
        c3 = jnp.dot(cum_mat, jnp.concatenate([hi, mid, lo], axis=1), preferred_element_type=jnp.float32)
        bc = (c3[:, 0:DK_TOT] + c3[:, DK_TOT:2 * DK_TOT]) + c3[:, 2 * DK_TOT:3 * DK_TOT]
        yield
        for c in range(WINDOW // GLA_CHUNK):
            rc = pl.ds(r0 + c * GLA_CHUNK, GLA_CHUNK)
            q_sts, k_sts, v_sts, kd_sts, decays, sts = [], [], [], [], [], []
            for p in range(H_C // 2):
                bp = bc[c * GLA_CHUNK:(c + 1) * GLA_CHUNK, p * LANES:(p + 1) * LANES]
                qp = p_ref[rc, O_QC + p * LANES:O_QC + (p + 1) * LANES]
                kp = p_ref[rc, O_KC + p * LANES:O_KC + (p + 1) * LANES]
                b_last = bp[GLA_CHUNK - 1:GLA_CHUNK, :]
                qd = qp * (DK_C ** -0.5) * jnp.exp(bp)
                ki = (kp * jnp.exp(-bp)).astype(jnp.bfloat16)
                kd = kp * jnp.exp(b_last - bp)
                q_sts.append(jnp.concatenate([jnp.where(low64, qd, 0.0), jnp.where(low64, 0.0, qd)],
                                             axis=0).astype(jnp.bfloat16))
                k_sts.append(jnp.concatenate([ki, ki], axis=0))
                v0 = p_ref[rc, O_VC + (2 * p) * DV_C:O_VC + (2 * p + 1) * DV_C]
                v1 = p_ref[rc, O_VC + (2 * p + 1) * DV_C:O_VC + (2 * p + 2) * DV_C]
                v_sts.append(jnp.concatenate([v0, v1], axis=0))
                kd_sts.append(jnp.concatenate([jnp.where(low64, kd, 0.0), jnp.where(low64, 0.0, kd)],
                                              axis=0).astype(jnp.bfloat16))
                decays.append(jnp.exp(b_last))
                yield
            zero = jnp.zeros((LANES, LANES), jnp.bfloat16)
            blockdiag = lambda a, b: jnp.concatenate([jnp.concatenate([a, zero], axis=1),
                                                      jnp.concatenate([zero, b], axis=1)], axis=0)
            q_all = jnp.concatenate(q_sts, axis=1)
            att = lax.dot_general(q_all, blockdiag(*k_sts), _NT, preferred_element_type=jnp.float32)
            att = jnp.where(bd_tril2, att, 0.0).astype(jnp.bfloat16)
            v_b = [v.astype(jnp.bfloat16) for v in v_sts]
            o_all = jnp.dot(att, blockdiag(*v_b), preferred_element_type=jnp.float32)
            d_all = jnp.dot(jnp.concatenate([v.T.astype(jnp.bfloat16) for v in v_sts], axis=1),
                            blockdiag(*kd_sts), preferred_element_type=jnp.float32)
            sts = carry["st"]
            o_inter = lax.dot_general(q_all, blockdiag(*[st.astype(jnp.bfloat16) for st in sts]), _NT,
                                      preferred_element_type=jnp.float32)
            o_all = o_all + o_inter
            carry["st"] = [sts[p] * decays[p] + d_all[:, p * LANES:(p + 1) * LANES] for p in range(H_C // 2)]
            for p in range(H_C // 2):
                o = o_all[:, p * DV_C:(p + 1) * DV_C]
                for hh in range(2):
                    hd = 2 * p + hh
                    oh = o[hh * GLA_CHUNK:(hh + 1) * GLA_CHUNK]
                    on = _rmsnorm(oh, gn_ref[:, hd * DV_C:(hd + 1) * DV_C])
                    gate = p_ref[rc, O_GC + hd * DV_C:O_GC + (hd + 1) * DV_C]
                    y_ref[rc, Y_C + hd * DV_C:Y_C + (hd + 1) * DV_C] = (on * _silu(gate)).astype(jnp.bfloat16)
                yield

    def half_mixers(p_ref, y_ref, first):
        for s in range(n_sub):
            yield from mixers(p_ref, y_ref, s, first if s == 0 else 0)

    def run_phase(sections, chunks, late_chunks):
        n_sections = n_sub * MIXER_SECTIONS
        plan = [[] for _ in range(n_sections)]
        for idx, run in enumerate(chunks):
            plan[idx * n_sections // len(chunks)].append(run)
        for s, runs in late_chunks.items():
            k0 = (s + 1) * MIXER_SECTIONS
            for idx, run in enumerate(runs):
                plan[k0 + idx * (n_sections - k0) // len(runs)].append(run)
        for k in range(n_sections):
            next(sections)
            for run in plan[k]:
                run()
        assert next(sections, None) is None

    carry = {"k": [kprev_ref[kv] for kv in range(KV_A)], "v": vprev_ref[...],
             "st": [st_ref[p] for p in range(H_C // 2)]}
    run_phase(half_mixers(pa_ref, ya_ref, jnp.where(i_id == 0, 1, 0)),
              in_proj_chunks(hb_ref, pb_ref),
              {s: out_proj_chunks(ya_ref, 0, s) for s in range(n_sub - 1)})
    run_phase(half_mixers(pb_ref, yb_ref, 0),
              out_proj_chunks(ya_ref, 0, n_sub - 1) + in_proj_chunks(hn_ref, pa_ref),
              {s: out_proj_chunks(yb_ref, half, s) for s in range(n_sub - 1)})
    final_norm(0)
    for run in out_proj_chunks(yb_ref, half, n_sub - 1):
        run()
    final_norm(half)
    for kv in range(KV_A):
        kprev_ref[kv] = carry["k"][kv]
    vprev_ref[...] = carry["v"]
    for p in range(H_C // 2):
        st_ref[p] = carry["st"][p]

    @pl.when(i_id == n_i - 1)
    def _():
        ko_ref[...] = pb_ref[half - WINDOW:half, O_KA:O_KA + LANES].T
        vo_ref[...] = pb_ref[half - WINDOW:half, O_VA:O_VA + LANES].T
        for p in range(H_C // 2):
            so_ref[p] = st_ref[p].T


def _prompt_layer(x, ng, win, wout, bias_tab, sinks, sw, bsf, lng, lnb, wup, bup, gn, fg, *, layer, final):
    bsz, seq, _ = x.shape
    tb = min(TOKEN_BLOCK, seq)
    half = tb // 2
    assert seq % tb == 0 and half % WINDOW == 0
    n_i = seq // tb
    grid = (bsz, n_i)

    def next_half(b, i):
        lin = jnp.minimum(b * n_i + i + 1, bsz * n_i - 1)
        return lin // n_i, (lin % n_i) * 2, 0

    def const_spec(shape):
        nd = len(shape)
        return pl.BlockSpec(shape, lambda b, i: (0,) * nd, pipeline_mode=pl.Buffered(1))

    def layer_spec(shape):
        nd = len(shape)
        return pl.BlockSpec((None,) + shape, lambda b, i: (layer,) + (0,) * nd, pipeline_mode=pl.Buffered(1))

    kern = functools.partial(_prompt_layer_kernel, tb=tb, layer=layer, final=final)
    return pl.pallas_call(
        kern,
        grid=grid,
        in_specs=[
            pl.BlockSpec((None, tb, D_MODEL), lambda b, i: (b, i, 0)),
            pl.BlockSpec((None, half, D_MODEL), next_half),
            layer_spec((1, D_MODEL)),
            layer_spec((D_MODEL, D_IN_PAD)),
            layer_spec((D_MIX, D_MODEL)),
            const_spec((2, KV_A, G_A * WINDOW, 2 * WINDOW)),
            pl.BlockSpec(memory_space=pltpu.SMEM),
            layer_spec((H_B, CHUNK_B, CHUNK_B)),
            layer_spec((CHUNK_B, D_B)),
            layer_spec((1, D_B)),
            layer_spec((1, D_B)),
            layer_spec((LANES, DK_TOT)),
            layer_spec((1, DK_TOT)),
            layer_spec((1, D_C)),
            const_spec((1, D_MODEL)),
        ],
        out_specs=[
            pl.BlockSpec((None, tb, D_MODEL), lambda b, i: (b, i, 0)),
            pl.BlockSpec((None, LANES, WINDOW), lambda b, i: (b, 0, 0)),
            pl.BlockSpec((None, LANES, WINDOW), lambda b, i: (b, 0, 0)),
            pl.BlockSpec((None, H_C // 2, LANES, DV_C), lambda b, i: (b, 0, 0, 0)),
        ],
        out_shape=[
            jax.ShapeDtypeStruct((bsz, seq, D_MODEL), jnp.float32),
            jax.ShapeDtypeStruct((bsz, LANES, WINDOW), jnp.float32),
            jax.ShapeDtypeStruct((bsz, LANES, WINDOW), jnp.float32),
            jax.ShapeDtypeStruct((bsz, H_C // 2, LANES, DV_C), jnp.float32),
        ],
        scratch_shapes=[
            pltpu.VMEM((half, D_IN_PAD), jnp.float32),
            pltpu.VMEM((half, D_IN_PAD), jnp.float32),
            pltpu.VMEM((half, D_MODEL), jnp.bfloat16),
            pltpu.VMEM((half, D_MODEL), jnp.bfloat16),
            pltpu.VMEM((half, D_MIX), jnp.bfloat16),
            pltpu.VMEM((half, D_MIX), jnp.bfloat16),
            pltpu.VMEM((KV_A, WINDOW, LANES), jnp.bfloat16),
            pltpu.VMEM((WINDOW, LANES), jnp.bfloat16),
            pltpu.VMEM((H_C // 2, DV_C, LANES), jnp.float32),
            pltpu.VMEM((H_B // 2, CHUNK_B, 2 * CHUNK_B), jnp.bfloat16),
        ],
        compiler_params=pltpu.CompilerParams(
            dimension_semantics=("arbitrary", "arbitrary"),
            vmem_limit_bytes=VMEM_LIMIT_BYTES),
        name="prompt_layer",
    )(x, x, ng, win, wout, bias_tab, sinks, sw, bsf, lng, lnb, wup, bup, gn, fg)


def _sample_kernel(x_ref, ng_ref, win_ref, wout_ref, biass_ref, sinkc_ref, w00_ref, b0_ref, lng_ref,
                   lnb_ref, wup_ref, bup_ref, gn_ref, fg_ref, bk_ref, bv_ref, s0_ref,
                   xo_ref, nk_ref, nv_ref, so_ref, cv_ref,
                   xcur_ref, proj_ref, y_ref, qs_ref, os_ref, qd_ref, ea_ref, oc_ref, *, nseq, gs):
    l_id = pl.program_id(0)
    i_id = pl.program_id(1)
    n_l = pl.num_programs(0)
    n_i = pl.num_programs(1)
    lane = lax.broadcasted_iota(jnp.int32, (nseq, LANES), 1)
    low = lane < HALF

    @pl.when((l_id == 0) & (i_id == 0))
    def _():
        xcur_ref[...] = x_ref[...]

    @pl.when(i_id == 0)
    def _():
        h = _rmsnorm(xcur_ref[...], ng_ref[...])
        proj_ref[...] = _bdot(h, win_ref[...])
        for g in range(G_A):
            qp = proj_ref[:, O_QA + g * LANES:O_QA + (g + 1) * LANES] * (HD_A ** -0.5)
            for kv in range(KV_A):
                keep = low if kv == 0 else jnp.logical_not(low)
                qs_ref[pl.ds(kv * G_A + g, nseq, stride=H_A), :] = jnp.where(keep, qp, 0.0)
        vn = _layernorm(proj_ref[:, O_VB:O_VB + D_B], lng_ref[...], lnb_ref[...])
        cv_ref[...] = vn
        mix = w00_ref[...] * vn + b0_ref[...]
        y_ref[:, Y_B:Y_B + D_B] = (proj_ref[:, O_UB:O_UB + D_B] * mix) * _silu(proj_ref[:, O_GB:O_GB + D_B])
        z = _bdot(proj_ref[:, O_LR:O_LR + LANES], wup_ref[...]) + bup_ref[...]
        la = _log_sigmoid(z) * (1.0 / GLA_TAU)
        ea = jnp.exp(la)
        qd = proj_ref[:, O_QC:O_QC + DK_TOT] * (DK_C ** -0.5) * ea
        ki = proj_ref[:, O_KC:O_KC + DK_TOT] * jnp.exp(-la)
        qd_ref[...] = qd
        ea_ref[...] = ea
        prod = qd * ki
        lane_c = lax.broadcasted_iota(jnp.int32, prod.shape, 1)
        for hd in range(H_C):
            in_head = (lane_c >= hd * DK_C) & (lane_c < (hd + 1) * DK_C)
            att = jnp.sum(jnp.where(in_head, prod, 0.0), axis=-1, keepdims=True)
            oc_ref[:, hd * DV_C:(hd + 1) * DV_C] = att * proj_ref[:, O_VC + hd * DV_C:O_VC + (hd + 1) * DV_C]

    g0 = pl.multiple_of(i_id * gs, gs)
    grows = pl.ds(g0, gs)

    pad_rows = jnp.zeros((LANES - gs, LANES), jnp.float32)
    k_cols = jnp.concatenate([proj_ref[grows, O_KA:O_KA + LANES], pad_rows], axis=0).T
    v_cols = jnp.concatenate([proj_ref[grows, O_VA:O_VA + LANES], pad_rows], axis=0).T
    newest = lax.broadcasted_iota(jnp.int32, (LANES, WINDOW), 1) == WINDOW - 1
    for j in range(gs):
        kt = jnp.where(newest, k_cols[:, j:j + 1], pltpu.roll(bk_ref[j], WINDOW - 1, 1))
        vt = jnp.where(newest, v_cols[:, j:j + 1], pltpu.roll(bv_ref[j], WINDOW - 1, 1))
        nk_ref[j] = kt
        nv_ref[j] = vt
        r8 = pl.ds(pl.multiple_of((g0 + j) * H_A, H_A), H_A)
        sc = _bdot(qs_ref[r8, :], kt) + biass_ref[...]
        p, inv = _softmax_sink(sc, sinkc_ref[...])
        os_ref[r8, :] = _bdot_nt(p, vt) * inv

    pad_c = jnp.zeros((LANES - gs, DK_TOT), jnp.float32)
    ea_cols = jnp.concatenate([ea_ref[grows, :], pad_c], axis=0).T
    k_cols_c = jnp.concatenate([proj_ref[grows, O_KC:O_KC + DK_TOT], pad_c], axis=0).T
    qd_g = qd_ref[grows, :]
    row_g = lax.broadcasted_iota(jnp.int32, (gs, LANES), 0)
    low_g = lax.broadcasted_iota(jnp.int32, (gs, LANES), 1) < HALF
    seqs_per_dot = 2 * LANES // DK_C
    for hd in range(H_C):
        v_h = proj_ref[grows, O_VC + hd * DV_C:O_VC + (hd + 1) * DV_C]
        q_pair = qd_g[:, (hd // 2) * LANES:(hd // 2 + 1) * LANES]
        q_other = pltpu.roll(q_pair, HALF, 1)
        q_lo, q_hi = (q_pair, q_other) if hd % 2 == 0 else (q_other, q_pair)
        acc = jnp.zeros((gs, DV_C), jnp.float32)
        for j0 in range(0, gs, seqs_per_dot):
            tiles, lhs = [], []
            for j in range(j0, j0 + seqs_per_dot):
                srows = pl.ds(j * DK_TOT + hd * DK_C, DK_C)
                s_old = s0_ref[srows, :]
                tiles.append(s_old.astype(jnp.bfloat16))
                crows = slice(hd * DK_C, (hd + 1) * DK_C)
                so_ref[srows, :] = ea_cols[crows, j:j + 1] * s_old + k_cols_c[crows, j:j + 1] * v_h[j:j + 1, :]
            for j in range(j0, j0 + seqs_per_dot, 2):
                lhs.append(jnp.where((row_g == j) & low_g, q_lo,
                                     jnp.where((row_g == j + 1) & jnp.logical_not(low_g), q_hi, 0.0)))
            acc = acc + _bdot(jnp.concatenate(lhs, axis=1), jnp.concatenate(tiles, axis=0))
        oc_ref[grows, hd * DV_C:(hd + 1) * DV_C] += acc

    @pl.when(i_id == n_i - 1)
    def _():
        for g in range(G_A):
            o0 = os_ref[pl.ds(g, nseq, stride=H_A), :]
            o1 = os_ref[pl.ds(G_A + g, nseq, stride=H_A), :]
            gate = proj_ref[:, O_GA + g * LANES:O_GA + (g + 1) * LANES]
            y_ref[:, Y_A + g * LANES:Y_A + (g + 1) * LANES] = jnp.where(low, o0, o1) * _silu(gate)
        for hd in range(H_C):
            on = _rmsnorm(oc_ref[:, hd * DV_C:(hd + 1) * DV_C], gn_ref[:, hd * DV_C:(hd + 1) * DV_C])
            gate = proj_ref[:, O_GC + hd * DV_C:O_GC + (hd + 1) * DV_C]
            y_ref[:, Y_C + hd * DV_C:Y_C + (hd + 1) * DV_C] = on * _silu(gate)
        x_new = xcur_ref[...] + _bdot(y_ref[...], wout_ref[...])
        xcur_ref[...] = x_new

        @pl.when(l_id == n_l - 1)
        def _():
            xo_ref[...] = _rmsnorm(x_new, fg_ref[...])


def _sample_path(x, ng, win, wout, bias_s, sink_c, w00, b0, lng, lnb, wup, bup, gn, fg, buf_kt, buf_vt, s0):
    depth = win.shape[0]
    nseq = x.shape[0]
    gs = min(SEQ_GROUP, nseq)
    assert nseq % gs == 0 and gs % 8 == 0
    grid = (depth, nseq // gs)

    def const_spec(shape):
        nd = len(shape)
        return pl.BlockSpec(shape, lambda l, i: (0,) * nd)

    def layer_spec(shape):
        nd = len(shape)
        return pl.BlockSpec((None,) + shape, lambda l, i: (l,) + (0,) * nd)

    state_spec = pl.BlockSpec((None, gs, LANES, WINDOW), lambda l, i: (l, i, 0, 0))
    gla_spec = pl.BlockSpec((None, gs * DK_TOT, DV_C), lambda l, i: (l, i, 0))
    kern = functools.partial(_sample_kernel, nseq=nseq, gs=gs)
    return pl.pallas_call(
        kern,
        grid=grid,
        in_specs=[
            const_spec((nseq, D_MODEL)),
            layer_spec((1, D_MODEL)),
            layer_spec((D_MODEL, D_IN_PAD)),
            layer_spec((D_MIX, D_MODEL)),
            const_spec((H_A, WINDOW)),
            layer_spec((H_A, 1)),
            layer_spec((1, D_B)),
            layer_spec((1, D_B)),
            layer_spec((1, D_B)),
            layer_spec((1, D_B)),
            layer_spec((LANES, DK_TOT)),
            layer_spec((1, DK_TOT)),
            layer_spec((1, D_C)),
            const_spec((1, D_MODEL)),
            state_spec, state_spec, gla_spec,
        ],
        out_specs=[
            const_spec((nseq, D_MODEL)),
            state_spec, state_spec, gla_spec,
            layer_spec((nseq, D_B)),
        ],
        out_shape=[
            jax.ShapeDtypeStruct((nseq, D_MODEL), jnp.float32),
            jax.ShapeDtypeStruct((depth, nseq, LANES, WINDOW), jnp.float32),
            jax.ShapeDtypeStruct((depth, nseq, LANES, WINDOW), jnp.float32),
            jax.ShapeDtypeStruct((depth, nseq * DK_TOT, DV_C), jnp.float32),
            jax.ShapeDtypeStruct((depth, nseq, D_B), jnp.float32),
        ],
        scratch_shapes=[
            pltpu.VMEM((nseq, D_MODEL), jnp.float32),
            pltpu.VMEM((nseq, D_IN_PAD), jnp.float32),
            pltpu.VMEM((nseq, D_MIX), jnp.float32),
            pltpu.VMEM((nseq * H_A, LANES), jnp.float32),
            pltpu.VMEM((nseq * H_A, LANES), jnp.float32),
            pltpu.VMEM((nseq, DK_TOT), jnp.float32),
            pltpu.VMEM((nseq, DK_TOT), jnp.float32),
            pltpu.VMEM((nseq, D_C), jnp.float32),
        ],
        compiler_params=pltpu.CompilerParams(
            dimension_semantics=("arbitrary", "arbitrary"),
            vmem_limit_bytes=VMEM_LIMIT_BYTES),
        name="sample_path",
    )(x, ng, win, wout, bias_s, sink_c, w00, b0, lng, lnb, wup, bup, gn, fg, buf_kt, buf_vt, s0)


def _window_minor(state):
    depth, n = state.shape[:2]
    return jnp.transpose(state, (0, 1, 3, 4, 2)).reshape(depth, n, LANES, WINDOW)


def _window_major(state_t):
    depth, n = state_t.shape[:2]
    return jnp.transpose(state_t.reshape(depth, n, KV_A, HD_A, WINDOW), (0, 1, 4, 2, 3))


def kernel(x_prompt, x_sample, state_swa_k, state_swa_v, state_gla, rel_bias, norm_g, w_in, sinks, spatial_w,
           spatial_b, chunk_ln_g, chunk_ln_b, gla_w_up, gla_b_up, gla_norm_g, w_out, final_norm_g):
    depth = w_in.shape[0]
    nseq = x_sample.shape[0]
    bsz = x_prompt.shape[0]
    f32 = jnp.float32

    win, wout = _prepare_weights(w_in, w_out)
    wup = jnp.pad(gla_w_up, ((0, 0), (0, LANES - GLA_RANK), (0, 0))).astype(jnp.bfloat16)
    bsf = jnp.repeat(jnp.swapaxes(spatial_b, 1, 2), DH_B, axis=2).astype(f32)
    w00 = jnp.repeat(spatial_w[:, :, 0, 0], DH_B, axis=1).astype(f32)[:, None, :]
    b0 = jnp.repeat(spatial_b[:, :, 0], DH_B, axis=1).astype(f32)[:, None, :]
    sink_c = sinks.astype(f32)[:, :, None]
    row = lambda a: a.astype(f32)[:, None, :]
    ng, lng, lnb, bup, gn = row(norm_g), row(chunk_ln_g), row(chunk_ln_b), row(gla_b_up), row(gla_norm_g)
    fg = final_norm_g.astype(f32)[None, :]
    sw = spatial_w.astype(f32)
    sinks = sinks.astype(f32)

    bias_tab = _bias_table(rel_bias)
    bias_s = bias_tab[0].reshape(H_A, WINDOW, 2 * WINDOW)[:, WINDOW - 1, WINDOW:]

    xp = x_prompt
    kp_l, vp_l, sp_l = [], [], []
    for l in range(depth):
        xp, kp, vp, sp = _prompt_layer(xp, ng, win, wout, bias_tab, sinks, sw, bsf, lng, lnb, wup, bup, gn, fg,
                                       layer=l, final=l == depth - 1)
        kp_l.append(kp); vp_l.append(vp); sp_l.append(sp)

    xs, ks_t, vs_t, ss, cv = _sample_path(
        x_sample.reshape(nseq, D_MODEL), ng, win, wout, bias_s, sink_c, w00, b0, lng, lnb, wup, bup, gn, fg,
        _window_minor(state_swa_k), _window_minor(state_swa_v), state_gla.reshape(depth, nseq * DK_TOT, DV_C))

    return (xp,
            xs.reshape(nseq, 1, D_MODEL),
            _window_major(jnp.stack(kp_l)),
            _window_major(jnp.stack(vp_l)),
            jnp.stack(sp_l).reshape(depth, bsz, H_C, DK_C, DV_C),
            _window_major(ks_t),
            _window_major(vs_t),
            ss.reshape(depth, nseq, H_C, DK_C, DV_C),
            cv.reshape(depth, nseq, 1, D_B))
```

```python
import functools

import numpy as np
import jax
import jax.numpy as jnp
from jax import lax
from jax.experimental import pallas as pl
from jax.experimental.pallas import tpu as pltpu

D_MODEL = 1024
D_A, HD_A, H_A, KV_A, G_A = 512, 64, 8, 2, 4
WINDOW, N_BUCKETS, MAX_DIST = 128, 32, 128
D_B, H_B, DH_B, CHUNK_B = 512, 8, 64, 128
D_C, H_C, DK_TOT, DK_C, DV_C = 512, 4, 256, 64, 128
GLA_RANK, GLA_TAU, GLA_CHUNK = 16, 16.0, 64
D_MIX = D_A + D_B + D_C
EPS = 1e-6
NEG = -1e30
SPLITS = [D_A, KV_A * HD_A, KV_A * HD_A, D_A, D_B, D_B, D_B, DK_TOT, DK_TOT, D_C, D_C, GLA_RANK]
D_IN = sum(SPLITS)

LANES = 128
HALF = LANES // 2
D_IN_PAD = ((D_IN + LANES - 1) // LANES) * LANES
VMEM_LIMIT_BYTES = 56 * 1024 * 1024

O_QA, O_KA, O_VA, O_GA = 0, 512, 640, 768
O_UB, O_VB, O_GB = 1280, 1792, 2304
O_QC, O_KC, O_VC, O_GC, O_LR = 2816, 3072, 3328, 3840, 4352
Y_A, Y_B, Y_C = 0, D_A, D_A + D_B

HEAD_PERM = [0, 4, 1, 5, 2, 6, 3, 7]

TOKEN_BLOCK = 512
SEQ_GROUP = 16
IN_PROJ_CHUNK = 256
IN_PROJ_ROWS = 256
OUT_PROJ_CHUNK = 256
MIXER_SECTIONS = 22
PREP_SPLIT = 2

_NT = (((1,), (1,)), ((), ()))


def _t5_bucket(dist):
    n = np.maximum(dist, 0)
    max_exact = N_BUCKETS // 2
    large = max_exact + (np.log(np.maximum(n, 1) / max_exact) / np.log(MAX_DIST / max_exact)
                         * (N_BUCKETS - max_exact)).astype(np.int32)
    large = np.minimum(large, N_BUCKETS - 1)
    return np.where(n < max_exact, n, large).astype(np.int32)


def _silu(x):
    return x * (1.0 / (1.0 + jnp.exp(-x)))


def _log_sigmoid(x):
    return jnp.minimum(x, 0.0) - jnp.log1p(jnp.exp(-jnp.abs(x)))


def _bdot(a, b):
    return jnp.dot(a.astype(jnp.bfloat16), b.astype(jnp.bfloat16), preferred_element_type=jnp.float32)


def _bdot_nt(a, b):
    return lax.dot_general(a.astype(jnp.bfloat16), b.astype(jnp.bfloat16), _NT,
                           preferred_element_type=jnp.float32)


def _rmsnorm(x, g):
    return x * lax.rsqrt(jnp.mean(x * x, axis=-1, keepdims=True) + EPS) * g


def _layernorm(v, g, b):
    mu = jnp.mean(v, axis=-1, keepdims=True)
    xc = v - mu
    var = jnp.mean(xc * xc, axis=-1, keepdims=True)
    return xc * lax.rsqrt(var + EPS) * g + b


def _softmax_sink(s, sink):
    m = jnp.maximum(jnp.max(s, axis=-1, keepdims=True), sink)
    p = jnp.exp(s - m)
    den = jnp.sum(p, axis=-1, keepdims=True) + jnp.exp(sink - m)
    return p, 1.0 / den


def _bias_table_kernel(rb_ref, bucket_ref, band_ref, out_ref):
    bucket = bucket_ref[...]
    band = band_ref[...] > 0
    own = lax.broadcasted_iota(jnp.int32, bucket.shape, 1) >= WINDOW
    for h in range(H_A):
        acc = jnp.zeros(bucket.shape, jnp.float32)
        for b in range(N_BUCKETS):
            acc = jnp.where(bucket == b, rb_ref[b, h], acc)
        kv, g = divmod(h, G_A)
        rows = pl.ds(g * WINDOW, WINDOW)
        out_ref[0, kv, rows, :] = jnp.where(band, acc, NEG)
        out_ref[1, kv, rows, :] = jnp.where(band & own, acc, NEG)


def _bias_table(rel_bias):
    i = np.arange(WINDOW)[:, None]
    j = np.arange(2 * WINDOW)[None, :]
    dist = i + WINDOW - j
    band = ((dist >= 0) & (dist < WINDOW)).astype(np.int32)
    return pl.pallas_call(
        _bias_table_kernel,
        out_shape=jax.ShapeDtypeStruct((2, KV_A, G_A * WINDOW, 2 * WINDOW), jnp.float32),
        in_specs=[pl.BlockSpec(memory_space=pltpu.SMEM),
                  pl.BlockSpec(memory_space=pltpu.VMEM),
                  pl.BlockSpec(memory_space=pltpu.VMEM)],
        out_specs=pl.BlockSpec(memory_space=pltpu.VMEM),
        name="rel_bias_table",
    )(rel_bias.astype(jnp.float32), jnp.asarray(_t5_bucket(dist)), jnp.asarray(band))


def _win_prep_kernel(wt_ref, lr_ref, o_ref):
    c = pl.program_id(1)
    tiles = wt_ref.shape[0] // LANES
    for cc in range(PREP_SPLIT):
        @pl.when(c == cc)
        def _(cc=cc):
            for t in range(tiles):
                col = (cc * tiles + t) * LANES
                base = next((b for b in (O_QA, O_GA) if b <= col < b + D_A), None)
                if base is None:
                    src = wt_ref[t * LANES:(t + 1) * LANES, :]
                else:
                    g = (col - base) // LANES
                    r0 = base - cc * tiles * LANES
                    assert r0 >= 0 and r0 + D_A <= wt_ref.shape[0]
                    src = jnp.concatenate([wt_ref[r0 + g * HD_A:r0 + (g + 1) * HD_A, :],
                                           wt_ref[r0 + (G_A + g) * HD_A:r0 + (G_A + g + 1) * HD_A, :]], axis=0)
                o_ref[:, col:col + LANES] = src.T.astype(jnp.bfloat16)
            if cc == PREP_SPLIT - 1:
                o_ref[:, O_LR:D_IN_PAD] = lr_ref[...].T.astype(jnp.bfloat16)


def _wout_prep_kernel(w_ref, o_ref):
    for j, h in enumerate(HEAD_PERM):
        o_ref[j * HD_A:(j + 1) * HD_A, :] = w_ref[h * HD_A:(h + 1) * HD_A, :].astype(jnp.bfloat16)
    o_ref[Y_B:, :] = w_ref[Y_B:, :].astype(jnp.bfloat16)


def _prepare_weights(w_in, w_out):
    depth = w_in.shape[0]
    cols = D_MODEL // PREP_SPLIT
    wt = jnp.swapaxes(w_in, 1, 2)
    assert O_LR % (PREP_SPLIT * LANES) == 0
    rows = O_LR // PREP_SPLIT
    wt_lr = jnp.pad(wt[:, O_LR:, :], ((0, 0), (0, D_IN_PAD - D_IN), (0, 0)))
    win = pl.pallas_call(
        _win_prep_kernel,
        grid=(depth, PREP_SPLIT),
        in_specs=[pl.BlockSpec((None, rows, D_MODEL), lambda l, r: (l, r, 0)),
                  pl.BlockSpec((None, LANES, D_MODEL), lambda l, r: (l, 0, 0))],
        out_specs=pl.BlockSpec((None, D_MODEL, D_IN_PAD), lambda l, r: (l, 0, 0)),
        out_shape=jax.ShapeDtypeStruct((depth, D_MODEL, D_IN_PAD), jnp.bfloat16),
        compiler_params=pltpu.CompilerParams(dimension_semantics=("arbitrary", "arbitrary"),
                                             vmem_limit_bytes=VMEM_LIMIT_BYTES),
        name="w_in_prep",
    )(wt, wt_lr)
    wout = pl.pallas_call(
        _wout_prep_kernel,
        grid=(depth, PREP_SPLIT),
        in_specs=[pl.BlockSpec((None, D_MIX, cols), lambda l, c: (l, 0, c))],
        out_specs=pl.BlockSpec((None, D_MIX, cols), lambda l, c: (l, 0, c)),
        out_shape=jax.ShapeDtypeStruct((depth, D_MIX, D_MODEL), jnp.bfloat16),
        compiler_params=pltpu.CompilerParams(dimension_semantics=("arbitrary", "arbitrary"),
                                             vmem_limit_bytes=VMEM_LIMIT_BYTES),
        name="w_out_prep",
    )(w_out)
    return win, wout


def _prompt_layer_kernel(x_ref, xn_ref, ng_ref, win_ref, wout_ref, bias_ref, sink_ref, sw_ref, bsf_ref, lng_ref,
                         lnb_ref, wup_ref, bup_ref, gn_ref, fg_ref,
                         xo_ref, ko_ref, vo_ref, so_ref,
                         pa_ref, pb_ref, hb_ref, hn_ref, ya_ref, yb_ref, kprev_ref, vprev_ref, st_ref, wm_ref,
                         *, tb, layer, final):
    b_id = pl.program_id(0)
    i_id = pl.program_id(1)
    n_i = pl.num_programs(1)
    half = tb // 2
    n_sub = half // WINDOW

    @pl.when((b_id == 0) & (i_id == 0))
    def _():
        r = lax.broadcasted_iota(jnp.int32, (CHUNK_B, CHUNK_B), 0)
        c = lax.broadcasted_iota(jnp.int32, (CHUNK_B, CHUNK_B), 1)
        for h in range(H_B):
            wm_ref[h // 2, :, (h % 2) * CHUNK_B:(h % 2 + 1) * CHUNK_B] = jnp.where(
                c <= r, sw_ref[h], 0.0).astype(jnp.bfloat16)
        pa_ref[...] = _bdot(_rmsnorm(x_ref[0:half, :], ng_ref[...]), win_ref[...])
        hb_ref[...] = _rmsnorm(x_ref[half:tb, :], ng_ref[...]).astype(jnp.bfloat16)

    @pl.when(i_id == 0)
    def _():
        kprev_ref[...] = jnp.zeros_like(kprev_ref)
        vprev_ref[...] = jnp.zeros_like(vprev_ref)
        st_ref[...] = jnp.zeros_like(st_ref)

    hn_ref[...] = _rmsnorm(xn_ref[0:half, :], ng_ref[...]).astype(jnp.bfloat16)

    def in_proj_chunks(h_ref, p_ref):
        def chunk(r0, c0, c1):
            def run():
                p_ref[r0:r0 + IN_PROJ_ROWS, c0:c1] = jnp.dot(h_ref[r0:r0 + IN_PROJ_ROWS, :], win_ref[:, c0:c1],
                                                             preferred_element_type=jnp.float32)
            return run
        edges = list(range(0, D_IN_PAD, IN_PROJ_CHUNK)) + [D_IN_PAD]
        return [chunk(r0, c0, c1) for c0, c1 in zip(edges[:-1], edges[1:])
                for r0 in range(0, half, IN_PROJ_ROWS)]

    def out_proj_chunks(y_ref, r0, s):
        def chunk(c0, c1):
            def run():
                rows = slice(r0 + s * WINDOW, r0 + (s + 1) * WINDOW)
                xo_ref[rows, c0:c1] = x_ref[rows, c0:c1] + jnp.dot(
                    y_ref[s * WINDOW:(s + 1) * WINDOW, :], wout_ref[:, c0:c1],
                    preferred_element_type=jnp.float32)
            return run
        return [chunk(c0, c0 + OUT_PROJ_CHUNK) for c0 in range(0, D_MODEL, OUT_PROJ_CHUNK)]

    def final_norm(r0):
        if final:
            xo_ref[r0:r0 + half, :] = _rmsnorm(xo_ref[r0:r0 + half, :], fg_ref[...])

    lane = lax.broadcasted_iota(jnp.int32, (WINDOW, LANES), 1)
    low = lane < HALF
    low64 = lax.broadcasted_iota(jnp.int32, (GLA_CHUNK, LANES), 1) < HALF
    ri = lax.broadcasted_iota(jnp.int32, (LANES, LANES), 0)
    ci = lax.broadcasted_iota(jnp.int32, (LANES, LANES), 1)
    bd_tril = ((ri >= GLA_CHUNK) == (ci >= GLA_CHUNK)) & (ci <= ri)
    cum_mat = jnp.where(bd_tril, 1.0, 0.0).astype(jnp.bfloat16)
    ri2 = lax.broadcasted_iota(jnp.int32, (LANES, 2 * LANES), 0)
    ci2 = lax.broadcasted_iota(jnp.int32, (LANES, 2 * LANES), 1) & (LANES - 1)
    bd_tril2 = ((ri2 >= GLA_CHUNK) == (ci2 >= GLA_CHUNK)) & (ci2 <= ri2)

    def mixers(p_ref, y_ref, s, first):
        r0 = s * WINDOW
        rows = pl.ds(r0, WINDOW)

        ks = p_ref[rows, O_KA:O_KA + LANES] * (HD_A ** -0.5)
        k_new = [jnp.where(low, ks, 0.0).astype(jnp.bfloat16), jnp.where(low, 0.0, ks).astype(jnp.bfloat16)]
        v_new = p_ref[rows, O_VA:O_VA + LANES].T.astype(jnp.bfloat16)
        vcat = jnp.concatenate([carry["v"], v_new], axis=1)
        q4 = jnp.concatenate([p_ref[rows, O_QA + g * LANES:O_QA + (g + 1) * LANES].astype(jnp.bfloat16)
                              for g in range(G_A)], axis=0)
        o_heads = []
        for kv in range(KV_A):
            kcat = jnp.concatenate([carry["k"][kv], k_new[kv]], axis=0)
            sc = lax.dot_general(q4, kcat, _NT, preferred_element_type=jnp.float32) + bias_ref[first, kv]
            carry["k"][kv] = k_new[kv]
            yield
            ps, invs = [], []
            for g in range(G_A):
                p, inv = _softmax_sink(sc[g * WINDOW:(g + 1) * WINDOW], sink_ref[layer, kv * G_A + g])
                ps.append(p.astype(jnp.bfloat16))
                invs.append(inv)
                if g % 2 == 1:
                    yield
            o_t = lax.dot_general(vcat[kv * HD_A:(kv + 1) * HD_A, :], jnp.concatenate(ps, axis=0), _NT,
                                  preferred_element_type=jnp.float32)
            o_heads.append((o_t, invs))
            yield
        carry["v"] = v_new
        for g in range(G_A):
            o_pair = jnp.concatenate([o_heads[kv][0][:, g * WINDOW:(g + 1) * WINDOW] for kv in range(KV_A)],
                                     axis=0).T
            ya = o_pair * jnp.where(low, o_heads[0][1][g], o_heads[1][1][g])
            gate = p_ref[rows, O_GA + g * LANES:O_GA + (g + 1) * LANES]
            y_ref[rows, Y_A + g * LANES:Y_A + (g + 1) * LANES] = (ya * _silu(gate)).astype(jnp.bfloat16)
        yield

        vn = _layernorm(p_ref[rows, O_VB:O_VB + D_B], lng_ref[...], lnb_ref[...]).astype(jnp.bfloat16)
        yield
        for j in range(H_B // 2):
            vp = vn[:, j * LANES:(j + 1) * LANES]
            zero = jnp.zeros_like(vp)
            v_st = jnp.concatenate([jnp.where(low, vp, zero), jnp.where(low, zero, vp)], axis=0)
            mix = (jnp.dot(wm_ref[j], v_st, preferred_element_type=jnp.float32)
                   + bsf_ref[:, j * LANES:(j + 1) * LANES])
            u = p_ref[rows, O_UB + j * LANES:O_UB + (j + 1) * LANES]
            gate = p_ref[rows, O_GB + j * LANES:O_GB + (j + 1) * LANES]
            y_ref[rows, Y_B + j * LANES:Y_B + (j + 1) * LANES] = (u * mix * _silu(gate)).astype(jnp.bfloat16)
            if j % 2 == 1:
                yield

        z = _bdot(p_ref[rows, O_LR:O_LR + LANES], wup_ref[...]) + bup_ref[...]
        la = _log_sigmoid(z) * (1.0 / GLA_TAU)
        yield
        hi = la.astype(jnp.bfloat16)
        r1 = la - hi.astype(jnp.float32)
        mid = r1.astype(jnp.bfloat16)
        lo = (r1 - mid.astype(jnp.float32)).astype(jnp.bfloat16)
        c3 = jnp.dot(cum_mat, jnp.concatenate([hi, mid, lo], axis=1), preferred_element_type=jnp.float32)
        bc = (c3[:, 0:DK_TOT] + c3[:, DK_TOT:2 * DK_TOT]) + c3[:, 2 * DK_TOT:3 * DK_TOT]
        yield
        for c in range(WINDOW // GLA_CHUNK):
            rc = pl.ds(r0 + c * GLA_CHUNK, GLA_CHUNK)
            q_sts, k_sts, v_sts, kd_sts, decays, sts = [], [], [], [], [], []
            for p in range(H_C // 2):
                bp = bc[c * GLA_CHUNK:(c + 1) * GLA_CHUNK, p * LANES:(p + 1) * LANES]
                qp = p_ref[rc, O_QC + p * LANES:O_QC + (p + 1) * LANES]
                kp = p_ref[rc, O_KC + p * LANES:O_KC + (p + 1) * LANES]
                b_last = bp[GLA_CHUNK - 1:GLA_CHUNK, :]
                qd = qp * (DK_C ** -0.5) * jnp.exp(bp)
                ki = (kp * jnp.exp(-bp)).astype(jnp.bfloat16)
                kd = kp * jnp.exp(b_last - bp)
                q_sts.append(jnp.concatenate([jnp.where(low64, qd, 0.0), jnp.where(low64, 0.0, qd)],
                                             axis=0).astype(jnp.bfloat16))
                k_sts.append(jnp.concatenate([ki, ki], axis=0))
                v0 = p_ref[rc, O_VC + (2 * p) * DV_C:O_VC + (2 * p + 1) * DV_C]
                v1 = p_ref[rc, O_VC + (2 * p + 1) * DV_C:O_VC + (2 * p + 2) * DV_C]
                v_sts.append(jnp.concatenate([v0, v1], axis=0))
                kd_sts.append(jnp.concatenate([jnp.where(low64, kd, 0.0), jnp.where(low64, 0.0, kd)],
                                              axis=0).astype(jnp.bfloat16))
                decays.append(jnp.exp(b_last))
                yield
            zero = jnp.zeros((LANES, LANES), jnp.bfloat16)
            blockdiag = lambda a, b: jnp.concatenate([jnp.concatenate([a, zero], axis=1),
                                                      jnp.concatenate([zero, b], axis=1)], axis=0)
            q_all = jnp.concatenate(q_sts, axis=1)
            att = lax.dot_general(q_all, blockdiag(*k_sts), _NT, preferred_element_type=jnp.float32)
            att = jnp.where(bd_tril2, att, 0.0).astype(jnp.bfloat16)
            v_b = [v.astype(jnp.bfloat16) for v in v_sts]
            o_all = jnp.dot(att, blockdiag(*v_b), preferred_element_type=jnp.float32)
            d_all = jnp.dot(jnp.concatenate([v.T.astype(jnp.bfloat16) for v in v_sts], axis=1),
                            blockdiag(*kd_sts), preferred_element_type=jnp.float32)
            sts = carry["st"]
            o_inter = lax.dot_general(q_all, blockdiag(*[st.astype(jnp.bfloat16) for st in sts]), _NT,
                                      preferred_element_type=jnp.float32)
            o_all = o_all + o_inter
            carry["st"] = [sts[p] * decays[p] + d_all[:, p * LANES:(p + 1) * LANES] for p in range(H_C // 2)]
            for p in range(H_C // 2):
                o = o_all[:, p * DV_C:(p + 1) * DV_C]
                for hh in range(2):
                    hd = 2 * p + hh
                    oh = o[hh * GLA_CHUNK:(hh + 1) * GLA_CHUNK]
                    on = _rmsnorm(oh, gn_ref[:, hd * DV_C:(hd + 1) * DV_C])
                    gate = p_ref[rc, O_GC + hd * DV_C:O_GC + (hd + 1) * DV_C]
                    y_ref[rc, Y_C + hd * DV_C:Y_C + (hd + 1) * DV_C] = (on * _silu(gate)).astype(jnp.bfloat16)
                yield

    def half_mixers(p_ref, y_ref, first):
        for s in range(n_sub):
            yield from mixers(p_ref, y_ref, s, first if s == 0 else 0)

    def run_phase(sections, chunks, late_chunks):
        n_sections = n_sub * MIXER_SECTIONS
        plan = [[] for _ in range(n_sections)]
        for idx, run in enumerate(chunks):
            plan[idx * n_sections // len(chunks)].append(run)
        for s, runs in late_chunks.items():
            k0 = (s + 1) * MIXER_SECTIONS
            for idx, run in enumerate(runs):
                plan[k0 + idx * (n_sections - k0) // len(runs)].append(run)
        for k in range(n_sections):
            next(sections)
            for run in plan[k]:
                run()
        assert next(sections, None) is None

    carry = {"k": [kprev_ref[kv] for kv in range(KV_A)], "v": vprev_ref[...],
             "st": [st_ref[p] for p in range(H_C // 2)]}
    run_phase(half_mixers(pa_ref, ya_ref, jnp.where(i_id == 0, 1, 0)),
              in_proj_chunks(hb_ref, pb_ref),
              {s: out_proj_chunks(ya_ref, 0, s) for s in range(n_sub - 1)})
    hb_ref[...] = _rmsnorm(xn_ref[half:tb, :], ng_ref[...]).astype(jnp.bfloat16)
    run_phase(half_mixers(pb_ref, yb_ref, 0),
              out_proj_chunks(ya_ref, 0, n_sub - 1) + in_proj_chunks(hn_ref, pa_ref),
              {s: out_proj_chunks(yb_ref, half, s) for s in range(n_sub - 1)})
    final_norm(0)
    for run in out_proj_chunks(yb_ref, half, n_sub - 1):
        run()
    final_norm(half)
    for kv in range(KV_A):
        kprev_ref[kv] = carry["k"][kv]
    vprev_ref[...] = carry["v"]
    for p in range(H_C // 2):
        st_ref[p] = carry["st"][p]

    @pl.when(i_id == n_i - 1)
    def _():
        ko_ref[...] = pb_ref[half - WINDOW:half, O_KA:O_KA + LANES].T
        vo_ref[...] = pb_ref[half - WINDOW:half, O_VA:O_VA + LANES].T
        for p in range(H_C // 2):
            so_ref[p] = st_ref[p].T


def _prompt_layer(x, ng, win, wout, bias_tab, sinks, sw, bsf, lng, lnb, wup, bup, gn, fg, *, layer, final):
    bsz, seq, _ = x.shape
    tb = min(TOKEN_BLOCK, seq)
    half = tb // 2
    assert seq % tb == 0 and half % WINDOW == 0
    n_i = seq // tb
    grid = (bsz, n_i)

    def next_block(b, i):
        lin = jnp.minimum(b * n_i + i + 1, bsz * n_i - 1)
        return lin // n_i, lin % n_i, 0

    def const_spec(shape):
        nd = len(shape)
        return pl.BlockSpec(shape, lambda b, i: (0,) * nd, pipeline_mode=pl.Buffered(1))

    def layer_spec(shape):
        nd = len(shape)
        return pl.BlockSpec((None,) + shape, lambda b, i: (layer,) + (0,) * nd, pipeline_mode=pl.Buffered(1))

    kern = functools.partial(_prompt_layer_kernel, tb=tb, layer=layer, final=final)
    return pl.pallas_call(
        kern,
        grid=grid,
        in_specs=[
            pl.BlockSpec((None, tb, D_MODEL), lambda b, i: (b, i, 0)),
            pl.BlockSpec((None, tb, D_MODEL), next_block),
            layer_spec((1, D_MODEL)),
            layer_spec((D_MODEL, D_IN_PAD)),
            layer_spec((D_MIX, D_MODEL)),
            const_spec((2, KV_A, G_A * WINDOW, 2 * WINDOW)),
            pl.BlockSpec(memory_space=pltpu.SMEM),
            layer_spec((H_B, CHUNK_B, CHUNK_B)),
            layer_spec((CHUNK_B, D_B)),
            layer_spec((1, D_B)),
            layer_spec((1, D_B)),
            layer_spec((LANES, DK_TOT)),
            layer_spec((1, DK_TOT)),
            layer_spec((1, D_C)),
            const_spec((1, D_MODEL)),
        ],
        out_specs=[
            pl.BlockSpec((None, tb, D_MODEL), lambda b, i: (b, i, 0)),
            pl.BlockSpec((None, LANES, WINDOW), lambda b, i: (b, 0, 0)),
            pl.BlockSpec((None, LANES, WINDOW), lambda b, i: (b, 0, 0)),
            pl.BlockSpec((None, H_C // 2, LANES, DV_C), lambda b, i: (b, 0, 0, 0)),
        ],
        out_shape=[
            jax.ShapeDtypeStruct((bsz, seq, D_MODEL), jnp.float32),
            jax.ShapeDtypeStruct((bsz, LANES, WINDOW), jnp.float32),
            jax.ShapeDtypeStruct((bsz, LANES, WINDOW), jnp.float32),
            jax.ShapeDtypeStruct((bsz, H_C // 2, LANES, DV_C), jnp.float32),
        ],
        scratch_shapes=[
            pltpu.VMEM((half, D_IN_PAD), jnp.float32),
            pltpu.VMEM((half, D_IN_PAD), jnp.float32),
            pltpu.VMEM((half, D_MODEL), jnp.bfloat16),
            pltpu.VMEM((half, D_MODEL), jnp.bfloat16),
            pltpu.VMEM((half, D_MIX), jnp.bfloat16),
            pltpu.VMEM((half, D_MIX), jnp.bfloat16),
            pltpu.VMEM((KV_A, WINDOW, LANES), jnp.bfloat16),
            pltpu.VMEM((WINDOW, LANES), jnp.bfloat16),
            pltpu.VMEM((H_C // 2, DV_C, LANES), jnp.float32),
            pltpu.VMEM((H_B // 2, CHUNK_B, 2 * CHUNK_B), jnp.bfloat16),
        ],
        compiler_params=pltpu.CompilerParams(
            dimension_semantics=("arbitrary", "arbitrary"),
            vmem_limit_bytes=VMEM_LIMIT_BYTES),
        name="prompt_layer",
    )(x, x, ng, win, wout, bias_tab, sinks, sw, bsf, lng, lnb, wup, bup, gn, fg)


def _sample_kernel(x_ref, ng_ref, win_ref, wout_ref, biass_ref, sinkc_ref, w00_ref, b0_ref, lng_ref,
                   lnb_ref, wup_ref, bup_ref, gn_ref, fg_ref, bk_ref, bv_ref, s0_ref,
                   xo_ref, nk_ref, nv_ref, so_ref, cv_ref,
                   xcur_ref, proj_ref, y_ref, qs_ref, os_ref, qd_ref, ea_ref, oc_ref, *, nseq, gs):
    l_id = pl.program_id(0)
    i_id = pl.program_id(1)
    n_l = pl.num_programs(0)
    n_i = pl.num_programs(1)
    lane = lax.broadcasted_iota(jnp.int32, (nseq, LANES), 1)
    low = lane < HALF

    @pl.when((l_id == 0) & (i_id == 0))
    def _():
        xcur_ref[...] = x_ref[...]

    @pl.when(i_id == 0)
    def _():
        h = _rmsnorm(xcur_ref[...], ng_ref[...])
        proj_ref[...] = _bdot(h, win_ref[...])
        for g in range(G_A):
            qp = proj_ref[:, O_QA + g * LANES:O_QA + (g + 1) * LANES] * (HD_A ** -0.5)
            for kv in range(KV_A):
                keep = low if kv == 0 else jnp.logical_not(low)
                qs_ref[pl.ds(kv * G_A + g, nseq, stride=H_A), :] = jnp.where(keep, qp, 0.0)
        vn = _layernorm(proj_ref[:, O_VB:O_VB + D_B], lng_ref[...], lnb_ref[...])
        cv_ref[...] = vn
        mix = w00_ref[...] * vn + b0_ref[...]
        y_ref[:, Y_B:Y_B + D_B] = (proj_ref[:, O_UB:O_UB + D_B] * mix) * _silu(proj_ref[:, O_GB:O_GB + D_B])
        z = _bdot(proj_ref[:, O_LR:O_LR + LANES], wup_ref[...]) + bup_ref[...]
        la = _log_sigmoid(z) * (1.0 / GLA_TAU)
        ea = jnp.exp(la)
        qd = proj_ref[:, O_QC:O_QC + DK_TOT] * (DK_C ** -0.5) * ea
        ki = proj_ref[:, O_KC:O_KC + DK_TOT] * jnp.exp(-la)
        qd_ref[...] = qd
        ea_ref[...] = ea
        prod = qd * ki
        lane_c = lax.broadcasted_iota(jnp.int32, prod.shape, 1)
        for hd in range(H_C):
            in_head = (lane_c >= hd * DK_C) & (lane_c < (hd + 1) * DK_C)
            att = jnp.sum(jnp.where(in_head, prod, 0.0), axis=-1, keepdims=True)
            oc_ref[:, hd * DV_C:(hd + 1) * DV_C] = att * proj_ref[:, O_VC + hd * DV_C:O_VC + (hd + 1) * DV_C]

    g0 = pl.multiple_of(i_id * gs, gs)
    grows = pl.ds(g0, gs)

    pad_rows = jnp.zeros((LANES - gs, LANES), jnp.float32)
    k_cols = jnp.concatenate([proj_ref[grows, O_KA:O_KA + LANES], pad_rows], axis=0).T
    v_cols = jnp.concatenate([proj_ref[grows, O_VA:O_VA + LANES], pad_rows], axis=0).T
    newest = lax.broadcasted_iota(jnp.int32, (LANES, WINDOW), 1) == WINDOW - 1
    for j in range(gs):
        kt = jnp.where(newest, k_cols[:, j:j + 1], pltpu.roll(bk_ref[j], WINDOW - 1, 1))
        vt = jnp.where(newest, v_cols[:, j:j + 1], pltpu.roll(bv_ref[j], WINDOW - 1, 1))
        nk_ref[j] = kt
        nv_ref[j] = vt
        r8 = pl.ds(pl.multiple_of((g0 + j) * H_A, H_A), H_A)
        sc = _bdot(qs_ref[r8, :], kt) + biass_ref[...]
        p, inv = _softmax_sink(sc, sinkc_ref[...])
        os_ref[r8, :] = _bdot_nt(p, vt) * inv

    pad_c = jnp.zeros((LANES - gs, DK_TOT), jnp.float32)
    ea_cols = jnp.concatenate([ea_ref[grows, :], pad_c], axis=0).T
    k_cols_c = jnp.concatenate([proj_ref[grows, O_KC:O_KC + DK_TOT], pad_c], axis=0).T
    qd_g = qd_ref[grows, :]
    row_g = lax.broadcasted_iota(jnp.int32, (gs, LANES), 0)
    low_g = lax.broadcasted_iota(jnp.int32, (gs, LANES), 1) < HALF
    seqs_per_dot = 2 * LANES // DK_C
    for hd in range(H_C):
        v_h = proj_ref[grows, O_VC + hd * DV_C:O_VC + (hd + 1) * DV_C]
        q_pair = qd_g[:, (hd // 2) * LANES:(hd // 2 + 1) * LANES]
        q_other = pltpu.roll(q_pair, HALF, 1)
        q_lo, q_hi = (q_pair, q_other) if hd % 2 == 0 else (q_other, q_pair)
        acc = jnp.zeros((gs, DV_C), jnp.float32)
        for j0 in range(0, gs, seqs_per_dot):
            tiles, lhs = [], []
            for j in range(j0, j0 + seqs_per_dot):
                srows = pl.ds(j * DK_TOT + hd * DK_C, DK_C)
                s_old = s0_ref[srows, :]
                tiles.append(s_old.astype(jnp.bfloat16))
                crows = slice(hd * DK_C, (hd + 1) * DK_C)
                so_ref[srows, :] = ea_cols[crows, j:j + 1] * s_old + k_cols_c[crows, j:j + 1] * v_h[j:j + 1, :]
            for j in range(j0, j0 + seqs_per_dot, 2):
                lhs.append(jnp.where((row_g == j) & low_g, q_lo,
                                     jnp.where((row_g == j + 1) & jnp.logical_not(low_g), q_hi, 0.0)))
            acc = acc + _bdot(jnp.concatenate(lhs, axis=1), jnp.concatenate(tiles, axis=0))
        oc_ref[grows, hd * DV_C:(hd + 1) * DV_C] += acc

    @pl.when(i_id == n_i - 1)
    def _():
        for g in range(G_A):
            o0 = os_ref[pl.ds(g, nseq, stride=H_A), :]
            o1 = os_ref[pl.ds(G_A + g, nseq, stride=H_A), :]
            gate = proj_ref[:, O_GA + g * LANES:O_GA + (g + 1) * LANES]
            y_ref[:, Y_A + g * LANES:Y_A + (g + 1) * LANES] = jnp.where(low, o0, o1) * _silu(gate)
        for hd in range(H_C):
            on = _rmsnorm(oc_ref[:, hd * DV_C:(hd + 1) * DV_C], gn_ref[:, hd * DV_C:(hd + 1) * DV_C])
            gate = proj_ref[:, O_GC + hd * DV_C:O_GC + (hd + 1) * DV_C]
            y_ref[:, Y_C + hd * DV_C:Y_C + (hd + 1) * DV_C] = on * _silu(gate)
        x_new = xcur_ref[...] + _bdot(y_ref[...], wout_ref[...])
        xcur_ref[...] = x_new

        @pl.when(l_id == n_l - 1)
        def _():
            xo_ref[...] = _rmsnorm(x_new, fg_ref[...])


def _sample_path(x, ng, win, wout, bias_s, sink_c, w00, b0, lng, lnb, wup, bup, gn, fg, buf_kt, buf_vt, s0):
    depth = win.shape[0]
    nseq = x.shape[0]
    gs = min(SEQ_GROUP, nseq)
    assert nseq % gs == 0 and gs % 8 == 0
    grid = (depth, nseq // gs)

    def const_spec(shape):
        nd = len(shape)
        return pl.BlockSpec(shape, lambda l, i: (0,) * nd)

    def layer_spec(shape):
        nd = len(shape)
        return pl.BlockSpec((None,) + shape, lambda l, i: (l,) + (0,) * nd)

    state_spec = pl.BlockSpec((None, gs, LANES, WINDOW), lambda l, i: (l, i, 0, 0))
    gla_spec = pl.BlockSpec((None, gs * DK_TOT, DV_C), lambda l, i: (l, i, 0))
    kern = functools.partial(_sample_kernel, nseq=nseq, gs=gs)
    return pl.pallas_call(
        kern,
        grid=grid,
        in_specs=[
            const_spec((nseq, D_MODEL)),
            layer_spec((1, D_MODEL)),
            layer_spec((D_MODEL, D_IN_PAD)),
            layer_spec((D_MIX, D_MODEL)),
            const_spec((H_A, WINDOW)),
            layer_spec((H_A, 1)),
            layer_spec((1, D_B)),
            layer_spec((1, D_B)),
            layer_spec((1, D_B)),
            layer_spec((1, D_B)),
            layer_spec((LANES, DK_TOT)),
            layer_spec((1, DK_TOT)),
            layer_spec((1, D_C)),
            const_spec((1, D_MODEL)),
            state_spec, state_spec, gla_spec,
        ],
        out_specs=[
            const_spec((nseq, D_MODEL)),
            state_spec, state_spec, gla_spec,
            layer_spec((nseq, D_B)),
        ],
        out_shape=[
            jax.ShapeDtypeStruct((nseq, D_MODEL), jnp.float32),
            jax.ShapeDtypeStruct((depth, nseq, LANES, WINDOW), jnp.float32),
            jax.ShapeDtypeStruct((depth, nseq, LANES, WINDOW), jnp.float32),
            jax.ShapeDtypeStruct((depth, nseq * DK_TOT, DV_C), jnp.float32),
            jax.ShapeDtypeStruct((depth, nseq, D_B), jnp.float32),
        ],
        scratch_shapes=[
            pltpu.VMEM((nseq, D_MODEL), jnp.float32),
            pltpu.VMEM((nseq, D_IN_PAD), jnp.float32),
            pltpu.VMEM((nseq, D_MIX), jnp.float32),
            pltpu.VMEM((nseq * H_A, LANES), jnp.float32),
            pltpu.VMEM((nseq * H_A, LANES), jnp.float32),
            pltpu.VMEM((nseq, DK_TOT), jnp.float32),
            pltpu.VMEM((nseq, DK_TOT), jnp.float32),
            pltpu.VMEM((nseq, D_C), jnp.float32),
        ],
        compiler_params=pltpu.CompilerParams(
            dimension_semantics=("arbitrary", "arbitrary"),
            vmem_limit_bytes=VMEM_LIMIT_BYTES),
        name="sample_path",
    )(x, ng, win, wout, bias_s, sink_c, w00, b0, lng, lnb, wup, bup, gn, fg, buf_kt, buf_vt, s0)


def _window_minor(state):
    depth, n = state.shape[:2]
    return jnp.transpose(state, (0, 1, 3, 4, 2)).reshape(depth, n, LANES, WINDOW)


def _window_major(state_t):
    depth, n = state_t.shape[:2]
    return jnp.transpose(state_t.reshape(depth, n, KV_A, HD_A, WINDOW), (0, 1, 4, 2, 3))


def kernel(x_prompt, x_sample, state_swa_k, state_swa_v, state_gla, rel_bias, norm_g, w_in, sinks, spatial_w,
           spatial_b, chunk_ln_g, chunk_ln_b, gla_w_up, gla_b_up, gla_norm_g, w_out, final_norm_g):
    depth = w_in.shape[0]
    nseq = x_sample.shape[0]
    bsz = x_prompt.shape[0]
    f32 = jnp.float32

    win, wout = _prepare_weights(w_in, w_out)
    wup = jnp.pad(gla_w_up, ((0, 0), (0, LANES - GLA_RANK), (0, 0))).astype(jnp.bfloat16)
    bsf = jnp.repeat(jnp.swapaxes(spatial_b, 1, 2), DH_B, axis=2).astype(f32)
    w00 = jnp.repeat(spatial_w[:, :, 0, 0], DH_B, axis=1).astype(f32)[:, None, :]
    b0 = jnp.repeat(spatial_b[:, :, 0], DH_B, axis=1).astype(f32)[:, None, :]
    sink_c = sinks.astype(f32)[:, :, None]
    row = lambda a: a.astype(f32)[:, None, :]
    ng, lng, lnb, bup, gn = row(norm_g), row(chunk_ln_g), row(chunk_ln_b), row(gla_b_up), row(gla_norm_g)
    fg = final_norm_g.astype(f32)[None, :]
    sw = spatial_w.astype(f32)
    sinks = sinks.astype(f32)

    bias_tab = _bias_table(rel_bias)
    bias_s = bias_tab[0].reshape(H_A, WINDOW, 2 * WINDOW)[:, WINDOW - 1, WINDOW:]

    xp = x_prompt
    kp_l, vp_l, sp_l = [], [], []
    for l in range(depth):
        xp, kp, vp, sp = _prompt_layer(xp, ng, win, wout, bias_tab, sinks, sw, bsf, lng, lnb, wup, bup, gn, fg,
                                       layer=l, final=l == depth - 1)
        kp_l.append(kp); vp_l.append(vp); sp_l.append(sp)

    xs, ks_t, vs_t, ss, cv = _sample_path(
        x_sample.reshape(nseq, D_MODEL), ng, win, wout, bias_s, sink_c, w00, b0, lng, lnb, wup, bup, gn, fg,
        _window_minor(state_swa_k), _window_minor(state_swa_v), state_gla.reshape(depth, nseq * DK_TOT, DV_C))

    return (xp,
            xs.reshape(nseq, 1, D_MODEL),
            _window_major(jnp.stack(kp_l)),
            _window_major(jnp.stack(vp_l)),
            jnp.stack(sp_l).reshape(depth, bsz, H_C, DK_C, DV_C),
            _window_major(ks_t),
            _window_major(vs_t),
            ss.reshape(depth, nseq, H_C, DK_C, DV_C),
            cv.reshape(depth, nseq, 1, D_B))
```

```python
import functools

import numpy as np
import jax
import jax.numpy as jnp
from jax import lax
from jax.experimental import pallas as pl
from jax.experimental.pallas import tpu as pltpu

D_MODEL = 1024
D_A, HD_A, H_A, KV_A, G_A = 512, 64, 8, 2, 4
WINDOW, N_BUCKETS, MAX_DIST = 128, 32, 128
D_B, H_B, DH_B, CHUNK_B = 512, 8, 64, 128
D_C, H_C, DK_TOT, DK_C, DV_C = 512, 4, 256, 64, 128
GLA_RANK, GLA_TAU, GLA_CHUNK = 16, 16.0, 64
D_MIX = D_A + D_B + D_C
EPS = 1e-6
NEG = -1e30
SPLITS = [D_A, KV_A * HD_A, KV_A * HD_A, D_A, D_B, D_B, D_B, DK_TOT, DK_TOT, D_C, D_C, GLA_RANK]
D_IN = sum(SPLITS)

LANES = 128
HALF = LANES // 2
D_IN_PAD = ((D_IN + LANES - 1) // LANES) * LANES
VMEM_LIMIT_BYTES = 56 * 1024 * 1024

O_QA, O_KA, O_VA, O_GA = 0, 512, 640, 768
O_UB, O_VB, O_GB = 1280, 1792, 2304
O_QC, O_KC, O_VC, O_GC, O_LR = 2816, 3072, 3328, 3840, 4352
Y_A, Y_B, Y_C = 0, D_A, D_A + D_B

HEAD_PERM = [0, 4, 1, 5, 2, 6, 3, 7]

TOKEN_BLOCK = 512
SEQ_GROUP = 16
IN_PROJ_CHUNK = 256
IN_PROJ_ROWS = 256
OUT_PROJ_CHUNK = 256
MIXER_SECTIONS = 22
PREP_SPLIT = 2

_NT = (((1,), (1,)), ((), ()))


def _t5_bucket(dist):
    n = np.maximum(dist, 0)
    max_exact = N_BUCKETS // 2
    large = max_exact + (np.log(np.maximum(n, 1) / max_exact) / np.log(MAX_DIST / max_exact)
                         * (N_BUCKETS - max_exact)).astype(np.int32)
    large = np.minimum(large, N_BUCKETS - 1)
    return np.where(n < max_exact, n, large).astype(np.int32)


def _silu(x):
    return x * (1.0 / (1.0 + jnp.exp(-x)))


def _log_sigmoid(x):
    return jnp.minimum(x, 0.0) - jnp.log1p(jnp.exp(-jnp.abs(x)))


def _bdot(a, b):
    return jnp.dot(a.astype(jnp.bfloat16), b.astype(jnp.bfloat16), preferred_element_type=jnp.float32)


def _bdot_nt(a, b):
    return lax.dot_general(a.astype(jnp.bfloat16), b.astype(jnp.bfloat16), _NT,
                           preferred_element_type=jnp.float32)


def _rmsnorm(x, g):
    return x * lax.rsqrt(jnp.mean(x * x, axis=-1, keepdims=True) + EPS) * g


def _layernorm(v, g, b):
    mu = jnp.mean(v, axis=-1, keepdims=True)
    xc = v - mu
    var = jnp.mean(xc * xc, axis=-1, keepdims=True)
    return xc * lax.rsqrt(var + EPS) * g + b


def _softmax_sink(s, sink):
    m = jnp.maximum(jnp.max(s, axis=-1, keepdims=True), sink)
    p = jnp.exp(s - m)
    den = jnp.sum(p, axis=-1, keepdims=True) + jnp.exp(sink - m)
    return p, 1.0 / den


def _bias_table_kernel(rb_ref, bucket_ref, band_ref, out_ref):
    bucket = bucket_ref[...]
    band = band_ref[...] > 0
    own = lax.broadcasted_iota(jnp.int32, bucket.shape, 1) >= WINDOW
    for h in range(H_A):
        acc = jnp.zeros(bucket.shape, jnp.float32)
        for b in range(N_BUCKETS):
            acc = jnp.where(bucket == b, rb_ref[b, h], acc)
        kv, g = divmod(h, G_A)
        rows = pl.ds(g * WINDOW, WINDOW)
        out_ref[0, kv, rows, :] = jnp.where(band, acc, NEG)
        out_ref[1, kv, rows, :] = jnp.where(band & own, acc, NEG)


def _bias_table(rel_bias):
    i = np.arange(WINDOW)[:, None]
    j = np.arange(2 * WINDOW)[None, :]
    dist = i + WINDOW - j
    band = ((dist >= 0) & (dist < WINDOW)).astype(np.int32)
    return pl.pallas_call(
        _bias_table_kernel,
        out_shape=jax.ShapeDtypeStruct((2, KV_A, G_A * WINDOW, 2 * WINDOW), jnp.float32),
        in_specs=[pl.BlockSpec(memory_space=pltpu.SMEM),
                  pl.BlockSpec(memory_space=pltpu.VMEM),
                  pl.BlockSpec(memory_space=pltpu.VMEM)],
        out_specs=pl.BlockSpec(memory_space=pltpu.VMEM),
        name="rel_bias_table",
    )(rel_bias.astype(jnp.float32), jnp.asarray(_t5_bucket(dist)), jnp.asarray(band))


def _win_prep_kernel(wt_ref, lr_ref, o_ref):
    c = pl.program_id(1)
    tiles = wt_ref.shape[0] // LANES
    for cc in range(PREP_SPLIT):
        @pl.when(c == cc)
        def _(cc=cc):
            for t in range(tiles):
                col = (cc * tiles + t) * LANES
                base = next((b for b in (O_QA, O_GA) if b <= col < b + D_A), None)
                if base is None:
                    src = wt_ref[t * LANES:(t + 1) * LANES, :]
                else:
                    g = (col - base) // LANES
                    r0 = base - cc * tiles * LANES
                    assert r0 >= 0 and r0 + D_A <= wt_ref.shape[0]
                    src = jnp.concatenate([wt_ref[r0 + g * HD_A:r0 + (g + 1) * HD_A, :],
                                           wt_ref[r0 + (G_A + g) * HD_A:r0 + (G_A + g + 1) * HD_A, :]], axis=0)
                o_ref[:, col:col + LANES] = src.T.astype(jnp.bfloat16)
            if cc == PREP_SPLIT - 1:
                o_ref[:, O_LR:D_IN_PAD] = lr_ref[...].T.astype(jnp.bfloat16)


def _wout_prep_kernel(w_ref, o_ref):
    for j, h in enumerate(HEAD_PERM):
        o_ref[j * HD_A:(j + 1) * HD_A, :] = w_ref[h * HD_A:(h + 1) * HD_A, :].astype(jnp.bfloat16)
    o_ref[Y_B:, :] = w_ref[Y_B:, :].astype(jnp.bfloat16)


def _prepare_weights(w_in, w_out):
    depth = w_in.shape[0]
    cols = D_MODEL // PREP_SPLIT
    wt = jnp.swapaxes(w_in, 1, 2)
    assert O_LR % (PREP_SPLIT * LANES) == 0
    rows = O_LR // PREP_SPLIT
    wt_lr = jnp.pad(wt[:, O_LR:, :], ((0, 0), (0, D_IN_PAD - D_IN), (0, 0)))
    win = pl.pallas_call(
        _win_prep_kernel,
        grid=(depth, PREP_SPLIT),
        in_specs=[pl.BlockSpec((None, rows, D_MODEL), lambda l, r: (l, r, 0)),
                  pl.BlockSpec((None, LANES, D_MODEL), lambda l, r: (l, 0, 0))],
        out_specs=pl.BlockSpec((None, D_MODEL, D_IN_PAD), lambda l, r: (l, 0, 0)),
        out_shape=jax.ShapeDtypeStruct((depth, D_MODEL, D_IN_PAD), jnp.bfloat16),
        compiler_params=pltpu.CompilerParams(dimension_semantics=("arbitrary", "arbitrary"),
                                             vmem_limit_bytes=VMEM_LIMIT_BYTES),
        name="w_in_prep",
    )(wt, wt_lr)
    wout = pl.pallas_call(
        _wout_prep_kernel,
        grid=(depth, PREP_SPLIT),
        in_specs=[pl.BlockSpec((None, D_MIX, cols), lambda l, c: (l, 0, c))],
        out_specs=pl.BlockSpec((None, D_MIX, cols), lambda l, c: (l, 0, c)),
        out_shape=jax.ShapeDtypeStruct((depth, D_MIX, D_MODEL), jnp.bfloat16),
        compiler_params=pltpu.CompilerParams(dimension_semantics=("arbitrary", "arbitrary"),
                                             vmem_limit_bytes=VMEM_LIMIT_BYTES),
        name="w_out_prep",
    )(w_out)
    return win, wout


def _prompt_layer_kernel(x0_ref, xb_ref, xn_ref, ng_ref, win_ref, wout_ref, bias_ref, sink_ref, sw_ref, bsf_ref,
                         lng_ref, lnb_ref, wup_ref, bup_ref, gn_ref, fg_ref,
                         xo_ref, ko_ref, vo_ref, so_ref,
                         xa_ref, pa_ref, pb_ref, hb_ref, hn_ref, ya_ref, yb_ref, kprev_ref, vprev_ref, st_ref,
                         wm_ref, *, tb, layer, final):
    b_id = pl.program_id(0)
    i_id = pl.program_id(1)
    n_i = pl.num_programs(1)
    half = tb // 2
    n_sub = half // WINDOW

    @pl.when((b_id == 0) & (i_id == 0))
    def _():
        r = lax.broadcasted_iota(jnp.int32, (CHUNK_B, CHUNK_B), 0)
        c = lax.broadcasted_iota(jnp.int32, (CHUNK_B, CHUNK_B), 1)
        for h in range(H_B):
            wm_ref[h // 2, :, (h % 2) * CHUNK_B:(h % 2 + 1) * CHUNK_B] = jnp.where(
                c <= r, sw_ref[h], 0.0).astype(jnp.bfloat16)
        xa_ref[...] = x0_ref[...]
        pa_ref[...] = _bdot(_rmsnorm(x0_ref[...], ng_ref[...]), win_ref[...])

    @pl.when(i_id == 0)
    def _():
        kprev_ref[...] = jnp.zeros_like(kprev_ref)
        vprev_ref[...] = jnp.zeros_like(vprev_ref)
        st_ref[...] = jnp.zeros_like(st_ref)

    hb_ref[...] = _rmsnorm(xb_ref[...], ng_ref[...]).astype(jnp.bfloat16)
    hn_ref[...] = _rmsnorm(xn_ref[...], ng_ref[...]).astype(jnp.bfloat16)

    def in_proj_chunks(h_ref, p_ref):
        def chunk(r0, c0, c1):
            def run():
                p_ref[r0:r0 + IN_PROJ_ROWS, c0:c1] = jnp.dot(h_ref[r0:r0 + IN_PROJ_ROWS, :], win_ref[:, c0:c1],
                                                             preferred_element_type=jnp.float32)
            return run
        edges = list(range(0, D_IN_PAD, IN_PROJ_CHUNK)) + [D_IN_PAD]
        return [chunk(r0, c0, c1) for c0, c1 in zip(edges[:-1], edges[1:])
                for r0 in range(0, half, IN_PROJ_ROWS)]

    def out_proj_chunks(y_ref, r0, s):
        def chunk(c0, c1):
            def run():
                rows = slice(s * WINDOW, (s + 1) * WINDOW)
                x_half = xa_ref if r0 == 0 else xb_ref
                xo_ref[r0 + s * WINDOW:r0 + (s + 1) * WINDOW, c0:c1] = x_half[rows, c0:c1] + jnp.dot(
                    y_ref[rows, :], wout_ref[:, c0:c1], preferred_element_type=jnp.float32)
            return run
        return [chunk(c0, c0 + OUT_PROJ_CHUNK) for c0 in range(0, D_MODEL, OUT_PROJ_CHUNK)]

    def final_norm(r0):
        if final:
            xo_ref[r0:r0 + half, :] = _rmsnorm(xo_ref[r0:r0 + half, :], fg_ref[...])

    lane = lax.broadcasted_iota(jnp.int32, (WINDOW, LANES), 1)
    low = lane < HALF
    low64 = lax.broadcasted_iota(jnp.int32, (GLA_CHUNK, LANES), 1) < HALF
    ri = lax.broadcasted_iota(jnp.int32, (LANES, LANES), 0)
    ci = lax.broadcasted_iota(jnp.int32, (LANES, LANES), 1)
    bd_tril = ((ri >= GLA_CHUNK) == (ci >= GLA_CHUNK)) & (ci <= ri)
    cum_mat = jnp.where(bd_tril, 1.0, 0.0).astype(jnp.bfloat16)
    ri2 = lax.broadcasted_iota(jnp.int32, (LANES, 2 * LANES), 0)
    ci2 = lax.broadcasted_iota(jnp.int32, (LANES, 2 * LANES), 1) & (LANES - 1)
    bd_tril2 = ((ri2 >= GLA_CHUNK) == (ci2 >= GLA_CHUNK)) & (ci2 <= ri2)

    def mixers(p_ref, y_ref, s, first):
        r0 = s * WINDOW
        rows = pl.ds(r0, WINDOW)

        ks = p_ref[rows, O_KA:O_KA + LANES] * (HD_A ** -0.5)
        k_new = [jnp.where(low, ks, 0.0).astype(jnp.bfloat16), jnp.where(low, 0.0, ks).astype(jnp.bfloat16)]
        v_new = p_ref[rows, O_VA:O_VA + LANES].T.astype(jnp.bfloat16)
        vcat = jnp.concatenate([carry["v"], v_new], axis=1)
        q4 = jnp.concatenate([p_ref[rows, O_QA + g * LANES:O_QA + (g + 1) * LANES].astype(jnp.bfloat16)
                              for g in range(G_A)], axis=0)
        o_heads = []
        for kv in range(KV_A):
            kcat = jnp.concatenate([carry["k"][kv], k_new[kv]], axis=0)
            sc = lax.dot_general(q4, kcat, _NT, preferred_element_type=jnp.float32) + bias_ref[first, kv]
            carry["k"][kv] = k_new[kv]
            yield
            ps, invs = [], []
            for g in range(G_A):
                p, inv = _softmax_sink(sc[g * WINDOW:(g + 1) * WINDOW], sink_ref[layer, kv * G_A + g])
                ps.append(p.astype(jnp.bfloat16))
                invs.append(inv)
                if g % 2 == 1:
                    yield
            o_t = lax.dot_general(vcat[kv * HD_A:(kv + 1) * HD_A, :], jnp.concatenate(ps, axis=0), _NT,
                                  preferred_element_type=jnp.float32)
            o_heads.append((o_t, invs))
            yield
        carry["v"] = v_new
        for g in range(G_A):
            o_pair = jnp.concatenate([o_heads[kv][0][:, g * WINDOW:(g + 1) * WINDOW] for kv in range(KV_A)],
                                     axis=0).T
            ya = o_pair * jnp.where(low, o_heads[0][1][g], o_heads[1][1][g])
            gate = p_ref[rows, O_GA + g * LANES:O_GA + (g + 1) * LANES]
            y_ref[rows, Y_A + g * LANES:Y_A + (g + 1) * LANES] = (ya * _silu(gate)).astype(jnp.bfloat16)
        yield

        vn = _layernorm(p_ref[rows, O_VB:O_VB + D_B], lng_ref[...], lnb_ref[...]).astype(jnp.bfloat16)
        yield
        for j in range(H_B // 2):
            vp = vn[:, j * LANES:(j + 1) * LANES]
            zero = jnp.zeros_like(vp)
            v_st = jnp.concatenate([jnp.where(low, vp, zero), jnp.where(low, zero, vp)], axis=0)
            mix = (jnp.dot(wm_ref[j], v_st, preferred_element_type=jnp.float32)
                   + bsf_ref[:, j * LANES:(j + 1) * LANES])
            u = p_ref[rows, O_UB + j * LANES:O_UB + (j + 1) * LANES]
            gate = p_ref[rows, O_GB + j * LANES:O_GB + (j + 1) * LANES]
            y_ref[rows, Y_B + j * LANES:Y_B + (j + 1) * LANES] = (u * mix * _silu(gate)).astype(jnp.bfloat16)
            if j % 2 == 1:
                yield

        z = _bdot(p_ref[rows, O_LR:O_LR + LANES], wup_ref[...]) + bup_ref[...]
        la = _log_sigmoid(z) * (1.0 / GLA_TAU)
        yield
        hi = la.astype(jnp.bfloat16)
        r1 = la - hi.astype(jnp.float32)
        mid = r1.astype(jnp.bfloat16)
        lo = (r1 - mid.astype(jnp.float32)).astype(jnp.bfloat16)
        c3 = jnp.dot(cum_mat, jnp.concatenate([hi, mid, lo], axis=1), preferred_element_type=jnp.float32)
        bc = (c3[:, 0:DK_TOT] + c3[:, DK_TOT:2 * DK_TOT]) + c3[:, 2 * DK_TOT:3 * DK_TOT]
        yield
        for c in range(WINDOW // GLA_CHUNK):
            rc = pl.ds(r0 + c * GLA_CHUNK, GLA_CHUNK)
            q_sts, k_sts, v_sts, kd_sts, decays, sts = [], [], [], [], [], []
            for p in range(H_C // 2):
                bp = bc[c * GLA_CHUNK:(c + 1) * GLA_CHUNK, p * LANES:(p + 1) * LANES]
                qp = p_ref[rc, O_QC + p * LANES:O_QC + (p + 1) * LANES]
                kp = p_ref[rc, O_KC + p * LANES:O_KC + (p + 1) * LANES]
                b_last = bp[GLA_CHUNK - 1:GLA_CHUNK, :]
                qd = qp * (DK_C ** -0.5) * jnp.exp(bp)
                ki = (kp * jnp.exp(-bp)).astype(jnp.bfloat16)
                kd = kp * jnp.exp(b_last - bp)
                q_sts.append(jnp.concatenate([jnp.where(low64, qd, 0.0), jnp.where(low64, 0.0, qd)],
                                             axis=0).astype(jnp.bfloat16))
                k_sts.append(jnp.concatenate([ki, ki], axis=0))
                v0 = p_ref[rc, O_VC + (2 * p) * DV_C:O_VC + (2 * p + 1) * DV_C]
                v1 = p_ref[rc, O_VC + (2 * p + 1) * DV_C:O_VC + (2 * p + 2) * DV_C]
                v_sts.append(jnp.concatenate([v0, v1], axis=0))
                kd_sts.append(jnp.concatenate([jnp.where(low64, kd, 0.0), jnp.where(low64, 0.0, kd)],
                                              axis=0).astype(jnp.bfloat16))
                decays.append(jnp.exp(b_last))
                yield
            zero = jnp.zeros((LANES, LANES), jnp.bfloat16)
            blockdiag = lambda a, b: jnp.concatenate([jnp.concatenate([a, zero], axis=1),
                                                      jnp.concatenate([zero, b], axis=1)], axis=0)
            q_all = jnp.concatenate(q_sts, axis=1)
            att = lax.dot_general(q_all, blockdiag(*k_sts), _NT, preferred_element_type=jnp.float32)
            att = jnp.where(bd_tril2, att, 0.0).astype(jnp.bfloat16)
            v_b = [v.astype(jnp.bfloat16) for v in v_sts]
            o_all = jnp.dot(att, blockdiag(*v_b), preferred_element_type=jnp.float32)
            d_all = jnp.dot(jnp.concatenate([v.T.astype(jnp.bfloat16) for v in v_sts], axis=1),
                            blockdiag(*kd_sts), preferred_element_type=jnp.float32)
            sts = carry["st"]
            o_inter = lax.dot_general(q_all, blockdiag(*[st.astype(jnp.bfloat16) for st in sts]), _NT,
                                      preferred_element_type=jnp.float32)
            o_all = o_all + o_inter
            carry["st"] = [sts[p] * decays[p] + d_all[:, p * LANES:(p + 1) * LANES] for p in range(H_C // 2)]
            for p in range(H_C // 2):
                o = o_all[:, p * DV_C:(p + 1) * DV_C]
                for hh in range(2):
                    hd = 2 * p + hh
                    oh = o[hh * GLA_CHUNK:(hh + 1) * GLA_CHUNK]
                    on = _rmsnorm(oh, gn_ref[:, hd * DV_C:(hd + 1) * DV_C])
                    gate = p_ref[rc, O_GC + hd * DV_C:O_GC + (hd + 1) * DV_C]
                    y_ref[rc, Y_C + hd * DV_C:Y_C + (hd + 1) * DV_C] = (on * _silu(gate)).astype(jnp.bfloat16)
                yield

    def half_mixers(p_ref, y_ref, first):
        for s in range(n_sub):
            yield from mixers(p_ref, y_ref, s, first if s == 0 else 0)

    def run_phase(sections, chunks, late_chunks):
        n_sections = n_sub * MIXER_SECTIONS
        plan = [[] for _ in range(n_sections)]
        for idx, run in enumerate(chunks):
            plan[idx * n_sections // len(chunks)].append(run)
        for s, runs in late_chunks.items():
            k0 = (s + 1) * MIXER_SECTIONS
            for idx, run in enumerate(runs):
                plan[k0 + idx * (n_sections - k0) // len(runs)].append(run)
        for k in range(n_sections):
            next(sections)
            for run in plan[k]:
                run()
        assert next(sections, None) is None

    carry = {"k": [kprev_ref[kv] for kv in range(KV_A)], "v": vprev_ref[...],
             "st": [st_ref[p] for p in range(H_C // 2)]}
    run_phase(half_mixers(pa_ref, ya_ref, jnp.where(i_id == 0, 1, 0)),
              in_proj_chunks(hb_ref, pb_ref),
              {s: out_proj_chunks(ya_ref, 0, s) for s in range(n_sub - 1)})
    run_phase(half_mixers(pb_ref, yb_ref, 0),
              out_proj_chunks(ya_ref, 0, n_sub - 1) + in_proj_chunks(hn_ref, pa_ref),
              {s: out_proj_chunks(yb_ref, half, s) for s in range(n_sub - 1)})
    final_norm(0)
    for run in out_proj_chunks(yb_ref, half, n_sub - 1):
        run()
    final_norm(half)
    xa_ref[...] = xn_ref[...]
    for kv in range(KV_A):
        kprev_ref[kv] = carry["k"][kv]
    vprev_ref[...] = carry["v"]
    for p in range(H_C // 2):
        st_ref[p] = carry["st"][p]

    @pl.when(i_id == n_i - 1)
    def _():
        ko_ref[...] = pb_ref[half - WINDOW:half, O_KA:O_KA + LANES].T
        vo_ref[...] = pb_ref[half - WINDOW:half, O_VA:O_VA + LANES].T
        for p in range(H_C // 2):
            so_ref[p] = st_ref[p].T


def _prompt_layer(x, ng, win, wout, bias_tab, sinks, sw, bsf, lng, lnb, wup, bup, gn, fg, *, layer, final):
    bsz, seq, _ = x.shape
    tb = min(TOKEN_BLOCK, seq)
    half = tb // 2
    assert seq % tb == 0 and half % WINDOW == 0
    n_i = seq // tb
    grid = (bsz, n_i)

    def next_half(b, i):
        lin = jnp.minimum(b * n_i + i + 1, bsz * n_i - 1)
        return lin // n_i, (lin % n_i) * 2, 0

    def const_spec(shape):
        nd = len(shape)
        return pl.BlockSpec(shape, lambda b, i: (0,) * nd, pipeline_mode=pl.Buffered(1))

    def layer_spec(shape):
        nd = len(shape)
        return pl.BlockSpec((None,) + shape, lambda b, i: (layer,) + (0,) * nd, pipeline_mode=pl.Buffered(1))

    kern = functools.partial(_prompt_layer_kernel, tb=tb, layer=layer, final=final)
    return pl.pallas_call(
        kern,
        grid=grid,
        in_specs=[
            pl.BlockSpec((None, half, D_MODEL), lambda b, i: (0, 0, 0), pipeline_mode=pl.Buffered(1)),
            pl.BlockSpec((None, half, D_MODEL), lambda b, i: (b, 2 * i + 1, 0)),
            pl.BlockSpec((None, half, D_MODEL), next_half),
            layer_spec((1, D_MODEL)),
            layer_spec((D_MODEL, D_IN_PAD)),
            layer_spec((D_MIX, D_MODEL)),
            const_spec((2, KV_A, G_A * WINDOW, 2 * WINDOW)),
            pl.BlockSpec(memory_space=pltpu.SMEM),
            layer_spec((H_B, CHUNK_B, CHUNK_B)),
            layer_spec((CHUNK_B, D_B)),
            layer_spec((1, D_B)),
            layer_spec((1, D_B)),
            layer_spec((LANES, DK_TOT)),
            layer_spec((1, DK_TOT)),
            layer_spec((1, D_C)),
            const_spec((1, D_MODEL)),
        ],
        out_specs=[
            pl.BlockSpec((None, tb, D_MODEL), lambda b, i: (b, i, 0)),
            pl.BlockSpec((None, LANES, WINDOW), lambda b, i: (b, 0, 0)),
            pl.BlockSpec((None, LANES, WINDOW), lambda b, i: (b, 0, 0)),
            pl.BlockSpec((None, H_C // 2, LANES, DV_C), lambda b, i: (b, 0, 0, 0)),
        ],
        out_shape=[
            jax.ShapeDtypeStruct((bsz, seq, D_MODEL), jnp.float32),
            jax.ShapeDtypeStruct((bsz, LANES, WINDOW), jnp.float32),
            jax.ShapeDtypeStruct((bsz, LANES, WINDOW), jnp.float32),
            jax.ShapeDtypeStruct((bsz, H_C // 2, LANES, DV_C), jnp.float32),
        ],
        scratch_shapes=[
            pltpu.VMEM((half, D_MODEL), jnp.float32),
            pltpu.VMEM((half, D_IN_PAD), jnp.float32),
            pltpu.VMEM((half, D_IN_PAD), jnp.float32),
            pltpu.VMEM((half, D_MODEL), jnp.bfloat16),
            pltpu.VMEM((half, D_MODEL), jnp.bfloat16),
            pltpu.VMEM((half, D_MIX), jnp.bfloat16),
            pltpu.VMEM((half, D_MIX), jnp.bfloat16),
            pltpu.VMEM((KV_A, WINDOW, LANES), jnp.bfloat16),
            pltpu.VMEM((WINDOW, LANES), jnp.bfloat16),
            pltpu.VMEM((H_C // 2, DV_C, LANES), jnp.float32),
            pltpu.VMEM((H_B // 2, CHUNK_B, 2 * CHUNK_B), jnp.bfloat16),
        ],
        compiler_params=pltpu.CompilerParams(
            dimension_semantics=("arbitrary", "arbitrary"),
            vmem_limit_bytes=VMEM_LIMIT_BYTES),
        name="prompt_layer",
    )(x, x, x, ng, win, wout, bias_tab, sinks, sw, bsf, lng, lnb, wup, bup, gn, fg)


def _sample_kernel(x_ref, ng_ref, win_ref, wout_ref, biass_ref, sinkc_ref, w00_ref, b0_ref, lng_ref,
                   lnb_ref, wup_ref, bup_ref, gn_ref, fg_ref, bk_ref, bv_ref, s0_ref,
                   xo_ref, nk_ref, nv_ref, so_ref, cv_ref,
                   xcur_ref, proj_ref, y_ref, qs_ref, os_ref, qd_ref, ea_ref, oc_ref, *, nseq, gs):
    l_id = pl.program_id(0)
    i_id = pl.program_id(1)
    n_l = pl.num_programs(0)
    n_i = pl.num_programs(1)
    lane = lax.broadcasted_iota(jnp.int32, (nseq, LANES), 1)
    low = lane < HALF

    @pl.when((l_id == 0) & (i_id == 0))
    def _():
        xcur_ref[...] = x_ref[...]

    @pl.when(i_id == 0)
    def _():
        h = _rmsnorm(xcur_ref[...], ng_ref[...])
        proj_ref[...] = _bdot(h, win_ref[...])
        for g in range(G_A):
            qp = proj_ref[:, O_QA + g * LANES:O_QA + (g + 1) * LANES] * (HD_A ** -0.5)
            for kv in range(KV_A):
                keep = low if kv == 0 else jnp.logical_not(low)
                qs_ref[pl.ds(kv * G_A + g, nseq, stride=H_A), :] = jnp.where(keep, qp, 0.0)
        vn = _layernorm(proj_ref[:, O_VB:O_VB + D_B], lng_ref[...], lnb_ref[...])
        cv_ref[...] = vn
        mix = w00_ref[...] * vn + b0_ref[...]
        y_ref[:, Y_B:Y_B + D_B] = (proj_ref[:, O_UB:O_UB + D_B] * mix) * _silu(proj_ref[:, O_GB:O_GB + D_B])
        z = _bdot(proj_ref[:, O_LR:O_LR + LANES], wup_ref[...]) + bup_ref[...]
        la = _log_sigmoid(z) * (1.0 / GLA_TAU)
        ea = jnp.exp(la)
        qd = proj_ref[:, O_QC:O_QC + DK_TOT] * (DK_C ** -0.5) * ea
        ki = proj_ref[:, O_KC:O_KC + DK_TOT] * jnp.exp(-la)
        qd_ref[...] = qd
        ea_ref[...] = ea
        prod = qd * ki
        lane_c = lax.broadcasted_iota(jnp.int32, prod.shape, 1)
        for hd in range(H_C):
            in_head = (lane_c >= hd * DK_C) & (lane_c < (hd + 1) * DK_C)
            att = jnp.sum(jnp.where(in_head, prod, 0.0), axis=-1, keepdims=True)
            oc_ref[:, hd * DV_C:(hd + 1) * DV_C] = att * proj_ref[:, O_VC + hd * DV_C:O_VC + (hd + 1) * DV_C]

    g0 = pl.multiple_of(i_id * gs, gs)
    grows = pl.ds(g0, gs)

    pad_rows = jnp.zeros((LANES - gs, LANES), jnp.float32)
    k_cols = jnp.concatenate([proj_ref[grows, O_KA:O_KA + LANES], pad_rows], axis=0).T
    v_cols = jnp.concatenate([proj_ref[grows, O_VA:O_VA + LANES], pad_rows], axis=0).T
    newest = lax.broadcasted_iota(jnp.int32, (LANES, WINDOW), 1) == WINDOW - 1
    for j in range(gs):
        kt = jnp.where(newest, k_cols[:, j:j + 1], pltpu.roll(bk_ref[j], WINDOW - 1, 1))
        vt = jnp.where(newest, v_cols[:, j:j + 1], pltpu.roll(bv_ref[j], WINDOW - 1, 1))
        nk_ref[j] = kt
        nv_ref[j] = vt
        r8 = pl.ds(pl.multiple_of((g0 + j) * H_A, H_A), H_A)
        sc = _bdot(qs_ref[r8, :], kt) + biass_ref[...]
        p, inv = _softmax_sink(sc, sinkc_ref[...])
        os_ref[r8, :] = _bdot_nt(p, vt) * inv

    pad_c = jnp.zeros((LANES - gs, DK_TOT), jnp.float32)
    ea_cols = jnp.concatenate([ea_ref[grows, :], pad_c], axis=0).T
    k_cols_c = jnp.concatenate([proj_ref[grows, O_KC:O_KC + DK_TOT], pad_c], axis=0).T
    qd_g = qd_ref[grows, :]
    row_g = lax.broadcasted_iota(jnp.int32, (gs, LANES), 0)
    low_g = lax.broadcasted_iota(jnp.int32, (gs, LANES), 1) < HALF
    seqs_per_dot = 2 * LANES // DK_C
    for hd in range(H_C):
        v_h = proj_ref[grows, O_VC + hd * DV_C:O_VC + (hd + 1) * DV_C]
        q_pair = qd_g[:, (hd // 2) * LANES:(hd // 2 + 1) * LANES]
        q_other = pltpu.roll(q_pair, HALF, 1)
        q_lo, q_hi = (q_pair, q_other) if hd % 2 == 0 else (q_other, q_pair)
        acc = jnp.zeros((gs, DV_C), jnp.float32)
        for j0 in range(0, gs, seqs_per_dot):
            tiles, lhs = [], []
            for j in range(j0, j0 + seqs_per_dot):
                srows = pl.ds(j * DK_TOT + hd * DK_C, DK_C)
                s_old = s0_ref[srows, :]
                tiles.append(s_old.astype(jnp.bfloat16))
                crows = slice(hd * DK_C, (hd + 1) * DK_C)
                so_ref[srows, :] = ea_cols[crows, j:j + 1] * s_old + k_cols_c[crows, j:j + 1] * v_h[j:j + 1, :]
            for j in range(j0, j0 + seqs_per_dot, 2):
                lhs.append(jnp.where((row_g == j) & low_g, q_lo,
                                     jnp.where((row_g == j + 1) & jnp.logical_not(low_g), q_hi, 0.0)))
            acc = acc + _bdot(jnp.concatenate(lhs, axis=1), jnp.concatenate(tiles, axis=0))
        oc_ref[grows, hd * DV_C:(hd + 1) * DV_C] += acc

    @pl.when(i_id == n_i - 1)
    def _():
        for g in range(G_A):
            o0 = os_ref[pl.ds(g, nseq, stride=H_A), :]
            o1 = os_ref[pl.ds(G_A + g, nseq, stride=H_A), :]
            gate = proj_ref[:, O_GA + g * LANES:O_GA + (g + 1) * LANES]
            y_ref[:, Y_A + g * LANES:Y_A + (g + 1) * LANES] = jnp.where(low, o0, o1) * _silu(gate)
        for hd in range(H_C):
            on = _rmsnorm(oc_ref[:, hd * DV_C:(hd + 1) * DV_C], gn_ref[:, hd * DV_C:(hd + 1) * DV_C])
            gate = proj_ref[:, O_GC + hd * DV_C:O_GC + (hd + 1) * DV_C]
            y_ref[:, Y_C + hd * DV_C:Y_C + (hd + 1) * DV_C] = on * _silu(gate)
        x_new = xcur_ref[...] + _bdot(y_ref[...], wout_ref[...])
        xcur_ref[...] = x_new

        @pl.when(l_id == n_l - 1)
        def _():
            xo_ref[...] = _rmsnorm(x_new, fg_ref[...])


def _sample_path(x, ng, win, wout, bias_s, sink_c, w00, b0, lng, lnb, wup, bup, gn, fg, buf_kt, buf_vt, s0):
    depth = win.shape[0]
    nseq = x.shape[0]
    gs = min(SEQ_GROUP, nseq)
    assert nseq % gs == 0 and gs % 8 == 0
    grid = (depth, nseq // gs)

    def const_spec(shape):
        nd = len(shape)
        return pl.BlockSpec(shape, lambda l, i: (0,) * nd)

    def layer_spec(shape):
        nd = len(shape)
        return pl.BlockSpec((None,) + shape, lambda l, i: (l,) + (0,) * nd)

    state_spec = pl.BlockSpec((None, gs, LANES, WINDOW), lambda l, i: (l, i, 0, 0))
    gla_spec = pl.BlockSpec((None, gs * DK_TOT, DV_C), lambda l, i: (l, i, 0))
    kern = functools.partial(_sample_kernel, nseq=nseq, gs=gs)
    return pl.pallas_call(
        kern,
        grid=grid,
        in_specs=[
            const_spec((nseq, D_MODEL)),
            layer_spec((1, D_MODEL)),
            layer_spec((D_MODEL, D_IN_PAD)),
            layer_spec((D_MIX, D_MODEL)),
            const_spec((H_A, WINDOW)),
            layer_spec((H_A, 1)),
            layer_spec((1, D_B)),
            layer_spec((1, D_B)),
            layer_spec((1, D_B)),
            layer_spec((1, D_B)),
            layer_spec((LANES, DK_TOT)),
            layer_spec((1, DK_TOT)),
            layer_spec((1, D_C)),
            const_spec((1, D_MODEL)),
            state_spec, state_spec, gla_spec,
        ],
        out_specs=[
            const_spec((nseq, D_MODEL)),
            state_spec, state_spec, gla_spec,
            layer_spec((nseq, D_B)),
        ],
        out_shape=[
            jax.ShapeDtypeStruct((nseq, D_MODEL), jnp.float32),
            jax.ShapeDtypeStruct((depth, nseq, LANES, WINDOW), jnp.float32),
            jax.ShapeDtypeStruct((depth, nseq, LANES, WINDOW), jnp.float32),
            jax.ShapeDtypeStruct((depth, nseq * DK_TOT, DV_C), jnp.float32),
            jax.ShapeDtypeStruct((depth, nseq, D_B), jnp.float32),
        ],
        scratch_shapes=[
            pltpu.VMEM((nseq, D_MODEL), jnp.float32),
            pltpu.VMEM((nseq, D_IN_PAD), jnp.float32),
            pltpu.VMEM((nseq, D_MIX), jnp.float32),
            pltpu.VMEM((nseq * H_A, LANES), jnp.float32),
            pltpu.VMEM((nseq * H_A, LANES), jnp.float32),
            pltpu.VMEM((nseq, DK_TOT), jnp.float32),
            pltpu.VMEM((nseq, DK_TOT), jnp.float32),
            pltpu.VMEM((nseq, D_C), jnp.float32),
        ],
        compiler_params=pltpu.CompilerParams(
            dimension_semantics=("arbitrary", "arbitrary"),
            vmem_limit_bytes=VMEM_LIMIT_BYTES),
        name="sample_path",
    )(x, ng, win, wout, bias_s, sink_c, w00, b0, lng, lnb, wup, bup, gn, fg, buf_kt, buf_vt, s0)


def _window_minor(state):
    depth, n = state.shape[:2]
    return jnp.transpose(state, (0, 1, 3, 4, 2)).reshape(depth, n, LANES, WINDOW)


def _window_major(state_t):
    depth, n = state_t.shape[:2]
    return jnp.transpose(state_t.reshape(depth, n, KV_A, HD_A, WINDOW), (0, 1, 4, 2, 3))


def kernel(x_prompt, x_sample, state_swa_k, state_swa_v, state_gla, rel_bias, norm_g, w_in, sinks, spatial_w,
           spatial_b, chunk_ln_g, chunk_ln_b, gla_w_up, gla_b_up, gla_norm_g, w_out, final_norm_g):
    depth = w_in.shape[0]
    nseq = x_sample.shape[0]
    bsz = x_prompt.shape[0]
    f32 = jnp.float32

    win, wout = _prepare_weights(w_in, w_out)
    wup = jnp.pad(gla_w_up, ((0, 0), (0, LANES - GLA_RANK), (0, 0))).astype(jnp.bfloat16)
    bsf = jnp.repeat(jnp.swapaxes(spatial_b, 1, 2), DH_B, axis=2).astype(f32)
    w00 = jnp.repeat(spatial_w[:, :, 0, 0], DH_B, axis=1).astype(f32)[:, None, :]
    b0 = jnp.repeat(spatial_b[:, :, 0], DH_B, axis=1).astype(f32)[:, None, :]
    sink_c = sinks.astype(f32)[:, :, None]
    row = lambda a: a.astype(f32)[:, None, :]
    ng, lng, lnb, bup, gn = row(norm_g), row(chunk_ln_g), row(chunk_ln_b), row(gla_b_up), row(gla_norm_g)
    fg = final_norm_g.astype(f32)[None, :]
    sw = spatial_w.astype(f32)
    sinks = sinks.astype(f32)

    bias_tab = _bias_table(rel_bias)
    bias_s = bias_tab[0].reshape(H_A, WINDOW, 2 * WINDOW)[:, WINDOW - 1, WINDOW:]

    xp = x_prompt
    kp_l, vp_l, sp_l = [], [], []
    for l in range(depth):
        xp, kp, vp, sp = _prompt_layer(xp, ng, win, wout, bias_tab, sinks, sw, bsf, lng, lnb, wup, bup, gn, fg,
                                       layer=l, final=l == depth - 1)
        kp_l.append(kp); vp_l.append(vp); sp_l.append(sp)

    xs, ks_t, vs_t, ss, cv = _sample_path(
        x_sample.reshape(nseq, D_MODEL), ng, win, wout, bias_s, sink_c, w00, b0, lng, lnb, wup, bup, gn, fg,
        _window_minor(state_swa_k), _window_minor(state_swa_v), state_gla.reshape(depth, nseq * DK_TOT, DV_C))

    return (xp,
            xs.reshape(nseq, 1, D_MODEL),
            _window_major(jnp.stack(kp_l)),
            _window_major(jnp.stack(vp_l)),
            jnp.stack(sp_l).reshape(depth, bsz, H_C, DK_C, DV_C),
            _window_major(ks_t),
            _window_major(vs_t),
            ss.reshape(depth, nseq, H_C, DK_C, DV_C),
            cv.reshape(depth, nseq, 1, D_B))
```

```python
import functools

import numpy as np
import jax
import jax.numpy as jnp
from jax import lax
from jax.experimental import pallas as pl
from jax.experimental.pallas import tpu as pltpu

D_MODEL = 1024
D_A, HD_A, H_A, KV_A, G_A = 512, 64, 8, 2, 4
WINDOW, N_BUCKETS, MAX_DIST = 128, 32, 128
D_B, H_B, DH_B, CHUNK_B = 512, 8, 64, 128
D_C, H_C, DK_TOT, DK_C, DV_C = 512, 4, 256, 64, 128
GLA_RANK, GLA_TAU, GLA_CHUNK = 16, 16.0, 64
D_MIX = D_A + D_B + D_C
EPS = 1e-6
NEG = -1e30
SPLITS = [D_A, KV_A * HD_A, KV_A * HD_A, D_A, D_B, D_B, D_B, DK_TOT, DK_TOT, D_C, D_C, GLA_RANK]
D_IN = sum(SPLITS)

LANES = 128
HALF = LANES // 2
D_IN_PAD = ((D_IN + LANES - 1) // LANES) * LANES
VMEM_LIMIT_BYTES = 56 * 1024 * 1024

O_QA, O_KA, O_VA, O_GA = 0, 512, 640, 768
O_UB, O_VB, O_GB = 1280, 1792, 2304
O_QC, O_KC, O_VC, O_GC, O_LR = 2816, 3072, 3328, 3840, 4352
Y_A, Y_B, Y_C = 0, D_A, D_A + D_B

HEAD_PERM = [0, 4, 1, 5, 2, 6, 3, 7]

TOKEN_BLOCK = 512
SEQ_GROUP = 16
IN_PROJ_CHUNK = 256
IN_PROJ_ROWS = 256
OUT_PROJ_CHUNK = 256
MIXER_SECTIONS = 19
PRE_SECTIONS = 5
PREP_SPLIT = 2

_NT = (((1,), (1,)), ((), ()))


def _t5_bucket(dist):
    n = np.maximum(dist, 0)
    max_exact = N_BUCKETS // 2
    large = max_exact + (np.log(np.maximum(n, 1) / max_exact) / np.log(MAX_DIST / max_exact)
                         * (N_BUCKETS - max_exact)).astype(np.int32)
    large = np.minimum(large, N_BUCKETS - 1)
    return np.where(n < max_exact, n, large).astype(np.int32)


def _silu(x):
    return x * (1.0 / (1.0 + jnp.exp(-x)))


def _log_sigmoid(x):
    return jnp.minimum(x, 0.0) - jnp.log1p(jnp.exp(-jnp.abs(x)))


def _bdot(a, b):
    return jnp.dot(a.astype(jnp.bfloat16), b.astype(jnp.bfloat16), preferred_element_type=jnp.float32)


def _bdot_nt(a, b):
    return lax.dot_general(a.astype(jnp.bfloat16), b.astype(jnp.bfloat16), _NT,
                           preferred_element_type=jnp.float32)


def _rmsnorm(x, g):
    return x * lax.rsqrt(jnp.mean(x * x, axis=-1, keepdims=True) + EPS) * g


def _layernorm(v, g, b):
    mu = jnp.mean(v, axis=-1, keepdims=True)
    xc = v - mu
    var = jnp.mean(xc * xc, axis=-1, keepdims=True)
    return xc * lax.rsqrt(var + EPS) * g + b


def _softmax_sink(s, sink):
    m = jnp.maximum(jnp.max(s, axis=-1, keepdims=True), sink)
    p = jnp.exp(s - m)
    den = jnp.sum(p, axis=-1, keepdims=True) + jnp.exp(sink - m)
    return p, 1.0 / den


def _bias_table_kernel(rb_ref, bucket_ref, band_ref, out_ref):
    bucket = bucket_ref[...]
    band = band_ref[...] > 0
    own = lax.broadcasted_iota(jnp.int32, bucket.shape, 1) >= WINDOW
    for h in range(H_A):
        acc = jnp.zeros(bucket.shape, jnp.float32)
        for b in range(N_BUCKETS):
            acc = jnp.where(bucket == b, rb_ref[b, h], acc)
        kv, g = divmod(h, G_A)
        rows = pl.ds(g * WINDOW, WINDOW)
        out_ref[0, kv, rows, :] = jnp.where(band, acc, NEG)
        out_ref[1, kv, rows, :] = jnp.where(band & own, acc, NEG)


def _bias_table(rel_bias):
    i = np.arange(WINDOW)[:, None]
    j = np.arange(2 * WINDOW)[None, :]
    dist = i + WINDOW - j
    band = ((dist >= 0) & (dist < WINDOW)).astype(np.int32)
    return pl.pallas_call(
        _bias_table_kernel,
        out_shape=jax.ShapeDtypeStruct((2, KV_A, G_A * WINDOW, 2 * WINDOW), jnp.float32),
        in_specs=[pl.BlockSpec(memory_space=pltpu.SMEM),
                  pl.BlockSpec(memory_space=pltpu.VMEM),
                  pl.BlockSpec(memory_space=pltpu.VMEM)],
        out_specs=pl.BlockSpec(memory_space=pltpu.VMEM),
        name="rel_bias_table",
    )(rel_bias.astype(jnp.float32), jnp.asarray(_t5_bucket(dist)), jnp.asarray(band))


def _win_prep_kernel(wt_ref, lr_ref, o_ref):
    c = pl.program_id(1)
    tiles = wt_ref.shape[0] // LANES
    for cc in range(PREP_SPLIT):
        @pl.when(c == cc)
        def _(cc=cc):
            for t in range(tiles):
                col = (cc * tiles + t) * LANES
                base = next((b for b in (O_QA, O_GA) if b <= col < b + D_A), None)
                if base is None:
                    src = wt_ref[t * LANES:(t + 1) * LANES, :]
                else:
                    g = (col - base) // LANES
                    r0 = base - cc * tiles * LANES
                    assert r0 >= 0 and r0 + D_A <= wt_ref.shape[0]
                    src = jnp.concatenate([wt_ref[r0 + g * HD_A:r0 + (g + 1) * HD_A, :],
                                           wt_ref[r0 + (G_A + g) * HD_A:r0 + (G_A + g + 1) * HD_A, :]], axis=0)
                o_ref[:, col:col + LANES] = src.T.astype(jnp.bfloat16)
            if cc == PREP_SPLIT - 1:
                o_ref[:, O_LR:D_IN_PAD] = lr_ref[...].T.astype(jnp.bfloat16)


def _wout_prep_kernel(w_ref, o_ref):
    for j, h in enumerate(HEAD_PERM):
        o_ref[j * HD_A:(j + 1) * HD_A, :] = w_ref[h * HD_A:(h + 1) * HD_A, :].astype(jnp.bfloat16)
    o_ref[Y_B:, :] = w_ref[Y_B:, :].astype(jnp.bfloat16)


def _prepare_weights(w_in, w_out):
    depth = w_in.shape[0]
    cols = D_MODEL // PREP_SPLIT
    wt = jnp.swapaxes(w_in, 1, 2)
    assert O_LR % (PREP_SPLIT * LANES) == 0
    rows = O_LR // PREP_SPLIT
    wt_lr = jnp.pad(wt[:, O_LR:, :], ((0, 0), (0, D_IN_PAD - D_IN), (0, 0)))
    win = pl.pallas_call(
        _win_prep_kernel,
        grid=(depth, PREP_SPLIT),
        in_specs=[pl.BlockSpec((None, rows, D_MODEL), lambda l, r: (l, r, 0)),
                  pl.BlockSpec((None, LANES, D_MODEL), lambda l, r: (l, 0, 0))],
        out_specs=pl.BlockSpec((None, D_MODEL, D_IN_PAD), lambda l, r: (l, 0, 0)),
        out_shape=jax.ShapeDtypeStruct((depth, D_MODEL, D_IN_PAD), jnp.bfloat16),
        compiler_params=pltpu.CompilerParams(dimension_semantics=("arbitrary", "arbitrary"),
                                             vmem_limit_bytes=VMEM_LIMIT_BYTES),
        name="w_in_prep",
    )(wt, wt_lr)
    wout = pl.pallas_call(
        _wout_prep_kernel,
        grid=(depth, PREP_SPLIT),
        in_specs=[pl.BlockSpec((None, D_MIX, cols), lambda l, c: (l, 0, c))],
        out_specs=pl.BlockSpec((None, D_MIX, cols), lambda l, c: (l, 0, c)),
        out_shape=jax.ShapeDtypeStruct((depth, D_MIX, D_MODEL), jnp.bfloat16),
        compiler_params=pltpu.CompilerParams(dimension_semantics=("arbitrary", "arbitrary"),
                                             vmem_limit_bytes=VMEM_LIMIT_BYTES),
        name="w_out_prep",
    )(w_out)
    return win, wout


def _prompt_layer_kernel(x_ref, xn_ref, ng_ref, win_ref, wout_ref, bias_ref, sink_ref, sw_ref, bsf_ref, lng_ref,
                         lnb_ref, wup_ref, bup_ref, gn_ref, fg_ref,
                         xo_ref, ko_ref, vo_ref, so_ref,
                         pa_ref, pb_ref, hb_ref, hn_ref, ya_ref, yb_ref, kprev_ref, vprev_ref, st_ref, wm_ref,
                         *, tb, layer, final):
    b_id = pl.program_id(0)
    i_id = pl.program_id(1)
    n_i = pl.num_programs(1)
    half = tb // 2
    n_sub = half // WINDOW

    @pl.when((b_id == 0) & (i_id == 0))
    def _():
        r = lax.broadcasted_iota(jnp.int32, (CHUNK_B, CHUNK_B), 0)
        c = lax.broadcasted_iota(jnp.int32, (CHUNK_B, CHUNK_B), 1)
        for h in range(H_B):
            wm_ref[h // 2, :, (h % 2) * CHUNK_B:(h % 2 + 1) * CHUNK_B] = jnp.where(
                c <= r, sw_ref[h], 0.0).astype(jnp.bfloat16)
        pa_ref[...] = _bdot(_rmsnorm(x_ref[0:half, :], ng_ref[...]), win_ref[...])

    @pl.when(i_id == 0)
    def _():
        kprev_ref[...] = jnp.zeros_like(kprev_ref)
        vprev_ref[...] = jnp.zeros_like(vprev_ref)
        st_ref[...] = jnp.zeros_like(st_ref)

    hb_ref[...] = _rmsnorm(x_ref[half:tb, :], ng_ref[...]).astype(jnp.bfloat16)
    hn_ref[...] = _rmsnorm(xn_ref[...], ng_ref[...]).astype(jnp.bfloat16)

    def in_proj_chunks(h_ref, p_ref):
        def chunk(r0, c0, c1):
            def run():
                p_ref[r0:r0 + IN_PROJ_ROWS, c0:c1] = jnp.dot(h_ref[r0:r0 + IN_PROJ_ROWS, :], win_ref[:, c0:c1],
                                                             preferred_element_type=jnp.float32)
            return run
        edges = list(range(0, D_IN_PAD, IN_PROJ_CHUNK)) + [D_IN_PAD]
        return [chunk(r0, c0, c1) for c0, c1 in zip(edges[:-1], edges[1:])
                for r0 in range(0, half, IN_PROJ_ROWS)]

    def out_proj_chunks(y_ref, r0, s):
        def chunk(c0, c1):
            def run():
                rows = slice(r0 + s * WINDOW, r0 + (s + 1) * WINDOW)
                xo_ref[rows, c0:c1] = x_ref[rows, c0:c1] + jnp.dot(
                    y_ref[s * WINDOW:(s + 1) * WINDOW, :], wout_ref[:, c0:c1],
                    preferred_element_type=jnp.float32)
            return run
        return [chunk(c0, c0 + OUT_PROJ_CHUNK) for c0 in range(0, D_MODEL, OUT_PROJ_CHUNK)]

    def final_norm(r0):
        if final:
            xo_ref[r0:r0 + half, :] = _rmsnorm(xo_ref[r0:r0 + half, :], fg_ref[...])

    lane = lax.broadcasted_iota(jnp.int32, (WINDOW, LANES), 1)
    low = lane < HALF
    low64 = lax.broadcasted_iota(jnp.int32, (GLA_CHUNK, LANES), 1) < HALF
    chunk_shift = GLA_CHUNK.bit_length() - 1
    rh = lax.broadcasted_iota(jnp.int32, (half, half), 0)
    ch = lax.broadcasted_iota(jnp.int32, (half, half), 1)
    cum_half = jnp.where((jnp.right_shift(rh, chunk_shift) == jnp.right_shift(ch, chunk_shift)) & (ch <= rh),
                         1.0, 0.0).astype(jnp.bfloat16)
    ri2 = lax.broadcasted_iota(jnp.int32, (LANES, 2 * LANES), 0)
    ci2 = lax.broadcasted_iota(jnp.int32, (LANES, 2 * LANES), 1) & (LANES - 1)
    bd_tril2 = ((ri2 >= GLA_CHUNK) == (ci2 >= GLA_CHUNK)) & (ci2 <= ri2)

    def mixers(p_ref, y_ref, s, first, pre):
        r0 = s * WINDOW
        rows = pl.ds(r0, WINDOW)

        ks = p_ref[rows, O_KA:O_KA + LANES] * (HD_A ** -0.5)
        k_new = [jnp.where(low, ks, 0.0).astype(jnp.bfloat16), jnp.where(low, 0.0, ks).astype(jnp.bfloat16)]
        v_new = p_ref[rows, O_VA:O_VA + LANES].T.astype(jnp.bfloat16)
        vcat = jnp.concatenate([carry["v"], v_new], axis=1)
        q4 = jnp.concatenate([p_ref[rows, O_QA + g * LANES:O_QA + (g + 1) * LANES].astype(jnp.bfloat16)
                              for g in range(G_A)], axis=0)
        o_heads = []
        for kv in range(KV_A):
            kcat = jnp.concatenate([carry["k"][kv], k_new[kv]], axis=0)
            sc = lax.dot_general(q4, kcat, _NT, preferred_element_type=jnp.float32) + bias_ref[first, kv]
            carry["k"][kv] = k_new[kv]
            yield
            ps, invs = [], []
            for g in range(G_A):
                p, inv = _softmax_sink(sc[g * WINDOW:(g + 1) * WINDOW], sink_ref[layer, kv * G_A + g])
                ps.append(p.astype(jnp.bfloat16))
                invs.append(inv)
                if g % 2 == 1:
                    yield
            o_t = lax.dot_general(vcat[kv * HD_A:(kv + 1) * HD_A, :], jnp.concatenate(ps, axis=0), _NT,
                                  preferred_element_type=jnp.float32)
            o_heads.append((o_t, invs))
            yield
        carry["v"] = v_new
        for g in range(G_A):
            o_pair = jnp.concatenate([o_heads[kv][0][:, g * WINDOW:(g + 1) * WINDOW] for kv in range(KV_A)],
                                     axis=0).T
            ya = o_pair * jnp.where(low, o_heads[0][1][g], o_heads[1][1][g])
            gate = p_ref[rows, O_GA + g * LANES:O_GA + (g + 1) * LANES]
            y_ref[rows, Y_A + g * LANES:Y_A + (g + 1) * LANES] = (ya * _silu(gate)).astype(jnp.bfloat16)
        yield

        for j in range(H_B // 2):
            mix = pre["mix"][j][:, s * LANES:(s + 1) * LANES] + bsf_ref[:, j * LANES:(j + 1) * LANES]
            u = p_ref[rows, O_UB + j * LANES:O_UB + (j + 1) * LANES]
            gate = p_ref[rows, O_GB + j * LANES:O_GB + (j + 1) * LANES]
            y_ref[rows, Y_B + j * LANES:Y_B + (j + 1) * LANES] = (u * mix * _silu(gate)).astype(jnp.bfloat16)
            if j % 2 == 1:
                yield

        bc = pre["bc"][r0:r0 + WINDOW]
        for c in range(WINDOW // GLA_CHUNK):
            rc = pl.ds(r0 + c * GLA_CHUNK, GLA_CHUNK)
            q_sts, k_sts, v_sts, kd_sts, decays, sts = [], [], [], [], [], []
            for p in range(H_C // 2):
                bp = bc[c * GLA_CHUNK:(c + 1) * GLA_CHUNK, p * LANES:(p + 1) * LANES]
                qp = p_ref[rc, O_QC + p * LANES:O_QC + (p + 1) * LANES]
                kp = p_ref[rc, O_KC + p * LANES:O_KC + (p + 1) * LANES]
                b_last = bp[GLA_CHUNK - 1:GLA_CHUNK, :]
                qd = qp * (DK_C ** -0.5) * jnp.exp(bp)
                ki = (kp * jnp.exp(-bp)).astype(jnp.bfloat16)
                kd = kp * jnp.exp(b_last - bp)
                q_sts.append(jnp.concatenate([jnp.where(low64, qd, 0.0), jnp.where(low64, 0.0, qd)],
                                             axis=0).astype(jnp.bfloat16))
                k_sts.append(jnp.concatenate([ki, ki], axis=0))
                v0 = p_ref[rc, O_VC + (2 * p) * DV_C:O_VC + (2 * p + 1) * DV_C]
                v1 = p_ref[rc, O_VC + (2 * p + 1) * DV_C:O_VC + (2 * p + 2) * DV_C]
                v_sts.append(jnp.concatenate([v0, v1], axis=0))
                kd_sts.append(jnp.concatenate([jnp.where(low64, kd, 0.0), jnp.where(low64, 0.0, kd)],
                                              axis=0).astype(jnp.bfloat16))
                decays.append(jnp.exp(b_last))
                yield
            zero = jnp.zeros((LANES, LANES), jnp.bfloat16)
            blockdiag = lambda a, b: jnp.concatenate([jnp.concatenate([a, zero], axis=1),
                                                      jnp.concatenate([zero, b], axis=1)], axis=0)
            q_all = jnp.concatenate(q_sts, axis=1)
            att = lax.dot_general(q_all, blockdiag(*k_sts), _NT, preferred_element_type=jnp.float32)
            att = jnp.where(bd_tril2, att, 0.0).astype(jnp.bfloat16)
            v_b = [v.astype(jnp.bfloat16) for v in v_sts]
            o_all = jnp.dot(att, blockdiag(*v_b), preferred_element_type=jnp.float32)
            d_all = jnp.dot(jnp.concatenate([v.T.astype(jnp.bfloat16) for v in v_sts], axis=1),
                            blockdiag(*kd_sts), preferred_element_type=jnp.float32)
            sts = carry["st"]
            o_inter = lax.dot_general(q_all, blockdiag(*[st.astype(jnp.bfloat16) for st in sts]), _NT,
                                      preferred_element_type=jnp.float32)
            o_all = o_all + o_inter
            carry["st"] = [sts[p] * decays[p] + d_all[:, p * LANES:(p + 1) * LANES] for p in range(H_C // 2)]
            for p in range(H_C // 2):
                o = o_all[:, p * DV_C:(p + 1) * DV_C]
                for hh in range(2):
                    hd = 2 * p + hh
                    oh = o[hh * GLA_CHUNK:(hh + 1) * GLA_CHUNK]
                    on = _rmsnorm(oh, gn_ref[:, hd * DV_C:(hd + 1) * DV_C])
                    gate = p_ref[rc, O_GC + hd * DV_C:O_GC + (hd + 1) * DV_C]
                    y_ref[rc, Y_C + hd * DV_C:Y_C + (hd + 1) * DV_C] = (on * _silu(gate)).astype(jnp.bfloat16)
                yield

    def half_pre(p_ref, pre):
        vn = _layernorm(p_ref[:, O_VB:O_VB + D_B], lng_ref[...], lnb_ref[...]).astype(jnp.bfloat16)
        yield
        pre["mix"] = []
        for j in range(H_B // 2):
            cols = []
            for s in range(n_sub):
                vp = vn[s * WINDOW:(s + 1) * WINDOW, j * LANES:(j + 1) * LANES]
                zero = jnp.zeros_like(vp)
                cols.append(jnp.concatenate([jnp.where(low, vp, zero), jnp.where(low, zero, vp)], axis=0))
            pre["mix"].append(jnp.dot(wm_ref[j], jnp.concatenate(cols, axis=1),
                                      preferred_element_type=jnp.float32))
            if j % 2 == 1:
                yield
        z = _bdot(p_ref[:, O_LR:O_LR + LANES], wup_ref[...]) + bup_ref[...]
        la = _log_sigmoid(z) * (1.0 / GLA_TAU)
        yield
        hi = la.astype(jnp.bfloat16)
        r1 = la - hi.astype(jnp.float32)
        mid = r1.astype(jnp.bfloat16)
        lo = (r1 - mid.astype(jnp.float32)).astype(jnp.bfloat16)
        c3 = jnp.dot(cum_half, jnp.concatenate([hi, mid, lo], axis=1), preferred_element_type=jnp.float32)
        pre["bc"] = (c3[:, 0:DK_TOT] + c3[:, DK_TOT:2 * DK_TOT]) + c3[:, 2 * DK_TOT:3 * DK_TOT]
        yield

    def half_mixers(p_ref, y_ref, first):
        pre = {}
        yield from half_pre(p_ref, pre)
        for s in range(n_sub):
            yield from mixers(p_ref, y_ref, s, first if s == 0 else 0, pre)

    def run_phase(sections, chunks, late_chunks):
        n_sections = PRE_SECTIONS + n_sub * MIXER_SECTIONS
        plan = [[] for _ in range(n_sections)]
        for idx, run in enumerate(chunks):
            plan[idx * n_sections // len(chunks)].append(run)
        for s, runs in late_chunks.items():
            k0 = PRE_SECTIONS + (s + 1) * MIXER_SECTIONS
            for idx, run in enumerate(runs):
                plan[k0 + idx * (n_sections - k0) // len(runs)].append(run)
        for k in range(n_sections):
            next(sections)
            for run in plan[k]:
                run()
        assert next(sections, None) is None

    carry = {"k": [kprev_ref[kv] for kv in range(KV_A)], "v": vprev_ref[...],
             "st": [st_ref[p] for p in range(H_C // 2)]}
    run_phase(half_mixers(pa_ref, ya_ref, jnp.where(i_id == 0, 1, 0)),
              in_proj_chunks(hb_ref, pb_ref),
              {s: out_proj_chunks(ya_ref, 0, s) for s in range(n_sub - 1)})
    run_phase(half_mixers(pb_ref, yb_ref, 0),
              out_proj_chunks(ya_ref, 0, n_sub - 1) + in_proj_chunks(hn_ref, pa_ref),
              {s: out_proj_chunks(yb_ref, half, s) for s in range(n_sub - 1)})
    final_norm(0)
    for run in out_proj_chunks(yb_ref, half, n_sub - 1):
        run()
    final_norm(half)
    for kv in range(KV_A):
        kprev_ref[kv] = carry["k"][kv]
    vprev_ref[...] = carry["v"]
    for p in range(H_C // 2):
        st_ref[p] = carry["st"][p]

    @pl.when(i_id == n_i - 1)
    def _():
        ko_ref[...] = pb_ref[half - WINDOW:half, O_KA:O_KA + LANES].T
        vo_ref[...] = pb_ref[half - WINDOW:half, O_VA:O_VA + LANES].T
        for p in range(H_C // 2):
            so_ref[p] = st_ref[p].T


def _prompt_layer(x, ng, win, wout, bias_tab, sinks, sw, bsf, lng, lnb, wup, bup, gn, fg, *, layer, final):
    bsz, seq, _ = x.shape
    tb = min(TOKEN_BLOCK, seq)
    half = tb // 2
    assert seq % tb == 0 and half % WINDOW == 0
    n_i = seq // tb
    grid = (bsz, n_i)

    def next_half(b, i):
        lin = jnp.minimum(b * n_i + i + 1, bsz * n_i - 1)
        return lin // n_i, (lin % n_i) * 2, 0

    def const_spec(shape):
        nd = len(shape)
        return pl.BlockSpec(shape, lambda b, i: (0,) * nd, pipeline_mode=pl.Buffered(1))

    def layer_spec(shape):
        nd = len(shape)
        return pl.BlockSpec((None,) + shape, lambda b, i: (layer,) + (0,) * nd, pipeline_mode=pl.Buffered(1))

    kern = functools.partial(_prompt_layer_kernel, tb=tb, layer=layer, final=final)
    return pl.pallas_call(
        kern,
        grid=grid,
        in_specs=[
            pl.BlockSpec((None, tb, D_MODEL), lambda b, i: (b, i, 0)),
            pl.BlockSpec((None, half, D_MODEL), next_half),
            layer_spec((1, D_MODEL)),
            layer_spec((D_MODEL, D_IN_PAD)),
            layer_spec((D_MIX, D_MODEL)),
            const_spec((2, KV_A, G_A * WINDOW, 2 * WINDOW)),
            pl.BlockSpec(memory_space=pltpu.SMEM),
            layer_spec((H_B, CHUNK_B, CHUNK_B)),
            layer_spec((CHUNK_B, D_B)),
            layer_spec((1, D_B)),
            layer_spec((1, D_B)),
            layer_spec((LANES, DK_TOT)),
            layer_spec((1, DK_TOT)),
            layer_spec((1, D_C)),
            const_spec((1, D_MODEL)),
        ],
        out_specs=[
            pl.BlockSpec((None, tb, D_MODEL), lambda b, i: (b, i, 0)),
            pl.BlockSpec((None, LANES, WINDOW), lambda b, i: (b, 0, 0)),
            pl.BlockSpec((None, LANES, WINDOW), lambda b, i: (b, 0, 0)),
            pl.BlockSpec((None, H_C // 2, LANES, DV_C), lambda b, i: (b, 0, 0, 0)),
        ],
        out_shape=[
            jax.ShapeDtypeStruct((bsz, seq, D_MODEL), jnp.float32),
            jax.ShapeDtypeStruct((bsz, LANES, WINDOW), jnp.float32),
            jax.ShapeDtypeStruct((bsz, LANES, WINDOW), jnp.float32),
            jax.ShapeDtypeStruct((bsz, H_C // 2, LANES, DV_C), jnp.float32),
        ],
        scratch_shapes=[
            pltpu.VMEM((half, D_IN_PAD), jnp.float32),
            pltpu.VMEM((half, D_IN_PAD), jnp.float32),
            pltpu.VMEM((half, D_MODEL), jnp.bfloat16),
            pltpu.VMEM((half, D_MODEL), jnp.bfloat16),
            pltpu.VMEM((half, D_MIX), jnp.bfloat16),
            pltpu.VMEM((half, D_MIX), jnp.bfloat16),
            pltpu.VMEM((KV_A, WINDOW, LANES), jnp.bfloat16),
            pltpu.VMEM((WINDOW, LANES), jnp.bfloat16),
            pltpu.VMEM((H_C // 2, DV_C, LANES), jnp.float32),
            pltpu.VMEM((H_B // 2, CHUNK_B, 2 * CHUNK_B), jnp.bfloat16),
        ],
        compiler_params=pltpu.CompilerParams(
            dimension_semantics=("arbitrary", "arbitrary"),
            vmem_limit_bytes=VMEM_LIMIT_BYTES),
        name="prompt_layer",
    )(x, x, ng, win, wout, bias_tab, sinks, sw, bsf, lng, lnb, wup, bup, gn, fg)


def _sample_kernel(x_ref, ng_ref, win_ref, wout_ref, biass_ref, sinkc_ref, w00_ref, b0_ref, lng_ref,
                   lnb_ref, wup_ref, bup_ref, gn_ref, fg_ref, bk_ref, bv_ref, s0_ref,
                   xo_ref, nk_ref, nv_ref, so_ref, cv_ref,
                   xcur_ref, proj_ref, y_ref, qs_ref, os_ref, qd_ref, ea_ref, oc_ref, *, nseq, gs):
    l_id = pl.program_id(0)
    i_id = pl.program_id(1)
    n_l = pl.num_programs(0)
    n_i = pl.num_programs(1)
    lane = lax.broadcasted_iota(jnp.int32, (nseq, LANES), 1)
    low = lane < HALF

    @pl.when((l_id == 0) & (i_id == 0))
    def _():
        xcur_ref[...] = x_ref[...]

    @pl.when(i_id == 0)
    def _():
        h = _rmsnorm(xcur_ref[...], ng_ref[...])
        proj_ref[...] = _bdot(h, win_ref[...])
        for g in range(G_A):
            qp = proj_ref[:, O_QA + g * LANES:O_QA + (g + 1) * LANES] * (HD_A ** -0.5)
            for kv in range(KV_A):
                keep = low if kv == 0 else jnp.logical_not(low)
                qs_ref[pl.ds(kv * G_A + g, nseq, stride=H_A), :] = jnp.where(keep, qp, 0.0)
        vn = _layernorm(proj_ref[:, O_VB:O_VB + D_B], lng_ref[...], lnb_ref[...])
        cv_ref[...] = vn
        mix = w00_ref[...] * vn + b0_ref[...]
        y_ref[:, Y_B:Y_B + D_B] = (proj_ref[:, O_UB:O_UB + D_B] * mix) * _silu(proj_ref[:, O_GB:O_GB + D_B])
        z = _bdot(proj_ref[:, O_LR:O_LR + LANES], wup_ref[...]) + bup_ref[...]
        la = _log_sigmoid(z) * (1.0 / GLA_TAU)
        ea = jnp.exp(la)
        qd = proj_ref[:, O_QC:O_QC + DK_TOT] * (DK_C ** -0.5) * ea
        ki = proj_ref[:, O_KC:O_KC + DK_TOT] * jnp.exp(-la)
        qd_ref[...] = qd
        ea_ref[...] = ea
        prod = qd * ki
        lane_c = lax.broadcasted_iota(jnp.int32, prod.shape, 1)
        for hd in range(H_C):
            in_head = (lane_c >= hd * DK_C) & (lane_c < (hd + 1) * DK_C)
            att = jnp.sum(jnp.where(in_head, prod, 0.0), axis=-1, keepdims=True)
            oc_ref[:, hd * DV_C:(hd + 1) * DV_C] = att * proj_ref[:, O_VC + hd * DV_C:O_VC + (hd + 1) * DV_C]

    g0 = pl.multiple_of(i_id * gs, gs)
    grows = pl.ds(g0, gs)

    pad_rows = jnp.zeros((LANES - gs, LANES), jnp.float32)
    k_cols = jnp.concatenate([proj_ref[grows, O_KA:O_KA + LANES], pad_rows], axis=0).T
    v_cols = jnp.concatenate([proj_ref[grows, O_VA:O_VA + LANES], pad_rows], axis=0).T
    newest = lax.broadcasted_iota(jnp.int32, (LANES, WINDOW), 1) == WINDOW - 1
    for j in range(gs):
        kt = jnp.where(newest, k_cols[:, j:j + 1], pltpu.roll(bk_ref[j], WINDOW - 1, 1))
        vt = jnp.where(newest, v_cols[:, j:j + 1], pltpu.roll(bv_ref[j], WINDOW - 1, 1))
        nk_ref[j] = kt
        nv_ref[j] = vt
        r8 = pl.ds(pl.multiple_of((g0 + j) * H_A, H_A), H_A)
        sc = _bdot(qs_ref[r8, :], kt) + biass_ref[...]
        p, inv = _softmax_sink(sc, sinkc_ref[...])
        os_ref[r8, :] = _bdot_nt(p, vt) * inv

    pad_c = jnp.zeros((LANES - gs, DK_TOT), jnp.float32)
    ea_cols = jnp.concatenate([ea_ref[grows, :], pad_c], axis=0).T
    k_cols_c = jnp.concatenate([proj_ref[grows, O_KC:O_KC + DK_TOT], pad_c], axis=0).T
    qd_g = qd_ref[grows, :]
    row_g = lax.broadcasted_iota(jnp.int32, (gs, LANES), 0)
    low_g = lax.broadcasted_iota(jnp.int32, (gs, LANES), 1) < HALF
    seqs_per_dot = 2 * LANES // DK_C
    for hd in range(H_C):
        v_h = proj_ref[grows, O_VC + hd * DV_C:O_VC + (hd + 1) * DV_C]
        q_pair = qd_g[:, (hd // 2) * LANES:(hd // 2 + 1) * LANES]
        q_other = pltpu.roll(q_pair, HALF, 1)
        q_lo, q_hi = (q_pair, q_other) if hd % 2 == 0 else (q_other, q_pair)
        acc = jnp.zeros((gs, DV_C), jnp.float32)
        for j0 in range(0, gs, seqs_per_dot):
            tiles, lhs = [], []
            for j in range(j0, j0 + seqs_per_dot):
                srows = pl.ds(j * DK_TOT + hd * DK_C, DK_C)
                s_old = s0_ref[srows, :]
                tiles.append(s_old.astype(jnp.bfloat16))
                crows = slice(hd * DK_C, (hd + 1) * DK_C)
                so_ref[srows, :] = ea_cols[crows, j:j + 1] * s_old + k_cols_c[crows, j:j + 1] * v_h[j:j + 1, :]
            for j in range(j0, j0 + seqs_per_dot, 2):
                lhs.append(jnp.where((row_g == j) & low_g, q_lo,
                                     jnp.where((row_g == j + 1) & jnp.logical_not(low_g), q_hi, 0.0)))
            acc = acc + _bdot(jnp.concatenate(lhs, axis=1), jnp.concatenate(tiles, axis=0))
        oc_ref[grows, hd * DV_C:(hd + 1) * DV_C] += acc

    @pl.when(i_id == n_i - 1)
    def _():
        for g in range(G_A):
            o0 = os_ref[pl.ds(g, nseq, stride=H_A), :]
            o1 = os_ref[pl.ds(G_A + g, nseq, stride=H_A), :]
            gate = proj_ref[:, O_GA + g * LANES:O_GA + (g + 1) * LANES]
            y_ref[:, Y_A + g * LANES:Y_A + (g + 1) * LANES] = jnp.where(low, o0, o1) * _silu(gate)
        for hd in range(H_C):
            on = _rmsnorm(oc_ref[:, hd * DV_C:(hd + 1) * DV_C], gn_ref[:, hd * DV_C:(hd + 1) * DV_C])
            gate = proj_ref[:, O_GC + hd * DV_C:O_GC + (hd + 1) * DV_C]
            y_ref[:, Y_C + hd * DV_C:Y_C + (hd + 1) * DV_C] = on * _silu(gate)
        x_new = xcur_ref[...] + _bdot(y_ref[...], wout_ref[...])
        xcur_ref[...] = x_new

        @pl.when(l_id == n_l - 1)
        def _():
            xo_ref[...] = _rmsnorm(x_new, fg_ref[...])


def _sample_path(x, ng, win, wout, bias_s, sink_c, w00, b0, lng, lnb, wup, bup, gn, fg, buf_kt, buf_vt, s0):
    depth = win.shape[0]
    nseq = x.shape[0]
    gs = min(SEQ_GROUP, nseq)
    assert nseq % gs == 0 and gs % 8 == 0
    grid = (depth, nseq // gs)

    def const_spec(shape):
        nd = len(shape)
        return pl.BlockSpec(shape, lambda l, i: (0,) * nd)

    def layer_spec(shape):
        nd = len(shape)
        return pl.BlockSpec((None,) + shape, lambda l, i: (l,) + (0,) * nd)

    state_spec = pl.BlockSpec((None, gs, LANES, WINDOW), lambda l, i: (l, i, 0, 0))
    gla_spec = pl.BlockSpec((None, gs * DK_TOT, DV_C), lambda l, i: (l, i, 0))
    kern = functools.partial(_sample_kernel, nseq=nseq, gs=gs)
    return pl.pallas_call(
        kern,
        grid=grid,
        in_specs=[
            const_spec((nseq, D_MODEL)),
            layer_spec((1, D_MODEL)),
            layer_spec((D_MODEL, D_IN_PAD)),
            layer_spec((D_MIX, D_MODEL)),
            const_spec((H_A, WINDOW)),
            layer_spec((H_A, 1)),
            layer_spec((1, D_B)),
            layer_spec((1, D_B)),
            layer_spec((1, D_B)),
            layer_spec((1, D_B)),
            layer_spec((LANES, DK_TOT)),
            layer_spec((1, DK_TOT)),
            layer_spec((1, D_C)),
            const_spec((1, D_MODEL)),
            state_spec, state_spec, gla_spec,
        ],
        out_specs=[
            const_spec((nseq, D_MODEL)),
            state_spec, state_spec, gla_spec,
            layer_spec((nseq, D_B)),
        ],
        out_shape=[
            jax.ShapeDtypeStruct((nseq, D_MODEL), jnp.float32),
            jax.ShapeDtypeStruct((depth, nseq, LANES, WINDOW), jnp.float32),
            jax.ShapeDtypeStruct((depth, nseq, LANES, WINDOW), jnp.float32),
            jax.ShapeDtypeStruct((depth, nseq * DK_TOT, DV_C), jnp.float32),
            jax.ShapeDtypeStruct((depth, nseq, D_B), jnp.float32),
        ],
        scratch_shapes=[
            pltpu.VMEM((nseq, D_MODEL), jnp.float32),
            pltpu.VMEM((nseq, D_IN_PAD), jnp.float32),
            pltpu.VMEM((nseq, D_MIX), jnp.float32),
            pltpu.VMEM((nseq * H_A, LANES), jnp.float32),
            pltpu.VMEM((nseq * H_A, LANES), jnp.float32),
            pltpu.VMEM((nseq, DK_TOT), jnp.float32),
            pltpu.VMEM((nseq, DK_TOT), jnp.float32),
            pltpu.VMEM((nseq, D_C), jnp.float32),
        ],
        compiler_params=pltpu.CompilerParams(
            dimension_semantics=("arbitrary", "arbitrary"),
            vmem_limit_bytes=VMEM_LIMIT_BYTES),
        name="sample_path",
    )(x, ng, win, wout, bias_s, sink_c, w00, b0, lng, lnb, wup, bup, gn, fg, buf_kt, buf_vt, s0)


def _window_minor(state):
    depth, n = state.shape[:2]
    return jnp.transpose(state, (0, 1, 3, 4, 2)).reshape(depth, n, LANES, WINDOW)


def _window_major(state_t):
    depth, n = state_t.shape[:2]
    return jnp.transpose(state_t.reshape(depth, n, KV_A, HD_A, WINDOW), (0, 1, 4, 2, 3))


def kernel(x_prompt, x_sample, state_swa_k, state_swa_v, state_gla, rel_bias, norm_g, w_in, sinks, spatial_w,
           spatial_b, chunk_ln_g, chunk_ln_b, gla_w_up, gla_b_up, gla_norm_g, w_out, final_norm_g):
    depth = w_in.shape[0]
    nseq = x_sample.shape[0]
    bsz = x_prompt.shape[0]
    f32 = jnp.float32

    win, wout = _prepare_weights(w_in, w_out)
    wup = jnp.pad(gla_w_up, ((0, 0), (0, LANES - GLA_RANK), (0, 0))).astype(jnp.bfloat16)
    bsf = jnp.repeat(jnp.swapaxes(spatial_b, 1, 2), DH_B, axis=2).astype(f32)
    w00 = jnp.repeat(spatial_w[:, :, 0, 0], DH_B, axis=1).astype(f32)[:, None, :]
    b0 = jnp.repeat(spatial_b[:, :, 0], DH_B, axis=1).astype(f32)[:, None, :]
    sink_c = sinks.astype(f32)[:, :, None]
    row = lambda a: a.astype(f32)[:, None, :]
    ng, lng, lnb, bup, gn = row(norm_g), row(chunk_ln_g), row(chunk_ln_b), row(gla_b_up), row(gla_norm_g)
    fg = final_norm_g.astype(f32)[None, :]
    sw = spatial_w.astype(f32)
    sinks = sinks.astype(f32)

    bias_tab = _bias_table(rel_bias)
    bias_s = bias_tab[0].reshape(H_A, WINDOW, 2 * WINDOW)[:, WINDOW - 1, WINDOW:]

    xp = x_prompt
    kp_l, vp_l, sp_l = [], [], []
    for l in range(depth):
        xp, kp, vp, sp = _prompt_layer(xp, ng, win, wout, bias_tab, sinks, sw, bsf, lng, lnb, wup, bup, gn, fg,
                                       layer=l, final=l == depth - 1)
        kp_l.append(kp); vp_l.append(vp); sp_l.append(sp)

    xs, ks_t, vs_t, ss, cv = _sample_path(
        x_sample.reshape(nseq, D_MODEL), ng, win, wout, bias_s, sink_c, w00, b0, lng, lnb, wup, bup, gn, fg,
        _window_minor(state_swa_k), _window_minor(state_swa_v), state_gla.reshape(depth, nseq * DK_TOT, DV_C))

    return (xp,
            xs.reshape(nseq, 1, D_MODEL),
            _window_major(jnp.stack(kp_l)),
            _window_major(jnp.stack(vp_l)),
            jnp.stack(sp_l).reshape(depth, bsz, H_C, DK_C, DV_C),
            _window_major(ks_t),
            _window_major(vs_t),
            ss.reshape(depth, nseq, H_C, DK_C, DV_C),
            cv.reshape(depth, nseq, 1, D_B))
```

```python
import functools

import numpy as np
import jax
import jax.numpy as jnp
from jax import lax
from jax.experimental import pallas as pl
from jax.experimental.pallas import tpu as pltpu

D_MODEL = 1024
D_A, HD_A, H_A, KV_A, G_A = 512, 64, 8, 2, 4
WINDOW, N_BUCKETS, MAX_DIST = 128, 32, 128
D_B, H_B, DH_B, CHUNK_B = 512, 8, 64, 128
D_C, H_C, DK_TOT, DK_C, DV_C = 512, 4, 256, 64, 128
GLA_RANK, GLA_TAU, GLA_CHUNK = 16, 16.0, 64
D_MIX = D_A + D_B + D_C
EPS = 1e-6
NEG = -1e30
SPLITS = [D_A, KV_A * HD_A, KV_A * HD_A, D_A, D_B, D_B, D_B, DK_TOT, DK_TOT, D_C, D_C, GLA_RANK]
D_IN = sum(SPLITS)

LANES = 128
HALF = LANES // 2
D_IN_PAD = ((D_IN + LANES - 1) // LANES) * LANES
VMEM_LIMIT_BYTES = 56 * 1024 * 1024

(O_QA, O_KA, O_VA, O_GA, O_UB, O_VB, O_GB, O_QC, O_KC, O_VC, O_GC, O_LR) = (
    int(o) for o in np.cumsum([0] + SPLITS[:-1]))
assert all(o % LANES == 0 for o in (O_QA, O_KA, O_VA, O_GA, O_UB, O_VB, O_GB, O_QC, O_KC, O_VC, O_GC, O_LR))
Y_A, Y_B, Y_C = 0, D_A, D_A + D_B

HEAD_PERM = [0, 4, 1, 5, 2, 6, 3, 7]

TOKEN_BLOCK = 512
SEQ_GROUP = 16
IN_PROJ_CHUNK = 256
OUT_PROJ_CHUNK = 256
MIXER_SECTIONS = 19
PRE_SECTIONS = 5
PREP_SPLIT = 2

_NT = (((1,), (1,)), ((), ()))


def _t5_bucket(dist):
    n = np.maximum(dist, 0)
    max_exact = N_BUCKETS // 2
    large = max_exact + (np.log(np.maximum(n, 1) / max_exact) / np.log(MAX_DIST / max_exact)
                         * (N_BUCKETS - max_exact)).astype(np.int32)
    large = np.minimum(large, N_BUCKETS - 1)
    return np.where(n < max_exact, n, large).astype(np.int32)


def _silu(x):
    return x * (1.0 / (1.0 + jnp.exp(-x)))


def _log_sigmoid(x):
    return jnp.minimum(x, 0.0) - jnp.log1p(jnp.exp(-jnp.abs(x)))


def _bdot(a, b):
    return jnp.dot(a.astype(jnp.bfloat16), b.astype(jnp.bfloat16), preferred_element_type=jnp.float32)


def _bdot_nt(a, b):
    return lax.dot_general(a.astype(jnp.bfloat16), b.astype(jnp.bfloat16), _NT,
                           preferred_element_type=jnp.float32)


def _rmsnorm(x, g):
    return x * lax.rsqrt(jnp.mean(x * x, axis=-1, keepdims=True) + EPS) * g


def _layernorm(v, g, b):
    mu = jnp.mean(v, axis=-1, keepdims=True)
    xc = v - mu
    var = jnp.mean(xc * xc, axis=-1, keepdims=True)
    return xc * lax.rsqrt(var + EPS) * g + b


def _softmax_sink(s, sink):
    m = jnp.maximum(jnp.max(s, axis=-1, keepdims=True), sink)
    p = jnp.exp(s - m)
    den = jnp.sum(p, axis=-1, keepdims=True) + jnp.exp(sink - m)
    return p, 1.0 / den


def _bias_table_kernel(rb_ref, bucket_ref, band_ref, out_ref):
    bucket = bucket_ref[...]
    band = band_ref[...] > 0
    own = lax.broadcasted_iota(jnp.int32, bucket.shape, 1) >= WINDOW
    for h in range(H_A):
        acc = jnp.zeros(bucket.shape, jnp.float32)
        for b in range(N_BUCKETS):
            acc = jnp.where(bucket == b, rb_ref[b, h], acc)
        kv, g = divmod(h, G_A)
        rows = pl.ds(g * WINDOW, WINDOW)
        out_ref[0, kv, rows, :] = jnp.where(band, acc, NEG)
        out_ref[1, kv, rows, :] = jnp.where(band & own, acc, NEG)


def _bias_table(rel_bias):
    i = np.arange(WINDOW)[:, None]
    j = np.arange(2 * WINDOW)[None, :]
    dist = i + WINDOW - j
    band = ((dist >= 0) & (dist < WINDOW)).astype(np.int32)
    return pl.pallas_call(
        _bias_table_kernel,
        out_shape=jax.ShapeDtypeStruct((2, KV_A, G_A * WINDOW, 2 * WINDOW), jnp.float32),
        in_specs=[pl.BlockSpec(memory_space=pltpu.SMEM),
                  pl.BlockSpec(memory_space=pltpu.VMEM),
                  pl.BlockSpec(memory_space=pltpu.VMEM)],
        out_specs=pl.BlockSpec(memory_space=pltpu.VMEM),
        name="rel_bias_table",
    )(rel_bias.astype(jnp.float32), jnp.asarray(_t5_bucket(dist)), jnp.asarray(band))


def _win_prep_kernel(wt_ref, lr_ref, o_ref):
    c = pl.program_id(1)
    tiles = wt_ref.shape[0] // LANES
    for cc in range(PREP_SPLIT):
        @pl.when(c == cc)
        def _(cc=cc):
            for t in range(tiles):
                col = (cc * tiles + t) * LANES
                base = next((b for b in (O_QA, O_GA) if b <= col < b + D_A), None)
                if base is None:
                    src = wt_ref[t * LANES:(t + 1) * LANES, :]
                else:
                    g = (col - base) // LANES
                    r0 = base - cc * tiles * LANES
                    assert r0 >= 0 and r0 + D_A <= wt_ref.shape[0]
                    src = jnp.concatenate([wt_ref[r0 + g * HD_A:r0 + (g + 1) * HD_A, :],
                                           wt_ref[r0 + (G_A + g) * HD_A:r0 + (G_A + g + 1) * HD_A, :]], axis=0)
                o_ref[:, col:col + LANES] = src.T.astype(jnp.bfloat16)
            if cc == PREP_SPLIT - 1:
                o_ref[:, O_LR:D_IN_PAD] = lr_ref[...].T.astype(jnp.bfloat16)


def _wout_prep_kernel(w_ref, o_ref):
    for j, h in enumerate(HEAD_PERM):
        o_ref[j * HD_A:(j + 1) * HD_A, :] = w_ref[h * HD_A:(h + 1) * HD_A, :].astype(jnp.bfloat16)
    o_ref[Y_B:, :] = w_ref[Y_B:, :].astype(jnp.bfloat16)


def _prepare_weights(w_in, w_out):
    depth = w_in.shape[0]
    cols = D_MODEL // PREP_SPLIT
    wt = jnp.swapaxes(w_in, 1, 2)
    assert O_LR % (PREP_SPLIT * LANES) == 0
    rows = O_LR // PREP_SPLIT
    wt_lr = jnp.pad(wt[:, O_LR:, :], ((0, 0), (0, D_IN_PAD - D_IN), (0, 0)))
    win = pl.pallas_call(
        _win_prep_kernel,
        grid=(depth, PREP_SPLIT),
        in_specs=[pl.BlockSpec((None, rows, D_MODEL), lambda l, r: (l, r, 0)),
                  pl.BlockSpec((None, LANES, D_MODEL), lambda l, r: (l, 0, 0))],
        out_specs=pl.BlockSpec((None, D_MODEL, D_IN_PAD), lambda l, r: (l, 0, 0)),
        out_shape=jax.ShapeDtypeStruct((depth, D_MODEL, D_IN_PAD), jnp.bfloat16),
        compiler_params=pltpu.CompilerParams(dimension_semantics=("arbitrary", "arbitrary"),
                                             vmem_limit_bytes=VMEM_LIMIT_BYTES),
        name="w_in_prep",
    )(wt, wt_lr)
    wout = pl.pallas_call(
        _wout_prep_kernel,
        grid=(depth, PREP_SPLIT),
        in_specs=[pl.BlockSpec((None, D_MIX, cols), lambda l, c: (l, 0, c))],
        out_specs=pl.BlockSpec((None, D_MIX, cols), lambda l, c: (l, 0, c)),
        out_shape=jax.ShapeDtypeStruct((depth, D_MIX, D_MODEL), jnp.bfloat16),
        compiler_params=pltpu.CompilerParams(dimension_semantics=("arbitrary", "arbitrary"),
                                             vmem_limit_bytes=VMEM_LIMIT_BYTES),
        name="w_out_prep",
    )(w_out)
    return win, wout


def _prompt_layer_kernel(x_ref, xn_ref, ng_ref, win_ref, wout_ref, bias_ref, sink_ref, sw_ref, bsf_ref, lng_ref,
                         lnb_ref, wup_ref, bup_ref, gn_ref, fg_ref,
                         xo_ref, ko_ref, vo_ref, so_ref,
                         pa_ref, pb_ref, hb_ref, hn_ref, ya_ref, yb_ref, kprev_ref, vprev_ref, st_ref, wm_ref,
                         *, tb, layer, final):
    b_id = pl.program_id(0)
    i_id = pl.program_id(1)
    n_i = pl.num_programs(1)
    half = tb // 2
    n_sub = half // WINDOW

    @pl.when((b_id == 0) & (i_id == 0))
    def _():
        r = lax.broadcasted_iota(jnp.int32, (CHUNK_B, CHUNK_B), 0)
        c = lax.broadcasted_iota(jnp.int32, (CHUNK_B, CHUNK_B), 1)
        for h in range(H_B):
            wm_ref[h // 2, :, (h % 2) * CHUNK_B:(h % 2 + 1) * CHUNK_B] = jnp.where(
                c <= r, sw_ref[h], 0.0).astype(jnp.bfloat16)
        pa_ref[...] = _bdot(_rmsnorm(x_ref[0:half, :], ng_ref[...]), win_ref[...])

    @pl.when(i_id == 0)
    def _():
        kprev_ref[...] = jnp.zeros_like(kprev_ref)
        vprev_ref[...] = jnp.zeros_like(vprev_ref)
        st_ref[...] = jnp.zeros_like(st_ref)

    hb_ref[...] = _rmsnorm(x_ref[half:tb, :], ng_ref[...]).astype(jnp.bfloat16)
    hn_ref[...] = _rmsnorm(xn_ref[...], ng_ref[...]).astype(jnp.bfloat16)

    def in_proj_chunks(h_ref, p_ref):
        def chunk(c0, c1):
            def run():
                p_ref[:, c0:c1] = jnp.dot(h_ref[...], win_ref[:, c0:c1], preferred_element_type=jnp.float32)
            return run
        edges = list(range(0, D_IN_PAD, IN_PROJ_CHUNK)) + [D_IN_PAD]
        return [chunk(c0, c1) for c0, c1 in zip(edges[:-1], edges[1:])]

    def out_proj_chunks(y_ref, r0, s):
        def chunk(c0, c1):
            def run():
                rows = slice(r0 + s * WINDOW, r0 + (s + 1) * WINDOW)
                xo_ref[rows, c0:c1] = x_ref[rows, c0:c1] + jnp.dot(
                    y_ref[s * WINDOW:(s + 1) * WINDOW, :], wout_ref[:, c0:c1],
                    preferred_element_type=jnp.float32)
            return run
        return [chunk(c0, c0 + OUT_PROJ_CHUNK) for c0 in range(0, D_MODEL, OUT_PROJ_CHUNK)]

    def final_norm(r0):
        if final:
            xo_ref[r0:r0 + half, :] = _rmsnorm(xo_ref[r0:r0 + half, :], fg_ref[...])

    lane = lax.broadcasted_iota(jnp.int32, (WINDOW, LANES), 1)
    low = lane < HALF
    low64 = lax.broadcasted_iota(jnp.int32, (GLA_CHUNK, LANES), 1) < HALF
    chunk_shift = GLA_CHUNK.bit_length() - 1
    rh = lax.broadcasted_iota(jnp.int32, (half, half), 0)
    ch = lax.broadcasted_iota(jnp.int32, (half, half), 1)
    cum_half = jnp.where((jnp.right_shift(rh, chunk_shift) == jnp.right_shift(ch, chunk_shift)) & (ch <= rh),
                         1.0, 0.0).astype(jnp.bfloat16)
    ri2 = lax.broadcasted_iota(jnp.int32, (LANES, 2 * LANES), 0)
    ci2 = lax.broadcasted_iota(jnp.int32, (LANES, 2 * LANES), 1) & (LANES - 1)
    bd_tril2 = ((ri2 >= GLA_CHUNK) == (ci2 >= GLA_CHUNK)) & (ci2 <= ri2)

    def mixers(p_ref, y_ref, s, first, pre):
        r0 = s * WINDOW
        rows = pl.ds(r0, WINDOW)

        ks = p_ref[rows, O_KA:O_KA + LANES] * (HD_A ** -0.5)
        k_new = [jnp.where(low, ks, 0.0).astype(jnp.bfloat16), jnp.where(low, 0.0, ks).astype(jnp.bfloat16)]
        v_new = p_ref[rows, O_VA:O_VA + LANES].T.astype(jnp.bfloat16)
        vcat = jnp.concatenate([carry["v"], v_new], axis=1)
        q4 = jnp.concatenate([p_ref[rows, O_QA + g * LANES:O_QA + (g + 1) * LANES].astype(jnp.bfloat16)
                              for g in range(G_A)], axis=0)
        o_heads = []
        for kv in range(KV_A):
            kcat = jnp.concatenate([carry["k"][kv], k_new[kv]], axis=0)
            sc = lax.dot_general(q4, kcat, _NT, preferred_element_type=jnp.float32) + bias_ref[first, kv]
            carry["k"][kv] = k_new[kv]
            yield
            ps, invs = [], []
            for g in range(G_A):
                p, inv = _softmax_sink(sc[g * WINDOW:(g + 1) * WINDOW], sink_ref[layer, kv * G_A + g])
                ps.append(p.astype(jnp.bfloat16))
                invs.append(inv)
                if g % 2 == 1:
                    yield
            o_t = lax.dot_general(vcat[kv * HD_A:(kv + 1) * HD_A, :], jnp.concatenate(ps, axis=0), _NT,
                                  preferred_element_type=jnp.float32)
            o_heads.append((o_t, invs))
            yield
        carry["v"] = v_new
        for g in range(G_A):
            o_pair = jnp.concatenate([o_heads[kv][0][:, g * WINDOW:(g + 1) * WINDOW] for kv in range(KV_A)],
                                     axis=0).T
            ya = o_pair * jnp.where(low, o_heads[0][1][g], o_heads[1][1][g])
            gate = p_ref[rows, O_GA + g * LANES:O_GA + (g + 1) * LANES]
            y_ref[rows, Y_A + g * LANES:Y_A + (g + 1) * LANES] = (ya * _silu(gate)).astype(jnp.bfloat16)
        yield

        for j in range(H_B // 2):
            mix = pre["mix"][j][:, s * LANES:(s + 1) * LANES] + bsf_ref[:, j * LANES:(j + 1) * LANES]
            u = p_ref[rows, O_UB + j * LANES:O_UB + (j + 1) * LANES]
            gate = p_ref[rows, O_GB + j * LANES:O_GB + (j + 1) * LANES]
            y_ref[rows, Y_B + j * LANES:Y_B + (j + 1) * LANES] = (u * mix * _silu(gate)).astype(jnp.bfloat16)
            if j % 2 == 1:
                yield

        bc = pre["bc"][r0:r0 + WINDOW]
        for c in range(WINDOW // GLA_CHUNK):
            rc = pl.ds(r0 + c * GLA_CHUNK, GLA_CHUNK)
            q_sts, k_sts, v_sts, kd_sts, decays, sts = [], [], [], [], [], []
            for p in range(H_C // 2):
                bp = bc[c * GLA_CHUNK:(c + 1) * GLA_CHUNK, p * LANES:(p + 1) * LANES]
                qp = p_ref[rc, O_QC + p * LANES:O_QC + (p + 1) * LANES]
                kp = p_ref[rc, O_KC + p * LANES:O_KC + (p + 1) * LANES]
                b_last = bp[GLA_CHUNK - 1:GLA_CHUNK, :]
                qd = qp * (DK_C ** -0.5) * jnp.exp(bp)
                ki = (kp * jnp.exp(-bp)).astype(jnp.bfloat16)
                kd = kp * jnp.exp(b_last - bp)
                q_sts.append(jnp.concatenate([jnp.where(low64, qd, 0.0), jnp.where(low64, 0.0, qd)],
                                             axis=0).astype(jnp.bfloat16))
                k_sts.append(jnp.concatenate([ki, ki], axis=0))
                v0 = p_ref[rc, O_VC + (2 * p) * DV_C:O_VC + (2 * p + 1) * DV_C]
                v1 = p_ref[rc, O_VC + (2 * p + 1) * DV_C:O_VC + (2 * p + 2) * DV_C]
                v_sts.append(jnp.concatenate([v0, v1], axis=0))
                kd_sts.append(jnp.concatenate([jnp.where(low64, kd, 0.0), jnp.where(low64, 0.0, kd)],
                                              axis=0).astype(jnp.bfloat16))
                decays.append(jnp.exp(b_last))
                yield
            zero = jnp.zeros((LANES, LANES), jnp.bfloat16)
            blockdiag = lambda a, b: jnp.concatenate([jnp.concatenate([a, zero], axis=1),
                                                      jnp.concatenate([zero, b], axis=1)], axis=0)
            q_all = jnp.concatenate(q_sts, axis=1)
            att = lax.dot_general(q_all, blockdiag(*k_sts), _NT, preferred_element_type=jnp.float32)
            att = jnp.where(bd_tril2, att, 0.0).astype(jnp.bfloat16)
            v_b = [v.astype(jnp.bfloat16) for v in v_sts]
            o_all = jnp.dot(att, blockdiag(*v_b), preferred_element_type=jnp.float32)
            d_all = jnp.dot(jnp.concatenate([v.T.astype(jnp.bfloat16) for v in v_sts], axis=1),
                            blockdiag(*kd_sts), preferred_element_type=jnp.float32)
            sts = carry["st"]
            o_inter = lax.dot_general(q_all, blockdiag(*[st.astype(jnp.bfloat16) for st in sts]), _NT,
                                      preferred_element_type=jnp.float32)
            o_all = o_all + o_inter
            carry["st"] = [sts[p] * decays[p] + d_all[:, p * LANES:(p + 1) * LANES] for p in range(H_C // 2)]
            for p in range(H_C // 2):
                o = o_all[:, p * DV_C:(p + 1) * DV_C]
                for hh in range(2):
                    hd = 2 * p + hh
                    oh = o[hh * GLA_CHUNK:(hh + 1) * GLA_CHUNK]
                    on = _rmsnorm(oh, gn_ref[:, hd * DV_C:(hd + 1) * DV_C])
                    gate = p_ref[rc, O_GC + hd * DV_C:O_GC + (hd + 1) * DV_C]
                    y_ref[rc, Y_C + hd * DV_C:Y_C + (hd + 1) * DV_C] = (on * _silu(gate)).astype(jnp.bfloat16)
                yield

    def half_pre(p_ref, pre):
        vn = _layernorm(p_ref[:, O_VB:O_VB + D_B], lng_ref[...], lnb_ref[...]).astype(jnp.bfloat16)
        yield
        pre["mix"] = []
        for j in range(H_B // 2):
            cols = []
            for s in range(n_sub):
                vp = vn[s * WINDOW:(s + 1) * WINDOW, j * LANES:(j + 1) * LANES]
                zero = jnp.zeros_like(vp)
                cols.append(jnp.concatenate([jnp.where(low, vp, zero), jnp.where(low, zero, vp)], axis=0))
            pre["mix"].append(jnp.dot(wm_ref[j], jnp.concatenate(cols, axis=1),
                                      preferred_element_type=jnp.float32))
            if j % 2 == 1:
                yield
        z = _bdot(p_ref[:, O_LR:O_LR + LANES], wup_ref[...]) + bup_ref[...]
        la = _log_sigmoid(z) * (1.0 / GLA_TAU)
        yield
        hi = la.astype(jnp.bfloat16)
        r1 = la - hi.astype(jnp.float32)
        mid = r1.astype(jnp.bfloat16)
        lo = (r1 - mid.astype(jnp.float32)).astype(jnp.bfloat16)
        c3 = jnp.dot(cum_half, jnp.concatenate([hi, mid, lo], axis=1), preferred_element_type=jnp.float32)
        pre["bc"] = (c3[:, 0:DK_TOT] + c3[:, DK_TOT:2 * DK_TOT]) + c3[:, 2 * DK_TOT:3 * DK_TOT]
        yield

    def half_mixers(p_ref, y_ref, first):
        pre = {}
        yield from half_pre(p_ref, pre)
        for s in range(n_sub):
            yield from mixers(p_ref, y_ref, s, first if s == 0 else 0, pre)

    def run_phase(sections, chunks, late_chunks):
        n_sections = PRE_SECTIONS + n_sub * MIXER_SECTIONS
        plan = [[] for _ in range(n_sections)]
        for idx, run in enumerate(chunks):
            plan[idx * n_sections // len(chunks)].append(run)
        for s, runs in late_chunks.items():
            k0 = PRE_SECTIONS + (s + 1) * MIXER_SECTIONS
            for idx, run in enumerate(runs):
                plan[k0 + idx * (n_sections - k0) // len(runs)].append(run)
        for k in range(n_sections):
            next(sections)
            for run in plan[k]:
                run()
        assert next(sections, None) is None

    carry = {"k": [kprev_ref[kv] for kv in range(KV_A)], "v": vprev_ref[...],
             "st": [st_ref[p] for p in range(H_C // 2)]}
    run_phase(half_mixers(pa_ref, ya_ref, jnp.where(i_id == 0, 1, 0)),
              in_proj_chunks(hb_ref, pb_ref),
              {s: out_proj_chunks(ya_ref, 0, s) for s in range(n_sub - 1)})
    run_phase(half_mixers(pb_ref, yb_ref, 0),
              out_proj_chunks(ya_ref, 0, n_sub - 1) + in_proj_chunks(hn_ref, pa_ref),
              {s: out_proj_chunks(yb_ref, half, s) for s in range(n_sub - 1)})
    final_norm(0)
    for run in out_proj_chunks(yb_ref, half, n_sub - 1):
        run()
    final_norm(half)
    for kv in range(KV_A):
        kprev_ref[kv] = carry["k"][kv]
    vprev_ref[...] = carry["v"]
    for p in range(H_C // 2):
        st_ref[p] = carry["st"][p]

    @pl.when(i_id == n_i - 1)
    def _():
        ko_ref[...] = pb_ref[half - WINDOW:half, O_KA:O_KA + LANES].T
        vo_ref[...] = pb_ref[half - WINDOW:half, O_VA:O_VA + LANES].T
        for p in range(H_C // 2):
            so_ref[p] = st_ref[p].T


def _prompt_layer(x, ng, win, wout, bias_tab, sinks, sw, bsf, lng, lnb, wup, bup, gn, fg, *, layer, final):
    bsz, seq, _ = x.shape
    tb = min(TOKEN_BLOCK, seq)
    half = tb // 2
    assert seq % tb == 0 and half % WINDOW == 0
    n_i = seq // tb
    grid = (bsz, n_i)

    def next_half(b, i):
        lin = jnp.minimum(b * n_i + i + 1, bsz * n_i - 1)
        return lin // n_i, (lin % n_i) * 2, 0

    def const_spec(shape):
        nd = len(shape)
        return pl.BlockSpec(shape, lambda b, i: (0,) * nd, pipeline_mode=pl.Buffered(1))

    def layer_spec(shape):
        nd = len(shape)
        return pl.BlockSpec((None,) + shape, lambda b, i: (layer,) + (0,) * nd, pipeline_mode=pl.Buffered(1))

    kern = functools.partial(_prompt_layer_kernel, tb=tb, layer=layer, final=final)
    return pl.pallas_call(
        kern,
        grid=grid,
        in_specs=[
            pl.BlockSpec((None, tb, D_MODEL), lambda b, i: (b, i, 0)),
            pl.BlockSpec((None, half, D_MODEL), next_half),
            layer_spec((1, D_MODEL)),
            layer_spec((D_MODEL, D_IN_PAD)),
            layer_spec((D_MIX, D_MODEL)),
            const_spec((2, KV_A, G_A * WINDOW, 2 * WINDOW)),
            pl.BlockSpec(memory_space=pltpu.SMEM),
            layer_spec((H_B, CHUNK_B, CHUNK_B)),
            layer_spec((CHUNK_B, D_B)),
            layer_spec((1, D_B)),
            layer_spec((1, D_B)),
            layer_spec((LANES, DK_TOT)),
            layer_spec((1, DK_TOT)),
            layer_spec((1, D_C)),
            const_spec((1, D_MODEL)),
        ],
        out_specs=[
            pl.BlockSpec((None, tb, D_MODEL), lambda b, i: (b, i, 0)),
            pl.BlockSpec((None, LANES, WINDOW), lambda b, i: (b, 0, 0)),
            pl.BlockSpec((None, LANES, WINDOW), lambda b, i: (b, 0, 0)),
            pl.BlockSpec((None, H_C // 2, LANES, DV_C), lambda b, i: (b, 0, 0, 0)),
        ],
        out_shape=[
            jax.ShapeDtypeStruct((bsz, seq, D_MODEL), jnp.float32),
            jax.ShapeDtypeStruct((bsz, LANES, WINDOW), jnp.float32),
            jax.ShapeDtypeStruct((bsz, LANES, WINDOW), jnp.float32),
            jax.ShapeDtypeStruct((bsz, H_C // 2, LANES, DV_C), jnp.float32),
        ],
        scratch_shapes=[
            pltpu.VMEM((half, D_IN_PAD), jnp.float32),
            pltpu.VMEM((half, D_IN_PAD), jnp.float32),
            pltpu.VMEM((half, D_MODEL), jnp.bfloat16),
            pltpu.VMEM((half, D_MODEL), jnp.bfloat16),
            pltpu.VMEM((half, D_MIX), jnp.bfloat16),
            pltpu.VMEM((half, D_MIX), jnp.bfloat16),
            pltpu.VMEM((KV_A, WINDOW, LANES), jnp.bfloat16),
            pltpu.VMEM((LANES, WINDOW), jnp.bfloat16),
            pltpu.VMEM((H_C // 2, DV_C, LANES), jnp.float32),
            pltpu.VMEM((H_B // 2, CHUNK_B, 2 * CHUNK_B), jnp.bfloat16),
        ],
        compiler_params=pltpu.CompilerParams(
            dimension_semantics=("arbitrary", "arbitrary"),
            vmem_limit_bytes=VMEM_LIMIT_BYTES),
        name="prompt_layer",
    )(x, x, ng, win, wout, bias_tab, sinks, sw, bsf, lng, lnb, wup, bup, gn, fg)


def _sample_kernel(x_ref, ng_ref, win_ref, wout_ref, biass_ref, sinkc_ref, w00_ref, b0_ref, lng_ref,
                   lnb_ref, wup_ref, bup_ref, gn_ref, fg_ref, bk_ref, bv_ref, s0_ref,
                   xo_ref, nk_ref, nv_ref, so_ref, cv_ref,
                   xcur_ref, proj_ref, y_ref, qs_ref, os_ref, qd_ref, ea_ref, oc_ref, *, nseq, gs):
    l_id = pl.program_id(0)
    i_id = pl.program_id(1)
    n_l = pl.num_programs(0)
    n_i = pl.num_programs(1)
    lane = lax.broadcasted_iota(jnp.int32, (nseq, LANES), 1)
    low = lane < HALF

    @pl.when((l_id == 0) & (i_id == 0))
    def _():
        xcur_ref[...] = x_ref[...]

    @pl.when(i_id == 0)
    def _():
        h = _rmsnorm(xcur_ref[...], ng_ref[...])
        proj_ref[...] = _bdot(h, win_ref[...])
        for g in range(G_A):
            qp = proj_ref[:, O_QA + g * LANES:O_QA + (g + 1) * LANES] * (HD_A ** -0.5)
            for kv in range(KV_A):
                keep = low if kv == 0 else jnp.logical_not(low)
                qs_ref[pl.ds(kv * G_A + g, nseq, stride=H_A), :] = jnp.where(keep, qp, 0.0)
        vn = _layernorm(proj_ref[:, O_VB:O_VB + D_B], lng_ref[...], lnb_ref[...])
        cv_ref[...] = vn
        mix = w00_ref[...] * vn + b0_ref[...]
        y_ref[:, Y_B:Y_B + D_B] = (proj_ref[:, O_UB:O_UB + D_B] * mix) * _silu(proj_ref[:, O_GB:O_GB + D_B])
        z = _bdot(proj_ref[:, O_LR:O_LR + LANES], wup_ref[...]) + bup_ref[...]
        la = _log_sigmoid(z) * (1.0 / GLA_TAU)
        ea = jnp.exp(la)
        qd = proj_ref[:, O_QC:O_QC + DK_TOT] * (DK_C ** -0.5) * ea
        ki = proj_ref[:, O_KC:O_KC + DK_TOT] * jnp.exp(-la)
        qd_ref[...] = qd
        ea_ref[...] = ea
        prod = qd * ki
        lane_c = lax.broadcasted_iota(jnp.int32, prod.shape, 1)
        for hd in range(H_C):
            in_head = (lane_c >= hd * DK_C) & (lane_c < (hd + 1) * DK_C)
            att = jnp.sum(jnp.where(in_head, prod, 0.0), axis=-1, keepdims=True)
            oc_ref[:, hd * DV_C:(hd + 1) * DV_C] = att * proj_ref[:, O_VC + hd * DV_C:O_VC + (hd + 1) * DV_C]

    g0 = pl.multiple_of(i_id * gs, gs)
    grows = pl.ds(g0, gs)

    pad_rows = jnp.zeros((LANES - gs, LANES), jnp.float32)
    k_cols = jnp.concatenate([proj_ref[grows, O_KA:O_KA + LANES], pad_rows], axis=0).T
    v_cols = jnp.concatenate([proj_ref[grows, O_VA:O_VA + LANES], pad_rows], axis=0).T
    newest = lax.broadcasted_iota(jnp.int32, (LANES, WINDOW), 1) == WINDOW - 1
    for j in range(gs):
        kt = jnp.where(newest, k_cols[:, j:j + 1], pltpu.roll(bk_ref[j], WINDOW - 1, 1))
        vt = jnp.where(newest, v_cols[:, j:j + 1], pltpu.roll(bv_ref[j], WINDOW - 1, 1))
        nk_ref[j] = kt
        nv_ref[j] = vt
        r8 = pl.ds(pl.multiple_of((g0 + j) * H_A, H_A), H_A)
        sc = _bdot(qs_ref[r8, :], kt) + biass_ref[...]
        p, inv = _softmax_sink(sc, sinkc_ref[...])
        os_ref[r8, :] = _bdot_nt(p, vt) * inv

    pad_c = jnp.zeros((LANES - gs, DK_TOT), jnp.float32)
    ea_cols = jnp.concatenate([ea_ref[grows, :], pad_c], axis=0).T
    k_cols_c = jnp.concatenate([proj_ref[grows, O_KC:O_KC + DK_TOT], pad_c], axis=0).T
    qd_g = qd_ref[grows, :]
    row_g = lax.broadcasted_iota(jnp.int32, (gs, LANES), 0)
    low_g = lax.broadcasted_iota(jnp.int32, (gs, LANES), 1) < HALF
    seqs_per_dot = 2 * LANES // DK_C
    for hd in range(H_C):
        v_h = proj_ref[grows, O_VC + hd * DV_C:O_VC + (hd + 1) * DV_C]
        q_pair = qd_g[:, (hd // 2) * LANES:(hd // 2 + 1) * LANES]
        q_other = pltpu.roll(q_pair, HALF, 1)
        q_lo, q_hi = (q_pair, q_other) if hd % 2 == 0 else (q_other, q_pair)
        acc = jnp.zeros((gs, DV_C), jnp.float32)
        for j0 in range(0, gs, seqs_per_dot):
            tiles, lhs = [], []
            for j in range(j0, j0 + seqs_per_dot):
                srows = pl.ds(j * DK_TOT + hd * DK_C, DK_C)
                s_old = s0_ref[srows, :]
                tiles.append(s_old.astype(jnp.bfloat16))
                crows = slice(hd * DK_C, (hd + 1) * DK_C)
                so_ref[srows, :] = ea_cols[crows, j:j + 1] * s_old + k_cols_c[crows, j:j + 1] * v_h[j:j + 1, :]
            for j in range(j0, j0 + seqs_per_dot, 2):
                lhs.append(jnp.where((row_g == j) & low_g, q_lo,
                                     jnp.where((row_g == j + 1) & jnp.logical_not(low_g), q_hi, 0.0)))
            acc = acc + _bdot(jnp.concatenate(lhs, axis=1), jnp.concatenate(tiles, axis=0))
        oc_ref[grows, hd * DV_C:(hd + 1) * DV_C] += acc

    @pl.when(i_id == n_i - 1)
    def _():
        for g in range(G_A):
            o0 = os_ref[pl.ds(g, nseq, stride=H_A), :]
            o1 = os_ref[pl.ds(G_A + g, nseq, stride=H_A), :]
            gate = proj_ref[:, O_GA + g * LANES:O_GA + (g + 1) * LANES]
            y_ref[:, Y_A + g * LANES:Y_A + (g + 1) * LANES] = jnp.where(low, o0, o1) * _silu(gate)
        for hd in range(H_C):
            on = _rmsnorm(oc_ref[:, hd * DV_C:(hd + 1) * DV_C], gn_ref[:, hd * DV_C:(hd + 1) * DV_C])
            gate = proj_ref[:, O_GC + hd * DV_C:O_GC + (hd + 1) * DV_C]
            y_ref[:, Y_C + hd * DV_C:Y_C + (hd + 1) * DV_C] = on * _silu(gate)
        x_new = xcur_ref[...] + _bdot(y_ref[...], wout_ref[...])
        xcur_ref[...] = x_new

        @pl.when(l_id == n_l - 1)
        def _():
            xo_ref[...] = _rmsnorm(x_new, fg_ref[...])


def _sample_path(x, ng, win, wout, bias_s, sink_c, w00, b0, lng, lnb, wup, bup, gn, fg, buf_kt, buf_vt, s0):
    depth = win.shape[0]
    nseq = x.shape[0]
    gs = min(SEQ_GROUP, nseq)
    assert nseq % gs == 0 and gs % 8 == 0
    grid = (depth, nseq // gs)

    def const_spec(shape):
        nd = len(shape)
        return pl.BlockSpec(shape, lambda l, i: (0,) * nd)

    def layer_spec(shape):
        nd = len(shape)
        return pl.BlockSpec((None,) + shape, lambda l, i: (l,) + (0,) * nd)

    state_spec = pl.BlockSpec((None, gs, LANES, WINDOW), lambda l, i: (l, i, 0, 0))
    gla_spec = pl.BlockSpec((None, gs * DK_TOT, DV_C), lambda l, i: (l, i, 0))
    kern = functools.partial(_sample_kernel, nseq=nseq, gs=gs)
    return pl.pallas_call(
        kern,
        grid=grid,
        in_specs=[
            const_spec((nseq, D_MODEL)),
            layer_spec((1, D_MODEL)),
            layer_spec((D_MODEL, D_IN_PAD)),
            layer_spec((D_MIX, D_MODEL)),
            const_spec((H_A, WINDOW)),
            layer_spec((H_A, 1)),
            layer_spec((1, D_B)),
            layer_spec((1, D_B)),
            layer_spec((1, D_B)),
            layer_spec((1, D_B)),
            layer_spec((LANES, DK_TOT)),
            layer_spec((1, DK_TOT)),
            layer_spec((1, D_C)),
            const_spec((1, D_MODEL)),
            state_spec, state_spec, gla_spec,
        ],
        out_specs=[
            const_spec((nseq, D_MODEL)),
            state_spec, state_spec, gla_spec,
            layer_spec((nseq, D_B)),
        ],
        out_shape=[
            jax.ShapeDtypeStruct((nseq, D_MODEL), jnp.float32),
            jax.ShapeDtypeStruct((depth, nseq, LANES, WINDOW), jnp.float32),
            jax.ShapeDtypeStruct((depth, nseq, LANES, WINDOW), jnp.float32),
            jax.ShapeDtypeStruct((depth, nseq * DK_TOT, DV_C), jnp.float32),
            jax.ShapeDtypeStruct((depth, nseq, D_B), jnp.float32),
        ],
        scratch_shapes=[
            pltpu.VMEM((nseq, D_MODEL), jnp.float32),
            pltpu.VMEM((nseq, D_IN_PAD), jnp.float32),
            pltpu.VMEM((nseq, D_MIX), jnp.float32),
            pltpu.VMEM((nseq * H_A, LANES), jnp.float32),
            pltpu.VMEM((nseq * H_A, LANES), jnp.float32),
            pltpu.VMEM((nseq, DK_TOT), jnp.float32),
            pltpu.VMEM((nseq, DK_TOT), jnp.float32),
            pltpu.VMEM((nseq, D_C), jnp.float32),
        ],
        compiler_params=pltpu.CompilerParams(
            dimension_semantics=("arbitrary", "arbitrary"),
            vmem_limit_bytes=VMEM_LIMIT_BYTES),
        name="sample_path",
    )(x, ng, win, wout, bias_s, sink_c, w00, b0, lng, lnb, wup, bup, gn, fg, buf_kt, buf_vt, s0)


def _window_minor(state):
    depth, n = state.shape[:2]
    return jnp.transpose(state, (0, 1, 3, 4, 2)).reshape(depth, n, LANES, WINDOW)


def _window_major(state_t):
    depth, n = state_t.shape[:2]
    return jnp.transpose(state_t.reshape(depth, n, KV_A, HD_A, WINDOW), (0, 1, 4, 2, 3))


def kernel(x_prompt, x_sample, state_swa_k, state_swa_v, state_gla, rel_bias, norm_g, w_in, sinks, spatial_w,
           spatial_b, chunk_ln_g, chunk_ln_b, gla_w_up, gla_b_up, gla_norm_g, w_out, final_norm_g):
    depth = w_in.shape[0]
    nseq = x_sample.shape[0]
    bsz = x_prompt.shape[0]
    f32 = jnp.float32

    win, wout = _prepare_weights(w_in, w_out)
    wup = jnp.pad(gla_w_up, ((0, 0), (0, LANES - GLA_RANK), (0, 0))).astype(jnp.bfloat16)
    bsf = jnp.repeat(jnp.swapaxes(spatial_b, 1, 2), DH_B, axis=2).astype(f32)
    w00 = jnp.repeat(spatial_w[:, :, 0, 0], DH_B, axis=1).astype(f32)[:, None, :]
    b0 = jnp.repeat(spatial_b[:, :, 0], DH_B, axis=1).astype(f32)[:, None, :]
    sink_c = sinks.astype(f32)[:, :, None]
    row = lambda a: a.astype(f32)[:, None, :]
    ng, lng, lnb, bup, gn = row(norm_g), row(chunk_ln_g), row(chunk_ln_b), row(gla_b_up), row(gla_norm_g)
    fg = final_norm_g.astype(f32)[None, :]
    sw = spatial_w.astype(f32)
    sinks = sinks.astype(f32)

    bias_tab = _bias_table(rel_bias)
    bias_s = bias_tab[0].reshape(H_A, WINDOW, 2 * WINDOW)[:, WINDOW - 1, WINDOW:]

    xp = x_prompt
    kp_l, vp_l, sp_l = [], [], []
    for l in range(depth):
        xp, kp, vp, sp = _prompt_layer(xp, ng, win, wout, bias_tab, sinks, sw, bsf, lng, lnb, wup, bup, gn, fg,
                                       layer=l, final=l == depth - 1)
        kp_l.append(kp); vp_l.append(vp); sp_l.append(sp)

    xs, ks_t, vs_t, ss, cv = _sample_path(
        x_sample.reshape(nseq, D_MODEL), ng, win, wout, bias_s, sink_c, w00, b0, lng, lnb, wup, bup, gn, fg,
        _window_minor(state_swa_k), _window_minor(state_swa_v), state_gla.reshape(depth, nseq * DK_TOT, DV_C))

    return (xp,
            xs.reshape(nseq, 1, D_MODEL),
            _window_major(jnp.stack(kp_l)),
            _window_major(jnp.stack(vp_l)),
            jnp.stack(sp_l).reshape(depth, bsz, H_C, DK_C, DV_C),
            _window_major(ks_t),
            _window_major(vs_t),
            ss.reshape(depth, nseq, H_C, DK_C, DV_C),
            cv.reshape(depth, nseq, 1, D_B))
```

```python
import functools

import numpy as np
import jax
import jax.numpy as jnp
from jax import lax
from jax.experimental import pallas as pl
from jax.experimental.pallas import tpu as pltpu

D_MODEL = 1024
D_A, HD_A, H_A, KV_A, G_A = 512, 64, 8, 2, 4
WINDOW, N_BUCKETS, MAX_DIST = 128, 32, 128
D_B, H_B, DH_B, CHUNK_B = 512, 8, 64, 128
D_C, H_C, DK_TOT, DK_C, DV_C = 512, 4, 256, 64, 128
GLA_RANK, GLA_TAU, GLA_CHUNK = 16, 16.0, 64
D_MIX = D_A + D_B + D_C
EPS = 1e-6
NEG = -1e30
SPLITS = [D_A, KV_A * HD_A, KV_A * HD_A, D_A, D_B, D_B, D_B, DK_TOT, DK_TOT, D_C, D_C, GLA_RANK]
D_IN = sum(SPLITS)

LANES = 128
HALF = LANES // 2
D_IN_PAD = ((D_IN + LANES - 1) // LANES) * LANES
VMEM_LIMIT_BYTES = 56 * 1024 * 1024

(O_QA, O_KA, O_VA, O_GA, O_UB, O_VB, O_GB, O_QC, O_KC, O_VC, O_GC, O_LR) = (
    int(o) for o in np.cumsum([0] + SPLITS[:-1]))
assert all(o % LANES == 0 for o in (O_QA, O_KA, O_VA, O_GA, O_UB, O_VB, O_GB, O_QC, O_KC, O_VC, O_GC, O_LR))
Y_A, Y_B, Y_C = 0, D_A, D_A + D_B

HEAD_PERM = [0, 4, 1, 5, 2, 6, 3, 7]

TOKEN_BLOCK = 512
SEQ_GROUP = 16
IN_PROJ_CHUNK = 256
OUT_PROJ_CHUNK = 256
MIXER_SECTIONS = 19
PRE_SECTIONS = 5
PREP_SPLIT = 2

_NT = (((1,), (1,)), ((), ()))


def _t5_bucket(dist):
    n = np.maximum(dist, 0)
    max_exact = N_BUCKETS // 2
    large = max_exact + (np.log(np.maximum(n, 1) / max_exact) / np.log(MAX_DIST / max_exact)
                         * (N_BUCKETS - max_exact)).astype(np.int32)
    large = np.minimum(large, N_BUCKETS - 1)
    return np.where(n < max_exact, n, large).astype(np.int32)


def _silu(x):
    return x * (1.0 / (1.0 + jnp.exp(-x)))


def _log_sigmoid(x):
    return jnp.minimum(x, 0.0) - jnp.log1p(jnp.exp(-jnp.abs(x)))


def _bdot(a, b):
    return jnp.dot(a.astype(jnp.bfloat16), b.astype(jnp.bfloat16), preferred_element_type=jnp.float32)


def _bdot_nt(a, b):
    return lax.dot_general(a.astype(jnp.bfloat16), b.astype(jnp.bfloat16), _NT,
                           preferred_element_type=jnp.float32)


def _rmsnorm(x, g):
    return x * lax.rsqrt(jnp.mean(x * x, axis=-1, keepdims=True) + EPS) * g


def _layernorm(v, g, b):
    mu = jnp.mean(v, axis=-1, keepdims=True)
    xc = v - mu
    var = jnp.mean(xc * xc, axis=-1, keepdims=True)
    return xc * lax.rsqrt(var + EPS) * g + b


def _softmax_sink(s, sink):
    m = jnp.maximum(jnp.max(s, axis=-1, keepdims=True), sink)
    p = jnp.exp(s - m)
    den = jnp.sum(p, axis=-1, keepdims=True) + jnp.exp(sink - m)
    return p, 1.0 / den


def _bias_table_kernel(rb_ref, bucket_ref, band_ref, out_ref):
    bucket = bucket_ref[...]
    band = band_ref[...] > 0
    own = lax.broadcasted_iota(jnp.int32, bucket.shape, 1) >= WINDOW
    for h in range(H_A):
        acc = jnp.zeros(bucket.shape, jnp.float32)
        for b in range(N_BUCKETS):
            acc = jnp.where(bucket == b, rb_ref[b, h], acc)
        kv, g = divmod(h, G_A)
        rows = pl.ds(g * WINDOW, WINDOW)
        out_ref[0, kv, rows, :] = jnp.where(band, acc, NEG)
        out_ref[1, kv, rows, :] = jnp.where(band & own, acc, NEG)


def _bias_table(rel_bias):
    i = np.arange(WINDOW)[:, None]
    j = np.arange(2 * WINDOW)[None, :]
    dist = i + WINDOW - j
    band = ((dist >= 0) & (dist < WINDOW)).astype(np.int32)
    return pl.pallas_call(
        _bias_table_kernel,
        out_shape=jax.ShapeDtypeStruct((2, KV_A, G_A * WINDOW, 2 * WINDOW), jnp.float32),
        in_specs=[pl.BlockSpec(memory_space=pltpu.SMEM),
                  pl.BlockSpec(memory_space=pltpu.VMEM),
                  pl.BlockSpec(memory_space=pltpu.VMEM)],
        out_specs=pl.BlockSpec(memory_space=pltpu.VMEM),
        name="rel_bias_table",
    )(rel_bias.astype(jnp.float32), jnp.asarray(_t5_bucket(dist)), jnp.asarray(band))


def _win_prep_kernel(wt_ref, lr_ref, o_ref):
    c = pl.program_id(1)
    tiles = wt_ref.shape[0] // LANES
    for cc in range(PREP_SPLIT):
        @pl.when(c == cc)
        def _(cc=cc):
            for t in range(tiles):
                col = (cc * tiles + t) * LANES
                base = next((b for b in (O_QA, O_GA) if b <= col < b + D_A), None)
                if base is None:
                    src = wt_ref[t * LANES:(t + 1) * LANES, :]
                else:
                    g = (col - base) // LANES
                    r0 = base - cc * tiles * LANES
                    assert r0 >= 0 and r0 + D_A <= wt_ref.shape[0]
                    src = jnp.concatenate([wt_ref[r0 + g * HD_A:r0 + (g + 1) * HD_A, :],
                                           wt_ref[r0 + (G_A + g) * HD_A:r0 + (G_A + g + 1) * HD_A, :]], axis=0)
                o_ref[:, col:col + LANES] = src.T.astype(jnp.bfloat16)
            if cc == PREP_SPLIT - 1:
                o_ref[:, O_LR:D_IN_PAD] = lr_ref[...].T.astype(jnp.bfloat16)


def _wout_prep_kernel(w_ref, o_ref):
    for j, h in enumerate(HEAD_PERM):
        o_ref[j * HD_A:(j + 1) * HD_A, :] = w_ref[h * HD_A:(h + 1) * HD_A, :].astype(jnp.bfloat16)
    o_ref[Y_B:, :] = w_ref[Y_B:, :].astype(jnp.bfloat16)


def _prepare_weights(w_in, w_out):
    depth = w_in.shape[0]
    cols = D_MODEL // PREP_SPLIT
    wt = jnp.swapaxes(w_in, 1, 2)
    assert O_LR % (PREP_SPLIT * LANES) == 0
    rows = O_LR // PREP_SPLIT
    wt_lr = jnp.pad(wt[:, O_LR:, :], ((0, 0), (0, D_IN_PAD - D_IN), (0, 0)))
    win = pl.pallas_call(
        _win_prep_kernel,
        grid=(depth, PREP_SPLIT),
        in_specs=[pl.BlockSpec((None, rows, D_MODEL), lambda l, r: (l, r, 0)),
                  pl.BlockSpec((None, LANES, D_MODEL), lambda l, r: (l, 0, 0))],
        out_specs=pl.BlockSpec((None, D_MODEL, D_IN_PAD), lambda l, r: (l, 0, 0)),
        out_shape=jax.ShapeDtypeStruct((depth, D_MODEL, D_IN_PAD), jnp.bfloat16),
        compiler_params=pltpu.CompilerParams(dimension_semantics=("arbitrary", "arbitrary"),
                                             vmem_limit_bytes=VMEM_LIMIT_BYTES),
        name="w_in_prep",
    )(wt, wt_lr)
    wout = pl.pallas_call(
        _wout_prep_kernel,
        grid=(depth, PREP_SPLIT),
        in_specs=[pl.BlockSpec((None, D_MIX, cols), lambda l, c: (l, 0, c))],
        out_specs=pl.BlockSpec((None, D_MIX, cols), lambda l, c: (l, 0, c)),
        out_shape=jax.ShapeDtypeStruct((depth, D_MIX, D_MODEL), jnp.bfloat16),
        compiler_params=pltpu.CompilerParams(dimension_semantics=("arbitrary", "arbitrary"),
                                             vmem_limit_bytes=VMEM_LIMIT_BYTES),
        name="w_out_prep",
    )(w_out)
    return win, wout


def _prompt_layer_kernel(x_ref, xn_ref, ng_ref, win_ref, wout_ref, bias_ref, sink_ref, sw_ref, bsf_ref, lng_ref,
                         lnb_ref, wup_ref, bup_ref, gn_ref, fg_ref,
                         xo_ref, ko_ref, vo_ref, so_ref,
                         pa_ref, pb_ref, hb_ref, hn_ref, ya_ref, yb_ref, kprev_ref, vprev_ref, st_ref, wm_ref,
                         *, tb, layer, final):
    b_id = pl.program_id(0)
    i_id = pl.program_id(1)
    n_i = pl.num_programs(1)
    half = tb // 2
    n_sub = half // WINDOW

    @pl.when((b_id == 0) & (i_id == 0))
    def _():
        r = lax.broadcasted_iota(jnp.int32, (CHUNK_B, CHUNK_B), 0)
        c = lax.broadcasted_iota(jnp.int32, (CHUNK_B, CHUNK_B), 1)
        for h in range(H_B):
            wm_ref[h // 2, :, (h % 2) * CHUNK_B:(h % 2 + 1) * CHUNK_B] = jnp.where(
                c <= r, sw_ref[h], 0.0).astype(jnp.bfloat16)
        pa_ref[...] = _bdot(_rmsnorm(x_ref[0:half, :], ng_ref[...]), win_ref[...])

    @pl.when(i_id == 0)
    def _():
        kprev_ref[...] = jnp.zeros_like(kprev_ref)
        vprev_ref[...] = jnp.zeros_like(vprev_ref)
        st_ref[...] = jnp.zeros_like(st_ref)

    hb_ref[...] = _rmsnorm(x_ref[half:tb, :], ng_ref[...]).astype(jnp.bfloat16)
    hn_ref[...] = _rmsnorm(xn_ref[...], ng_ref[...]).astype(jnp.bfloat16)

    def in_proj_chunks(h_ref, p_ref):
        def chunk(c0, c1):
            def run():
                p_ref[:, c0:c1] = jnp.dot(h_ref[...], win_ref[:, c0:c1], preferred_element_type=jnp.float32)
            return run
        edges = list(range(0, D_IN_PAD, IN_PROJ_CHUNK)) + [D_IN_PAD]
        return [chunk(c0, c1) for c0, c1 in zip(edges[:-1], edges[1:])]

    def out_proj_chunks(y_ref, r0, s):
        def chunk(c0, c1):
            def run():
                rows = slice(r0 + s * WINDOW, r0 + (s + 1) * WINDOW)
                xo_ref[rows, c0:c1] = x_ref[rows, c0:c1] + jnp.dot(
                    y_ref[s * WINDOW:(s + 1) * WINDOW, :], wout_ref[:, c0:c1],
                    preferred_element_type=jnp.float32)
            return run
        return [chunk(c0, c0 + OUT_PROJ_CHUNK) for c0 in range(0, D_MODEL, OUT_PROJ_CHUNK)]

    def final_norm(r0):
        if final:
            xo_ref[r0:r0 + half, :] = _rmsnorm(xo_ref[r0:r0 + half, :], fg_ref[...])

    lane = lax.broadcasted_iota(jnp.int32, (WINDOW, LANES), 1)
    low = lane < HALF
    low64 = lax.broadcasted_iota(jnp.int32, (GLA_CHUNK, LANES), 1) < HALF
    chunk_shift = GLA_CHUNK.bit_length() - 1
    rh = lax.broadcasted_iota(jnp.int32, (half, half), 0)
    ch = lax.broadcasted_iota(jnp.int32, (half, half), 1)
    cum_half = jnp.where((jnp.right_shift(rh, chunk_shift) == jnp.right_shift(ch, chunk_shift)) & (ch <= rh),
                         1.0, 0.0).astype(jnp.bfloat16)
    ri2 = lax.broadcasted_iota(jnp.int32, (LANES, 2 * LANES), 0)
    ci2 = lax.broadcasted_iota(jnp.int32, (LANES, 2 * LANES), 1) & (LANES - 1)
    bd_tril2 = ((ri2 >= GLA_CHUNK) == (ci2 >= GLA_CHUNK)) & (ci2 <= ri2)

    def mixers(p_ref, y_ref, s, first, pre):
        r0 = s * WINDOW
        rows = pl.ds(r0, WINDOW)

        ks = p_ref[rows, O_KA:O_KA + LANES] * (HD_A ** -0.5)
        k_new = [jnp.where(low, ks, 0.0).astype(jnp.bfloat16), jnp.where(low, 0.0, ks).astype(jnp.bfloat16)]
        v_new = p_ref[rows, O_VA:O_VA + LANES].T.astype(jnp.bfloat16)
        vcat = jnp.concatenate([carry["v"], v_new], axis=1)
        q4 = jnp.concatenate([p_ref[rows, O_QA + g * LANES:O_QA + (g + 1) * LANES].astype(jnp.bfloat16)
                              for g in range(G_A)], axis=0)
        o_heads = []
        for kv in range(KV_A):
            kcat = jnp.concatenate([carry["k"][kv], k_new[kv]], axis=0)
            sc = lax.dot_general(q4, kcat, _NT, preferred_element_type=jnp.float32) + bias_ref[first, kv]
            carry["k"][kv] = k_new[kv]
            yield
            ps, invs = [], []
            for g in range(G_A):
                p, inv = _softmax_sink(sc[g * WINDOW:(g + 1) * WINDOW], sink_ref[layer, kv * G_A + g])
                ps.append(p.astype(jnp.bfloat16))
                invs.append(inv)
                if g % 2 == 1:
                    yield
            o_t = lax.dot_general(vcat[kv * HD_A:(kv + 1) * HD_A, :], jnp.concatenate(ps, axis=0), _NT,
                                  preferred_element_type=jnp.float32)
            o_heads.append((o_t, invs))
            yield
        carry["v"] = v_new
        for g in range(G_A):
            o_pair = jnp.concatenate([o_heads[kv][0][:, g * WINDOW:(g + 1) * WINDOW] for kv in range(KV_A)],
                                     axis=0).T
            ya = o_pair * jnp.where(low, o_heads[0][1][g], o_heads[1][1][g])
            gate = p_ref[rows, O_GA + g * LANES:O_GA + (g + 1) * LANES]
            y_ref[rows, Y_A + g * LANES:Y_A + (g + 1) * LANES] = (ya * _silu(gate)).astype(jnp.bfloat16)
        yield

        for j in range(H_B // 2):
            mix = pre["mix"][j][:, s * LANES:(s + 1) * LANES] + bsf_ref[:, j * LANES:(j + 1) * LANES]
            u = p_ref[rows, O_UB + j * LANES:O_UB + (j + 1) * LANES]
            gate = p_ref[rows, O_GB + j * LANES:O_GB + (j + 1) * LANES]
            y_ref[rows, Y_B + j * LANES:Y_B + (j + 1) * LANES] = (u * mix * _silu(gate)).astype(jnp.bfloat16)
            if j % 2 == 1:
                yield

        bc = pre["bc"][r0:r0 + WINDOW]
        for c in range(WINDOW // GLA_CHUNK):
            rc = pl.ds(r0 + c * GLA_CHUNK, GLA_CHUNK)
            q_sts, k_sts, v_sts, kd_sts, decays, sts = [], [], [], [], [], []
            for p in range(H_C // 2):
                bp = bc[c * GLA_CHUNK:(c + 1) * GLA_CHUNK, p * LANES:(p + 1) * LANES]
                qp = p_ref[rc, O_QC + p * LANES:O_QC + (p + 1) * LANES]
                kp = p_ref[rc, O_KC + p * LANES:O_KC + (p + 1) * LANES]
                b_last = bp[GLA_CHUNK - 1:GLA_CHUNK, :]
                qd = qp * (DK_C ** -0.5) * jnp.exp(bp)
                ki = (kp * jnp.exp(-bp)).astype(jnp.bfloat16)
                kd = kp * jnp.exp(b_last - bp)
                q_sts.append(jnp.concatenate([jnp.where(low64, qd, 0.0), jnp.where(low64, 0.0, qd)],
                                             axis=0).astype(jnp.bfloat16))
                k_sts.append(jnp.concatenate([ki, ki], axis=0))
                v0 = p_ref[rc, O_VC + (2 * p) * DV_C:O_VC + (2 * p + 1) * DV_C]
                v1 = p_ref[rc, O_VC + (2 * p + 1) * DV_C:O_VC + (2 * p + 2) * DV_C]
                v_sts.append(jnp.concatenate([v0, v1], axis=0))
                kd_sts.append(jnp.concatenate([jnp.where(low64, kd, 0.0), jnp.where(low64, 0.0, kd)],
                                              axis=0).astype(jnp.bfloat16))
                decays.append(jnp.exp(b_last))
                yield
            zero = jnp.zeros((LANES, LANES), jnp.bfloat16)
            blockdiag = lambda a, b: jnp.concatenate([jnp.concatenate([a, zero], axis=1),
                                                      jnp.concatenate([zero, b], axis=1)], axis=0)
            q_all = jnp.concatenate(q_sts, axis=1)
            att = lax.dot_general(q_all, blockdiag(*k_sts), _NT, preferred_element_type=jnp.float32)
            att = jnp.where(bd_tril2, att, 0.0).astype(jnp.bfloat16)
            v_b = [v.astype(jnp.bfloat16) for v in v_sts]
            o_all = jnp.dot(att, blockdiag(*v_b), preferred_element_type=jnp.float32)
            d_all = jnp.dot(jnp.concatenate([v.T.astype(jnp.bfloat16) for v in v_sts], axis=1),
                            blockdiag(*kd_sts), preferred_element_type=jnp.float32)
            sts = carry["st"]
            o_inter = lax.dot_general(q_all, blockdiag(*[st.astype(jnp.bfloat16) for st in sts]), _NT,
                                      preferred_element_type=jnp.float32)
            o_all = o_all + o_inter
            carry["st"] = [sts[p] * decays[p] + d_all[:, p * LANES:(p + 1) * LANES] for p in range(H_C // 2)]
            for p in range(H_C // 2):
                o = o_all[:, p * DV_C:(p + 1) * DV_C]
                for hh in range(2):
                    hd = 2 * p + hh
                    oh = o[hh * GLA_CHUNK:(hh + 1) * GLA_CHUNK]
                    on = _rmsnorm(oh, gn_ref[:, hd * DV_C:(hd + 1) * DV_C])
                    gate = p_ref[rc, O_GC + hd * DV_C:O_GC + (hd + 1) * DV_C]
                    y_ref[rc, Y_C + hd * DV_C:Y_C + (hd + 1) * DV_C] = (on * _silu(gate)).astype(jnp.bfloat16)
                yield

    def half_pre(p_ref, pre):
        vn = _layernorm(p_ref[:, O_VB:O_VB + D_B], lng_ref[...], lnb_ref[...]).astype(jnp.bfloat16)
        yield
        pre["mix"] = []
        for j in range(H_B // 2):
            cols = []
            for s in range(n_sub):
                vp = vn[s * WINDOW:(s + 1) * WINDOW, j * LANES:(j + 1) * LANES]
                zero = jnp.zeros_like(vp)
                cols.append(jnp.concatenate([jnp.where(low, vp, zero), jnp.where(low, zero, vp)], axis=0))
            pre["mix"].append(jnp.dot(wm_ref[j], jnp.concatenate(cols, axis=1),
                                      preferred_element_type=jnp.float32))
            if j % 2 == 1:
                yield
        z = _bdot(p_ref[:, O_LR:O_LR + LANES], wup_ref[...]) + bup_ref[...]
        la = _log_sigmoid(z) * (1.0 / GLA_TAU)
        yield
        hi = la.astype(jnp.bfloat16)
        r1 = la - hi.astype(jnp.float32)
        mid = r1.astype(jnp.bfloat16)
        lo = (r1 - mid.astype(jnp.float32)).astype(jnp.bfloat16)
        c3 = jnp.dot(cum_half, jnp.concatenate([hi, mid, lo], axis=1), preferred_element_type=jnp.float32)
        pre["bc"] = (c3[:, 0:DK_TOT] + c3[:, DK_TOT:2 * DK_TOT]) + c3[:, 2 * DK_TOT:3 * DK_TOT]
        yield

    def half_mixers(p_ref, y_ref, first):
        pre = {}
        yield from half_pre(p_ref, pre)
        for s in range(n_sub):
            yield from mixers(p_ref, y_ref, s, first if s == 0 else 0, pre)

    def run_phase(sections, chunks, late_chunks):
        n_sections = PRE_SECTIONS + n_sub * MIXER_SECTIONS
        plan = [[] for _ in range(n_sections)]
        for idx, run in enumerate(chunks):
            plan[idx * n_sections // len(chunks)].append(run)
        for s, runs in late_chunks.items():
            k0 = PRE_SECTIONS + (s + 1) * MIXER_SECTIONS
            for idx, run in enumerate(runs):
                plan[k0 + idx * (n_sections - k0) // len(runs)].append(run)
        for k in range(n_sections):
            next(sections)
            for run in plan[k]:
                run()
        assert next(sections, None) is None

    carry = {"k": [kprev_ref[kv] for kv in range(KV_A)], "v": vprev_ref[...],
             "st": [st_ref[p] for p in range(H_C // 2)]}
    run_phase(half_mixers(pa_ref, ya_ref, jnp.where(i_id == 0, 1, 0)),
              in_proj_chunks(hb_ref, pb_ref),
              {s: out_proj_chunks(ya_ref, 0, s) for s in range(n_sub - 1)})
    run_phase(half_mixers(pb_ref, yb_ref, 0),
              out_proj_chunks(ya_ref, 0, n_sub - 1) + in_proj_chunks(hn_ref, pa_ref),
              {s: out_proj_chunks(yb_ref, half, s) for s in range(n_sub - 1)})
    final_norm(0)
    for run in out_proj_chunks(yb_ref, half, n_sub - 1):
        run()
    final_norm(half)
    for kv in range(KV_A):
        kprev_ref[kv] = carry["k"][kv]
    vprev_ref[...] = carry["v"]
    for p in range(H_C // 2):
        st_ref[p] = carry["st"][p]

    @pl.when(i_id == n_i - 1)
    def _():
        ko_ref[...] = pb_ref[half - WINDOW:half, O_KA:O_KA + LANES].T
        vo_ref[...] = pb_ref[half - WINDOW:half, O_VA:O_VA + LANES].T
        for p in range(H_C // 2):
            so_ref[p] = st_ref[p].T


def _prompt_layer(x, ng, win, wout, bias_tab, sinks, sw, bsf, lng, lnb, wup, bup, gn, fg, *, layer, final):
    bsz, seq, _ = x.shape
    tb = min(TOKEN_BLOCK, seq)
    half = tb // 2
    assert seq % tb == 0 and half % WINDOW == 0
    n_i = seq // tb
    grid = (bsz, n_i)

    def next_half(b, i):
        lin = jnp.minimum(b * n_i + i + 1, bsz * n_i - 1)
        return lin // n_i, (lin % n_i) * 2, 0

    def const_spec(shape):
        nd = len(shape)
        return pl.BlockSpec(shape, lambda b, i: (0,) * nd, pipeline_mode=pl.Buffered(1))

    def layer_spec(shape):
        nd = len(shape)
        return pl.BlockSpec((None,) + shape, lambda b, i: (layer,) + (0,) * nd, pipeline_mode=pl.Buffered(1))

    kern = functools.partial(_prompt_layer_kernel, tb=tb, layer=layer, final=final)
    return pl.pallas_call(
        kern,
        grid=grid,
        in_specs=[
            pl.BlockSpec((None, tb, D_MODEL), lambda b, i: (b, i, 0)),
            pl.BlockSpec((None, half, D_MODEL), next_half),
            layer_spec((1, D_MODEL)),
            layer_spec((D_MODEL, D_IN_PAD)),
            layer_spec((D_MIX, D_MODEL)),
            const_spec((2, KV_A, G_A * WINDOW, 2 * WINDOW)),
            pl.BlockSpec(memory_space=pltpu.SMEM),
            layer_spec((H_B, CHUNK_B, CHUNK_B)),
            layer_spec((CHUNK_B, D_B)),
            layer_spec((1, D_B)),
            layer_spec((1, D_B)),
            layer_spec((LANES, DK_TOT)),
            layer_spec((1, DK_TOT)),
            layer_spec((1, D_C)),
            const_spec((1, D_MODEL)),
        ],
        out_specs=[
            pl.BlockSpec((None, tb, D_MODEL), lambda b, i: (b, i, 0)),
            pl.BlockSpec((None, LANES, WINDOW), lambda b, i: (b, 0, 0)),
            pl.BlockSpec((None, LANES, WINDOW), lambda b, i: (b, 0, 0)),
            pl.BlockSpec((None, H_C // 2, LANES, DV_C), lambda b, i: (b, 0, 0, 0)),
        ],
        out_shape=[
            jax.ShapeDtypeStruct((bsz, seq, D_MODEL), jnp.float32),
            jax.ShapeDtypeStruct((bsz, LANES, WINDOW), jnp.float32),
            jax.ShapeDtypeStruct((bsz, LANES, WINDOW), jnp.float32),
            jax.ShapeDtypeStruct((bsz, H_C // 2, LANES, DV_C), jnp.float32),
        ],
        scratch_shapes=[
            pltpu.VMEM((half, D_IN_PAD), jnp.float32),
            pltpu.VMEM((half, D_IN_PAD), jnp.float32),
            pltpu.VMEM((half, D_MODEL), jnp.bfloat16),
            pltpu.VMEM((half, D_MODEL), jnp.bfloat16),
            pltpu.VMEM((half, D_MIX), jnp.bfloat16),
            pltpu.VMEM((half, D_MIX), jnp.bfloat16),
            pltpu.VMEM((KV_A, WINDOW, LANES), jnp.bfloat16),
            pltpu.VMEM((LANES, WINDOW), jnp.bfloat16),
            pltpu.VMEM((H_C // 2, DV_C, LANES), jnp.float32),
            pltpu.VMEM((H_B // 2, CHUNK_B, 2 * CHUNK_B), jnp.bfloat16),
        ],
        compiler_params=pltpu.CompilerParams(
            dimension_semantics=("arbitrary", "arbitrary"),
            vmem_limit_bytes=VMEM_LIMIT_BYTES),
        name="prompt_layer",
    )(x, x, ng, win, wout, bias_tab, sinks, sw, bsf, lng, lnb, wup, bup, gn, fg)


def _sample_kernel(x_ref, ng_ref, win_ref, wout_ref, biass_ref, sinkc_ref, w00_ref, b0_ref, lng_ref,
                   lnb_ref, wup_ref, bup_ref, gn_ref, fg_ref, bk_ref, bv_ref, s0_ref,
                   xo_ref, nk_ref, nv_ref, so_ref, cv_ref,
                   xcur_ref, proj_ref, y_ref, qs_ref, os_ref, qd_ref, ea_ref, oc_ref, *, nseq, gs):
    l_id = pl.program_id(0)
    i_id = pl.program_id(1)
    n_l = pl.num_programs(0)
    n_i = pl.num_programs(1)
    lane = lax.broadcasted_iota(jnp.int32, (nseq, LANES), 1)
    low = lane < HALF

    @pl.when((l_id == 0) & (i_id == 0))
    def _():
        xcur_ref[...] = x_ref[...]

    @pl.when(i_id == 0)
    def _():
        h = _rmsnorm(xcur_ref[...], ng_ref[...])
        proj_ref[...] = _bdot(h, win_ref[...])
        for g in range(G_A):
            qp = proj_ref[:, O_QA + g * LANES:O_QA + (g + 1) * LANES] * (HD_A ** -0.5)
            for kv in range(KV_A):
                keep = low if kv == 0 else jnp.logical_not(low)
                qs_ref[pl.ds(kv * G_A + g, nseq, stride=H_A), :] = jnp.where(keep, qp, 0.0)
        vn = _layernorm(proj_ref[:, O_VB:O_VB + D_B], lng_ref[...], lnb_ref[...])
        cv_ref[...] = vn
        mix = w00_ref[...] * vn + b0_ref[...]
        y_ref[:, Y_B:Y_B + D_B] = (proj_ref[:, O_UB:O_UB + D_B] * mix) * _silu(proj_ref[:, O_GB:O_GB + D_B])
        z = _bdot(proj_ref[:, O_LR:O_LR + LANES], wup_ref[...]) + bup_ref[...]
        la = _log_sigmoid(z) * (1.0 / GLA_TAU)
        ea = jnp.exp(la)
        qd = proj_ref[:, O_QC:O_QC + DK_TOT] * (DK_C ** -0.5) * ea
        ki = proj_ref[:, O_KC:O_KC + DK_TOT] * jnp.exp(-la)
        qd_ref[...] = qd
        ea_ref[...] = ea
        prod = qd * ki
        lane_c = lax.broadcasted_iota(jnp.int32, prod.shape, 1)
        for hd in range(H_C):
            in_head = (lane_c >= hd * DK_C) & (lane_c < (hd + 1) * DK_C)
            att = jnp.sum(jnp.where(in_head, prod, 0.0), axis=-1, keepdims=True)
            oc_ref[:, hd * DV_C:(hd + 1) * DV_C] = att * proj_ref[:, O_VC + hd * DV_C:O_VC + (hd + 1) * DV_C]

    g0 = pl.multiple_of(i_id * gs, gs)
    grows = pl.ds(g0, gs)

    pad_rows = jnp.zeros((LANES - gs, LANES), jnp.float32)
    k_cols = jnp.concatenate([proj_ref[grows, O_KA:O_KA + LANES], pad_rows], axis=0).T
    v_cols = jnp.concatenate([proj_ref[grows, O_VA:O_VA + LANES], pad_rows], axis=0).T
    newest = lax.broadcasted_iota(jnp.int32, (LANES, WINDOW), 1) == WINDOW - 1
    rows_g = pl.ds(pl.multiple_of(g0 * H_A, gs * H_A), gs * H_A)
    q_g = qs_ref[rows_g, :]
    sc_rows = []
    for j in range(gs):
        kt = jnp.where(newest, k_cols[:, j:j + 1], pltpu.roll(bk_ref[j], WINDOW - 1, 1))
        nk_ref[j] = kt
        nv_ref[j] = jnp.where(newest, v_cols[:, j:j + 1], pltpu.roll(bv_ref[j], WINDOW - 1, 1))
        sc_rows.append(_bdot(q_g[j * H_A:(j + 1) * H_A], kt))
    sc = jnp.concatenate(sc_rows, axis=0) + jnp.concatenate([biass_ref[...]] * gs, axis=0)
    p, inv = _softmax_sink(sc, jnp.concatenate([sinkc_ref[...]] * gs, axis=0))
    o_rows = [_bdot_nt(p[j * H_A:(j + 1) * H_A], nv_ref[j]) for j in range(gs)]
    os_ref[rows_g, :] = jnp.concatenate(o_rows, axis=0) * inv

    pad_c = jnp.zeros((LANES - gs, DK_TOT), jnp.float32)
    ea_cols = jnp.concatenate([ea_ref[grows, :], pad_c], axis=0).T
    k_cols_c = jnp.concatenate([proj_ref[grows, O_KC:O_KC + DK_TOT], pad_c], axis=0).T
    qd_g = qd_ref[grows, :]
    row_g = lax.broadcasted_iota(jnp.int32, (gs, LANES), 0)
    low_g = lax.broadcasted_iota(jnp.int32, (gs, LANES), 1) < HALF
    seqs_per_dot = 2 * LANES // DK_C
    for hd in range(H_C):
        v_h = proj_ref[grows, O_VC + hd * DV_C:O_VC + (hd + 1) * DV_C]
        q_pair = qd_g[:, (hd // 2) * LANES:(hd // 2 + 1) * LANES]
        q_other = pltpu.roll(q_pair, HALF, 1)
        q_lo, q_hi = (q_pair, q_other) if hd % 2 == 0 else (q_other, q_pair)
        acc = jnp.zeros((gs, DV_C), jnp.float32)
        for j0 in range(0, gs, seqs_per_dot):
            tiles, lhs = [], []
            for j in range(j0, j0 + seqs_per_dot):
                srows = pl.ds(j * DK_TOT + hd * DK_C, DK_C)
                s_old = s0_ref[srows, :]
                tiles.append(s_old.astype(jnp.bfloat16))
                crows = slice(hd * DK_C, (hd + 1) * DK_C)
                so_ref[srows, :] = ea_cols[crows, j:j + 1] * s_old + k_cols_c[crows, j:j + 1] * v_h[j:j + 1, :]
            for j in range(j0, j0 + seqs_per_dot, 2):
                lhs.append(jnp.where((row_g == j) & low_g, q_lo,
                                     jnp.where((row_g == j + 1) & jnp.logical_not(low_g), q_hi, 0.0)))
            acc = acc + _bdot(jnp.concatenate(lhs, axis=1), jnp.concatenate(tiles, axis=0))
        oc_ref[grows, hd * DV_C:(hd + 1) * DV_C] += acc

    @pl.when(i_id == n_i - 1)
    def _():
        for g in range(G_A):
            o0 = os_ref[pl.ds(g, nseq, stride=H_A), :]
            o1 = os_ref[pl.ds(G_A + g, nseq, stride=H_A), :]
            gate = proj_ref[:, O_GA + g * LANES:O_GA + (g + 1) * LANES]
            y_ref[:, Y_A + g * LANES:Y_A + (g + 1) * LANES] = jnp.where(low, o0, o1) * _silu(gate)
        for hd in range(H_C):
            on = _rmsnorm(oc_ref[:, hd * DV_C:(hd + 1) * DV_C], gn_ref[:, hd * DV_C:(hd + 1) * DV_C])
            gate = proj_ref[:, O_GC + hd * DV_C:O_GC + (hd + 1) * DV_C]
            y_ref[:, Y_C + hd * DV_C:Y_C + (hd + 1) * DV_C] = on * _silu(gate)
        x_new = xcur_ref[...] + _bdot(y_ref[...], wout_ref[...])
        xcur_ref[...] = x_new

        @pl.when(l_id == n_l - 1)
        def _():
            xo_ref[...] = _rmsnorm(x_new, fg_ref[...])


def _sample_path(x, ng, win, wout, bias_s, sink_c, w00, b0, lng, lnb, wup, bup, gn, fg, buf_kt, buf_vt, s0):
    depth = win.shape[0]
    nseq = x.shape[0]
    gs = min(SEQ_GROUP, nseq)
    assert nseq % gs == 0 and gs % 8 == 0
    grid = (depth, nseq // gs)

    def const_spec(shape):
        nd = len(shape)
        return pl.BlockSpec(shape, lambda l, i: (0,) * nd)

    def layer_spec(shape):
        nd = len(shape)
        return pl.BlockSpec((None,) + shape, lambda l, i: (l,) + (0,) * nd)

    state_spec = pl.BlockSpec((None, gs, LANES, WINDOW), lambda l, i: (l, i, 0, 0))
    gla_spec = pl.BlockSpec((None, gs * DK_TOT, DV_C), lambda l, i: (l, i, 0))
    kern = functools.partial(_sample_kernel, nseq=nseq, gs=gs)
    return pl.pallas_call(
        kern,
        grid=grid,
        in_specs=[
            const_spec((nseq, D_MODEL)),
            layer_spec((1, D_MODEL)),
            layer_spec((D_MODEL, D_IN_PAD)),
            layer_spec((D_MIX, D_MODEL)),
            const_spec((H_A, WINDOW)),
            layer_spec((H_A, 1)),
            layer_spec((1, D_B)),
            layer_spec((1, D_B)),
            layer_spec((1, D_B)),
            layer_spec((1, D_B)),
            layer_spec((LANES, DK_TOT)),
            layer_spec((1, DK_TOT)),
            layer_spec((1, D_C)),
            const_spec((1, D_MODEL)),
            state_spec, state_spec, gla_spec,
        ],
        out_specs=[
            const_spec((nseq, D_MODEL)),
            state_spec, state_spec, gla_spec,
            layer_spec((nseq, D_B)),
        ],
        out_shape=[
            jax.ShapeDtypeStruct((nseq, D_MODEL), jnp.float32),
            jax.ShapeDtypeStruct((depth, nseq, LANES, WINDOW), jnp.float32),
            jax.ShapeDtypeStruct((depth, nseq, LANES, WINDOW), jnp.float32),
            jax.ShapeDtypeStruct((depth, nseq * DK_TOT, DV_C), jnp.float32),
            jax.ShapeDtypeStruct((depth, nseq, D_B), jnp.float32),
        ],
        scratch_shapes=[
            pltpu.VMEM((nseq, D_MODEL), jnp.float32),
            pltpu.VMEM((nseq, D_IN_PAD), jnp.float32),
            pltpu.VMEM((nseq, D_MIX), jnp.float32),
            pltpu.VMEM((nseq * H_A, LANES), jnp.float32),
            pltpu.VMEM((nseq * H_A, LANES), jnp.float32),
            pltpu.VMEM((nseq, DK_TOT), jnp.float32),
            pltpu.VMEM((nseq, DK_TOT), jnp.float32),
            pltpu.VMEM((nseq, D_C), jnp.float32),
        ],
        compiler_params=pltpu.CompilerParams(
            dimension_semantics=("arbitrary", "arbitrary"),
            vmem_limit_bytes=VMEM_LIMIT_BYTES),
        name="sample_path",
    )(x, ng, win, wout, bias_s, sink_c, w00, b0, lng, lnb, wup, bup, gn, fg, buf_kt, buf_vt, s0)


def _window_minor(state):
    depth, n = state.shape[:2]
    return jnp.transpose(state, (0, 1, 3, 4, 2)).reshape(depth, n, LANES, WINDOW)


def _window_major(state_t):
    depth, n = state_t.shape[:2]
    return jnp.transpose(state_t.reshape(depth, n, KV_A, HD_A, WINDOW), (0, 1, 4, 2, 3))


def kernel(x_prompt, x_sample, state_swa_k, state_swa_v, state_gla, rel_bias, norm_g, w_in, sinks, spatial_w,
           spatial_b, chunk_ln_g, chunk_ln_b, gla_w_up, gla_b_up, gla_norm_g, w_out, final_norm_g):
    depth = w_in.shape[0]
    nseq = x_sample.shape[0]
    bsz = x_prompt.shape[0]
    f32 = jnp.float32

    win, wout = _prepare_weights(w_in, w_out)
    wup = jnp.pad(gla_w_up, ((0, 0), (0, LANES - GLA_RANK), (0, 0))).astype(jnp.bfloat16)
    bsf = jnp.repeat(jnp.swapaxes(spatial_b, 1, 2), DH_B, axis=2).astype(f32)
    w00 = jnp.repeat(spatial_w[:, :, 0, 0], DH_B, axis=1).astype(f32)[:, None, :]
    b0 = jnp.repeat(spatial_b[:, :, 0], DH_B, axis=1).astype(f32)[:, None, :]
    sink_c = sinks.astype(f32)[:, :, None]
    row = lambda a: a.astype(f32)[:, None, :]
    ng, lng, lnb, bup, gn = row(norm_g), row(chunk_ln_g), row(chunk_ln_b), row(gla_b_up), row(gla_norm_g)
    fg = final_norm_g.astype(f32)[None, :]
    sw = spatial_w.astype(f32)
    sinks = sinks.astype(f32)

    bias_tab = _bias_table(rel_bias)
    bias_s = bias_tab[0].reshape(H_A, WINDOW, 2 * WINDOW)[:, WINDOW - 1, WINDOW:]

    xp = x_prompt
    kp_l, vp_l, sp_l = [], [], []
    for l in range(depth):
        xp, kp, vp, sp = _prompt_layer(xp, ng, win, wout, bias_tab, sinks, sw, bsf, lng, lnb, wup, bup, gn, fg,
                                       layer=l, final=l == depth - 1)
        kp_l.append(kp); vp_l.append(vp); sp_l.append(sp)

    xs, ks_t, vs_t, ss, cv = _sample_path(
        x_sample.reshape(nseq, D_MODEL), ng, win, wout, bias_s, sink_c, w00, b0, lng, lnb, wup, bup, gn, fg,
        _window_minor(state_swa_k), _window_minor(state_swa_v), state_gla.reshape(depth, nseq * DK_TOT, DV_C))

    return (xp,
            xs.reshape(nseq, 1, D_MODEL),
            _window_major(jnp.stack(kp_l)),
            _window_major(jnp.stack(vp_l)),
            jnp.stack(sp_l).reshape(depth, bsz, H_C, DK_C, DV_C),
            _window_major(ks_t),
            _window_major(vs_t),
            ss.reshape(depth, nseq, H_C, DK_C, DV_C),
            cv.reshape(depth, nseq, 1, D_B))
```

```python
import functools

import numpy as np
import jax
import jax.numpy as jnp
from jax import lax
from jax.experimental import pallas as pl
from jax.experimental.pallas import tpu as pltpu

D_MODEL = 1024
D_A, HD_A, H_A, KV_A, G_A = 512, 64, 8, 2, 4
WINDOW, N_BUCKETS, MAX_DIST = 128, 32, 128
D_B, H_B, DH_B, CHUNK_B = 512, 8, 64, 128
D_C, H_C, DK_TOT, DK_C, DV_C = 512, 4, 256, 64, 128
GLA_RANK, GLA_TAU, GLA_CHUNK = 16, 16.0, 64
D_MIX = D_A + D_B + D_C
EPS = 1e-6
NEG = -1e30
SPLITS = [D_A, KV_A * HD_A, KV_A * HD_A, D_A, D_B, D_B, D_B, DK_TOT, DK_TOT, D_C, D_C, GLA_RANK]
D_IN = sum(SPLITS)

LANES = 128
HALF = LANES // 2
D_IN_PAD = ((D_IN + LANES - 1) // LANES) * LANES
VMEM_LIMIT_BYTES = 56 * 1024 * 1024

(O_QA, O_KA, O_VA, O_GA, O_UB, O_VB, O_GB, O_QC, O_KC, O_VC, O_GC, O_LR) = (
    int(o) for o in np.cumsum([0] + SPLITS[:-1]))
assert all(o % LANES == 0 for o in (O_QA, O_KA, O_VA, O_GA, O_UB, O_VB, O_GB, O_QC, O_KC, O_VC, O_GC, O_LR))
Y_A, Y_B, Y_C = 0, D_A, D_A + D_B

HEAD_PERM = [0, 4, 1, 5, 2, 6, 3, 7]

TOKEN_BLOCK = 512
SEQ_GROUP = 16
IN_PROJ_CHUNK = 256
OUT_PROJ_CHUNK = 256
MIXER_SECTIONS = 19
PRE_SECTIONS = 5
PREP_SPLIT = 2

_NT = (((1,), (1,)), ((), ()))


def _t5_bucket(dist):
    n = np.maximum(dist, 0)
    max_exact = N_BUCKETS // 2
    large = max_exact + (np.log(np.maximum(n, 1) / max_exact) / np.log(MAX_DIST / max_exact)
                         * (N_BUCKETS - max_exact)).astype(np.int32)
    large = np.minimum(large, N_BUCKETS - 1)
    return np.where(n < max_exact, n, large).astype(np.int32)


def _silu(x):
    return x * (1.0 / (1.0 + jnp.exp(-x)))


def _log_sigmoid(x):
    return jnp.minimum(x, 0.0) - jnp.log1p(jnp.exp(-jnp.abs(x)))


def _bdot(a, b):
    return jnp.dot(a.astype(jnp.bfloat16), b.astype(jnp.bfloat16), preferred_element_type=jnp.float32)


def _bdot_nt(a, b):
    return lax.dot_general(a.astype(jnp.bfloat16), b.astype(jnp.bfloat16), _NT,
                           preferred_element_type=jnp.float32)


def _rmsnorm(x, g):
    return x * lax.rsqrt(jnp.mean(x * x, axis=-1, keepdims=True) + EPS) * g


def _layernorm(v, g, b):
    mu = jnp.mean(v, axis=-1, keepdims=True)
    xc = v - mu
    var = jnp.mean(xc * xc, axis=-1, keepdims=True)
    return xc * lax.rsqrt(var + EPS) * g + b


def _softmax_sink(s, sink):
    m = jnp.maximum(jnp.max(s, axis=-1, keepdims=True), sink)
    p = jnp.exp(s - m)
    den = jnp.sum(p, axis=-1, keepdims=True) + jnp.exp(sink - m)
    return p, 1.0 / den


def _bias_table_kernel(rb_ref, bucket_ref, band_ref, out_ref):
    bucket = bucket_ref[...]
    band = band_ref[...] > 0
    own = lax.broadcasted_iota(jnp.int32, bucket.shape, 1) >= WINDOW
    for h in range(H_A):
        acc = jnp.zeros(bucket.shape, jnp.float32)
        for b in range(N_BUCKETS):
            acc = jnp.where(bucket == b, rb_ref[b, h], acc)
        kv, g = divmod(h, G_A)
        rows = pl.ds(g * WINDOW, WINDOW)
        out_ref[0, kv, rows, :] = jnp.where(band, acc, NEG)
        out_ref[1, kv, rows, :] = jnp.where(band & own, acc, NEG)


def _bias_table(rel_bias):
    i = np.arange(WINDOW)[:, None]
    j = np.arange(2 * WINDOW)[None, :]
    dist = i + WINDOW - j
    band = ((dist >= 0) & (dist < WINDOW)).astype(np.int32)
    return pl.pallas_call(
        _bias_table_kernel,
        out_shape=jax.ShapeDtypeStruct((2, KV_A, G_A * WINDOW, 2 * WINDOW), jnp.float32),
        in_specs=[pl.BlockSpec(memory_space=pltpu.SMEM),
                  pl.BlockSpec(memory_space=pltpu.VMEM),
                  pl.BlockSpec(memory_space=pltpu.VMEM)],
        out_specs=pl.BlockSpec(memory_space=pltpu.VMEM),
        name="rel_bias_table",
    )(rel_bias.astype(jnp.float32), jnp.asarray(_t5_bucket(dist)), jnp.asarray(band))


def _win_prep_kernel(wt_ref, lr_ref, o_ref):
    c = pl.program_id(1)
    tiles = wt_ref.shape[0] // LANES
    for cc in range(PREP_SPLIT):
        @pl.when(c == cc)
        def _(cc=cc):
            for t in range(tiles):
                col = (cc * tiles + t) * LANES
                base = next((b for b in (O_QA, O_GA) if b <= col < b + D_A), None)
                if base is None:
                    src = wt_ref[t * LANES:(t + 1) * LANES, :]
                else:
                    g = (col - base) // LANES
                    r0 = base - cc * tiles * LANES
                    assert r0 >= 0 and r0 + D_A <= wt_ref.shape[0]
                    src = jnp.concatenate([wt_ref[r0 + g * HD_A:r0 + (g + 1) * HD_A, :],
                                           wt_ref[r0 + (G_A + g) * HD_A:r0 + (G_A + g + 1) * HD_A, :]], axis=0)
                o_ref[:, col:col + LANES] = src.T.astype(jnp.bfloat16)
            if cc == PREP_SPLIT - 1:
                o_ref[:, O_LR:D_IN_PAD] = lr_ref[...].T.astype(jnp.bfloat16)


def _wout_prep_kernel(w_ref, o_ref):
    for j, h in enumerate(HEAD_PERM):
        o_ref[j * HD_A:(j + 1) * HD_A, :] = w_ref[h * HD_A:(h + 1) * HD_A, :].astype(jnp.bfloat16)
    o_ref[Y_B:, :] = w_ref[Y_B:, :].astype(jnp.bfloat16)


def _prepare_weights(w_in, w_out):
    depth = w_in.shape[0]
    cols = D_MODEL // PREP_SPLIT
    wt = jnp.swapaxes(w_in, 1, 2)
    assert O_LR % (PREP_SPLIT * LANES) == 0
    rows = O_LR // PREP_SPLIT
    wt_lr = jnp.pad(wt[:, O_LR:, :], ((0, 0), (0, D_IN_PAD - D_IN), (0, 0)))
    win = pl.pallas_call(
        _win_prep_kernel,
        grid=(depth, PREP_SPLIT),
        in_specs=[pl.BlockSpec((None, rows, D_MODEL), lambda l, r: (l, r, 0)),
                  pl.BlockSpec((None, LANES, D_MODEL), lambda l, r: (l, 0, 0))],
        out_specs=pl.BlockSpec((None, D_MODEL, D_IN_PAD), lambda l, r: (l, 0, 0)),
        out_shape=jax.ShapeDtypeStruct((depth, D_MODEL, D_IN_PAD), jnp.bfloat16),
        compiler_params=pltpu.CompilerParams(dimension_semantics=("arbitrary", "arbitrary"),
                                             vmem_limit_bytes=VMEM_LIMIT_BYTES),
        name="w_in_prep",
    )(wt, wt_lr)
    wout = pl.pallas_call(
        _wout_prep_kernel,
        grid=(depth, PREP_SPLIT),
        in_specs=[pl.BlockSpec((None, D_MIX, cols), lambda l, c: (l, 0, c))],
        out_specs=pl.BlockSpec((None, D_MIX, cols), lambda l, c: (l, 0, c)),
        out_shape=jax.ShapeDtypeStruct((depth, D_MIX, D_MODEL), jnp.bfloat16),
        compiler_params=pltpu.CompilerParams(dimension_semantics=("arbitrary", "arbitrary"),
                                             vmem_limit_bytes=VMEM_LIMIT_BYTES),
        name="w_out_prep",
    )(w_out)
    return win, wout


def _prompt_layer_kernel(x_ref, xn_ref, ng_ref, win_ref, wout_ref, bias_ref, sink_ref, sw_ref, bsf_ref, lng_ref,
                         lnb_ref, wup_ref, bup_ref, gn_ref, fg_ref,
                         xo_ref, ko_ref, vo_ref, so_ref,
                         pa_ref, pb_ref, hb_ref, hn_ref, ya_ref, yb_ref, kprev_ref, vprev_ref, st_ref, wm_ref,
                         *, tb, layer, final):
    b_id = pl.program_id(0)
    i_id = pl.program_id(1)
    n_i = pl.num_programs(1)
    half = tb // 2
    n_sub = half // WINDOW

    @pl.when((b_id == 0) & (i_id == 0))
    def _():
        r = lax.broadcasted_iota(jnp.int32, (CHUNK_B, CHUNK_B), 0)
        c = lax.broadcasted_iota(jnp.int32, (CHUNK_B, CHUNK_B), 1)
        for h in range(H_B):
            wm_ref[h // 2, :, (h % 2) * CHUNK_B:(h % 2 + 1) * CHUNK_B] = jnp.where(
                c <= r, sw_ref[h], 0.0).astype(jnp.bfloat16)
        pa_ref[...] = _bdot(_rmsnorm(x_ref[0:half, :], ng_ref[...]), win_ref[...])

    @pl.when(i_id == 0)
    def _():
        kprev_ref[...] = jnp.zeros_like(kprev_ref)
        vprev_ref[...] = jnp.zeros_like(vprev_ref)
        st_ref[...] = jnp.zeros_like(st_ref)

    hb_ref[...] = _rmsnorm(x_ref[half:tb, :], ng_ref[...]).astype(jnp.bfloat16)
    hn_ref[...] = _rmsnorm(xn_ref[...], ng_ref[...]).astype(jnp.bfloat16)

    def in_proj_chunks(h_ref, p_ref):
        def chunk(c0, c1):
            def run():
                p_ref[:, c0:c1] = jnp.dot(h_ref[...], win_ref[:, c0:c1], preferred_element_type=jnp.float32)
            return run
        edges = list(range(0, D_IN_PAD, IN_PROJ_CHUNK)) + [D_IN_PAD]
        return [chunk(c0, c1) for c0, c1 in zip(edges[:-1], edges[1:])]

    def out_proj_chunks(y_ref, r0, s):
        def chunk(c0, c1):
            def run():
                rows = slice(r0 + s * WINDOW, r0 + (s + 1) * WINDOW)
                xo_ref[rows, c0:c1] = x_ref[rows, c0:c1] + jnp.dot(
                    y_ref[s * WINDOW:(s + 1) * WINDOW, :], wout_ref[:, c0:c1],
                    preferred_element_type=jnp.float32)
            return run
        return [chunk(c0, c0 + OUT_PROJ_CHUNK) for c0 in range(0, D_MODEL, OUT_PROJ_CHUNK)]

    def final_norm(r0):
        if final:
            xo_ref[r0:r0 + half, :] = _rmsnorm(xo_ref[r0:r0 + half, :], fg_ref[...])

    lane = lax.broadcasted_iota(jnp.int32, (WINDOW, LANES), 1)
    low = lane < HALF
    low64 = lax.broadcasted_iota(jnp.int32, (GLA_CHUNK, LANES), 1) < HALF
    chunk_shift = GLA_CHUNK.bit_length() - 1
    rh = lax.broadcasted_iota(jnp.int32, (half, half), 0)
    ch = lax.broadcasted_iota(jnp.int32, (half, half), 1)
    cum_half = jnp.where((jnp.right_shift(rh, chunk_shift) == jnp.right_shift(ch, chunk_shift)) & (ch <= rh),
                         1.0, 0.0).astype(jnp.bfloat16)
    ri2 = lax.broadcasted_iota(jnp.int32, (LANES, 2 * LANES), 0)
    ci2 = lax.broadcasted_iota(jnp.int32, (LANES, 2 * LANES), 1) & (LANES - 1)
    bd_tril2 = ((ri2 >= GLA_CHUNK) == (ci2 >= GLA_CHUNK)) & (ci2 <= ri2)

    def mixers(p_ref, y_ref, s, first, pre):
        r0 = s * WINDOW
        rows = pl.ds(r0, WINDOW)

        ks = p_ref[rows, O_KA:O_KA + LANES] * (HD_A ** -0.5)
        k_new = [jnp.where(low, ks, 0.0).astype(jnp.bfloat16), jnp.where(low, 0.0, ks).astype(jnp.bfloat16)]
        v_new = p_ref[rows, O_VA:O_VA + LANES].T.astype(jnp.bfloat16)
        vcat = jnp.concatenate([carry["v"], v_new], axis=1)
        q4 = jnp.concatenate([p_ref[rows, O_QA + g * LANES:O_QA + (g + 1) * LANES].astype(jnp.bfloat16)
                              for g in range(G_A)], axis=0)
        o_heads = []
        for kv in range(KV_A):
            kcat = jnp.concatenate([carry["k"][kv], k_new[kv]], axis=0)
            sc = lax.dot_general(q4, kcat, _NT, preferred_element_type=jnp.float32) + bias_ref[first, kv]
            carry["k"][kv] = k_new[kv]
            yield
            ps, invs = [], []
            for g in range(G_A):
                p, inv = _softmax_sink(sc[g * WINDOW:(g + 1) * WINDOW], sink_ref[layer, kv * G_A + g])
                ps.append(p.astype(jnp.bfloat16))
                invs.append(inv)
                if g % 2 == 1:
                    yield
            o_t = lax.dot_general(vcat[kv * HD_A:(kv + 1) * HD_A, :], jnp.concatenate(ps, axis=0), _NT,
                                  preferred_element_type=jnp.float32)
            o_heads.append((o_t, invs))
            yield
        carry["v"] = v_new
        for g in range(G_A):
            o_pair = jnp.concatenate([o_heads[kv][0][:, g * WINDOW:(g + 1) * WINDOW] for kv in range(KV_A)],
                                     axis=0).T
            ya = o_pair * jnp.where(low, o_heads[0][1][g], o_heads[1][1][g])
            gate = p_ref[rows, O_GA + g * LANES:O_GA + (g + 1) * LANES]
            y_ref[rows, Y_A + g * LANES:Y_A + (g + 1) * LANES] = (ya * _silu(gate)).astype(jnp.bfloat16)
        yield

        for j in range(H_B // 2):
            mix = pre["mix"][j][:, s * LANES:(s + 1) * LANES] + bsf_ref[:, j * LANES:(j + 1) * LANES]
            u = p_ref[rows, O_UB + j * LANES:O_UB + (j + 1) * LANES]
            gate = p_ref[rows, O_GB + j * LANES:O_GB + (j + 1) * LANES]
            y_ref[rows, Y_B + j * LANES:Y_B + (j + 1) * LANES] = (u * mix * _silu(gate)).astype(jnp.bfloat16)
            if j % 2 == 1:
                yield

        bc = pre["bc"][r0:r0 + WINDOW]
        for c in range(WINDOW // GLA_CHUNK):
            rc = pl.ds(r0 + c * GLA_CHUNK, GLA_CHUNK)
            q_sts, k_sts, v_sts, kd_sts, decays, sts = [], [], [], [], [], []
            for p in range(H_C // 2):
                bp = bc[c * GLA_CHUNK:(c + 1) * GLA_CHUNK, p * LANES:(p + 1) * LANES]
                qp = p_ref[rc, O_QC + p * LANES:O_QC + (p + 1) * LANES]
                kp = p_ref[rc, O_KC + p * LANES:O_KC + (p + 1) * LANES]
                b_last = bp[GLA_CHUNK - 1:GLA_CHUNK, :]
                qd = qp * (DK_C ** -0.5) * jnp.exp(bp)
                ki = (kp * jnp.exp(-bp)).astype(jnp.bfloat16)
                kd = kp * jnp.exp(b_last - bp)
                q_sts.append(jnp.concatenate([jnp.where(low64, qd, 0.0), jnp.where(low64, 0.0, qd)],
                                             axis=0).astype(jnp.bfloat16))
                k_sts.append(jnp.concatenate([ki, ki], axis=0))
                v0 = p_ref[rc, O_VC + (2 * p) * DV_C:O_VC + (2 * p + 1) * DV_C]
                v1 = p_ref[rc, O_VC + (2 * p + 1) * DV_C:O_VC + (2 * p + 2) * DV_C]
                v_sts.append(jnp.concatenate([v0, v1], axis=0))
                kd_sts.append(jnp.concatenate([jnp.where(low64, kd, 0.0), jnp.where(low64, 0.0, kd)],
                                              axis=0).astype(jnp.bfloat16))
                decays.append(jnp.exp(b_last))
                yield
            zero = jnp.zeros((LANES, LANES), jnp.bfloat16)
            blockdiag = lambda a, b: jnp.concatenate([jnp.concatenate([a, zero], axis=1),
                                                      jnp.concatenate([zero, b], axis=1)], axis=0)
            q_all = jnp.concatenate(q_sts, axis=1)
            att = lax.dot_general(q_all, blockdiag(*k_sts), _NT, preferred_element_type=jnp.float32)
            att = jnp.where(bd_tril2, att, 0.0).astype(jnp.bfloat16)
            v_b = [v.astype(jnp.bfloat16) for v in v_sts]
            o_all = jnp.dot(att, blockdiag(*v_b), preferred_element_type=jnp.float32)
            d_all = jnp.dot(jnp.concatenate([v.T.astype(jnp.bfloat16) for v in v_sts], axis=1),
                            blockdiag(*kd_sts), preferred_element_type=jnp.float32)
            sts = carry["st"]
            o_inter = lax.dot_general(q_all, blockdiag(*[st.astype(jnp.bfloat16) for st in sts]), _NT,
                                      preferred_element_type=jnp.float32)
            o_all = o_all + o_inter
            carry["st"] = [sts[p] * decays[p] + d_all[:, p * LANES:(p + 1) * LANES] for p in range(H_C // 2)]
            for p in range(H_C // 2):
                o = o_all[:, p * DV_C:(p + 1) * DV_C]
                for hh in range(2):
                    hd = 2 * p + hh
                    oh = o[hh * GLA_CHUNK:(hh + 1) * GLA_CHUNK]
                    on = _rmsnorm(oh, gn_ref[:, hd * DV_C:(hd + 1) * DV_C])
                    gate = p_ref[rc, O_GC + hd * DV_C:O_GC + (hd + 1) * DV_C]
                    y_ref[rc, Y_C + hd * DV_C:Y_C + (hd + 1) * DV_C] = (on * _silu(gate)).astype(jnp.bfloat16)
                yield

    def half_pre(p_ref, pre):
        vn = _layernorm(p_ref[:, O_VB:O_VB + D_B], lng_ref[...], lnb_ref[...]).astype(jnp.bfloat16)
        yield
        pre["mix"] = []
        for j in range(H_B // 2):
            cols = []
            for s in range(n_sub):
                vp = vn[s * WINDOW:(s + 1) * WINDOW, j * LANES:(j + 1) * LANES]
                zero = jnp.zeros_like(vp)
                cols.append(jnp.concatenate([jnp.where(low, vp, zero), jnp.where(low, zero, vp)], axis=0))
            pre["mix"].append(jnp.dot(wm_ref[j], jnp.concatenate(cols, axis=1),
                                      preferred_element_type=jnp.float32))
            if j % 2 == 1:
                yield
        z = _bdot(p_ref[:, O_LR:O_LR + LANES], wup_ref[...]) + bup_ref[...]
        la = _log_sigmoid(z) * (1.0 / GLA_TAU)
        yield
        hi = la.astype(jnp.bfloat16)
        r1 = la - hi.astype(jnp.float32)
        mid = r1.astype(jnp.bfloat16)
        lo = (r1 - mid.astype(jnp.float32)).astype(jnp.bfloat16)
        c3 = jnp.dot(cum_half, jnp.concatenate([hi, mid, lo], axis=1), preferred_element_type=jnp.float32)
        pre["bc"] = (c3[:, 0:DK_TOT] + c3[:, DK_TOT:2 * DK_TOT]) + c3[:, 2 * DK_TOT:3 * DK_TOT]
        yield

    def half_mixers(p_ref, y_ref, first):
        pre = {}
        yield from half_pre(p_ref, pre)
        for s in range(n_sub):
            yield from mixers(p_ref, y_ref, s, first if s == 0 else 0, pre)

    def run_phase(sections, chunks, late_chunks):
        n_sections = PRE_SECTIONS + n_sub * MIXER_SECTIONS
        plan = [[] for _ in range(n_sections)]
        for idx, run in enumerate(chunks):
            plan[idx * n_sections // len(chunks)].append(run)
        for s, runs in late_chunks.items():
            k0 = PRE_SECTIONS + (s + 1) * MIXER_SECTIONS
            for idx, run in enumerate(runs):
                plan[k0 + idx * (n_sections - k0) // len(runs)].append(run)
        for k in range(n_sections):
            next(sections)
            for run in plan[k]:
                run()
        assert next(sections, None) is None

    carry = {"k": [kprev_ref[kv] for kv in range(KV_A)], "v": vprev_ref[...],
             "st": [st_ref[p] for p in range(H_C // 2)]}
    run_phase(half_mixers(pa_ref, ya_ref, jnp.where(i_id == 0, 1, 0)),
              in_proj_chunks(hb_ref, pb_ref),
              {s: out_proj_chunks(ya_ref, 0, s) for s in range(n_sub - 1)})
    run_phase(half_mixers(pb_ref, yb_ref, 0),
              out_proj_chunks(ya_ref, 0, n_sub - 1) + in_proj_chunks(hn_ref, pa_ref),
              {s: out_proj_chunks(yb_ref, half, s) for s in range(n_sub - 1)})
    final_norm(0)
    for run in out_proj_chunks(yb_ref, half, n_sub - 1):
        run()
    final_norm(half)
    for kv in range(KV_A):
        kprev_ref[kv] = carry["k"][kv]
    vprev_ref[...] = carry["v"]
    for p in range(H_C // 2):
        st_ref[p] = carry["st"][p]

    @pl.when(i_id == n_i - 1)
    def _():
        ko_ref[...] = pb_ref[half - WINDOW:half, O_KA:O_KA + LANES].T
        vo_ref[...] = pb_ref[half - WINDOW:half, O_VA:O_VA + LANES].T
        for p in range(H_C // 2):
            so_ref[p] = st_ref[p].T


def _prompt_layer(x, ng, win, wout, bias_tab, sinks, sw, bsf, lng, lnb, wup, bup, gn, fg, *, layer, final):
    bsz, seq, _ = x.shape
    tb = min(TOKEN_BLOCK, seq)
    half = tb // 2
    assert seq % tb == 0 and half % WINDOW == 0
    n_i = seq // tb
    grid = (bsz, n_i)

    def next_half(b, i):
        lin = jnp.minimum(b * n_i + i + 1, bsz * n_i - 1)
        return lin // n_i, (lin % n_i) * 2, 0

    def const_spec(shape):
        nd = len(shape)
        return pl.BlockSpec(shape, lambda b, i: (0,) * nd, pipeline_mode=pl.Buffered(1))

    def layer_spec(shape):
        nd = len(shape)
        return pl.BlockSpec((None,) + shape, lambda b, i: (layer,) + (0,) * nd, pipeline_mode=pl.Buffered(1))

    kern = functools.partial(_prompt_layer_kernel, tb=tb, layer=layer, final=final)
    return pl.pallas_call(
        kern,
        grid=grid,
        in_specs=[
            pl.BlockSpec((None, tb, D_MODEL), lambda b, i: (b, i, 0)),
            pl.BlockSpec((None, half, D_MODEL), next_half),
            layer_spec((1, D_MODEL)),
            layer_spec((D_MODEL, D_IN_PAD)),
            layer_spec((D_MIX, D_MODEL)),
            const_spec((2, KV_A, G_A * WINDOW, 2 * WINDOW)),
            pl.BlockSpec(memory_space=pltpu.SMEM),
            layer_spec((H_B, CHUNK_B, CHUNK_B)),
            layer_spec((CHUNK_B, D_B)),
            layer_spec((1, D_B)),
            layer_spec((1, D_B)),
            layer_spec((LANES, DK_TOT)),
            layer_spec((1, DK_TOT)),
            layer_spec((1, D_C)),
            const_spec((1, D_MODEL)),
        ],
        out_specs=[
            pl.BlockSpec((None, tb, D_MODEL), lambda b, i: (b, i, 0)),
            pl.BlockSpec((None, LANES, WINDOW), lambda b, i: (b, 0, 0)),
            pl.BlockSpec((None, LANES, WINDOW), lambda b, i: (b, 0, 0)),
            pl.BlockSpec((None, H_C // 2, LANES, DV_C), lambda b, i: (b, 0, 0, 0)),
        ],
        out_shape=[
            jax.ShapeDtypeStruct((bsz, seq, D_MODEL), jnp.float32),
            jax.ShapeDtypeStruct((bsz, LANES, WINDOW), jnp.float32),
            jax.ShapeDtypeStruct((bsz, LANES, WINDOW), jnp.float32),
            jax.ShapeDtypeStruct((bsz, H_C // 2, LANES, DV_C), jnp.float32),
        ],
        scratch_shapes=[
            pltpu.VMEM((half, D_IN_PAD), jnp.float32),
            pltpu.VMEM((half, D_IN_PAD), jnp.float32),
            pltpu.VMEM((half, D_MODEL), jnp.bfloat16),
            pltpu.VMEM((half, D_MODEL), jnp.bfloat16),
            pltpu.VMEM((half, D_MIX), jnp.bfloat16),
            pltpu.VMEM((half, D_MIX), jnp.bfloat16),
            pltpu.VMEM((KV_A, WINDOW, LANES), jnp.bfloat16),
            pltpu.VMEM((LANES, WINDOW), jnp.bfloat16),
            pltpu.VMEM((H_C // 2, DV_C, LANES), jnp.float32),
            pltpu.VMEM((H_B // 2, CHUNK_B, 2 * CHUNK_B), jnp.bfloat16),
        ],
        compiler_params=pltpu.CompilerParams(
            dimension_semantics=("arbitrary", "arbitrary"),
            vmem_limit_bytes=VMEM_LIMIT_BYTES),
        name="prompt_layer",
    )(x, x, ng, win, wout, bias_tab, sinks, sw, bsf, lng, lnb, wup, bup, gn, fg)


def _sample_kernel(x_ref, ng_ref, win_ref, wout_ref, biass_ref, sinkc_ref, w00_ref, b0_ref, lng_ref,
                   lnb_ref, wup_ref, bup_ref, gn_ref, fg_ref, bk_ref, bv_ref, s0_ref,
                   xo_ref, nk_ref, nv_ref, so_ref, cv_ref,
                   xcur_ref, proj_ref, y_ref, qs_ref, os_ref, qd_ref, ea_ref, oc_ref, *, nseq, gs):
    l_id = pl.program_id(0)
    i_id = pl.program_id(1)
    n_l = pl.num_programs(0)
    n_i = pl.num_programs(1)
    lane = lax.broadcasted_iota(jnp.int32, (nseq, LANES), 1)
    low = lane < HALF

    @pl.when((l_id == 0) & (i_id == 0))
    def _():
        xcur_ref[...] = x_ref[...]

    @pl.when(i_id == 0)
    def _():
        h = _rmsnorm(xcur_ref[...], ng_ref[...])
        proj_ref[...] = _bdot(h, win_ref[...])
        for g in range(G_A):
            qp = proj_ref[:, O_QA + g * LANES:O_QA + (g + 1) * LANES] * (HD_A ** -0.5)
            for kv in range(KV_A):
                keep = low if kv == 0 else jnp.logical_not(low)
                qs_ref[pl.ds(kv * G_A + g, nseq, stride=H_A), :] = jnp.where(keep, qp, 0.0)
        vn = _layernorm(proj_ref[:, O_VB:O_VB + D_B], lng_ref[...], lnb_ref[...])
        cv_ref[...] = vn
        mix = w00_ref[...] * vn + b0_ref[...]
        y_ref[:, Y_B:Y_B + D_B] = (proj_ref[:, O_UB:O_UB + D_B] * mix) * _silu(proj_ref[:, O_GB:O_GB + D_B])
        z = _bdot(proj_ref[:, O_LR:O_LR + LANES], wup_ref[...]) + bup_ref[...]
        la = _log_sigmoid(z) * (1.0 / GLA_TAU)
        ea = jnp.exp(la)
        qd = proj_ref[:, O_QC:O_QC + DK_TOT] * (DK_C ** -0.5) * ea
        ki = proj_ref[:, O_KC:O_KC + DK_TOT] * jnp.exp(-la)
        qd_ref[...] = qd
        ea_ref[...] = ea
        prod = qd * ki
        lane_c = lax.broadcasted_iota(jnp.int32, prod.shape, 1)
        for hd in range(H_C):
            in_head = (lane_c >= hd * DK_C) & (lane_c < (hd + 1) * DK_C)
            att = jnp.sum(jnp.where(in_head, prod, 0.0), axis=-1, keepdims=True)
            oc_ref[:, hd * DV_C:(hd + 1) * DV_C] = att * proj_ref[:, O_VC + hd * DV_C:O_VC + (hd + 1) * DV_C]

    g0 = pl.multiple_of(i_id * gs, gs)
    grows = pl.ds(g0, gs)

    pad_rows = jnp.zeros((LANES - gs, LANES), jnp.float32)
    k_cols = jnp.concatenate([proj_ref[grows, O_KA:O_KA + LANES], pad_rows], axis=0).T
    v_cols = jnp.concatenate([proj_ref[grows, O_VA:O_VA + LANES], pad_rows], axis=0).T
    newest = lax.broadcasted_iota(jnp.int32, (LANES, WINDOW), 1) == WINDOW - 1
    rows_g = pl.ds(pl.multiple_of(g0 * H_A, gs * H_A), gs * H_A)
    q_g = qs_ref[rows_g, :]
    sc_rows = []
    for j in range(gs):
        kt = jnp.where(newest, k_cols[:, j:j + 1], pltpu.roll(bk_ref[j], WINDOW - 1, 1))
        nk_ref[j] = kt
        nv_ref[j] = jnp.where(newest, v_cols[:, j:j + 1], pltpu.roll(bv_ref[j], WINDOW - 1, 1))
        sc_rows.append(_bdot(q_g[j * H_A:(j + 1) * H_A], kt))
    sc = jnp.concatenate(sc_rows, axis=0) + jnp.concatenate([biass_ref[...]] * gs, axis=0)
    p, inv = _softmax_sink(sc, jnp.concatenate([sinkc_ref[...]] * gs, axis=0))
    o_rows = [_bdot_nt(p[j * H_A:(j + 1) * H_A], nv_ref[j]) for j in range(gs)]
    os_ref[rows_g, :] = jnp.concatenate(o_rows, axis=0) * inv

    pad_c = jnp.zeros((LANES - gs, DK_TOT), jnp.float32)
    ea_cols = jnp.concatenate([ea_ref[grows, :], pad_c], axis=0).T
    kc_g = proj_ref[grows, O_KC:O_KC + DK_TOT]
    qd_g = qd_ref[grows, :]
    row_g = lax.broadcasted_iota(jnp.int32, (gs, LANES), 0)
    low_g = lax.broadcasted_iota(jnp.int32, (gs, LANES), 1) < HALF
    seqs_per_dot = 2 * LANES // DK_C
    for hd in range(H_C):
        v_h = proj_ref[grows, O_VC + hd * DV_C:O_VC + (hd + 1) * DV_C]
        q_pair = qd_g[:, (hd // 2) * LANES:(hd // 2 + 1) * LANES]
        q_other = pltpu.roll(q_pair, HALF, 1)
        q_lo, q_hi = (q_pair, q_other) if hd % 2 == 0 else (q_other, q_pair)
        k_pair = kc_g[:, (hd // 2) * LANES:(hd // 2 + 1) * LANES]
        k_other = pltpu.roll(k_pair, HALF, 1)
        k_lo, k_hi = (k_pair, k_other) if hd % 2 == 0 else (k_other, k_pair)
        acc = jnp.zeros((gs, DV_C), jnp.float32)
        for j0 in range(0, gs, seqs_per_dot):
            lhs, lhs_k = [], []
            for j in range(j0, j0 + seqs_per_dot, 2):
                sel_lo = (row_g == j) & low_g
                sel_hi = (row_g == j + 1) & jnp.logical_not(low_g)
                lhs.append(jnp.where(sel_lo, q_lo, jnp.where(sel_hi, q_hi, 0.0)))
                lhs_k.append(jnp.where(sel_lo, k_lo, jnp.where(sel_hi, k_hi, 0.0)))
            d_s = lax.dot_general(jnp.concatenate(lhs_k, axis=1).astype(jnp.bfloat16), v_h.astype(jnp.bfloat16),
                                  (((0,), (0,)), ((), ())), preferred_element_type=jnp.float32)
            tiles = []
            for j in range(j0, j0 + seqs_per_dot):
                srows = pl.ds(j * DK_TOT + hd * DK_C, DK_C)
                s_old = s0_ref[srows, :]
                tiles.append(s_old.astype(jnp.bfloat16))
                crows = slice(hd * DK_C, (hd + 1) * DK_C)
                so_ref[srows, :] = (ea_cols[crows, j:j + 1] * s_old
                                    + d_s[(j - j0) * DK_C:(j - j0 + 1) * DK_C])
            acc = acc + _bdot(jnp.concatenate(lhs, axis=1), jnp.concatenate(tiles, axis=0))
        oc_ref[grows, hd * DV_C:(hd + 1) * DV_C] += acc

    @pl.when(i_id == n_i - 1)
    def _():
        for g in range(G_A):
            o0 = os_ref[pl.ds(g, nseq, stride=H_A), :]
            o1 = os_ref[pl.ds(G_A + g, nseq, stride=H_A), :]
            gate = proj_ref[:, O_GA + g * LANES:O_GA + (g + 1) * LANES]
            y_ref[:, Y_A + g * LANES:Y_A + (g + 1) * LANES] = jnp.where(low, o0, o1) * _silu(gate)
        for hd in range(H_C):
            on = _rmsnorm(oc_ref[:, hd * DV_C:(hd + 1) * DV_C], gn_ref[:, hd * DV_C:(hd + 1) * DV_C])
            gate = proj_ref[:, O_GC + hd * DV_C:O_GC + (hd + 1) * DV_C]
            y_ref[:, Y_C + hd * DV_C:Y_C + (hd + 1) * DV_C] = on * _silu(gate)
        x_new = xcur_ref[...] + _bdot(y_ref[...], wout_ref[...])
        xcur_ref[...] = x_new

        @pl.when(l_id == n_l - 1)
        def _():
            xo_ref[...] = _rmsnorm(x_new, fg_ref[...])


def _sample_path(x, ng, win, wout, bias_s, sink_c, w00, b0, lng, lnb, wup, bup, gn, fg, buf_kt, buf_vt, s0):
    depth = win.shape[0]
    nseq = x.shape[0]
    gs = min(SEQ_GROUP, nseq)
    assert nseq % gs == 0 and gs % 8 == 0
    grid = (depth, nseq // gs)

    def const_spec(shape):
        nd = len(shape)
        return pl.BlockSpec(shape, lambda l, i: (0,) * nd)

    def layer_spec(shape):
        nd = len(shape)
        return pl.BlockSpec((None,) + shape, lambda l, i: (l,) + (0,) * nd)

    state_spec = pl.BlockSpec((None, gs, LANES, WINDOW), lambda l, i: (l, i, 0, 0))
    gla_spec = pl.BlockSpec((None, gs * DK_TOT, DV_C), lambda l, i: (l, i, 0))
    kern = functools.partial(_sample_kernel, nseq=nseq, gs=gs)
    return pl.pallas_call(
        kern,
        grid=grid,
        in_specs=[
            const_spec((nseq, D_MODEL)),
            layer_spec((1, D_MODEL)),
            layer_spec((D_MODEL, D_IN_PAD)),
            layer_spec((D_MIX, D_MODEL)),
            const_spec((H_A, WINDOW)),
            layer_spec((H_A, 1)),
            layer_spec((1, D_B)),
            layer_spec((1, D_B)),
            layer_spec((1, D_B)),
            layer_spec((1, D_B)),
            layer_spec((LANES, DK_TOT)),
            layer_spec((1, DK_TOT)),
            layer_spec((1, D_C)),
            const_spec((1, D_MODEL)),
            state_spec, state_spec, gla_spec,
        ],
        out_specs=[
            const_spec((nseq, D_MODEL)),
            state_spec, state_spec, gla_spec,
            layer_spec((nseq, D_B)),
        ],
        out_shape=[
            jax.ShapeDtypeStruct((nseq, D_MODEL), jnp.float32),
            jax.ShapeDtypeStruct((depth, nseq, LANES, WINDOW), jnp.float32),
            jax.ShapeDtypeStruct((depth, nseq, LANES, WINDOW), jnp.float32),
            jax.ShapeDtypeStruct((depth, nseq * DK_TOT, DV_C), jnp.float32),
            jax.ShapeDtypeStruct((depth, nseq, D_B), jnp.float32),
        ],
        scratch_shapes=[
            pltpu.VMEM((nseq, D_MODEL), jnp.float32),
            pltpu.VMEM((nseq, D_IN_PAD), jnp.float32),
            pltpu.VMEM((nseq, D_MIX), jnp.float32),
            pltpu.VMEM((nseq * H_A, LANES), jnp.float32),
            pltpu.VMEM((nseq * H_A, LANES), jnp.float32),
            pltpu.VMEM((nseq, DK_TOT), jnp.float32),
            pltpu.VMEM((nseq, DK_TOT), jnp.float32),
            pltpu.VMEM((nseq, D_C), jnp.float32),
        ],
        compiler_params=pltpu.CompilerParams(
            dimension_semantics=("arbitrary", "arbitrary"),
            vmem_limit_bytes=VMEM_LIMIT_BYTES),
        name="sample_path",
    )(x, ng, win, wout, bias_s, sink_c, w00, b0, lng, lnb, wup, bup, gn, fg, buf_kt, buf_vt, s0)


def _window_minor(state):
    depth, n = state.shape[:2]
    return jnp.transpose(state, (0, 1, 3, 4, 2)).reshape(depth, n, LANES, WINDOW)


def _window_major(state_t):
    depth, n = state_t.shape[:2]
    return jnp.transpose(state_t.reshape(depth, n, KV_A, HD_A, WINDOW), (0, 1, 4, 2, 3))


def kernel(x_prompt, x_sample, state_swa_k, state_swa_v, state_gla, rel_bias, norm_g, w_in, sinks, spatial_w,
           spatial_b, chunk_ln_g, chunk_ln_b, gla_w_up, gla_b_up, gla_norm_g, w_out, final_norm_g):
    depth = w_in.shape[0]
    nseq = x_sample.shape[0]
    bsz = x_prompt.shape[0]
    f32 = jnp.float32

    win, wout = _prepare_weights(w_in, w_out)
    wup = jnp.pad(gla_w_up, ((0, 0), (0, LANES - GLA_RANK), (0, 0))).astype(jnp.bfloat16)
    bsf = jnp.repeat(jnp.swapaxes(spatial_b, 1, 2), DH_B, axis=2).astype(f32)
    w00 = jnp.repeat(spatial_w[:, :, 0, 0], DH_B, axis=1).astype(f32)[:, None, :]
    b0 = jnp.repeat(spatial_b[:, :, 0], DH_B, axis=1).astype(f32)[:, None, :]
    sink_c = sinks.astype(f32)[:, :, None]
    row = lambda a: a.astype(f32)[:, None, :]
    ng, lng, lnb, bup, gn = row(norm_g), row(chunk_ln_g), row(chunk_ln_b), row(gla_b_up), row(gla_norm_g)
    fg = final_norm_g.astype(f32)[None, :]
    sw = spatial_w.astype(f32)
    sinks = sinks.astype(f32)

    bias_tab = _bias_table(rel_bias)
    bias_s = bias_tab[0].reshape(H_A, WINDOW, 2 * WINDOW)[:, WINDOW - 1, WINDOW:]

    xp = x_prompt
    kp_l, vp_l, sp_l = [], [], []
    for l in range(depth):
        xp, kp, vp, sp = _prompt_layer(xp, ng, win, wout, bias_tab, sinks, sw, bsf, lng, lnb, wup, bup, gn, fg,
                                       layer=l, final=l == depth - 1)
        kp_l.append(kp); vp_l.append(vp); sp_l.append(sp)

    xs, ks_t, vs_t, ss, cv = _sample_path(
        x_sample.reshape(nseq, D_MODEL), ng, win, wout, bias_s, sink_c, w00, b0, lng, lnb, wup, bup, gn, fg,
        _window_minor(state_swa_k), _window_minor(state_swa_v), state_gla.reshape(depth, nseq * DK_TOT, DV_C))

    return (xp,
            xs.reshape(nseq, 1, D_MODEL),
            _window_major(jnp.stack(kp_l)),
            _window_major(jnp.stack(vp_l)),
            jnp.stack(sp_l).reshape(depth, bsz, H_C, DK_C, DV_C),
            _window_major(ks_t),
            _window_major(vs_t),
            ss.reshape(depth, nseq, H_C, DK_C, DV_C),
            cv.reshape(depth, nseq, 1, D_B))
```

```python
import functools

import numpy as np
import jax
import jax.numpy as jnp
from jax import lax
from jax.experimental import pallas as pl
from jax.experimental.pallas import tpu as pltpu

D_MODEL = 1024
D_A, HD_A, H_A, KV_A, G_A = 512, 64, 8, 2, 4
WINDOW, N_BUCKETS, MAX_DIST = 128, 32, 128
D_B, H_B, DH_B, CHUNK_B = 512, 8, 64, 128
D_C, H_C, DK_TOT, DK_C, DV_C = 512, 4, 256, 64, 128
GLA_RANK, GLA_TAU, GLA_CHUNK = 16, 16.0, 64
D_MIX = D_A + D_B + D_C
EPS = 1e-6
NEG = -1e30
SPLITS = [D_A, KV_A * HD_A, KV_A * HD_A, D_A, D_B, D_B, D_B, DK_TOT, DK_TOT, D_C, D_C, GLA_RANK]
D_IN = sum(SPLITS)

LANES = 128
HALF = LANES // 2
LR_PAD = ((GLA_RANK + LANES - 1) // LANES) * LANES
VMEM_LIMIT_BYTES = 56 * 1024 * 1024

(O_QA, O_KA, O_VA, O_GA, O_UB, O_VB, O_GB, O_QC, O_KC, O_VC, O_GC, O_LR) = (
    int(o) for o in np.cumsum([0] + SPLITS[:-1]))
assert all(o % LANES == 0 for o in (O_QA, O_KA, O_VA, O_GA, O_UB, O_VB, O_GB, O_QC, O_KC, O_VC, O_GC, O_LR))
D_IN_PAD = O_LR + DK_TOT
Y_A, Y_B, Y_C = 0, D_A, D_A + D_B

HEAD_PERM = [0, 4, 1, 5, 2, 6, 3, 7]

TOKEN_BLOCK = 512
SEQ_GROUP = 16
IN_PROJ_CHUNK = 256
OUT_PROJ_CHUNK = 256
MIXER_SECTIONS = 19
PRE_SECTIONS = 5
PREP_SPLIT = 2

_NT = (((1,), (1,)), ((), ()))


def _t5_bucket(dist):
    n = np.maximum(dist, 0)
    max_exact = N_BUCKETS // 2
    large = max_exact + (np.log(np.maximum(n, 1) / max_exact) / np.log(MAX_DIST / max_exact)
                         * (N_BUCKETS - max_exact)).astype(np.int32)
    large = np.minimum(large, N_BUCKETS - 1)
    return np.where(n < max_exact, n, large).astype(np.int32)


def _silu(x):
    return x * (1.0 / (1.0 + jnp.exp(-x)))


def _log_sigmoid(x):
    return jnp.minimum(x, 0.0) - jnp.log1p(jnp.exp(-jnp.abs(x)))


def _bdot(a, b):
    return jnp.dot(a.astype(jnp.bfloat16), b.astype(jnp.bfloat16), preferred_element_type=jnp.float32)


def _bdot_nt(a, b):
    return lax.dot_general(a.astype(jnp.bfloat16), b.astype(jnp.bfloat16), _NT,
                           preferred_element_type=jnp.float32)


def _rmsnorm(x, g):
    return x * lax.rsqrt(jnp.mean(x * x, axis=-1, keepdims=True) + EPS) * g


def _layernorm(v, g, b):
    mu = jnp.mean(v, axis=-1, keepdims=True)
    xc = v - mu
    var = jnp.mean(xc * xc, axis=-1, keepdims=True)
    return xc * lax.rsqrt(var + EPS) * g + b


def _softmax_sink(s, sink):
    m = jnp.maximum(jnp.max(s, axis=-1, keepdims=True), sink)
    p = jnp.exp(s - m)
    den = jnp.sum(p, axis=-1, keepdims=True) + jnp.exp(sink - m)
    return p, 1.0 / den


def _bias_table_kernel(rb_ref, bucket_ref, band_ref, out_ref):
    bucket = bucket_ref[...]
    band = band_ref[...] > 0
    own = lax.broadcasted_iota(jnp.int32, bucket.shape, 1) >= WINDOW
    for h in range(H_A):
        acc = jnp.zeros(bucket.shape, jnp.float32)
        for b in range(N_BUCKETS):
            acc = jnp.where(bucket == b, rb_ref[b, h], acc)
        kv, g = divmod(h, G_A)
        rows = pl.ds(g * WINDOW, WINDOW)
        out_ref[0, kv, rows, :] = jnp.where(band, acc, NEG)
        out_ref[1, kv, rows, :] = jnp.where(band & own, acc, NEG)


def _bias_table(rel_bias):
    i = np.arange(WINDOW)[:, None]
    j = np.arange(2 * WINDOW)[None, :]
    dist = i + WINDOW - j
    band = ((dist >= 0) & (dist < WINDOW)).astype(np.int32)
    return pl.pallas_call(
        _bias_table_kernel,
        out_shape=jax.ShapeDtypeStruct((2, KV_A, G_A * WINDOW, 2 * WINDOW), jnp.float32),
        in_specs=[pl.BlockSpec(memory_space=pltpu.SMEM),
                  pl.BlockSpec(memory_space=pltpu.VMEM),
                  pl.BlockSpec(memory_space=pltpu.VMEM)],
        out_specs=pl.BlockSpec(memory_space=pltpu.VMEM),
        name="rel_bias_table",
    )(rel_bias.astype(jnp.float32), jnp.asarray(_t5_bucket(dist)), jnp.asarray(band))


def _win_prep_kernel(wt_ref, lr_ref, wup_ref, o_ref):
    c = pl.program_id(1)
    tiles = wt_ref.shape[0] // LANES
    for cc in range(PREP_SPLIT):
        @pl.when(c == cc)
        def _(cc=cc):
            for t in range(tiles):
                col = (cc * tiles + t) * LANES
                base = next((b for b in (O_QA, O_GA) if b <= col < b + D_A), None)
                if base is None:
                    src = wt_ref[t * LANES:(t + 1) * LANES, :]
                else:
                    g = (col - base) // LANES
                    r0 = base - cc * tiles * LANES
                    assert r0 >= 0 and r0 + D_A <= wt_ref.shape[0]
                    src = jnp.concatenate([wt_ref[r0 + g * HD_A:r0 + (g + 1) * HD_A, :],
                                           wt_ref[r0 + (G_A + g) * HD_A:r0 + (G_A + g + 1) * HD_A, :]], axis=0)
                o_ref[:, col:col + LANES] = src.T.astype(jnp.bfloat16)
            if cc == PREP_SPLIT - 1:
                o_ref[:, O_LR:D_IN_PAD] = jnp.dot(lr_ref[...].T, wup_ref[...], precision=lax.Precision.HIGHEST,
                                                  preferred_element_type=jnp.float32).astype(jnp.bfloat16)


def _wout_prep_kernel(w_ref, o_ref):
    for j, h in enumerate(HEAD_PERM):
        o_ref[j * HD_A:(j + 1) * HD_A, :] = w_ref[h * HD_A:(h + 1) * HD_A, :].astype(jnp.bfloat16)
    o_ref[Y_B:, :] = w_ref[Y_B:, :].astype(jnp.bfloat16)


def _prepare_weights(w_in, w_out, w_up):
    depth = w_in.shape[0]
    cols = D_MODEL // PREP_SPLIT
    wt = jnp.swapaxes(w_in, 1, 2)
    assert O_LR % (PREP_SPLIT * LANES) == 0
    rows = O_LR // PREP_SPLIT
    wt_lr = jnp.pad(wt[:, O_LR:, :], ((0, 0), (0, LR_PAD - GLA_RANK), (0, 0)))
    wup = jnp.pad(w_up.astype(jnp.float32), ((0, 0), (0, LR_PAD - GLA_RANK), (0, 0)))
    win = pl.pallas_call(
        _win_prep_kernel,
        grid=(depth, PREP_SPLIT),
        in_specs=[pl.BlockSpec((None, rows, D_MODEL), lambda l, r: (l, r, 0)),
                  pl.BlockSpec((None, LR_PAD, D_MODEL), lambda l, r: (l, 0, 0)),
                  pl.BlockSpec((None, LR_PAD, DK_TOT), lambda l, r: (l, 0, 0))],
        out_specs=pl.BlockSpec((None, D_MODEL, D_IN_PAD), lambda l, r: (l, 0, 0)),
        out_shape=jax.ShapeDtypeStruct((depth, D_MODEL, D_IN_PAD), jnp.bfloat16),
        compiler_params=pltpu.CompilerParams(dimension_semantics=("arbitrary", "arbitrary"),
                                             vmem_limit_bytes=VMEM_LIMIT_BYTES),
        name="w_in_prep",
    )(wt, wt_lr, wup)
    wout = pl.pallas_call(
        _wout_prep_kernel,
        grid=(depth, PREP_SPLIT),
        in_specs=[pl.BlockSpec((None, D_MIX, cols), lambda l, c: (l, 0, c))],
        out_specs=pl.BlockSpec((None, D_MIX, cols), lambda l, c: (l, 0, c)),
        out_shape=jax.ShapeDtypeStruct((depth, D_MIX, D_MODEL), jnp.bfloat16),
        compiler_params=pltpu.CompilerParams(dimension_semantics=("arbitrary", "arbitrary"),
                                             vmem_limit_bytes=VMEM_LIMIT_BYTES),
        name="w_out_prep",
    )(w_out)
    return win, wout


def _prompt_layer_kernel(x_ref, xn_ref, ng_ref, win_ref, wout_ref, bias_ref, sink_ref, sw_ref, bsf_ref, lng_ref,
                         lnb_ref, bup_ref, gn_ref, fg_ref,
                         xo_ref, ko_ref, vo_ref, so_ref,
                         pa_ref, pb_ref, hb_ref, hn_ref, ya_ref, yb_ref, kprev_ref, vprev_ref, st_ref, wm_ref,
                         *, tb, layer, final):
    b_id = pl.program_id(0)
    i_id = pl.program_id(1)
    n_i = pl.num_programs(1)
    half = tb // 2
    n_sub = half // WINDOW

    @pl.when((b_id == 0) & (i_id == 0))
    def _():
        r = lax.broadcasted_iota(jnp.int32, (CHUNK_B, CHUNK_B), 0)
        c = lax.broadcasted_iota(jnp.int32, (CHUNK_B, CHUNK_B), 1)
        for h in range(H_B):
            wm_ref[h // 2, :, (h % 2) * CHUNK_B:(h % 2 + 1) * CHUNK_B] = jnp.where(
                c <= r, sw_ref[h], 0.0).astype(jnp.bfloat16)
        pa_ref[...] = _bdot(_rmsnorm(x_ref[0:half, :], ng_ref[...]), win_ref[...])

    @pl.when(i_id == 0)
    def _():
        kprev_ref[...] = jnp.zeros_like(kprev_ref)
        vprev_ref[...] = jnp.zeros_like(vprev_ref)
        st_ref[...] = jnp.zeros_like(st_ref)

    hb_ref[...] = _rmsnorm(x_ref[half:tb, :], ng_ref[...]).astype(jnp.bfloat16)
    hn_ref[...] = _rmsnorm(xn_ref[...], ng_ref[...]).astype(jnp.bfloat16)

    def in_proj_chunks(h_ref, p_ref):
        def chunk(c0, c1):
            def run():
                p_ref[:, c0:c1] = jnp.dot(h_ref[...], win_ref[:, c0:c1], preferred_element_type=jnp.float32)
            return run
        edges = list(range(0, D_IN_PAD, IN_PROJ_CHUNK)) + [D_IN_PAD]
        return [chunk(c0, c1) for c0, c1 in zip(edges[:-1], edges[1:])]

    def out_proj_chunks(y_ref, r0, s):
        def chunk(c0, c1):
            def run():
                rows = slice(r0 + s * WINDOW, r0 + (s + 1) * WINDOW)
                xo_ref[rows, c0:c1] = x_ref[rows, c0:c1] + jnp.dot(
                    y_ref[s * WINDOW:(s + 1) * WINDOW, :], wout_ref[:, c0:c1],
                    preferred_element_type=jnp.float32)
            return run
        return [chunk(c0, c0 + OUT_PROJ_CHUNK) for c0 in range(0, D_MODEL, OUT_PROJ_CHUNK)]

    def final_norm(r0):
        if final:
            xo_ref[r0:r0 + half, :] = _rmsnorm(xo_ref[r0:r0 + half, :], fg_ref[...])

    lane = lax.broadcasted_iota(jnp.int32, (WINDOW, LANES), 1)
    low = lane < HALF
    low64 = lax.broadcasted_iota(jnp.int32, (GLA_CHUNK, LANES), 1) < HALF
    chunk_shift = GLA_CHUNK.bit_length() - 1
    rh = lax.broadcasted_iota(jnp.int32, (half, half), 0)
    ch = lax.broadcasted_iota(jnp.int32, (half, half), 1)
    cum_half = jnp.where((jnp.right_shift(rh, chunk_shift) == jnp.right_shift(ch, chunk_shift)) & (ch <= rh),
                         1.0, 0.0).astype(jnp.bfloat16)
    ri2 = lax.broadcasted_iota(jnp.int32, (LANES, 2 * LANES), 0)
    ci2 = lax.broadcasted_iota(jnp.int32, (LANES, 2 * LANES), 1) & (LANES - 1)
    bd_tril2 = ((ri2 >= GLA_CHUNK) == (ci2 >= GLA_CHUNK)) & (ci2 <= ri2)

    def mixers(p_ref, y_ref, s, first, pre):
        r0 = s * WINDOW
        rows = pl.ds(r0, WINDOW)

        ks = p_ref[rows, O_KA:O_KA + LANES] * (HD_A ** -0.5)
        k_new = [jnp.where(low, ks, 0.0).astype(jnp.bfloat16), jnp.where(low, 0.0, ks).astype(jnp.bfloat16)]
        v_new = p_ref[rows, O_VA:O_VA + LANES].T.astype(jnp.bfloat16)
        vcat = jnp.concatenate([carry["v"], v_new], axis=1)
        q4 = jnp.concatenate([p_ref[rows, O_QA + g * LANES:O_QA + (g + 1) * LANES].astype(jnp.bfloat16)
                              for g in range(G_A)], axis=0)
        o_heads = []
        for kv in range(KV_A):
            kcat = jnp.concatenate([carry["k"][kv], k_new[kv]], axis=0)
            sc = lax.dot_general(q4, kcat, _NT, preferred_element_type=jnp.float32) + bias_ref[first, kv]
            carry["k"][kv] = k_new[kv]
            yield
            ps, invs = [], []
            for g in range(G_A):
                p, inv = _softmax_sink(sc[g * WINDOW:(g + 1) * WINDOW], sink_ref[layer, kv * G_A + g])
                ps.append(p.astype(jnp.bfloat16))
                invs.append(inv)
                if g % 2 == 1:
                    yield
            o_t = lax.dot_general(vcat[kv * HD_A:(kv + 1) * HD_A, :], jnp.concatenate(ps, axis=0), _NT,
                                  preferred_element_type=jnp.float32)
            o_heads.append((o_t, invs))
            yield
        carry["v"] = v_new
        for g in range(G_A):
            o_pair = jnp.concatenate([o_heads[kv][0][:, g * WINDOW:(g + 1) * WINDOW] for kv in range(KV_A)],
                                     axis=0).T
            ya = o_pair * jnp.where(low, o_heads[0][1][g], o_heads[1][1][g])
            gate = p_ref[rows, O_GA + g * LANES:O_GA + (g + 1) * LANES]
            y_ref[rows, Y_A + g * LANES:Y_A + (g + 1) * LANES] = (ya * _silu(gate)).astype(jnp.bfloat16)
        yield

        for j in range(H_B // 2):
            mix = pre["mix"][j][:, s * LANES:(s + 1) * LANES] + bsf_ref[:, j * LANES:(j + 1) * LANES]
            u = p_ref[rows, O_UB + j * LANES:O_UB + (j + 1) * LANES]
            gate = p_ref[rows, O_GB + j * LANES:O_GB + (j + 1) * LANES]
            y_ref[rows, Y_B + j * LANES:Y_B + (j + 1) * LANES] = (u * mix * _silu(gate)).astype(jnp.bfloat16)
            if j % 2 == 1:
                yield

        bc = pre["bc"][r0:r0 + WINDOW]
        for c in range(WINDOW // GLA_CHUNK):
            rc = pl.ds(r0 + c * GLA_CHUNK, GLA_CHUNK)
            q_sts, k_sts, v_sts, kd_sts, decays, sts = [], [], [], [], [], []
            for p in range(H_C // 2):
                bp = bc[c * GLA_CHUNK:(c + 1) * GLA_CHUNK, p * LANES:(p + 1) * LANES]
                qp = p_ref[rc, O_QC + p * LANES:O_QC + (p + 1) * LANES]
                kp = p_ref[rc, O_KC + p * LANES:O_KC + (p + 1) * LANES]
                b_last = bp[GLA_CHUNK - 1:GLA_CHUNK, :]
                qd = qp * (DK_C ** -0.5) * jnp.exp(bp)
                ki = (kp * jnp.exp(-bp)).astype(jnp.bfloat16)
                kd = kp * jnp.exp(b_last - bp)
                q_sts.append(jnp.concatenate([jnp.where(low64, qd, 0.0), jnp.where(low64, 0.0, qd)],
                                             axis=0).astype(jnp.bfloat16))
                k_sts.append(jnp.concatenate([ki, ki], axis=0))
                v0 = p_ref[rc, O_VC + (2 * p) * DV_C:O_VC + (2 * p + 1) * DV_C]
                v1 = p_ref[rc, O_VC + (2 * p + 1) * DV_C:O_VC + (2 * p + 2) * DV_C]
                v_sts.append(jnp.concatenate([v0, v1], axis=0))
                kd_sts.append(jnp.concatenate([jnp.where(low64, kd, 0.0), jnp.where(low64, 0.0, kd)],
                                              axis=0).astype(jnp.bfloat16))
                decays.append(jnp.exp(b_last))
                yield
            zero = jnp.zeros((LANES, LANES), jnp.bfloat16)
            blockdiag = lambda a, b: jnp.concatenate([jnp.concatenate([a, zero], axis=1),
                                                      jnp.concatenate([zero, b], axis=1)], axis=0)
            q_all = jnp.concatenate(q_sts, axis=1)
            att = lax.dot_general(q_all, blockdiag(*k_sts), _NT, preferred_element_type=jnp.float32)
            att = jnp.where(bd_tril2, att, 0.0).astype(jnp.bfloat16)
            v_b = [v.astype(jnp.bfloat16) for v in v_sts]
            o_all = jnp.dot(att, blockdiag(*v_b), preferred_element_type=jnp.float32)
            d_all = jnp.dot(jnp.concatenate([v.T.astype(jnp.bfloat16) for v in v_sts], axis=1),
                            blockdiag(*kd_sts), preferred_element_type=jnp.float32)
            sts = carry["st"]
            o_inter = lax.dot_general(q_all, blockdiag(*[st.astype(jnp.bfloat16) for st in sts]), _NT,
                                      preferred_element_type=jnp.float32)
            o_all = o_all + o_inter
            carry["st"] = [sts[p] * decays[p] + d_all[:, p * LANES:(p + 1) * LANES] for p in range(H_C // 2)]
            for p in range(H_C // 2):
                o = o_all[:, p * DV_C:(p + 1) * DV_C]
                for hh in range(2):
                    hd = 2 * p + hh
                    oh = o[hh * GLA_CHUNK:(hh + 1) * GLA_CHUNK]
                    on = _rmsnorm(oh, gn_ref[:, hd * DV_C:(hd + 1) * DV_C])
                    gate = p_ref[rc, O_GC + hd * DV_C:O_GC + (hd + 1) * DV_C]
                    y_ref[rc, Y_C + hd * DV_C:Y_C + (hd + 1) * DV_C] = (on * _silu(gate)).astype(jnp.bfloat16)
                yield

    def half_pre(p_ref, pre):
        vn = _layernorm(p_ref[:, O_VB:O_VB + D_B], lng_ref[...], lnb_ref[...]).astype(jnp.bfloat16)
        yield
        pre["mix"] = []
        for j in range(H_B // 2):
            cols = []
            for s in range(n_sub):
                vp = vn[s * WINDOW:(s + 1) * WINDOW, j * LANES:(j + 1) * LANES]
                zero = jnp.zeros_like(vp)
                cols.append(jnp.concatenate([jnp.where(low, vp, zero), jnp.where(low, zero, vp)], axis=0))
            pre["mix"].append(jnp.dot(wm_ref[j], jnp.concatenate(cols, axis=1),
                                      preferred_element_type=jnp.float32))
            if j % 2 == 1:
                yield
        z = p_ref[:, O_LR:O_LR + DK_TOT] + bup_ref[...]
        la = _log_sigmoid(z) * (1.0 / GLA_TAU)
        yield
        hi = la.astype(jnp.bfloat16)
        r1 = la - hi.astype(jnp.float32)
        mid = r1.astype(jnp.bfloat16)
        lo = (r1 - mid.astype(jnp.float32)).astype(jnp.bfloat16)
        c3 = jnp.dot(cum_half, jnp.concatenate([hi, mid, lo], axis=1), preferred_element_type=jnp.float32)
        pre["bc"] = (c3[:, 0:DK_TOT] + c3[:, DK_TOT:2 * DK_TOT]) + c3[:, 2 * DK_TOT:3 * DK_TOT]
        yield

    def half_mixers(p_ref, y_ref, first):
        pre = {}
        yield from half_pre(p_ref, pre)
        for s in range(n_sub):
            yield from mixers(p_ref, y_ref, s, first if s == 0 else 0, pre)

    def run_phase(sections, chunks, late_chunks):
        n_sections = PRE_SECTIONS + n_sub * MIXER_SECTIONS
        plan = [[] for _ in range(n_sections)]
        for idx, run in enumerate(chunks):
            plan[idx * n_sections // len(chunks)].append(run)
        for s, runs in late_chunks.items():
            k0 = PRE_SECTIONS + (s + 1) * MIXER_SECTIONS
            for idx, run in enumerate(runs):
                plan[k0 + idx * (n_sections - k0) // len(runs)].append(run)
        for k in range(n_sections):
            next(sections)
            for run in plan[k]:
                run()
        assert next(sections, None) is None

    carry = {"k": [kprev_ref[kv] for kv in range(KV_A)], "v": vprev_ref[...],
             "st": [st_ref[p] for p in range(H_C // 2)]}
    run_phase(half_mixers(pa_ref, ya_ref, jnp.where(i_id == 0, 1, 0)),
              in_proj_chunks(hb_ref, pb_ref),
              {s: out_proj_chunks(ya_ref, 0, s) for s in range(n_sub - 1)})
    run_phase(half_mixers(pb_ref, yb_ref, 0),
              out_proj_chunks(ya_ref, 0, n_sub - 1) + in_proj_chunks(hn_ref, pa_ref),
              {s: out_proj_chunks(yb_ref, half, s) for s in range(n_sub - 1)})
    final_norm(0)
    for run in out_proj_chunks(yb_ref, half, n_sub - 1):
        run()
    final_norm(half)
    for kv in range(KV_A):
        kprev_ref[kv] = carry["k"][kv]
    vprev_ref[...] = carry["v"]
    for p in range(H_C // 2):
        st_ref[p] = carry["st"][p]

    @pl.when(i_id == n_i - 1)
    def _():
        ko_ref[...] = pb_ref[half - WINDOW:half, O_KA:O_KA + LANES].T
        vo_ref[...] = pb_ref[half - WINDOW:half, O_VA:O_VA + LANES].T
        for p in range(H_C // 2):
            so_ref[p] = st_ref[p].T


def _prompt_layer(x, ng, win, wout, bias_tab, sinks, sw, bsf, lng, lnb, bup, gn, fg, *, layer, final):
    bsz, seq, _ = x.shape
    tb = min(TOKEN_BLOCK, seq)
    half = tb // 2
    assert seq % tb == 0 and half % WINDOW == 0
    n_i = seq // tb
    grid = (bsz, n_i)

    def next_half(b, i):
        lin = jnp.minimum(b * n_i + i + 1, bsz * n_i - 1)
        return lin // n_i, (lin % n_i) * 2, 0

    def const_spec(shape):
        nd = len(shape)
        return pl.BlockSpec(shape, lambda b, i: (0,) * nd, pipeline_mode=pl.Buffered(1))

    def layer_spec(shape):
        nd = len(shape)
        return pl.BlockSpec((None,) + shape, lambda b, i: (layer,) + (0,) * nd, pipeline_mode=pl.Buffered(1))

    kern = functools.partial(_prompt_layer_kernel, tb=tb, layer=layer, final=final)
    return pl.pallas_call(
        kern,
        grid=grid,
        in_specs=[
            pl.BlockSpec((None, tb, D_MODEL), lambda b, i: (b, i, 0)),
            pl.BlockSpec((None, half, D_MODEL), next_half),
            layer_spec((1, D_MODEL)),
            layer_spec((D_MODEL, D_IN_PAD)),
            layer_spec((D_MIX, D_MODEL)),
            const_spec((2, KV_A, G_A * WINDOW, 2 * WINDOW)),
            pl.BlockSpec(memory_space=pltpu.SMEM),
            layer_spec((H_B, CHUNK_B, CHUNK_B)),
            layer_spec((CHUNK_B, D_B)),
            layer_spec((1, D_B)),
            layer_spec((1, D_B)),
            layer_spec((1, DK_TOT)),
            layer_spec((1, D_C)),
            const_spec((1, D_MODEL)),
        ],
        out_specs=[
            pl.BlockSpec((None, tb, D_MODEL), lambda b, i: (b, i, 0)),
            pl.BlockSpec((None, LANES, WINDOW), lambda b, i: (b, 0, 0)),
            pl.BlockSpec((None, LANES, WINDOW), lambda b, i: (b, 0, 0)),
            pl.BlockSpec((None, H_C // 2, LANES, DV_C), lambda b, i: (b, 0, 0, 0)),
        ],
        out_shape=[
            jax.ShapeDtypeStruct((bsz, seq, D_MODEL), jnp.float32),
            jax.ShapeDtypeStruct((bsz, LANES, WINDOW), jnp.float32),
            jax.ShapeDtypeStruct((bsz, LANES, WINDOW), jnp.float32),
            jax.ShapeDtypeStruct((bsz, H_C // 2, LANES, DV_C), jnp.float32),
        ],
        scratch_shapes=[
            pltpu.VMEM((half, D_IN_PAD), jnp.float32),
            pltpu.VMEM((half, D_IN_PAD), jnp.float32),
            pltpu.VMEM((half, D_MODEL), jnp.bfloat16),
            pltpu.VMEM((half, D_MODEL), jnp.bfloat16),
            pltpu.VMEM((half, D_MIX), jnp.bfloat16),
            pltpu.VMEM((half, D_MIX), jnp.bfloat16),
            pltpu.VMEM((KV_A, WINDOW, LANES), jnp.bfloat16),
            pltpu.VMEM((LANES, WINDOW), jnp.bfloat16),
            pltpu.VMEM((H_C // 2, DV_C, LANES), jnp.float32),
            pltpu.VMEM((H_B // 2, CHUNK_B, 2 * CHUNK_B), jnp.bfloat16),
        ],
        compiler_params=pltpu.CompilerParams(
            dimension_semantics=("arbitrary", "arbitrary"),
            vmem_limit_bytes=VMEM_LIMIT_BYTES),
        name="prompt_layer",
    )(x, x, ng, win, wout, bias_tab, sinks, sw, bsf, lng, lnb, bup, gn, fg)


def _sample_kernel(x_ref, ng_ref, win_ref, wout_ref, biass_ref, sinkc_ref, w00_ref, b0_ref, lng_ref,
                   lnb_ref, bup_ref, gn_ref, fg_ref, bk_ref, bv_ref, s0_ref,
                   xo_ref, nk_ref, nv_ref, so_ref, cv_ref,
                   xcur_ref, proj_ref, y_ref, qs_ref, os_ref, qd_ref, ea_ref, oc_ref, *, nseq, gs):
    l_id = pl.program_id(0)
    i_id = pl.program_id(1)
    n_l = pl.num_programs(0)
    n_i = pl.num_programs(1)
    lane = lax.broadcasted_iota(jnp.int32, (nseq, LANES), 1)
    low = lane < HALF

    @pl.when((l_id == 0) & (i_id == 0))
    def _():
        xcur_ref[...] = x_ref[...]

    @pl.when(i_id == 0)
    def _():
        h = _rmsnorm(xcur_ref[...], ng_ref[...])
        proj_ref[...] = _bdot(h, win_ref[...])
        for g in range(G_A):
            qp = proj_ref[:, O_QA + g * LANES:O_QA + (g + 1) * LANES] * (HD_A ** -0.5)
            for kv in range(KV_A):
                keep = low if kv == 0 else jnp.logical_not(low)
                qs_ref[pl.ds(kv * G_A + g, nseq, stride=H_A), :] = jnp.where(keep, qp, 0.0)
        vn = _layernorm(proj_ref[:, O_VB:O_VB + D_B], lng_ref[...], lnb_ref[...])
        cv_ref[...] = vn
        mix = w00_ref[...] * vn + b0_ref[...]
        y_ref[:, Y_B:Y_B + D_B] = (proj_ref[:, O_UB:O_UB + D_B] * mix) * _silu(proj_ref[:, O_GB:O_GB + D_B])
        z = proj_ref[:, O_LR:O_LR + DK_TOT] + bup_ref[...]
        la = _log_sigmoid(z) * (1.0 / GLA_TAU)
        ea = jnp.exp(la)
        qd = proj_ref[:, O_QC:O_QC + DK_TOT] * (DK_C ** -0.5) * ea
        ki = proj_ref[:, O_KC:O_KC + DK_TOT] * jnp.exp(-la)
        qd_ref[...] = qd
        ea_ref[...] = ea
        prod = qd * ki
        lane_c = lax.broadcasted_iota(jnp.int32, prod.shape, 1)
        for hd in range(H_C):
            in_head = (lane_c >= hd * DK_C) & (lane_c < (hd + 1) * DK_C)
            att = jnp.sum(jnp.where(in_head, prod, 0.0), axis=-1, keepdims=True)
            oc_ref[:, hd * DV_C:(hd + 1) * DV_C] = att * proj_ref[:, O_VC + hd * DV_C:O_VC + (hd + 1) * DV_C]

    g0 = pl.multiple_of(i_id * gs, gs)
    grows = pl.ds(g0, gs)

    pad_rows = jnp.zeros((LANES - gs, LANES), jnp.float32)
    k_cols = jnp.concatenate([proj_ref[grows, O_KA:O_KA + LANES], pad_rows], axis=0).T
    v_cols = jnp.concatenate([proj_ref[grows, O_VA:O_VA + LANES], pad_rows], axis=0).T
    newest = lax.broadcasted_iota(jnp.int32, (LANES, WINDOW), 1) == WINDOW - 1
    rows_g = pl.ds(pl.multiple_of(g0 * H_A, gs * H_A), gs * H_A)
    q_g = qs_ref[rows_g, :]
    sc_rows = []
    for j in range(gs):
        kt = jnp.where(newest, k_cols[:, j:j + 1], pltpu.roll(bk_ref[j], WINDOW - 1, 1))
        nk_ref[j] = kt
        nv_ref[j] = jnp.where(newest, v_cols[:, j:j + 1], pltpu.roll(bv_ref[j], WINDOW - 1, 1))
        sc_rows.append(_bdot(q_g[j * H_A:(j + 1) * H_A], kt))
    sc = jnp.concatenate(sc_rows, axis=0) + jnp.concatenate([biass_ref[...]] * gs, axis=0)
    p, inv = _softmax_sink(sc, jnp.concatenate([sinkc_ref[...]] * gs, axis=0))
    o_rows = [_bdot_nt(p[j * H_A:(j + 1) * H_A], nv_ref[j]) for j in range(gs)]
    os_ref[rows_g, :] = jnp.concatenate(o_rows, axis=0) * inv

    pad_c = jnp.zeros((LANES - gs, DK_TOT), jnp.float32)
    ea_cols = jnp.concatenate([ea_ref[grows, :], pad_c], axis=0).T
    k_cols_c = jnp.concatenate([proj_ref[grows, O_KC:O_KC + DK_TOT], pad_c], axis=0).T
    qd_g = qd_ref[grows, :]
    row_g = lax.broadcasted_iota(jnp.int32, (gs, LANES), 0)
    low_g = lax.broadcasted_iota(jnp.int32, (gs, LANES), 1) < HALF
    seqs_per_dot = 2 * LANES // DK_C
    for hd in range(H_C):
        v_h = proj_ref[grows, O_VC + hd * DV_C:O_VC + (hd + 1) * DV_C]
        q_pair = qd_g[:, (hd // 2) * LANES:(hd // 2 + 1) * LANES]
        q_other = pltpu.roll(q_pair, HALF, 1)
        q_lo, q_hi = (q_pair, q_other) if hd % 2 == 0 else (q_other, q_pair)
        acc = jnp.zeros((gs, DV_C), jnp.float32)
        for j0 in range(0, gs, seqs_per_dot):
            tiles, lhs = [], []
            for j in range(j0, j0 + seqs_per_dot):
                srows = pl.ds(j * DK_TOT + hd * DK_C, DK_C)
                s_old = s0_ref[srows, :]
                tiles.append(s_old.astype(jnp.bfloat16))
                crows = slice(hd * DK_C, (hd + 1) * DK_C)
                so_ref[srows, :] = ea_cols[crows, j:j + 1] * s_old + k_cols_c[crows, j:j + 1] * v_h[j:j + 1, :]
            for j in range(j0, j0 + seqs_per_dot, 2):
                lhs.append(jnp.where((row_g == j) & low_g, q_lo,
                                     jnp.where((row_g == j + 1) & jnp.logical_not(low_g), q_hi, 0.0)))
            acc = acc + _bdot(jnp.concatenate(lhs, axis=1), jnp.concatenate(tiles, axis=0))
        oc_ref[grows, hd * DV_C:(hd + 1) * DV_C] += acc

    @pl.when(i_id == n_i - 1)
    def _():
        for g in range(G_A):
            o0 = os_ref[pl.ds(g, nseq, stride=H_A), :]
            o1 = os_ref[pl.ds(G_A + g, nseq, stride=H_A), :]
            gate = proj_ref[:, O_GA + g * LANES:O_GA + (g + 1) * LANES]
            y_ref[:, Y_A + g * LANES:Y_A + (g + 1) * LANES] = jnp.where(low, o0, o1) * _silu(gate)
        for hd in range(H_C):
            on = _rmsnorm(oc_ref[:, hd * DV_C:(hd + 1) * DV_C], gn_ref[:, hd * DV_C:(hd + 1) * DV_C])
            gate = proj_ref[:, O_GC + hd * DV_C:O_GC + (hd + 1) * DV_C]
            y_ref[:, Y_C + hd * DV_C:Y_C + (hd + 1) * DV_C] = on * _silu(gate)
        x_new = xcur_ref[...] + _bdot(y_ref[...], wout_ref[...])
        xcur_ref[...] = x_new

        @pl.when(l_id == n_l - 1)
        def _():
            xo_ref[...] = _rmsnorm(x_new, fg_ref[...])


def _sample_path(x, ng, win, wout, bias_s, sink_c, w00, b0, lng, lnb, bup, gn, fg, buf_kt, buf_vt, s0):
    depth = win.shape[0]
    nseq = x.shape[0]
    gs = min(SEQ_GROUP, nseq)
    assert nseq % gs == 0 and gs % 8 == 0
    grid = (depth, nseq // gs)

    def const_spec(shape):
        nd = len(shape)
        return pl.BlockSpec(shape, lambda l, i: (0,) * nd)

    def layer_spec(shape):
        nd = len(shape)
        return pl.BlockSpec((None,) + shape, lambda l, i: (l,) + (0,) * nd)

    state_spec = pl.BlockSpec((None, gs, LANES, WINDOW), lambda l, i: (l, i, 0, 0))
    gla_spec = pl.BlockSpec((None, gs * DK_TOT, DV_C), lambda l, i: (l, i, 0))
    kern = functools.partial(_sample_kernel, nseq=nseq, gs=gs)
    return pl.pallas_call(
        kern,
        grid=grid,
        in_specs=[
            const_spec((nseq, D_MODEL)),
            layer_spec((1, D_MODEL)),
            layer_spec((D_MODEL, D_IN_PAD)),
            layer_spec((D_MIX, D_MODEL)),
            const_spec((H_A, WINDOW)),
            layer_spec((H_A, 1)),
            layer_spec((1, D_B)),
            layer_spec((1, D_B)),
            layer_spec((1, D_B)),
            layer_spec((1, D_B)),
            layer_spec((1, DK_TOT)),
            layer_spec((1, D_C)),
            const_spec((1, D_MODEL)),
            state_spec, state_spec, gla_spec,
        ],
        out_specs=[
            const_spec((nseq, D_MODEL)),
            state_spec, state_spec, gla_spec,
            layer_spec((nseq, D_B)),
        ],
        out_shape=[
            jax.ShapeDtypeStruct((nseq, D_MODEL), jnp.float32),
            jax.ShapeDtypeStruct((depth, nseq, LANES, WINDOW), jnp.float32),
            jax.ShapeDtypeStruct((depth, nseq, LANES, WINDOW), jnp.float32),
            jax.ShapeDtypeStruct((depth, nseq * DK_TOT, DV_C), jnp.float32),
            jax.ShapeDtypeStruct((depth, nseq, D_B), jnp.float32),
        ],
        scratch_shapes=[
            pltpu.VMEM((nseq, D_MODEL), jnp.float32),
            pltpu.VMEM((nseq, D_IN_PAD), jnp.float32),
            pltpu.VMEM((nseq, D_MIX), jnp.float32),
            pltpu.VMEM((nseq * H_A, LANES), jnp.float32),
            pltpu.VMEM((nseq * H_A, LANES), jnp.float32),
            pltpu.VMEM((nseq, DK_TOT), jnp.float32),
            pltpu.VMEM((nseq, DK_TOT), jnp.float32),
            pltpu.VMEM((nseq, D_C), jnp.float32),
        ],
        compiler_params=pltpu.CompilerParams(
            dimension_semantics=("arbitrary", "arbitrary"),
            vmem_limit_bytes=VMEM_LIMIT_BYTES),
        name="sample_path",
    )(x, ng, win, wout, bias_s, sink_c, w00, b0, lng, lnb, bup, gn, fg, buf_kt, buf_vt, s0)


def _window_minor(state):
    depth, n = state.shape[:2]
    return jnp.transpose(state, (0, 1, 3, 4, 2)).reshape(depth, n, LANES, WINDOW)


def _window_major(state_t):
    depth, n = state_t.shape[:2]
    return jnp.transpose(state_t.reshape(depth, n, KV_A, HD_A, WINDOW), (0, 1, 4, 2, 3))


def kernel(x_prompt, x_sample, state_swa_k, state_swa_v, state_gla, rel_bias, norm_g, w_in, sinks, spatial_w,
           spatial_b, chunk_ln_g, chunk_ln_b, gla_w_up, gla_b_up, gla_norm_g, w_out, final_norm_g):
    depth = w_in.shape[0]
    nseq = x_sample.shape[0]
    bsz = x_prompt.shape[0]
    f32 = jnp.float32

    win, wout = _prepare_weights(w_in, w_out, gla_w_up)
    bsf = jnp.repeat(jnp.swapaxes(spatial_b, 1, 2), DH_B, axis=2).astype(f32)
    w00 = jnp.repeat(spatial_w[:, :, 0, 0], DH_B, axis=1).astype(f32)[:, None, :]
    b0 = jnp.repeat(spatial_b[:, :, 0], DH_B, axis=1).astype(f32)[:, None, :]
    sink_c = sinks.astype(f32)[:, :, None]
    row = lambda a: a.astype(f32)[:, None, :]
    ng, lng, lnb, bup, gn = row(norm_g), row(chunk_ln_g), row(chunk_ln_b), row(gla_b_up), row(gla_norm_g)
    fg = final_norm_g.astype(f32)[None, :]
    sw = spatial_w.astype(f32)
    sinks = sinks.astype(f32)

    bias_tab = _bias_table(rel_bias)
    bias_s = bias_tab[0].reshape(H_A, WINDOW, 2 * WINDOW)[:, WINDOW - 1, WINDOW:]

    xp = x_prompt
    kp_l, vp_l, sp_l = [], [], []
    for l in range(depth):
        xp, kp, vp, sp = _prompt_layer(xp, ng, win, wout, bias_tab, sinks, sw, bsf, lng, lnb, bup, gn, fg,
                                       layer=l, final=l == depth - 1)
        kp_l.append(kp); vp_l.append(vp); sp_l.append(sp)

    xs, ks_t, vs_t, ss, cv = _sample_path(
        x_sample.reshape(nseq, D_MODEL), ng, win, wout, bias_s, sink_c, w00, b0, lng, lnb, bup, gn, fg,
        _window_minor(state_swa_k), _window_minor(state_swa_v), state_gla.reshape(depth, nseq * DK_TOT, DV_C))

    return (xp,
            xs.reshape(nseq, 1, D_MODEL),
            _window_major(jnp.stack(kp_l)),
            _window_major(jnp.stack(vp_l)),
            jnp.stack(sp_l).reshape(depth, bsz, H_C, DK_C, DV_C),
            _window_major(ks_t),
            _window_major(vs_t),
            ss.reshape(depth, nseq, H_C, DK_C, DV_C),
            cv.reshape(depth, nseq, 1, D_B))
```

```python
import functools

import numpy as np
import jax
import jax.numpy as jnp
from jax import lax
from jax.experimental import pallas as pl
from jax.experimental.pallas import tpu as pltpu

D_MODEL = 1024
D_A, HD_A, H_A, KV_A, G_A = 512, 64, 8, 2, 4
WINDOW, N_BUCKETS, MAX_DIST = 128, 32, 128
D_B, H_B, DH_B, CHUNK_B = 512, 8, 64, 128
D_C, H_C, DK_TOT, DK_C, DV_C = 512, 4, 256, 64, 128
GLA_RANK, GLA_TAU, GLA_CHUNK = 16, 16.0, 64
D_MIX = D_A + D_B + D_C
EPS = 1e-6
NEG = -1e30
SPLITS = [D_A, KV_A * HD_A, KV_A * HD_A, D_A, D_B, D_B, D_B, DK_TOT, DK_TOT, D_C, D_C, GLA_RANK]
D_IN = sum(SPLITS)

LANES = 128
HALF = LANES // 2
D_IN_PAD = ((D_IN + LANES - 1) // LANES) * LANES
VMEM_LIMIT_BYTES = 56 * 1024 * 1024

(O_QA, O_KA, O_VA, O_GA, O_UB, O_VB, O_GB, O_QC, O_KC, O_VC, O_GC, O_LR) = (
    int(o) for o in np.cumsum([0] + SPLITS[:-1]))
assert all(o % LANES == 0 for o in (O_QA, O_KA, O_VA, O_GA, O_UB, O_VB, O_GB, O_QC, O_KC, O_VC, O_GC, O_LR))
Y_A, Y_B, Y_C = 0, D_A, D_A + D_B

HEAD_PERM = [0, 4, 1, 5, 2, 6, 3, 7]

TOKEN_BLOCK = 512
SEQ_GROUP = 16
IN_PROJ_CHUNK = 256
OUT_PROJ_CHUNK = 256
MIXER_SECTIONS = 19
PRE_SECTIONS = 5
PREP_SPLIT = 2

_NT = (((1,), (1,)), ((), ()))


def _t5_bucket(dist):
    n = np.maximum(dist, 0)
    max_exact = N_BUCKETS // 2
    large = max_exact + (np.log(np.maximum(n, 1) / max_exact) / np.log(MAX_DIST / max_exact)
                         * (N_BUCKETS - max_exact)).astype(np.int32)
    large = np.minimum(large, N_BUCKETS - 1)
    return np.where(n < max_exact, n, large).astype(np.int32)


def _silu(x):
    return x * (1.0 / (1.0 + jnp.exp(-x)))


def _log_sigmoid(x):
    return jnp.minimum(x, 0.0) - jnp.log1p(jnp.exp(-jnp.abs(x)))


def _bdot(a, b):
    return jnp.dot(a.astype(jnp.bfloat16), b.astype(jnp.bfloat16), preferred_element_type=jnp.float32)


def _bdot_nt(a, b):
    return lax.dot_general(a.astype(jnp.bfloat16), b.astype(jnp.bfloat16), _NT,
                           preferred_element_type=jnp.float32)


def _rmsnorm(x, g):
    return x * lax.rsqrt(jnp.mean(x * x, axis=-1, keepdims=True) + EPS) * g


def _layernorm(v, g, b):
    mu = jnp.mean(v, axis=-1, keepdims=True)
    xc = v - mu
    var = jnp.mean(xc * xc, axis=-1, keepdims=True)
    return xc * lax.rsqrt(var + EPS) * g + b


def _softmax_sink(s, sink):
    m = jnp.maximum(jnp.max(s, axis=-1, keepdims=True), sink)
    p = jnp.exp(s - m)
    den = jnp.sum(p, axis=-1, keepdims=True) + jnp.exp(sink - m)
    return p, 1.0 / den


def _bias_table_kernel(rb_ref, bucket_ref, band_ref, out_ref):
    bucket = bucket_ref[...]
    band = band_ref[...] > 0
    own = lax.broadcasted_iota(jnp.int32, bucket.shape, 1) >= WINDOW
    for h in range(H_A):
        acc = jnp.zeros(bucket.shape, jnp.float32)
        for b in range(N_BUCKETS):
            acc = jnp.where(bucket == b, rb_ref[b, h], acc)
        kv, g = divmod(h, G_A)
        rows = pl.ds(g * WINDOW, WINDOW)
        out_ref[0, kv, rows, :] = jnp.where(band, acc, NEG)
        out_ref[1, kv, rows, :] = jnp.where(band & own, acc, NEG)


def _bias_table(rel_bias):
    i = np.arange(WINDOW)[:, None]
    j = np.arange(2 * WINDOW)[None, :]
    dist = i + WINDOW - j
    band = ((dist >= 0) & (dist < WINDOW)).astype(np.int32)
    return pl.pallas_call(
        _bias_table_kernel,
        out_shape=jax.ShapeDtypeStruct((2, KV_A, G_A * WINDOW, 2 * WINDOW), jnp.float32),
        in_specs=[pl.BlockSpec(memory_space=pltpu.SMEM),
                  pl.BlockSpec(memory_space=pltpu.VMEM),
                  pl.BlockSpec(memory_space=pltpu.VMEM)],
        out_specs=pl.BlockSpec(memory_space=pltpu.VMEM),
        name="rel_bias_table",
    )(rel_bias.astype(jnp.float32), jnp.asarray(_t5_bucket(dist)), jnp.asarray(band))


def _win_prep_kernel(wt_ref, lr_ref, o_ref):
    c = pl.program_id(1)
    tiles = wt_ref.shape[0] // LANES
    for cc in range(PREP_SPLIT):
        @pl.when(c == cc)
        def _(cc=cc):
            for t in range(tiles):
                col = (cc * tiles + t) * LANES
                base = next((b for b in (O_QA, O_GA) if b <= col < b + D_A), None)
                if base is None:
                    src = wt_ref[t * LANES:(t + 1) * LANES, :]
                else:
                    g = (col - base) // LANES
                    r0 = base - cc * tiles * LANES
                    assert r0 >= 0 and r0 + D_A <= wt_ref.shape[0]
                    src = jnp.concatenate([wt_ref[r0 + g * HD_A:r0 + (g + 1) * HD_A, :],
                                           wt_ref[r0 + (G_A + g) * HD_A:r0 + (G_A + g + 1) * HD_A, :]], axis=0)
                o_ref[:, col:col + LANES] = src.T.astype(jnp.bfloat16)
            if cc == PREP_SPLIT - 1:
                o_ref[:, O_LR:D_IN_PAD] = lr_ref[...].T.astype(jnp.bfloat16)


def _wout_prep_kernel(w_ref, o_ref):
    for j, h in enumerate(HEAD_PERM):
        o_ref[j * HD_A:(j + 1) * HD_A, :] = w_ref[h * HD_A:(h + 1) * HD_A, :].astype(jnp.bfloat16)
    o_ref[Y_B:, :] = w_ref[Y_B:, :].astype(jnp.bfloat16)


def _prepare_weights(w_in, w_out):
    depth = w_in.shape[0]
    cols = D_MODEL // PREP_SPLIT
    wt = jnp.swapaxes(w_in, 1, 2)
    assert O_LR % (PREP_SPLIT * LANES) == 0
    rows = O_LR // PREP_SPLIT
    wt_lr = jnp.pad(wt[:, O_LR:, :], ((0, 0), (0, D_IN_PAD - D_IN), (0, 0)))
    win = pl.pallas_call(
        _win_prep_kernel,
        grid=(depth, PREP_SPLIT),
        in_specs=[pl.BlockSpec((None, rows, D_MODEL), lambda l, r: (l, r, 0)),
                  pl.BlockSpec((None, LANES, D_MODEL), lambda l, r: (l, 0, 0))],
        out_specs=pl.BlockSpec((None, D_MODEL, D_IN_PAD), lambda l, r: (l, 0, 0)),
        out_shape=jax.ShapeDtypeStruct((depth, D_MODEL, D_IN_PAD), jnp.bfloat16),
        compiler_params=pltpu.CompilerParams(dimension_semantics=("arbitrary", "arbitrary"),
                                             vmem_limit_bytes=VMEM_LIMIT_BYTES),
        name="w_in_prep",
    )(wt, wt_lr)
    wout = pl.pallas_call(
        _wout_prep_kernel,
        grid=(depth, PREP_SPLIT),
        in_specs=[pl.BlockSpec((None, D_MIX, cols), lambda l, c: (l, 0, c))],
        out_specs=pl.BlockSpec((None, D_MIX, cols), lambda l, c: (l, 0, c)),
        out_shape=jax.ShapeDtypeStruct((depth, D_MIX, D_MODEL), jnp.bfloat16),
        compiler_params=pltpu.CompilerParams(dimension_semantics=("arbitrary", "arbitrary"),
                                             vmem_limit_bytes=VMEM_LIMIT_BYTES),
        name="w_out_prep",
    )(w_out)
    return win, wout


def _prompt_layer_kernel(x_ref, xn_ref, ng_ref, win_ref, wout_ref, bias_ref, sink_ref, sw_ref, bsf_ref, lng_ref,
                         lnb_ref, wup_ref, bup_ref, gn_ref, fg_ref,
                         xo_ref, ko_ref, vo_ref, so_ref,
                         pa_ref, pb_ref, hb_ref, hn_ref, ya_ref, yb_ref, kprev_ref, vprev_ref, st_ref, wm_ref,
                         *, tb, layer, final):
    b_id = pl.program_id(0)
    i_id = pl.program_id(1)
    n_i = pl.num_programs(1)
    half = tb // 2
    n_sub = half // WINDOW

    @pl.when((b_id == 0) & (i_id == 0))
    def _():
        r = lax.broadcasted_iota(jnp.int32, (CHUNK_B, CHUNK_B), 0)
        c = lax.broadcasted_iota(jnp.int32, (CHUNK_B, CHUNK_B), 1)
        for h in range(H_B):
            wm_ref[h // 2, :, (h % 2) * CHUNK_B:(h % 2 + 1) * CHUNK_B] = jnp.where(
                c <= r, sw_ref[h], 0.0).astype(jnp.bfloat16)
        pa_ref[...] = _bdot(_rmsnorm(x_ref[0:half, :], ng_ref[...]), win_ref[...])

    @pl.when(i_id == 0)
    def _():
        kprev_ref[...] = jnp.zeros_like(kprev_ref)
        vprev_ref[...] = jnp.zeros_like(vprev_ref)
        st_ref[...] = jnp.zeros_like(st_ref)

    hb_ref[...] = _rmsnorm(x_ref[half:tb, :], ng_ref[...]).astype(jnp.bfloat16)
    hn_ref[...] = _rmsnorm(xn_ref[...], ng_ref[...]).astype(jnp.bfloat16)

    def in_proj_chunks(h_ref, p_ref):
        def chunk(c0, c1):
            def run():
                p_ref[:, c0:c1] = jnp.dot(h_ref[...], win_ref[:, c0:c1], preferred_element_type=jnp.float32)
            return run
        edges = list(range(0, D_IN_PAD, IN_PROJ_CHUNK)) + [D_IN_PAD]
        return [chunk(c0, c1) for c0, c1 in zip(edges[:-1], edges[1:])]

    def out_proj_chunks(y_ref, r0, s):
        def chunk(c0, c1):
            def run():
                rows = slice(r0 + s * WINDOW, r0 + (s + 1) * WINDOW)
                xo_ref[rows, c0:c1] = x_ref[rows, c0:c1] + jnp.dot(
                    y_ref[s * WINDOW:(s + 1) * WINDOW, :], wout_ref[:, c0:c1],
                    preferred_element_type=jnp.float32)
            return run
        return [chunk(c0, c0 + OUT_PROJ_CHUNK) for c0 in range(0, D_MODEL, OUT_PROJ_CHUNK)]

    def final_norm(r0):
        if final:
            xo_ref[r0:r0 + half, :] = _rmsnorm(xo_ref[r0:r0 + half, :], fg_ref[...])

    lane = lax.broadcasted_iota(jnp.int32, (WINDOW, LANES), 1)
    low = lane < HALF
    low64 = lax.broadcasted_iota(jnp.int32, (GLA_CHUNK, LANES), 1) < HALF
    chunk_shift = GLA_CHUNK.bit_length() - 1
    rh = lax.broadcasted_iota(jnp.int32, (half, half), 0)
    ch = lax.broadcasted_iota(jnp.int32, (half, half), 1)
    cum_half = jnp.where((jnp.right_shift(rh, chunk_shift) == jnp.right_shift(ch, chunk_shift)) & (ch <= rh),
                         1.0, 0.0).astype(jnp.bfloat16)
    ri2 = lax.broadcasted_iota(jnp.int32, (LANES, 2 * LANES), 0)
    ci2 = lax.broadcasted_iota(jnp.int32, (LANES, 2 * LANES), 1) & (LANES - 1)
    bd_tril2 = ((ri2 >= GLA_CHUNK) == (ci2 >= GLA_CHUNK)) & (ci2 <= ri2)

    def mixers(p_ref, y_ref, s, first, pre):
        r0 = s * WINDOW
        rows = pl.ds(r0, WINDOW)

        ks = p_ref[rows, O_KA:O_KA + LANES] * (HD_A ** -0.5)
        k_new = [jnp.where(low, ks, 0.0).astype(jnp.bfloat16), jnp.where(low, 0.0, ks).astype(jnp.bfloat16)]
        v_new = p_ref[rows, O_VA:O_VA + LANES].T.astype(jnp.bfloat16)
        vcat = jnp.concatenate([carry["v"], v_new], axis=1)
        q4 = jnp.concatenate([p_ref[rows, O_QA + g * LANES:O_QA + (g + 1) * LANES].astype(jnp.bfloat16)
                              for g in range(G_A)], axis=0)
        o_heads = []
        for kv in range(KV_A):
            kcat = jnp.concatenate([carry["k"][kv], k_new[kv]], axis=0)
            sc = lax.dot_general(q4, kcat, _NT, preferred_element_type=jnp.float32) + bias_ref[first, kv]
            carry["k"][kv] = k_new[kv]
            yield
            ps, invs = [], []
            for g in range(G_A):
                p, inv = _softmax_sink(sc[g * WINDOW:(g + 1) * WINDOW], sink_ref[layer, kv * G_A + g])
                ps.append(p.astype(jnp.bfloat16))
                invs.append(inv)
                if g % 2 == 1:
                    yield
            o_t = lax.dot_general(vcat[kv * HD_A:(kv + 1) * HD_A, :], jnp.concatenate(ps, axis=0), _NT,
                                  preferred_element_type=jnp.float32)
            o_heads.append((o_t, invs))
            yield
        carry["v"] = v_new
        for g in range(G_A):
            o_pair = jnp.concatenate([o_heads[kv][0][:, g * WINDOW:(g + 1) * WINDOW] for kv in range(KV_A)],
                                     axis=0).T
            ya = o_pair * jnp.where(low, o_heads[0][1][g], o_heads[1][1][g])
            gate = p_ref[rows, O_GA + g * LANES:O_GA + (g + 1) * LANES]
            y_ref[rows, Y_A + g * LANES:Y_A + (g + 1) * LANES] = (ya * _silu(gate)).astype(jnp.bfloat16)
        yield

        for j in range(H_B // 2):
            mix = pre["mix"][j][:, s * LANES:(s + 1) * LANES] + bsf_ref[:, j * LANES:(j + 1) * LANES]
            u = p_ref[rows, O_UB + j * LANES:O_UB + (j + 1) * LANES]
            gate = p_ref[rows, O_GB + j * LANES:O_GB + (j + 1) * LANES]
            y_ref[rows, Y_B + j * LANES:Y_B + (j + 1) * LANES] = (u * mix * _silu(gate)).astype(jnp.bfloat16)
            if j % 2 == 1:
                yield

        bc = pre["bc"][r0:r0 + WINDOW]
        for c in range(WINDOW // GLA_CHUNK):
            rc = pl.ds(r0 + c * GLA_CHUNK, GLA_CHUNK)
            q_sts, k_sts, v_sts, kd_sts, decays, sts = [], [], [], [], [], []
            for p in range(H_C // 2):
                bp = bc[c * GLA_CHUNK:(c + 1) * GLA_CHUNK, p * LANES:(p + 1) * LANES]
                qp = p_ref[rc, O_QC + p * LANES:O_QC + (p + 1) * LANES]
                kp = p_ref[rc, O_KC + p * LANES:O_KC + (p + 1) * LANES]
                b_last = bp[GLA_CHUNK - 1:GLA_CHUNK, :]
                qd = qp * (DK_C ** -0.5) * jnp.exp(bp)
                ki = (kp * jnp.exp(-bp)).astype(jnp.bfloat16)
                kd = kp * jnp.exp(b_last - bp)
                q_sts.append(jnp.concatenate([jnp.where(low64, qd, 0.0), jnp.where(low64, 0.0, qd)],
                                             axis=0).astype(jnp.bfloat16))
                k_sts.append(jnp.concatenate([ki, ki], axis=0))
                v0 = p_ref[rc, O_VC + (2 * p) * DV_C:O_VC + (2 * p + 1) * DV_C]
                v1 = p_ref[rc, O_VC + (2 * p + 1) * DV_C:O_VC + (2 * p + 2) * DV_C]
                v_sts.append(jnp.concatenate([v0, v1], axis=0))
                kd_sts.append(jnp.concatenate([jnp.where(low64, kd, 0.0), jnp.where(low64, 0.0, kd)],
                                              axis=0).astype(jnp.bfloat16))
                decays.append(jnp.exp(b_last))
                yield
            zero = jnp.zeros((LANES, LANES), jnp.bfloat16)
            blockdiag = lambda a, b: jnp.concatenate([jnp.concatenate([a, zero], axis=1),
                                                      jnp.concatenate([zero, b], axis=1)], axis=0)
            q_all = jnp.concatenate(q_sts, axis=1)
            att = lax.dot_general(q_all, blockdiag(*k_sts), _NT, preferred_element_type=jnp.float32)
            att = jnp.where(bd_tril2, att, 0.0).astype(jnp.bfloat16)
            v_b = [v.astype(jnp.bfloat16) for v in v_sts]
            o_all = jnp.dot(att, blockdiag(*v_b), preferred_element_type=jnp.float32)
            d_all = jnp.dot(jnp.concatenate([v.T.astype(jnp.bfloat16) for v in v_sts], axis=1),
                            blockdiag(*kd_sts), preferred_element_type=jnp.float32)
            sts = carry["st"]
            o_inter = lax.dot_general(q_all, blockdiag(*[st.astype(jnp.bfloat16) for st in sts]), _NT,
                                      preferred_element_type=jnp.float32)
            o_all = o_all + o_inter
            carry["st"] = [sts[p] * decays[p] + d_all[:, p * LANES:(p + 1) * LANES] for p in range(H_C // 2)]
            for p in range(H_C // 2):
                o = o_all[:, p * DV_C:(p + 1) * DV_C]
                for hh in range(2):
                    hd = 2 * p + hh
                    oh = o[hh * GLA_CHUNK:(hh + 1) * GLA_CHUNK]
                    on = _rmsnorm(oh, gn_ref[:, hd * DV_C:(hd + 1) * DV_C])
                    gate = p_ref[rc, O_GC + hd * DV_C:O_GC + (hd + 1) * DV_C]
                    y_ref[rc, Y_C + hd * DV_C:Y_C + (hd + 1) * DV_C] = (on * _silu(gate)).astype(jnp.bfloat16)
                yield

    def half_pre(p_ref, pre):
        vn = _layernorm(p_ref[:, O_VB:O_VB + D_B], lng_ref[...], lnb_ref[...]).astype(jnp.bfloat16)
        yield
        pre["mix"] = []
        for j in range(H_B // 2):
            cols = []
            for s in range(n_sub):
                vp = vn[s * WINDOW:(s + 1) * WINDOW, j * LANES:(j + 1) * LANES]
                zero = jnp.zeros_like(vp)
                cols.append(jnp.concatenate([jnp.where(low, vp, zero), jnp.where(low, zero, vp)], axis=0))
            pre["mix"].append(jnp.dot(wm_ref[j], jnp.concatenate(cols, axis=1),
                                      preferred_element_type=jnp.float32))
            if j % 2 == 1:
                yield
        z = _bdot(p_ref[:, O_LR:O_LR + LANES], wup_ref[...]) + bup_ref[...]
        la = _log_sigmoid(z) * (1.0 / GLA_TAU)
        yield
        hi = la.astype(jnp.bfloat16)
        r1 = la - hi.astype(jnp.float32)
        mid = r1.astype(jnp.bfloat16)
        lo = (r1 - mid.astype(jnp.float32)).astype(jnp.bfloat16)
        c3 = jnp.dot(cum_half, jnp.concatenate([hi, mid, lo], axis=1), preferred_element_type=jnp.float32)
        pre["bc"] = (c3[:, 0:DK_TOT] + c3[:, DK_TOT:2 * DK_TOT]) + c3[:, 2 * DK_TOT:3 * DK_TOT]
        yield

    def half_mixers(p_ref, y_ref, first):
        pre = {}
        yield from half_pre(p_ref, pre)
        for s in range(n_sub):
            yield from mixers(p_ref, y_ref, s, first if s == 0 else 0, pre)

    def run_phase(sections, chunks, late_chunks):
        n_sections = PRE_SECTIONS + n_sub * MIXER_SECTIONS
        plan = [[] for _ in range(n_sections)]
        for idx, run in enumerate(chunks):
            plan[idx * n_sections // len(chunks)].append(run)
        for s, runs in late_chunks.items():
            k0 = PRE_SECTIONS + (s + 1) * MIXER_SECTIONS
            for idx, run in enumerate(runs):
                plan[k0 + idx * (n_sections - k0) // len(runs)].append(run)
        for k in range(n_sections):
            next(sections)
            for run in plan[k]:
                run()
        assert next(sections, None) is None

    carry = {"k": [kprev_ref[kv] for kv in range(KV_A)], "v": vprev_ref[...],
             "st": [st_ref[p] for p in range(H_C // 2)]}
    run_phase(half_mixers(pa_ref, ya_ref, jnp.where(i_id == 0, 1, 0)),
              in_proj_chunks(hb_ref, pb_ref),
              {s: out_proj_chunks(ya_ref, 0, s) for s in range(n_sub - 1)})
    run_phase(half_mixers(pb_ref, yb_ref, 0),
              out_proj_chunks(ya_ref, 0, n_sub - 1) + in_proj_chunks(hn_ref, pa_ref),
              {s: out_proj_chunks(yb_ref, half, s) for s in range(n_sub - 1)})
    final_norm(0)
    for run in out_proj_chunks(yb_ref, half, n_sub - 1):
        run()
    final_norm(half)
    for kv in range(KV_A):
        kprev_ref[kv] = carry["k"][kv]
    vprev_ref[...] = carry["v"]
    for p in range(H_C // 2):
        st_ref[p] = carry["st"][p]

    @pl.when(i_id == n_i - 1)
    def _():
        ko_ref[...] = pb_ref[half - WINDOW:half, O_KA:O_KA + LANES].T
        vo_ref[...] = pb_ref[half - WINDOW:half, O_VA:O_VA + LANES].T
        for p in range(H_C // 2):
            so_ref[p] = st_ref[p].T


def _prompt_layer(x, ng, win, wout, bias_tab, sinks, sw, bsf, lng, lnb, wup, bup, gn, fg, *, layer, final):
    bsz, seq, _ = x.shape
    tb = min(TOKEN_BLOCK, seq)
    half = tb // 2
    assert seq % tb == 0 and half % WINDOW == 0
    n_i = seq // tb
    grid = (bsz, n_i)

    def next_half(b, i):
        lin = jnp.minimum(b * n_i + i + 1, bsz * n_i - 1)
        return lin // n_i, (lin % n_i) * 2, 0

    def const_spec(shape):
        nd = len(shape)
        return pl.BlockSpec(shape, lambda b, i: (0,) * nd, pipeline_mode=pl.Buffered(1))

    def layer_spec(shape):
        nd = len(shape)
        return pl.BlockSpec((None,) + shape, lambda b, i: (layer,) + (0,) * nd, pipeline_mode=pl.Buffered(1))

    kern = functools.partial(_prompt_layer_kernel, tb=tb, layer=layer, final=final)
    return pl.pallas_call(
        kern,
        grid=grid,
        in_specs=[
            pl.BlockSpec((None, tb, D_MODEL), lambda b, i: (b, i, 0)),
            pl.BlockSpec((None, half, D_MODEL), next_half),
            layer_spec((1, D_MODEL)),
            layer_spec((D_MODEL, D_IN_PAD)),
            layer_spec((D_MIX, D_MODEL)),
            const_spec((2, KV_A, G_A * WINDOW, 2 * WINDOW)),
            pl.BlockSpec(memory_space=pltpu.SMEM),
            layer_spec((H_B, CHUNK_B, CHUNK_B)),
            layer_spec((CHUNK_B, D_B)),
            layer_spec((1, D_B)),
            layer_spec((1, D_B)),
            layer_spec((LANES, DK_TOT)),
            layer_spec((1, DK_TOT)),
            layer_spec((1, D_C)),
            const_spec((1, D_MODEL)),
        ],
        out_specs=[
            pl.BlockSpec((None, tb, D_MODEL), lambda b, i: (b, i, 0)),
            pl.BlockSpec((None, LANES, WINDOW), lambda b, i: (b, 0, 0)),
            pl.BlockSpec((None, LANES, WINDOW), lambda b, i: (b, 0, 0)),
            pl.BlockSpec((None, H_C // 2, LANES, DV_C), lambda b, i: (b, 0, 0, 0)),
        ],
        out_shape=[
            jax.ShapeDtypeStruct((bsz, seq, D_MODEL), jnp.float32),
            jax.ShapeDtypeStruct((bsz, LANES, WINDOW), jnp.float32),
            jax.ShapeDtypeStruct((bsz, LANES, WINDOW), jnp.float32),
            jax.ShapeDtypeStruct((bsz, H_C // 2, LANES, DV_C), jnp.float32),
        ],
        scratch_shapes=[
            pltpu.VMEM((half, D_IN_PAD), jnp.float32),
            pltpu.VMEM((half, D_IN_PAD), jnp.float32),
            pltpu.VMEM((half, D_MODEL), jnp.bfloat16),
            pltpu.VMEM((half, D_MODEL), jnp.bfloat16),
            pltpu.VMEM((half, D_MIX), jnp.bfloat16),
            pltpu.VMEM((half, D_MIX), jnp.bfloat16),
            pltpu.VMEM((KV_A, WINDOW, LANES), jnp.bfloat16),
            pltpu.VMEM((LANES, WINDOW), jnp.bfloat16),
            pltpu.VMEM((H_C // 2, DV_C, LANES), jnp.float32),
            pltpu.VMEM((H_B // 2, CHUNK_B, 2 * CHUNK_B), jnp.bfloat16),
        ],
        compiler_params=pltpu.CompilerParams(
            dimension_semantics=("arbitrary", "arbitrary"),
            vmem_limit_bytes=VMEM_LIMIT_BYTES),
        name="prompt_layer",
    )(x, x, ng, win, wout, bias_tab, sinks, sw, bsf, lng, lnb, wup, bup, gn, fg)


def _sample_kernel(x_ref, ng_ref, win_ref, wout_ref, biass_ref, sinkc_ref, w00_ref, b0_ref, lng_ref,
                   lnb_ref, wup_ref, bup_ref, gn_ref, fg_ref, bk_ref, bv_ref, s0_ref,
                   xo_ref, nk_ref, nv_ref, so_ref, cv_ref,
                   xcur_ref, proj_ref, y_ref, qs_ref, os_ref, qd_ref, ea_ref, oc_ref, *, nseq, gs):
    l_id = pl.program_id(0)
    i_id = pl.program_id(1)
    n_l = pl.num_programs(0)
    n_i = pl.num_programs(1)
    lane = lax.broadcasted_iota(jnp.int32, (nseq, LANES), 1)
    low = lane < HALF

    @pl.when((l_id == 0) & (i_id == 0))
    def _():
        xcur_ref[...] = x_ref[...]

    @pl.when(i_id == 0)
    def _():
        h = _rmsnorm(xcur_ref[...], ng_ref[...])
        proj_ref[...] = _bdot(h, win_ref[...])
        for g in range(G_A):
            qp = proj_ref[:, O_QA + g * LANES:O_QA + (g + 1) * LANES] * (HD_A ** -0.5)
            for kv in range(KV_A):
                keep = low if kv == 0 else jnp.logical_not(low)
                qs_ref[pl.ds(kv * G_A + g, nseq, stride=H_A), :] = jnp.where(keep, qp, 0.0)
        vn = _layernorm(proj_ref[:, O_VB:O_VB + D_B], lng_ref[...], lnb_ref[...])
        cv_ref[...] = vn
        mix = w00_ref[...] * vn + b0_ref[...]
        y_ref[:, Y_B:Y_B + D_B] = (proj_ref[:, O_UB:O_UB + D_B] * mix) * _silu(proj_ref[:, O_GB:O_GB + D_B])
        z = _bdot(proj_ref[:, O_LR:O_LR + LANES], wup_ref[...]) + bup_ref[...]
        la = _log_sigmoid(z) * (1.0 / GLA_TAU)
        ea = jnp.exp(la)
        qd = proj_ref[:, O_QC:O_QC + DK_TOT] * (DK_C ** -0.5) * ea
        ki = proj_ref[:, O_KC:O_KC + DK_TOT] * jnp.exp(-la)
        qd_ref[...] = qd
        ea_ref[...] = ea
        prod = qd * ki
        lane_c = lax.broadcasted_iota(jnp.int32, prod.shape, 1)
        for hd in range(H_C):
            in_head = (lane_c >= hd * DK_C) & (lane_c < (hd + 1) * DK_C)
            att = jnp.sum(jnp.where(in_head, prod, 0.0), axis=-1, keepdims=True)
            oc_ref[:, hd * DV_C:(hd + 1) * DV_C] = att * proj_ref[:, O_VC + hd * DV_C:O_VC + (hd + 1) * DV_C]

    g0 = pl.multiple_of(i_id * gs, gs)
    grows = pl.ds(g0, gs)

    pad_rows = jnp.zeros((LANES - gs, LANES), jnp.float32)
    k_cols = jnp.concatenate([proj_ref[grows, O_KA:O_KA + LANES], pad_rows], axis=0).T
    v_cols = jnp.concatenate([proj_ref[grows, O_VA:O_VA + LANES], pad_rows], axis=0).T
    newest = lax.broadcasted_iota(jnp.int32, (LANES, WINDOW), 1) == WINDOW - 1
    rows_g = pl.ds(pl.multiple_of(g0 * H_A, gs * H_A), gs * H_A)
    q_g = qs_ref[rows_g, :]
    sc_rows = []
    for j in range(gs):
        kt = jnp.where(newest, k_cols[:, j:j + 1], pltpu.roll(bk_ref[j], WINDOW - 1, 1))
        nk_ref[j] = kt
        sc_rows.append(_bdot(q_g[j * H_A:(j + 1) * H_A], kt))
    for j in range(gs):
        nv_ref[j] = jnp.where(newest, v_cols[:, j:j + 1], pltpu.roll(bv_ref[j], WINDOW - 1, 1))
    sc = jnp.concatenate(sc_rows, axis=0) + jnp.concatenate([biass_ref[...]] * gs, axis=0)
    p, inv = _softmax_sink(sc, jnp.concatenate([sinkc_ref[...]] * gs, axis=0))
    o_rows = [_bdot_nt(p[j * H_A:(j + 1) * H_A], nv_ref[j]) for j in range(gs)]
    os_ref[rows_g, :] = jnp.concatenate(o_rows, axis=0) * inv

    pad_c = jnp.zeros((LANES - gs, DK_TOT), jnp.float32)
    ea_cols = jnp.concatenate([ea_ref[grows, :], pad_c], axis=0).T
    kc_g = proj_ref[grows, O_KC:O_KC + DK_TOT]
    qd_g = qd_ref[grows, :]
    row_g = lax.broadcasted_iota(jnp.int32, (gs, LANES), 0)
    low_g = lax.broadcasted_iota(jnp.int32, (gs, LANES), 1) < HALF
    seqs_per_dot = 2 * LANES // DK_C
    for hd in range(H_C):
        v_h = proj_ref[grows, O_VC + hd * DV_C:O_VC + (hd + 1) * DV_C]
        q_pair = qd_g[:, (hd // 2) * LANES:(hd // 2 + 1) * LANES]
        q_other = pltpu.roll(q_pair, HALF, 1)
        q_lo, q_hi = (q_pair, q_other) if hd % 2 == 0 else (q_other, q_pair)
        k_pair = kc_g[:, (hd // 2) * LANES:(hd // 2 + 1) * LANES]
        k_other = pltpu.roll(k_pair, HALF, 1)
        k_lo, k_hi = (k_pair, k_other) if hd % 2 == 0 else (k_other, k_pair)
        acc = jnp.zeros((gs, DV_C), jnp.float32)
        for j0 in range(0, gs, seqs_per_dot):
            lhs, lhs_k = [], []
            for j in range(j0, j0 + seqs_per_dot, 2):
                sel_lo = (row_g == j) & low_g
                sel_hi = (row_g == j + 1) & jnp.logical_not(low_g)
                lhs.append(jnp.where(sel_lo, q_lo, jnp.where(sel_hi, q_hi, 0.0)))
                lhs_k.append(jnp.where(sel_lo, k_lo, jnp.where(sel_hi, k_hi, 0.0)))
            d_s = lax.dot_general(jnp.concatenate(lhs_k, axis=1).astype(jnp.bfloat16), v_h.astype(jnp.bfloat16),
                                  (((0,), (0,)), ((), ())), preferred_element_type=jnp.float32)
            tiles = []
            for j in range(j0, j0 + seqs_per_dot):
                srows = pl.ds(j * DK_TOT + hd * DK_C, DK_C)
                s_old = s0_ref[srows, :]
                tiles.append(s_old.astype(jnp.bfloat16))
                crows = slice(hd * DK_C, (hd + 1) * DK_C)
                so_ref[srows, :] = (ea_cols[crows, j:j + 1] * s_old
                                    + d_s[(j - j0) * DK_C:(j - j0 + 1) * DK_C])
            acc = acc + _bdot(jnp.concatenate(lhs, axis=1), jnp.concatenate(tiles, axis=0))
        oc_ref[grows, hd * DV_C:(hd + 1) * DV_C] += acc

    @pl.when(i_id == n_i - 1)
    def _():
        for g in range(G_A):
            o0 = os_ref[pl.ds(g, nseq, stride=H_A), :]
            o1 = os_ref[pl.ds(G_A + g, nseq, stride=H_A), :]
            gate = proj_ref[:, O_GA + g * LANES:O_GA + (g + 1) * LANES]
            y_ref[:, Y_A + g * LANES:Y_A + (g + 1) * LANES] = jnp.where(low, o0, o1) * _silu(gate)
        for hd in range(H_C):
            on = _rmsnorm(oc_ref[:, hd * DV_C:(hd + 1) * DV_C], gn_ref[:, hd * DV_C:(hd + 1) * DV_C])
            gate = proj_ref[:, O_GC + hd * DV_C:O_GC + (hd + 1) * DV_C]
            y_ref[:, Y_C + hd * DV_C:Y_C + (hd + 1) * DV_C] = on * _silu(gate)
        x_new = xcur_ref[...] + _bdot(y_ref[...], wout_ref[...])
        xcur_ref[...] = x_new

        @pl.when(l_id == n_l - 1)
        def _():
            xo_ref[...] = _rmsnorm(x_new, fg_ref[...])


def _sample_path(x, ng, win, wout, bias_s, sink_c, w00, b0, lng, lnb, wup, bup, gn, fg, buf_kt, buf_vt, s0):
    depth = win.shape[0]
    nseq = x.shape[0]
    gs = min(SEQ_GROUP, nseq)
    assert nseq % gs == 0 and gs % 8 == 0
    grid = (depth, nseq // gs)

    def const_spec(shape):
        nd = len(shape)
        return pl.BlockSpec(shape, lambda l, i: (0,) * nd)

    def layer_spec(shape):
        nd = len(shape)
        return pl.BlockSpec((None,) + shape, lambda l, i: (l,) + (0,) * nd)

    state_spec = pl.BlockSpec((None, gs, LANES, WINDOW), lambda l, i: (l, i, 0, 0))
    gla_spec = pl.BlockSpec((None, gs * DK_TOT, DV_C), lambda l, i: (l, i, 0))
    kern = functools.partial(_sample_kernel, nseq=nseq, gs=gs)
    return pl.pallas_call(
        kern,
        grid=grid,
        in_specs=[
            const_spec((nseq, D_MODEL)),
            layer_spec((1, D_MODEL)),
            layer_spec((D_MODEL, D_IN_PAD)),
            layer_spec((D_MIX, D_MODEL)),
            const_spec((H_A, WINDOW)),
            layer_spec((H_A, 1)),
            layer_spec((1, D_B)),
            layer_spec((1, D_B)),
            layer_spec((1, D_B)),
            layer_spec((1, D_B)),
            layer_spec((LANES, DK_TOT)),
            layer_spec((1, DK_TOT)),
            layer_spec((1, D_C)),
            const_spec((1, D_MODEL)),
            state_spec, state_spec, gla_spec,
        ],
        out_specs=[
            const_spec((nseq, D_MODEL)),
            state_spec, state_spec, gla_spec,
            layer_spec((nseq, D_B)),
        ],
        out_shape=[
            jax.ShapeDtypeStruct((nseq, D_MODEL), jnp.float32),
            jax.ShapeDtypeStruct((depth, nseq, LANES, WINDOW), jnp.float32),
            jax.ShapeDtypeStruct((depth, nseq, LANES, WINDOW), jnp.float32),
            jax.ShapeDtypeStruct((depth, nseq * DK_TOT, DV_C), jnp.float32),
            jax.ShapeDtypeStruct((depth, nseq, D_B), jnp.float32),
        ],
        scratch_shapes=[
            pltpu.VMEM((nseq, D_MODEL), jnp.float32),
            pltpu.VMEM((nseq, D_IN_PAD), jnp.float32),
            pltpu.VMEM((nseq, D_MIX), jnp.float32),
            pltpu.VMEM((nseq * H_A, LANES), jnp.float32),
            pltpu.VMEM((nseq * H_A, LANES), jnp.float32),
            pltpu.VMEM((nseq, DK_TOT), jnp.float32),
            pltpu.VMEM((nseq, DK_TOT), jnp.float32),
            pltpu.VMEM((nseq, D_C), jnp.float32),
        ],
        compiler_params=pltpu.CompilerParams(
            dimension_semantics=("arbitrary", "arbitrary"),
            vmem_limit_bytes=VMEM_LIMIT_BYTES),
        name="sample_path",
    )(x, ng, win, wout, bias_s, sink_c, w00, b0, lng, lnb, wup, bup, gn, fg, buf_kt, buf_vt, s0)


def _window_minor(state):
    depth, n = state.shape[:2]
    return jnp.transpose(state, (0, 1, 3, 4, 2)).reshape(depth, n, LANES, WINDOW)


def _window_major(state_t):
    depth, n = state_t.shape[:2]
    return jnp.transpose(state_t.reshape(depth, n, KV_A, HD_A, WINDOW), (0, 1, 4, 2, 3))


def kernel(x_prompt, x_sample, state_swa_k, state_swa_v, state_gla, rel_bias, norm_g, w_in, sinks, spatial_w,
           spatial_b, chunk_ln_g, chunk_ln_b, gla_w_up, gla_b_up, gla_norm_g, w_out, final_norm_g):
    depth = w_in.shape[0]
    nseq = x_sample.shape[0]
    bsz = x_prompt.shape[0]
    f32 = jnp.float32

    win, wout = _prepare_weights(w_in, w_out)
    wup = jnp.pad(gla_w_up, ((0, 0), (0, LANES - GLA_RANK), (0, 0))).astype(jnp.bfloat16)
    bsf = jnp.repeat(jnp.swapaxes(spatial_b, 1, 2), DH_B, axis=2).astype(f32)
    w00 = jnp.repeat(spatial_w[:, :, 0, 0], DH_B, axis=1).astype(f32)[:, None, :]
    b0 = jnp.repeat(spatial_b[:, :, 0], DH_B, axis=1).astype(f32)[:, None, :]
    sink_c = sinks.astype(f32)[:, :, None]
    row = lambda a: a.astype(f32)[:, None, :]
    ng, lng, lnb, bup, gn = row(norm_g), row(chunk_ln_g), row(chunk_ln_b), row(gla_b_up), row(gla_norm_g)
    fg = final_norm_g.astype(f32)[None, :]
    sw = spatial_w.astype(f32)
    sinks = sinks.astype(f32)

    bias_tab = _bias_table(rel_bias)
    bias_s = bias_tab[0].reshape(H_A, WINDOW, 2 * WINDOW)[:, WINDOW - 1, WINDOW:]

    xp = x_prompt
    kp_l, vp_l, sp_l = [], [], []
    for l in range(depth):
        xp, kp, vp, sp = _prompt_layer(xp, ng, win, wout, bias_tab, sinks, sw, bsf, lng, lnb, wup, bup, gn, fg,
                                       layer=l, final=l == depth - 1)
        kp_l.append(kp); vp_l.append(vp); sp_l.append(sp)

    xs, ks_t, vs_t, ss, cv = _sample_path(
        x_sample.reshape(nseq, D_MODEL), ng, win, wout, bias_s, sink_c, w00, b0, lng, lnb, wup, bup, gn, fg,
        _window_minor(state_swa_k), _window_minor(state_swa_v), state_gla.reshape(depth, nseq * DK_TOT, DV_C))

    return (xp,
            xs.reshape(nseq, 1, D_MODEL),
            _window_major(jnp.stack(kp_l)),
            _window_major(jnp.stack(vp_l)),
            jnp.stack(sp_l).reshape(depth, bsz, H_C, DK_C, DV_C),
            _window_major(ks_t),
            _window_major(vs_t),
            ss.reshape(depth, nseq, H_C, DK_C, DV_C),
            cv.reshape(depth, nseq, 1, D_B))
```
